```python
import math
import jax, jax.numpy as jnp
from jax import lax
import numpy as np

D_MODEL = 1024
BATCH = 8
SEQ = 4096
DEPTH = 1

N_HEADS_A = 8
HEAD_DIM_A = 128
WIDTH_A = N_HEADS_A * HEAD_DIM_A
SHORT_CONV = 5
CHUNK = 64
WIDTH_B = D_MODEL
DW_CONV = 31
N_EXPERTS = 32
TOP_K = 4
D_FF_EXPERT = D_MODEL
SWIGLU_ALPHA = 1.702
SWIGLU_LIMIT = 7.0
EXPERT_BLOCK = 512
DN_ALPHA = (2.0 * DEPTH) ** 0.25
DN_BETA = (8.0 * DEPTH) ** -0.25
LN_EPS = 1e-5
RMS_EPS = 1e-6
L2_EPS = 1e-6
COL_SIZES = (WIDTH_A, WIDTH_A, WIDTH_A, WIDTH_A,
             N_HEADS_A, N_HEADS_A, N_HEADS_A, N_HEADS_A,
             2 * WIDTH_B, 2 * D_MODEL)
N_IN_COLS = sum(COL_SIZES)

kernel_name = 'hybrid_deltanet_conformer_moe_encoder'


def layer_norm(x, g, b):
    xf = x.astype(jnp.float32)
    mu = jnp.mean(xf, axis=-1, keepdims=True)
    var = jnp.mean(jnp.square(xf - mu), axis=-1, keepdims=True)
    y = (xf - mu) * lax.rsqrt(var + LN_EPS) * g.astype(jnp.float32) + b.astype(jnp.float32)
    return y.astype(x.dtype)


def l2norm(t):
    return t * lax.rsqrt(jnp.sum(jnp.square(t), axis=-1, keepdims=True) + L2_EPS)


def depthwise_conv(x, w):
    width, ch = w.shape
    pad = width // 2
    return lax.conv_general_dilated(
        x, w.astype(x.dtype)[:, None, :], window_strides=(1,), padding=[(pad, pad)],
        dimension_numbers=('NWC', 'WIO', 'NWC'), feature_group_count=ch)


def split_cols(u):
    offsets = np.cumsum(COL_SIZES)[:-1].tolist()
    return jnp.split(u, offsets, axis=-1)


def gated_delta_rule(q, k, v, log_a, beta):
    b, h, s, dk = k.shape
    dv = v.shape[-1]
    n = s // CHUNK
    q = q.reshape(b, h, n, CHUNK, dk)
    k = k.reshape(b, h, n, CHUNK, dk)
    v = v.reshape(b, h, n, CHUNK, dv)
    beta = beta.reshape(b, h, n, CHUNK)
    g = jnp.cumsum(log_a.reshape(b, h, n, CHUNK), axis=-1)
    lower = jnp.tril(jnp.ones((CHUNK, CHUNK), dtype=bool))
    strict = jnp.tril(jnp.ones((CHUNK, CHUNK), dtype=bool), -1)
    decay = jnp.exp(jnp.where(lower, g[..., :, None] - g[..., None, :], -jnp.inf))
    kk = jnp.einsum('bhnid,bhnjd->bhnij', k * beta[..., None], k)
    eye = jnp.eye(CHUNK, dtype=k.dtype)
    a_mat = jnp.where(strict, kk * decay, 0.0) + eye
    t_inv = lax.linalg.triangular_solve(
        a_mat, jnp.broadcast_to(eye, a_mat.shape), left_side=True, lower=True, unit_diagonal=True)
    u = jnp.einsum('bhnij,bhnjd->bhnid', t_inv, v * beta[..., None])
    w = jnp.einsum('bhnij,bhnjd->bhnid', t_inv, k * (beta * jnp.exp(g))[..., None])
    qk = jnp.einsum('bhnid,bhnjd->bhnij', q, k) * decay
    q_dec = q * jnp.exp(g)[..., None]
    k_tail = k * jnp.exp(g[..., -1:] - g)[..., None]
    chunk_dec = jnp.exp(g[..., -1])

    def step(state, inp):
        qk_c, qd_c, w_c, u_c, kt_c, cd_c = inp
        v_new = u_c - jnp.einsum('bhik,bhkv->bhiv', w_c, state)
        o_c = (jnp.einsum('bhik,bhkv->bhiv', qd_c, state)
               + jnp.einsum('bhij,bhjv->bhiv', qk_c, v_new))
        state = state * cd_c[..., None, None] + jnp.einsum('bhik,bhiv->bhkv', kt_c, v_new)
        return state, o_c

    xs = (jnp.moveaxis(qk, 2, 0), jnp.moveaxis(q_dec, 2, 0), jnp.moveaxis(w, 2, 0),
          jnp.moveaxis(u, 2, 0), jnp.moveaxis(k_tail, 2, 0), jnp.moveaxis(chunk_dec, 2, 0))
    state0 = jnp.zeros((b, h, dk, dv), jnp.float32)
    _, o = lax.scan(step, state0, xs)
    return jnp.moveaxis(o, 0, 2).reshape(b, h, s, dv)


def deltanet_branch(uq, uk, uv, uz, ub_f, ub_b, ua_f, ua_b, conv_qkv, a_log, dt_bias, norm_g, w_o):
    bsz, s, _ = uq.shape
    qkv = jax.nn.silu(depthwise_conv(jnp.concatenate([uq, uk, uv], axis=-1), conv_qkv))
    q, k, v = jnp.split(qkv.astype(jnp.float32), 3, axis=-1)

    def heads(t):
        return t.reshape(bsz, s, N_HEADS_A, HEAD_DIM_A).transpose(0, 2, 1, 3)

    def per_head(t):
        return t.astype(jnp.float32).transpose(0, 2, 1)

    q = l2norm(heads(q)) * (HEAD_DIM_A ** -0.5)
    k = l2norm(heads(k))
    v = heads(v)
    a_log = a_log.astype(jnp.float32)
    dt_bias = dt_bias.astype(jnp.float32)
    beta_f = jax.nn.sigmoid(per_head(ub_f))
    beta_b = jax.nn.sigmoid(per_head(ub_b))
    la_f = -jnp.exp(a_log[0])[:, None] * jax.nn.softplus(per_head(ua_f) + dt_bias[0][:, None])
    la_b = -jnp.exp(a_log[1])[:, None] * jax.nn.softplus(per_head(ua_b) + dt_bias[1][:, None])
    o_f = gated_delta_rule(q, k, v, la_f, beta_f)
    o_b = jnp.flip(gated_delta_rule(jnp.flip(q, 2), jnp.flip(k, 2), jnp.flip(v, 2),
                                    jnp.flip(la_b, 2), jnp.flip(beta_b, 2)), 2)
    o = (o_f + o_b).transpose(0, 2, 1, 3)
    z = uz.astype(jnp.float32).reshape(bsz, s, N_HEADS_A, HEAD_DIM_A)
    o = (o * lax.rsqrt(jnp.mean(jnp.square(o), axis=-1, keepdims=True) + RMS_EPS)
         * norm_g.astype(jnp.float32) * jax.nn.silu(z))
    return o.reshape(bsz, s, WIDTH_A).astype(uq.dtype) @ w_o


def conformer_branch(uglu, b_glu, conv_dw, b_dw, ln_g, ln_b, w_o, b_o):
    hg = uglu + b_glu
    y = hg[..., :WIDTH_B] * jax.nn.sigmoid(hg[..., WIDTH_B:])
    y = depthwise_conv(y, conv_dw) + b_dw
    y = jax.nn.silu(layer_norm(y, ln_g, ln_b))
    return y @ w_o + b_o


def clamped_swiglu(hgu):
    glu = jnp.minimum(hgu[..., :D_FF_EXPERT], SWIGLU_LIMIT)
    lin = jnp.clip(hgu[..., D_FF_EXPERT:], -SWIGLU_LIMIT, SWIGLU_LIMIT)
    return glu * jax.nn.sigmoid(SWIGLU_ALPHA * glu) * (lin + 1.0)


def moe_ffn(h, w_router, b_router, w_gu, b_gu, w_down, b_down):
    bsz, s, d = h.shape
    x = h.reshape(-1, d)
    t = x.shape[0]
    logits = (x @ w_router + b_router).astype(jnp.float32)
    top_val, top_idx = lax.top_k(logits, TOP_K)
    gate = jax.nn.softmax(top_val, axis=-1)
    n_assign = t * TOP_K
    e_flat = top_idx.reshape(-1).astype(jnp.int32)
    tok_flat = jnp.arange(n_assign, dtype=jnp.int32) // TOP_K
    gate_flat = gate.reshape(-1)
    order = jnp.argsort(e_flat)
    e_sorted = e_flat[order]
    counts = jnp.zeros((N_EXPERTS,), jnp.int32).at[e_flat].add(1)
    start = jnp.cumsum(counts) - counts
    padded = (counts + EXPERT_BLOCK - 1) // EXPERT_BLOCK * EXPERT_BLOCK
    pad_end = jnp.cumsum(padded)
    pad_start = pad_end - padded
    dest = pad_start[e_sorted] + jnp.arange(n_assign, dtype=jnp.int32) - start[e_sorted]
    n_blocks = -(-(n_assign + N_EXPERTS * (EXPERT_BLOCK - 1)) // EXPERT_BLOCK)
    n_slots = n_blocks * EXPERT_BLOCK
    slot_tok = jnp.zeros((n_slots,), jnp.int32).at[dest].set(tok_flat[order])
    slot_gate = jnp.zeros((n_slots,), jnp.float32).at[dest].set(gate_flat[order])
    block_start = jnp.arange(n_blocks, dtype=jnp.int32) * EXPERT_BLOCK
    block_exp = jnp.minimum(jnp.searchsorted(pad_end, block_start, side='right'),
                            N_EXPERTS - 1).astype(jnp.int32)

    def expert_block(acc, inp):
        tok, gt, e = inp
        xb = x[tok]
        hgu = xb @ w_gu[e] + b_gu[e]
        y = clamped_swiglu(hgu) @ w_down[e] + b_down[e]
        return acc.at[tok].add((y * gt[:, None]).astype(jnp.float32)), None

    acc0 = jnp.zeros((t, d), jnp.float32)
    out, _ = lax.scan(expert_block, acc0,
                      (slot_tok.reshape(n_blocks, EXPERT_BLOCK),
                       slot_gate.reshape(n_blocks, EXPERT_BLOCK), block_exp))
    return out.astype(h.dtype).reshape(bsz, s, d)


def setup_inputs(seed: int = 0) -> dict:
    key = jax.random.key(seed)
    ks = jax.random.split(key, 32)
    L = DEPTH

    def nrm(k, shape, scale):
        return jax.random.normal(k, shape, jnp.float32) * scale

    x = jax.random.normal(ks[0], (BATCH, SEQ, D_MODEL), jnp.float32)
    emb_ln_g = 1.0 + nrm(ks[1], (D_MODEL,), 0.02)
    emb_ln_b = nrm(ks[2], (D_MODEL,), 0.02)
    w_in = nrm(ks[3], (L, D_MODEL, N_IN_COLS), D_MODEL ** -0.5)
    conv_qkv = nrm(ks[4], (L, SHORT_CONV, 3 * WIDTH_A), SHORT_CONV ** -0.5)
    a_log = jnp.log(jax.random.uniform(ks[5], (L, 2, N_HEADS_A), jnp.float32, 1.0, 16.0))
    dt = jnp.exp(jax.random.uniform(ks[6], (L, 2, N_HEADS_A), jnp.float32,
                                    math.log(1e-3), math.log(1e-1)))
    dt_bias = dt + jnp.log(-jnp.expm1(-dt))
    dn_norm_g = 1.0 + nrm(ks[7], (L, HEAD_DIM_A), 0.02)
    w_a_o = nrm(ks[8], (L, WIDTH_A, D_MODEL), WIDTH_A ** -0.5 * DN_BETA)
    b_glu = nrm(ks[9], (L, 2 * WIDTH_B), 0.01)
    conv_dw = nrm(ks[10], (L, DW_CONV, WIDTH_B), DW_CONV ** -0.5)
    b_dw = nrm(ks[11], (L, WIDTH_B), 0.01)
    conv_ln_g = 1.0 + nrm(ks[12], (L, WIDTH_B), 0.02)
    conv_ln_b = nrm(ks[13], (L, WIDTH_B), 0.02)
    w_b_o = nrm(ks[14], (L, WIDTH_B, D_MODEL), WIDTH_B ** -0.5 * DN_BETA)
    b_b_o = nrm(ks[15], (L, D_MODEL), 0.01)
    b_gate = nrm(ks[16], (L, 2 * D_MODEL), 0.1)
    w_out = nrm(ks[17], (L, D_MODEL, D_MODEL), D_MODEL ** -0.5 * DN_BETA)
    ln1_g = 1.0 + nrm(ks[18], (L, D_MODEL), 0.02)
    ln1_b = nrm(ks[19], (L, D_MODEL), 0.02)
    w_router = nrm(ks[20], (L, D_MODEL, N_EXPERTS), D_MODEL ** -0.5)
    b_router = nrm(ks[21], (L, N_EXPERTS), 0.01)
    w_gu = nrm(ks[22], (L, N_EXPERTS, D_MODEL, 2 * D_FF_EXPERT), D_MODEL ** -0.5)
    b_gu = nrm(ks[23], (L, N_EXPERTS, 2 * D_FF_EXPERT), 0.01)
    w_down = nrm(ks[24], (L, N_EXPERTS, D_FF_EXPERT, D_MODEL), D_FF_EXPERT ** -0.5 * DN_BETA)
    b_down = nrm(ks[25], (L, N_EXPERTS, D_MODEL), 0.01)
    ln2_g = 1.0 + nrm(ks[26], (L, D_MODEL), 0.02)
    ln2_b = nrm(ks[27], (L, D_MODEL), 0.02)
    return {'x': x, 'emb_ln_g': emb_ln_g, 'emb_ln_b': emb_ln_b, 'w_in': w_in,
            'conv_qkv': conv_qkv, 'a_log': a_log, 'dt_bias': dt_bias, 'dn_norm_g': dn_norm_g,
            'w_a_o': w_a_o, 'b_glu': b_glu, 'conv_dw': conv_dw, 'b_dw': b_dw,
            'conv_ln_g': conv_ln_g, 'conv_ln_b': conv_ln_b, 'w_b_o': w_b_o, 'b_b_o': b_b_o,
            'b_gate': b_gate, 'w_out': w_out, 'ln1_g': ln1_g, 'ln1_b': ln1_b,
            'w_router': w_router, 'b_router': b_router, 'w_gu': w_gu, 'b_gu': b_gu,
            'w_down': w_down, 'b_down': b_down, 'ln2_g': ln2_g, 'ln2_b': ln2_b}


def reference(x, emb_ln_g, emb_ln_b, w_in, conv_qkv, a_log, dt_bias, dn_norm_g, w_a_o,
              b_glu, conv_dw, b_dw, conv_ln_g, conv_ln_b, w_b_o, b_b_o, b_gate, w_out,
              ln1_g, ln1_b, w_router, b_router, w_gu, b_gu, w_down, b_down, ln2_g, ln2_b):
    h = layer_norm(x, emb_ln_g, emb_ln_b)
    for l in range(DEPTH):
        u = h @ w_in[l]
        uq, uk, uv, uz, ub_f, ub_b, ua_f, ua_b, uglu, ugate = split_cols(u)
        y_a = deltanet_branch(uq, uk, uv, uz, ub_f, ub_b, ua_f, ua_b,
                              conv_qkv[l], a_log[l], dt_bias[l], dn_norm_g[l], w_a_o[l])
        y_b = conformer_branch(uglu, b_glu[l], conv_dw[l], b_dw[l],
                               conv_ln_g[l], conv_ln_b[l], w_b_o[l], b_b_o[l])
        g_a, g_b = jnp.split(jax.nn.sigmoid(ugate + b_gate[l]), 2, axis=-1)
        mix = (g_a * y_a + g_b * y_b) @ w_out[l]
        h = layer_norm(DN_ALPHA * h + mix, ln1_g[l], ln1_b[l])
        f = moe_ffn(h, w_router[l], b_router[l], w_gu[l], b_gu[l], w_down[l], b_down[l])
        h = layer_norm(DN_ALPHA * h + f, ln2_g[l], ln2_b[l])
    return h
```

```python
import functools

import jax
import jax.numpy as jnp
from jax import lax
from jax.experimental import pallas as pl
from jax.experimental.pallas import tpu as pltpu

F32 = jnp.float32
BF16 = jnp.bfloat16

D_MODEL = 1024
N_HEADS = 8
HEAD_DIM = 128
WIDTH_A = N_HEADS * HEAD_DIM
SHORT_CONV = 5
CHUNK = 64
WIDTH_B = D_MODEL
DW_CONV = 31
N_EXPERTS = 32
TOP_K = 4
D_FF = D_MODEL
SWIGLU_ALPHA = 1.702
SWIGLU_LIMIT = 7.0
DN_ALPHA = 2.0 ** 0.25
LN_EPS = 1e-5
RMS_EPS = 1e-6
L2_EPS = 1e-6
LANES = 128
NEG_BIG = -1e30

ROW_TILE = 512
DELTA_ROWS = 512
EXPERT_ROWS = 512
COMBINE_ROWS = 128
VMEM_LIMIT = 56 * 1024 * 1024


def _params(*sem):
    return pltpu.CompilerParams(dimension_semantics=sem, vmem_limit_bytes=VMEM_LIMIT)


def _layer_norm(x, g, b):
    mu = jnp.mean(x, axis=-1, keepdims=True)
    xc = x - mu
    var = jnp.mean(xc * xc, axis=-1, keepdims=True)
    return xc * lax.rsqrt(var + LN_EPS) * g + b


def _sigmoid(x):
    return 1.0 / (1.0 + jnp.exp(-x))


def _dot(a, b):
    return jnp.dot(a, b, preferred_element_type=F32)


def _dot_hi(a, b):
    return jnp.dot(a, b, preferred_element_type=F32, precision=lax.Precision.HIGHEST)


def _ln0_kernel(x_ref, g_ref, b_ref, o_ref):
    o_ref[...] = _layer_norm(x_ref[...], g_ref[...], b_ref[...]).astype(o_ref.dtype)


def _ln0(x2, g, b):
    t, d = x2.shape
    tm = min(ROW_TILE, t)
    return pl.pallas_call(
        _ln0_kernel,
        out_shape=jax.ShapeDtypeStruct((t, d), BF16),
        grid=(t // tm,),
        in_specs=[pl.BlockSpec((tm, d), lambda i: (i, 0)),
                  pl.BlockSpec((1, d), lambda i: (0, 0)),
                  pl.BlockSpec((1, d), lambda i: (0, 0))],
        out_specs=pl.BlockSpec((tm, d), lambda i: (i, 0)),
        compiler_params=_params("parallel"),
        name="ln0",
    )(x2, g.reshape(1, d), b.reshape(1, d))


def _proj_kernel(a_ref, w_ref, o_ref, *, tn):
    a = a_ref[...]
    for n0 in range(0, o_ref.shape[1], tn):
        o_ref[:, n0:n0 + tn] = _dot(a, w_ref[:, n0:n0 + tn]).astype(o_ref.dtype)


def _proj(h0, w):
    t, d = h0.shape
    n = w.shape[1]
    tm = min(ROW_TILE, t)
    return pl.pallas_call(
        functools.partial(_proj_kernel, tn=512),
        out_shape=jax.ShapeDtypeStruct((t, n), BF16),
        grid=(t // tm,),
        in_specs=[pl.BlockSpec((tm, d), lambda i: (i, 0)),
                  pl.BlockSpec((d, n), lambda i: (0, 0))],
        out_specs=pl.BlockSpec((tm, n), lambda i: (i, 0)),
        compiler_params=_params("parallel"),
        name="proj_qkvz",
    )(h0, w)


def _glu_kernel(a_ref, wa_ref, wb_ref, ba_ref, bb_ref, o_ref, *, tn):
    a = a_ref[...]
    for n0 in range(0, o_ref.shape[1], tn):
        lin = _dot(a, wa_ref[:, n0:n0 + tn]) + ba_ref[:, n0:n0 + tn]
        gate = _dot(a, wb_ref[:, n0:n0 + tn]) + bb_ref[:, n0:n0 + tn]
        o_ref[:, n0:n0 + tn] = (lin * _sigmoid(gate)).astype(o_ref.dtype)


def _proj_glu(h0, wa, wb, ba, bb):
    t, d = h0.shape
    n = wa.shape[1]
    tm = min(ROW_TILE, t)
    return pl.pallas_call(
        functools.partial(_glu_kernel, tn=512),
        out_shape=jax.ShapeDtypeStruct((t, n), BF16),
        grid=(t // tm,),
        in_specs=[pl.BlockSpec((tm, d), lambda i: (i, 0)),
                  pl.BlockSpec((d, n), lambda i: (0, 0)),
                  pl.BlockSpec((d, n), lambda i: (0, 0)),
                  pl.BlockSpec((1, n), lambda i: (0, 0)),
                  pl.BlockSpec((1, n), lambda i: (0, 0))],
        out_specs=pl.BlockSpec((tm, n), lambda i: (i, 0)),
        compiler_params=_params("parallel"),
        name="proj_glu",
    )(h0, wa, wb, ba.reshape(1, n), bb.reshape(1, n))


def _gate_kernel(a_ref, w_ref, b_ref, o_ref, *, tn):
    a = a_ref[...]
    for n0 in range(0, o_ref.shape[1], tn):
        o_ref[:, n0:n0 + tn] = _sigmoid(_dot(a, w_ref[:, n0:n0 + tn]) + b_ref[:, n0:n0 + tn]).astype(o_ref.dtype)


def _proj_gate(h0, w, b):
    t, d = h0.shape
    n = w.shape[1]
    tm = min(ROW_TILE, t)
    return pl.pallas_call(
        functools.partial(_gate_kernel, tn=512),
        out_shape=jax.ShapeDtypeStruct((t, n), BF16),
        grid=(t // tm,),
        in_specs=[pl.BlockSpec((tm, d), lambda i: (i, 0)),
                  pl.BlockSpec((d, n), lambda i: (0, 0)),
                  pl.BlockSpec((1, n), lambda i: (0, 0))],
        out_specs=pl.BlockSpec((tm, n), lambda i: (i, 0)),
        compiler_params=_params("parallel"),
        name="proj_gate",
    )(h0, w, b.reshape(1, n))


def _chunk_cumsum(x, reverse):
    rows = x.shape[0]
    pos = lax.broadcasted_iota(jnp.int32, x.shape, 0) % CHUNK
    s = 1
    while s < CHUNK:
        if reverse:
            shifted = pltpu.roll(x, rows - s, axis=0)
            x = x + jnp.where(pos < CHUNK - s, shifted, 0.0)
        else:
            shifted = pltpu.roll(x, s, axis=0)
            x = x + jnp.where(pos >= s, shifted, 0.0)
        s *= 2
    return x


def _dn_gates_kernel(a_ref, w_ref, alog_ref, dtb_ref, g_ref, gt_ref):
    u = _dot(a_ref[...], w_ref[...])
    lane = lax.broadcasted_iota(jnp.int32, u.shape, 1)
    beta = _sigmoid(u)
    x = u + dtb_ref[...]
    softplus = jnp.maximum(x, 0.0) + jnp.log(1.0 + jnp.exp(-jnp.abs(x)))
    log_a = -jnp.exp(alog_ref[...]) * softplus
    g_fwd = _chunk_cumsum(log_a, reverse=False)
    g_bwd = _chunk_cumsum(log_a, reverse=True)
    out = jnp.where(lane < 2 * N_HEADS, beta, jnp.where(lane < 3 * N_HEADS, g_fwd, g_bwd))
    g_ref[...] = out
    gt_ref[...] = out.T[2 * N_HEADS:4 * N_HEADS, :]


def _dn_gates(h0, w_small, a_log, dt_bias):
    t, d = h0.shape
    tm = min(ROW_TILE, t)
    pad = LANES - 4 * N_HEADS
    w = jnp.pad(w_small, ((0, 0), (0, pad)))
    alog = jnp.pad(a_log.reshape(1, 2 * N_HEADS), ((0, 0), (2 * N_HEADS, LANES - 4 * N_HEADS)))
    dtb = jnp.pad(dt_bias.reshape(1, 2 * N_HEADS), ((0, 0), (2 * N_HEADS, LANES - 4 * N_HEADS)))
    return pl.pallas_call(
        _dn_gates_kernel,
        out_shape=(jax.ShapeDtypeStruct((t, LANES), F32),
                   jax.ShapeDtypeStruct((2 * N_HEADS, t), F32)),
        grid=(t // tm,),
        in_specs=[pl.BlockSpec((tm, d), lambda i: (i, 0)),
                  pl.BlockSpec((d, LANES), lambda i: (0, 0)),
                  pl.BlockSpec((1, LANES), lambda i: (0, 0)),
                  pl.BlockSpec((1, LANES), lambda i: (0, 0))],
        out_specs=(pl.BlockSpec((tm, LANES), lambda i: (i, 0)),
                   pl.BlockSpec((2 * N_HEADS, tm), lambda i: (0, i))),
        compiler_params=_params("parallel"),
        name="dn_gates",
    )(h0, w, alog, dtb)


def _conv_rows(xp_ref, w, taps, base, r0, rows):
    acc = xp_ref[base + r0:base + r0 + rows, :] * w[0:1, :]
    for k in range(1, taps):
        acc = acc + xp_ref[base + k + r0:base + k + r0 + rows, :] * w[k:k + 1, :]
    return acc


def _fill_padded(xp_ref, x_ref, pad, seq):
    zeros = jnp.zeros((pad, xp_ref.shape[1]), F32)
    xp_ref[0:pad, :] = zeros
    xp_ref[pad + seq:pad + seq + pad, :] = zeros
    xp_ref[pad:pad + seq, :] = x_ref[...].astype(F32)


def _qkv_conv_kernel(u_ref, w_ref, o_ref, xp_ref, *, seq, rows):
    pad = 8
    j = pl.program_id(1)
    _fill_padded(xp_ref, u_ref, pad, seq)
    w = w_ref[...]
    is_q = j < N_HEADS
    is_qk = j < 2 * N_HEADS
    for r0 in range(0, seq, rows):
        y = _conv_rows(xp_ref, w, SHORT_CONV, pad - SHORT_CONV // 2, r0, rows)
        y = y * _sigmoid(y)
        inv = lax.rsqrt(jnp.sum(y * y, axis=-1, keepdims=True) + L2_EPS)
        scale = jnp.where(is_q, inv * (HEAD_DIM ** -0.5), jnp.where(is_qk, inv, 1.0))
        o_ref[r0:r0 + rows, :] = (y * scale).astype(o_ref.dtype)


def _qkv_conv(u_qkvz, conv_w, bsz, seq):
    t = bsz * seq
    ncol = 3 * N_HEADS
    rows = min(256, seq)
    return pl.pallas_call(
        functools.partial(_qkv_conv_kernel, seq=seq, rows=rows),
        out_shape=jax.ShapeDtypeStruct((t, 3 * WIDTH_A), BF16),
        grid=(bsz, ncol),
        in_specs=[pl.BlockSpec((seq, HEAD_DIM), lambda b, j: (b, j)),
                  pl.BlockSpec((SHORT_CONV, HEAD_DIM), lambda b, j: (0, j))],
        out_specs=pl.BlockSpec((seq, HEAD_DIM), lambda b, j: (b, j)),
        scratch_shapes=[pltpu.VMEM((seq + 16, HEAD_DIM), F32)],
        compiler_params=_params("parallel", "parallel"),
        name="qkv_conv",
    )(u_qkvz, conv_w)


def _dw_conv_kernel(x_ref, w_ref, b_ref, o_ref, xp_ref, *, seq, rows):
    pad = 16
    _fill_padded(xp_ref, x_ref, pad, seq)
    w = w_ref[...]
    for r0 in range(0, seq, rows):
        y = _conv_rows(xp_ref, w, DW_CONV, pad - DW_CONV // 2, r0, rows) + b_ref[...]
        o_ref[r0:r0 + rows, :] = y.astype(o_ref.dtype)


def _dw_conv(glu, conv_w, b_dw, bsz, seq):
    t = bsz * seq
    rows = min(256, seq)
    return pl.pallas_call(
        functools.partial(_dw_conv_kernel, seq=seq, rows=rows),
        out_shape=jax.ShapeDtypeStruct((t, WIDTH_B), BF16),
        grid=(bsz, WIDTH_B // LANES),
        in_specs=[pl.BlockSpec((seq, LANES), lambda b, j: (b, j)),
                  pl.BlockSpec((DW_CONV, LANES), lambda b, j: (0, j)),
                  pl.BlockSpec((1, LANES), lambda b, j: (0, j))],
        out_specs=pl.BlockSpec((seq, LANES), lambda b, j: (b, j)),
        scratch_shapes=[pltpu.VMEM((seq + 32, LANES), F32)],
        compiler_params=_params("parallel", "parallel"),
        name="dw_conv",
    )(glu, conv_w, b_dw.reshape(1, WIDTH_B))


def _unit_tri_inverse(lmat, eye):
    x = eye - lmat
    p = _dot_hi(lmat, lmat)
    s = 2
    while 2 * s < CHUNK:
        x = x + _dot_hi(x, p)
        p = _dot_hi(p, p)
        s *= 2
    return x + _dot_hi(x, p)


def _delta_kernel(q_ref, k_ref, v_ref, g_ref, gt_ref, o_ref, s_ref, *, reverse, nc):
    h = pl.program_id(1)

    @pl.when(pl.program_id(2) == 0)
    def _():
        s_ref[...] = jnp.zeros_like(s_ref)

    dir_off = N_HEADS if reverse else 0
    gall = g_ref[...]
    lane = lax.broadcasted_iota(jnp.int32, gall.shape, 1)
    beta_all = jnp.sum(jnp.where(lane == h + dir_off, gall, 0.0), axis=1, keepdims=True)
    gcol_all = jnp.sum(jnp.where(lane == 2 * N_HEADS + h + dir_off, gall, 0.0), axis=1, keepdims=True)
    grow_all = gt_ref[pl.ds(h + dir_off, 1), :]

    ri = lax.broadcasted_iota(jnp.int32, (CHUNK, CHUNK), 0)
    ci = lax.broadcasted_iota(jnp.int32, (CHUNK, CHUNK), 1)
    incl = (ri <= ci) if reverse else (ri >= ci)
    strict = (ri < ci) if reverse else (ri > ci)
    eye = (ri == ci).astype(F32)

    state = s_ref[...]
    order = range(nc - 1, -1, -1) if reverse else range(nc)
    for c in order:
        r0 = c * CHUNK
        k = k_ref[r0:r0 + CHUNK, :]
        kf = k.astype(F32)
        qf = q_ref[r0:r0 + CHUNK, :].astype(F32)
        vf = v_ref[r0:r0 + CHUNK, :].astype(F32)
        beta = beta_all[r0:r0 + CHUNK, :]
        gcol = gcol_all[r0:r0 + CHUNK, :]
        grow = grow_all[:, r0:r0 + CHUNK]
        glast = gcol[0:1, :] if reverse else gcol[CHUNK - 1:CHUNK, :]

        decay = jnp.exp(jnp.where(incl, gcol - grow, NEG_BIG))
        kb = kf * beta
        lhs = jnp.concatenate([kb, qf], axis=0).astype(BF16)
        kkqk = lax.dot_general(lhs, k, (((1,), (1,)), ((), ())), preferred_element_type=F32)
        lmat = jnp.where(strict, kkqk[:CHUNK, :] * decay, 0.0)
        qk = kkqk[CHUNK:, :] * decay
        tinv = _unit_tri_inverse(lmat, eye)

        eg = jnp.exp(gcol)
        rhs = jnp.concatenate([vf * beta, kb * eg], axis=1).astype(BF16)
        uw = _dot(tinv.astype(BF16), rhs)
        u = uw[:, :HEAD_DIM]
        w = uw[:, HEAD_DIM:]
        qd = qf * eg
        kt = kf * jnp.exp(glast - gcol)

        ws = _dot(jnp.concatenate([w, qd], axis=0).astype(BF16), state.astype(BF16))
        v_new = (u - ws[:CHUNK, :]).astype(BF16)
        o = ws[CHUNK:, :] + _dot(qk.astype(BF16), v_new)
        state = state * jnp.exp(glast) + _dot(kt.T.astype(BF16), v_new)
        o_ref[r0:r0 + CHUNK, :] = o.astype(o_ref.dtype)
    s_ref[...] = state


def _delta_rule(qkv, g, gt, bsz, seq, reverse):
    t = bsz * seq
    rows = min(DELTA_ROWS, seq)
    nblk = seq // rows

    def rb(b, i):
        return b * nblk + ((nblk - 1 - i) if reverse else i)

    return pl.pallas_call(
        functools.partial(_delta_kernel, reverse=reverse, nc=rows // CHUNK),
        out_shape=jax.ShapeDtypeStruct((t, WIDTH_A), F32),
        grid=(bsz, N_HEADS, nblk),
        in_specs=[pl.BlockSpec((rows, HEAD_DIM), lambda b, h, i: (rb(b, i), h)),
                  pl.BlockSpec((rows, HEAD_DIM), lambda b, h, i: (rb(b, i), N_HEADS + h)),
                  pl.BlockSpec((rows, HEAD_DIM), lambda b, h, i: (rb(b, i), 2 * N_HEADS + h)),
                  pl.BlockSpec((rows, LANES), lambda b, h, i: (rb(b, i), 0)),
                  pl.BlockSpec((2 * N_HEADS, rows), lambda b, h, i: (0, rb(b, i)))],
        out_specs=pl.BlockSpec((rows, HEAD_DIM), lambda b, h, i: (rb(b, i), h)),
        scratch_shapes=[pltpu.VMEM((HEAD_DIM, HEAD_DIM), F32)],
        compiler_params=_params("parallel", "parallel", "arbitrary"),
        name="delta_bwd" if reverse else "delta_fwd",
    )(qkv, qkv, qkv, g, gt)


def _mix_kernel(of_ref, ob_ref, z_ref, yc_ref, gate_ref, x_ref,
                eg_ref, eb_ref, ng_ref, wao_ref, cg_ref, cb_ref, wbo_ref, bbo_ref,
                wout_ref, l1g_ref, l1b_ref, wr_ref, br_ref,
                h1_ref, logit_ref):
    o = of_ref[...] + ob_ref[...]
    z = z_ref[...].astype(F32)
    ng = ng_ref[...]
    parts = []
    for hh in range(N_HEADS):
        sl = slice(hh * HEAD_DIM, (hh + 1) * HEAD_DIM)
        oh = o[:, sl]
        zh = z[:, sl]
        inv = lax.rsqrt(jnp.mean(oh * oh, axis=-1, keepdims=True) + RMS_EPS)
        parts.append((oh * inv * ng * (zh * _sigmoid(zh))).astype(BF16))
    y_a = _dot(jnp.concatenate(parts, axis=1), wao_ref[...])

    yc = _layer_norm(yc_ref[...].astype(F32), cg_ref[...], cb_ref[...])
    y_b = _dot((yc * _sigmoid(yc)).astype(BF16), wbo_ref[...]) + bbo_ref[...]

    gates = gate_ref[...].astype(F32)
    mixed = gates[:, :D_MODEL] * y_a + gates[:, D_MODEL:] * y_b
    mix = _dot(mixed.astype(BF16), wout_ref[...])

    h0 = _layer_norm(x_ref[...], eg_ref[...], eb_ref[...])
    h1 = _layer_norm(DN_ALPHA * h0 + mix, l1g_ref[...], l1b_ref[...])
    h1_ref[...] = h1
    logit_ref[...] = _dot_hi(h1, wr_ref[...]) + br_ref[...]


def _mix(o_f, o_b, u_qkvz, yc, gates, x2, emb_g, emb_b, norm_g, w_a_o, cg, cb, w_b_o, b_b_o,
         w_out, l1g, l1b, w_router, b_router):
    t, d = x2.shape
    tm = min(256, t)
    row = lambda i: (i, 0)
    const = lambda i: (0, 0)
    wr = jnp.pad(w_router, ((0, 0), (0, LANES - N_EXPERTS)))
    br = jnp.pad(b_router.reshape(1, N_EXPERTS), ((0, 0), (0, LANES - N_EXPERTS)), constant_values=NEG_BIG)
    vec = lambda a: a.reshape(1, -1)
    return pl.pallas_call(
        _mix_kernel,
        out_shape=(jax.ShapeDtypeStruct((t, d), F32), jax.ShapeDtypeStruct((t, LANES), F32)),
        grid=(t // tm,),
        in_specs=[pl.BlockSpec((tm, d), row), pl.BlockSpec((tm, d), row),
                  pl.BlockSpec((tm, d), lambda i: (i, 3)),
                  pl.BlockSpec((tm, d), row), pl.BlockSpec((tm, 2 * d), row), pl.BlockSpec((tm, d), row),
                  pl.BlockSpec((1, d), const), pl.BlockSpec((1, d), const),
                  pl.BlockSpec((1, HEAD_DIM), const), pl.BlockSpec((d, d), const),
                  pl.BlockSpec((1, d), const), pl.BlockSpec((1, d), const),
                  pl.BlockSpec((d, d), const), pl.BlockSpec((1, d), const),
                  pl.BlockSpec((d, d), const), pl.BlockSpec((1, d), const), pl.BlockSpec((1, d), const),
                  pl.BlockSpec((d, LANES), const), pl.BlockSpec((1, LANES), const)],
        out_specs=(pl.BlockSpec((tm, d), row), pl.BlockSpec((tm, LANES), row)),
        compiler_params=_params("parallel"),
        name="mix",
    )(o_f, o_b, u_qkvz, yc, gates, x2, vec(emb_g), vec(emb_b), vec(norm_g), w_a_o, vec(cg), vec(cb),
      w_b_o, vec(b_b_o), w_out, vec(l1g), vec(l1b), wr, br)


def _gather_rows(idx_ref, src_hbm, buf_ref, sem, n):
    def issue(r, carry):
        pltpu.make_async_copy(src_hbm.at[pl.ds(idx_ref[0, 0, r], 1), :],
                              buf_ref.at[pl.ds(r, 1), :], sem).start()
        return carry

    lax.fori_loop(0, n, issue, 0)

    def drain(r, carry):
        pltpu.make_async_copy(src_hbm.at[pl.ds(0, 1), :], buf_ref.at[pl.ds(r, 1), :], sem).wait()
        return carry

    lax.fori_loop(0, n, drain, 0)


def _dispatch_kernel(idx_ref, h_hbm, o_ref, buf_ref, sem, *, n):
    _gather_rows(idx_ref, h_hbm, buf_ref, sem, n)
    o_ref[...] = buf_ref[...].astype(o_ref.dtype)


def _dispatch(h1, slot_tok, n_blocks):
    d = h1.shape[1]
    bm = EXPERT_ROWS
    return pl.pallas_call(
        functools.partial(_dispatch_kernel, n=bm),
        out_shape=jax.ShapeDtypeStruct((n_blocks * bm, d), BF16),
        grid=(n_blocks,),
        in_specs=[pl.BlockSpec((1, 1, bm), lambda i: (i, 0, 0), memory_space=pltpu.SMEM),
                  pl.BlockSpec(memory_space=pl.ANY)],
        out_specs=pl.BlockSpec((bm, d), lambda i: (i, 0)),
        scratch_shapes=[pltpu.VMEM((bm, d), F32), pltpu.SemaphoreType.DMA(())],
        compiler_params=_params("arbitrary"),
        name="moe_dispatch",
    )(slot_tok.reshape(n_blocks, 1, bm), h1)


def _expert_kernel(be_ref, nb_ref, x_ref, wgu_ref, bgu_ref, wd_ref, bd_ref, o_ref):
    @pl.when(pl.program_id(0) < nb_ref[0])
    def _():
        hgu = _dot(x_ref[...], wgu_ref[0]) + bgu_ref[0]
        glu = jnp.minimum(hgu[:, :D_FF], SWIGLU_LIMIT)
        lin = jnp.clip(hgu[:, D_FF:], -SWIGLU_LIMIT, SWIGLU_LIMIT)
        act = glu * _sigmoid(SWIGLU_ALPHA * glu) * (lin + 1.0)
        o_ref[...] = _dot(act.astype(BF16), wd_ref[0]) + bd_ref[0]

    @pl.when(pl.program_id(0) >= nb_ref[0])
    def _():
        o_ref[...] = jnp.zeros_like(o_ref)


def _experts(xs, block_exp, n_used, w_gu, b_gu, w_down, b_down, n_blocks):
    d = xs.shape[1]
    bm = EXPERT_ROWS
    grid_spec = pltpu.PrefetchScalarGridSpec(
        num_scalar_prefetch=2,
        grid=(n_blocks,),
        in_specs=[pl.BlockSpec((bm, d), lambda i, be, nb: (i, 0)),
                  pl.BlockSpec((1, d, 2 * D_FF), lambda i, be, nb: (be[i], 0, 0)),
                  pl.BlockSpec((1, 1, 2 * D_FF), lambda i, be, nb: (be[i], 0, 0)),
                  pl.BlockSpec((1, D_FF, d), lambda i, be, nb: (be[i], 0, 0)),
                  pl.BlockSpec((1, 1, d), lambda i, be, nb: (be[i], 0, 0))],
        out_specs=pl.BlockSpec((bm, d), lambda i, be, nb: (i, 0)),
    )
    return pl.pallas_call(
        _expert_kernel,
        out_shape=jax.ShapeDtypeStruct((n_blocks * bm, d), F32),
        grid_spec=grid_spec,
        compiler_params=_params("arbitrary"),
        name="moe_experts",
    )(block_exp, n_used, xs, w_gu, b_gu.reshape(N_EXPERTS, 1, 2 * D_FF), w_down,
      b_down.reshape(N_EXPERTS, 1, d))


def _combine_kernel(idx_ref, ys_hbm, gate_ref, h_ref, g_ref, b_ref, o_ref, buf_ref, sem, *, tm):
    _gather_rows(idx_ref, ys_hbm, buf_ref, sem, TOP_K * tm)
    gate = gate_ref[...]
    f = gate[:, 0:1] * buf_ref[0:tm, :]
    for k in range(1, TOP_K):
        f = f + gate[:, k:k + 1] * buf_ref[k * tm:(k + 1) * tm, :]
    o_ref[...] = _layer_norm(DN_ALPHA * h_ref[...] + f, g_ref[...], b_ref[...])


def _combine(ys, pos, gate, h1, ln_g, ln_b):
    t, d = h1.shape
    tm = min(COMBINE_ROWS, t)
    nt = t // tm
    idx = pos.reshape(nt, tm, TOP_K).transpose(0, 2, 1).reshape(nt, 1, TOP_K * tm)
    return pl.pallas_call(
        functools.partial(_combine_kernel, tm=tm),
        out_shape=jax.ShapeDtypeStruct((t, d), F32),
        grid=(nt,),
        in_specs=[pl.BlockSpec((1, 1, TOP_K * tm), lambda i: (i, 0, 0), memory_space=pltpu.SMEM),
                  pl.BlockSpec(memory_space=pl.ANY),
                  pl.BlockSpec((tm, TOP_K), lambda i: (i, 0)),
                  pl.BlockSpec((tm, d), lambda i: (i, 0)),
                  pl.BlockSpec((1, d), lambda i: (0, 0)),
                  pl.BlockSpec((1, d), lambda i: (0, 0))],
        out_specs=pl.BlockSpec((tm, d), lambda i: (i, 0)),
        scratch_shapes=[pltpu.VMEM((TOP_K * tm, d), F32), pltpu.SemaphoreType.DMA(())],
        compiler_params=_params("arbitrary"),
        name="moe_combine",
    )(idx, ys, gate, h1, ln_g.reshape(1, d), ln_b.reshape(1, d))


def _route(logits, t):
    bm = EXPERT_ROWS
    top_val, top_idx = lax.top_k(logits[:, :N_EXPERTS], TOP_K)
    gate = jax.nn.softmax(top_val, axis=-1)
    e_flat = top_idx.reshape(-1).astype(jnp.int32)
    n_assign = t * TOP_K
    onehot = (e_flat[:, None] == jnp.arange(N_EXPERTS, dtype=jnp.int32)[None, :]).astype(jnp.int32)
    csum = jnp.cumsum(onehot, axis=0)
    counts = csum[-1]
    rank = jnp.sum(onehot * csum, axis=1) - 1
    padded = (counts + bm - 1) // bm * bm
    pad_end = jnp.cumsum(padded)
    pad_start = pad_end - padded
    pos = pad_start[e_flat] + rank
    n_blocks = -(-(n_assign + N_EXPERTS * (bm - 1)) // bm)
    tok_flat = jnp.arange(n_assign, dtype=jnp.int32) // TOP_K
    slot_tok = jnp.zeros((n_blocks * bm,), jnp.int32).at[pos].set(tok_flat)
    block_start = jnp.arange(n_blocks, dtype=jnp.int32) * bm
    block_exp = jnp.minimum(jnp.searchsorted(pad_end, block_start, side='right'),
                            N_EXPERTS - 1).astype(jnp.int32)
    n_used = (pad_end[-1] // bm).astype(jnp.int32).reshape(1)
    return gate, pos.reshape(t, TOP_K).astype(jnp.int32), slot_tok, block_exp, n_used, n_blocks


def kernel(x, emb_ln_g, emb_ln_b, w_in, conv_qkv, a_log, dt_bias, dn_norm_g, w_a_o, b_glu, conv_dw, b_dw, conv_ln_g, conv_ln_b, w_b_o, b_b_o, b_gate, w_out, ln1_g, ln1_b, w_router, b_router, w_gu, b_gu, w_down, b_down, ln2_g, ln2_b):
    bsz, seq, d = x.shape
    t = bsz * seq
    x2 = x.reshape(t, d)
    wi = w_in[0]
    c0 = 4 * WIDTH_A
    c1 = c0 + 4 * N_HEADS
    c2 = c1 + 2 * WIDTH_B

    h0 = _ln0(x2, emb_ln_g, emb_ln_b)
    u_qkvz = _proj(h0, wi[:, :c0].astype(BF16))
    g, gt = _dn_gates(h0, wi[:, c0:c1].astype(BF16), a_log[0], dt_bias[0])
    glu = _proj_glu(h0, wi[:, c1:c1 + WIDTH_B].astype(BF16), wi[:, c1 + WIDTH_B:c2].astype(BF16),
                    b_glu[0, :WIDTH_B], b_glu[0, WIDTH_B:])
    gates = _proj_gate(h0, wi[:, c2:].astype(BF16), b_gate[0])

    qkv = _qkv_conv(u_qkvz, conv_qkv[0], bsz, seq)
    o_f = _delta_rule(qkv, g, gt, bsz, seq, reverse=False)
    o_b = _delta_rule(qkv, g, gt, bsz, seq, reverse=True)
    yc = _dw_conv(glu, conv_dw[0], b_dw[0], bsz, seq)

    h1, logits = _mix(o_f, o_b, u_qkvz, yc, gates, x2, emb_ln_g, emb_ln_b, dn_norm_g[0],
                      w_a_o[0].astype(BF16), conv_ln_g[0], conv_ln_b[0], w_b_o[0].astype(BF16), b_b_o[0],
                      w_out[0].astype(BF16), ln1_g[0], ln1_b[0], w_router[0], b_router[0])

    gate, pos, slot_tok, block_exp, n_used, n_blocks = _route(logits, t)
    xs = _dispatch(h1, slot_tok, n_blocks)
    ys = _experts(xs, block_exp, n_used, w_gu[0].astype(BF16), b_gu[0], w_down[0].astype(BF16), b_down[0],
                  n_blocks)
    out = _combine(ys, pos, gate, h1, ln2_g[0], ln2_b[0])
    return out.reshape(bsz, seq, d)
```

```python
import functools

import jax
import jax.numpy as jnp
from jax import lax
from jax.experimental import pallas as pl
from jax.experimental.pallas import tpu as pltpu

F32 = jnp.float32
BF16 = jnp.bfloat16

D_MODEL = 1024
N_HEADS = 8
HEAD_DIM = 128
WIDTH_A = N_HEADS * HEAD_DIM
SHORT_CONV = 5
CHUNK = 64
WIDTH_B = D_MODEL
DW_CONV = 31
N_EXPERTS = 32
TOP_K = 4
D_FF = D_MODEL
SWIGLU_ALPHA = 1.702
SWIGLU_LIMIT = 7.0
DN_ALPHA = 2.0 ** 0.25
LN_EPS = 1e-5
RMS_EPS = 1e-6
L2_EPS = 1e-6
LANES = 128
NEG_BIG = -1e30

ROW_TILE = 512
DELTA_ROWS = 256
EXPERT_ROWS = 512
COMBINE_ROWS = 128
GATHER_UNROLL = 8
VMEM_LIMIT = 56 * 1024 * 1024


def _params(*sem):
    return pltpu.CompilerParams(dimension_semantics=sem, vmem_limit_bytes=VMEM_LIMIT)


def _layer_norm(x, g, b):
    mu = jnp.mean(x, axis=-1, keepdims=True)
    xc = x - mu
    var = jnp.mean(xc * xc, axis=-1, keepdims=True)
    return xc * lax.rsqrt(var + LN_EPS) * g + b


def _sigmoid(x):
    return 1.0 / (1.0 + jnp.exp(-x))


def _dot(a, b):
    return jnp.dot(a, b, preferred_element_type=F32)


def _dot_hi(a, b):
    return jnp.dot(a, b, preferred_element_type=F32, precision=lax.Precision.HIGHEST)


def _ln0_kernel(x_ref, g_ref, b_ref, o_ref):
    o_ref[...] = _layer_norm(x_ref[...], g_ref[...], b_ref[...]).astype(o_ref.dtype)


def _ln0(x2, g, b):
    t, d = x2.shape
    tm = min(ROW_TILE, t)
    return pl.pallas_call(
        _ln0_kernel,
        out_shape=jax.ShapeDtypeStruct((t, d), BF16),
        grid=(t // tm,),
        in_specs=[pl.BlockSpec((tm, d), lambda i: (i, 0)),
                  pl.BlockSpec((1, d), lambda i: (0, 0)),
                  pl.BlockSpec((1, d), lambda i: (0, 0))],
        out_specs=pl.BlockSpec((tm, d), lambda i: (i, 0)),
        compiler_params=_params("parallel"),
        name="ln0",
    )(x2, g.reshape(1, d), b.reshape(1, d))


def _proj_kernel(a_ref, w_ref, o_ref, *, tn):
    a = a_ref[...]
    for n0 in range(0, o_ref.shape[1], tn):
        o_ref[:, n0:n0 + tn] = _dot(a, w_ref[:, n0:n0 + tn]).astype(o_ref.dtype)


def _proj(h0, w):
    t, d = h0.shape
    n = w.shape[1]
    tm = min(ROW_TILE, t)
    return pl.pallas_call(
        functools.partial(_proj_kernel, tn=512),
        out_shape=jax.ShapeDtypeStruct((t, n), BF16),
        grid=(t // tm,),
        in_specs=[pl.BlockSpec((tm, d), lambda i: (i, 0)),
                  pl.BlockSpec((d, n), lambda i: (0, 0))],
        out_specs=pl.BlockSpec((tm, n), lambda i: (i, 0)),
        compiler_params=_params("parallel"),
        name="proj_qkvz",
    )(h0, w)


def _glu_kernel(a_ref, wa_ref, wb_ref, ba_ref, bb_ref, o_ref, *, tn):
    a = a_ref[...]
    for n0 in range(0, o_ref.shape[1], tn):
        lin = _dot(a, wa_ref[:, n0:n0 + tn]) + ba_ref[:, n0:n0 + tn]
        gate = _dot(a, wb_ref[:, n0:n0 + tn]) + bb_ref[:, n0:n0 + tn]
        o_ref[:, n0:n0 + tn] = (lin * _sigmoid(gate)).astype(o_ref.dtype)


def _proj_glu(h0, wa, wb, ba, bb):
    t, d = h0.shape
    n = wa.shape[1]
    tm = min(ROW_TILE, t)
    return pl.pallas_call(
        functools.partial(_glu_kernel, tn=512),
        out_shape=jax.ShapeDtypeStruct((t, n), BF16),
        grid=(t // tm,),
        in_specs=[pl.BlockSpec((tm, d), lambda i: (i, 0)),
                  pl.BlockSpec((d, n), lambda i: (0, 0)),
                  pl.BlockSpec((d, n), lambda i: (0, 0)),
                  pl.BlockSpec((1, n), lambda i: (0, 0)),
                  pl.BlockSpec((1, n), lambda i: (0, 0))],
        out_specs=pl.BlockSpec((tm, n), lambda i: (i, 0)),
        compiler_params=_params("parallel"),
        name="proj_glu",
    )(h0, wa, wb, ba.reshape(1, n), bb.reshape(1, n))


def _gate_kernel(a_ref, w_ref, b_ref, o_ref, *, tn):
    a = a_ref[...]
    for n0 in range(0, o_ref.shape[1], tn):
        o_ref[:, n0:n0 + tn] = _sigmoid(_dot(a, w_ref[:, n0:n0 + tn]) + b_ref[:, n0:n0 + tn]).astype(o_ref.dtype)


def _proj_gate(h0, w, b):
    t, d = h0.shape
    n = w.shape[1]
    tm = min(ROW_TILE, t)
    return pl.pallas_call(
        functools.partial(_gate_kernel, tn=512),
        out_shape=jax.ShapeDtypeStruct((t, n), BF16),
        grid=(t // tm,),
        in_specs=[pl.BlockSpec((tm, d), lambda i: (i, 0)),
                  pl.BlockSpec((d, n), lambda i: (0, 0)),
                  pl.BlockSpec((1, n), lambda i: (0, 0))],
        out_specs=pl.BlockSpec((tm, n), lambda i: (i, 0)),
        compiler_params=_params("parallel"),
        name="proj_gate",
    )(h0, w, b.reshape(1, n))


def _chunk_cumsum(x, reverse):
    rows = x.shape[0]
    pos = lax.broadcasted_iota(jnp.int32, x.shape, 0) % CHUNK
    s = 1
    while s < CHUNK:
        if reverse:
            shifted = pltpu.roll(x, rows - s, axis=0)
            x = x + jnp.where(pos < CHUNK - s, shifted, 0.0)
        else:
            shifted = pltpu.roll(x, s, axis=0)
            x = x + jnp.where(pos >= s, shifted, 0.0)
        s *= 2
    return x


def _dn_gates_kernel(a_ref, w_ref, alog_ref, dtb_ref, g_ref, gt_ref):
    u = _dot(a_ref[...], w_ref[...])
    lane = lax.broadcasted_iota(jnp.int32, u.shape, 1)
    beta = _sigmoid(u)
    x = u + dtb_ref[...]
    softplus = jnp.maximum(x, 0.0) + jnp.log(1.0 + jnp.exp(-jnp.abs(x)))
    log_a = -jnp.exp(alog_ref[...]) * softplus
    g_fwd = _chunk_cumsum(log_a, reverse=False)
    g_bwd = _chunk_cumsum(log_a, reverse=True)
    out = jnp.where(lane < 2 * N_HEADS, beta, jnp.where(lane < 3 * N_HEADS, g_fwd, g_bwd))
    g_ref[...] = out
    gt_ref[...] = out.T[2 * N_HEADS:4 * N_HEADS, :]


def _dn_gates(h0, w_small, a_log, dt_bias):
    t, d = h0.shape
    tm = min(ROW_TILE, t)
    pad = LANES - 4 * N_HEADS
    w = jnp.pad(w_small, ((0, 0), (0, pad)))
    alog = jnp.pad(a_log.reshape(1, 2 * N_HEADS), ((0, 0), (2 * N_HEADS, LANES - 4 * N_HEADS)))
    dtb = jnp.pad(dt_bias.reshape(1, 2 * N_HEADS), ((0, 0), (2 * N_HEADS, LANES - 4 * N_HEADS)))
    return pl.pallas_call(
        _dn_gates_kernel,
        out_shape=(jax.ShapeDtypeStruct((t, LANES), F32),
                   jax.ShapeDtypeStruct((2 * N_HEADS, t), F32)),
        grid=(t // tm,),
        in_specs=[pl.BlockSpec((tm, d), lambda i: (i, 0)),
                  pl.BlockSpec((d, LANES), lambda i: (0, 0)),
                  pl.BlockSpec((1, LANES), lambda i: (0, 0)),
                  pl.BlockSpec((1, LANES), lambda i: (0, 0))],
        out_specs=(pl.BlockSpec((tm, LANES), lambda i: (i, 0)),
                   pl.BlockSpec((2 * N_HEADS, tm), lambda i: (0, i))),
        compiler_params=_params("parallel"),
        name="dn_gates",
    )(h0, w, alog, dtb)


def _conv_rows(xp_ref, w, taps, base, r0, rows):
    acc = xp_ref[base + r0:base + r0 + rows, :] * w[0:1, :]
    for k in range(1, taps):
        acc = acc + xp_ref[base + k + r0:base + k + r0 + rows, :] * w[k:k + 1, :]
    return acc


def _fill_padded(xp_ref, x_ref, pad, seq):
    zeros = jnp.zeros((pad, xp_ref.shape[1]), F32)
    xp_ref[0:pad, :] = zeros
    xp_ref[pad + seq:pad + seq + pad, :] = zeros
    xp_ref[pad:pad + seq, :] = x_ref[...].astype(F32)


def _qkv_conv_kernel(u_ref, w_ref, o_ref, xp_ref, *, seq, rows):
    pad = 8
    j = pl.program_id(1)
    _fill_padded(xp_ref, u_ref, pad, seq)
    w = w_ref[...]
    is_q = j < N_HEADS
    is_qk = j < 2 * N_HEADS
    for r0 in range(0, seq, rows):
        y = _conv_rows(xp_ref, w, SHORT_CONV, pad - SHORT_CONV // 2, r0, rows)
        y = y * _sigmoid(y)
        inv = lax.rsqrt(jnp.sum(y * y, axis=-1, keepdims=True) + L2_EPS)
        scale = jnp.where(is_q, inv * (HEAD_DIM ** -0.5), jnp.where(is_qk, inv, 1.0))
        o_ref[r0:r0 + rows, :] = (y * scale).astype(o_ref.dtype)


def _qkv_conv(u_qkvz, conv_w, bsz, seq):
    t = bsz * seq
    ncol = 3 * N_HEADS
    rows = min(256, seq)
    return pl.pallas_call(
        functools.partial(_qkv_conv_kernel, seq=seq, rows=rows),
        out_shape=jax.ShapeDtypeStruct((t, 3 * WIDTH_A), BF16),
        grid=(bsz, ncol),
        in_specs=[pl.BlockSpec((seq, HEAD_DIM), lambda b, j: (b, j)),
                  pl.BlockSpec((SHORT_CONV, HEAD_DIM), lambda b, j: (0, j))],
        out_specs=pl.BlockSpec((seq, HEAD_DIM), lambda b, j: (b, j)),
        scratch_shapes=[pltpu.VMEM((seq + 16, HEAD_DIM), F32)],
        compiler_params=_params("parallel", "parallel"),
        name="qkv_conv",
    )(u_qkvz, conv_w)


def _dw_conv_kernel(x_ref, w_ref, b_ref, o_ref, xp_ref, *, seq, rows):
    pad = 16
    _fill_padded(xp_ref, x_ref, pad, seq)
    w = w_ref[...]
    for r0 in range(0, seq, rows):
        y = _conv_rows(xp_ref, w, DW_CONV, pad - DW_CONV // 2, r0, rows) + b_ref[...]
        o_ref[r0:r0 + rows, :] = y.astype(o_ref.dtype)


def _dw_conv(glu, conv_w, b_dw, bsz, seq):
    t = bsz * seq
    rows = min(256, seq)
    return pl.pallas_call(
        functools.partial(_dw_conv_kernel, seq=seq, rows=rows),
        out_shape=jax.ShapeDtypeStruct((t, WIDTH_B), BF16),
        grid=(bsz, WIDTH_B // LANES),
        in_specs=[pl.BlockSpec((seq, LANES), lambda b, j: (b, j)),
                  pl.BlockSpec((DW_CONV, LANES), lambda b, j: (0, j)),
                  pl.BlockSpec((1, LANES), lambda b, j: (0, j))],
        out_specs=pl.BlockSpec((seq, LANES), lambda b, j: (b, j)),
        scratch_shapes=[pltpu.VMEM((seq + 32, LANES), F32)],
        compiler_params=_params("parallel", "parallel"),
        name="dw_conv",
    )(glu, conv_w, b_dw.reshape(1, WIDTH_B))


def _bmm(a, b):
    return lax.dot_general(a, b, (((2,), (1,)), ((0,), (0,))), preferred_element_type=F32)


def _bmm_nt(a, b):
    return lax.dot_general(a, b, (((2,), (2,)), ((0,), (0,))), preferred_element_type=F32)


def _bmm_tn(a, b):
    return lax.dot_general(a, b, (((1,), (1,)), ((0,), (0,))), preferred_element_type=F32)


def _unit_tri_inverse(lmat, eye):
    lb = lmat.astype(BF16)
    x = eye - lmat
    p = _bmm(lb, lb)
    s = 2
    while 2 * s < CHUNK:
        pb = p.astype(BF16)
        xp = _bmm(jnp.concatenate([x.astype(BF16), pb], axis=1), pb)
        x = x + xp[:, :CHUNK]
        p = xp[:, CHUNK:]
        s *= 2
    return x + _bmm(x.astype(BF16), p.astype(BF16))


def _delta_kernel(q_ref, k_ref, v_ref, g_ref, gt_ref, o_ref, s_ref, *, reverse, nc):
    @pl.when(pl.program_id(1) == 0)
    def _():
        s_ref[...] = jnp.zeros_like(s_ref)

    dir_off = N_HEADS if reverse else 0
    ri = lax.broadcasted_iota(jnp.int32, (CHUNK, CHUNK), 0)
    ci = lax.broadcasted_iota(jnp.int32, (CHUNK, CHUNK), 1)
    incl = (ri <= ci) if reverse else (ri >= ci)
    strict = (ri < ci) if reverse else (ri > ci)
    eye = (ri == ci).astype(F32)

    def heads(ref, rows):
        return jnp.stack([ref[rows, hh * HEAD_DIM:(hh + 1) * HEAD_DIM] for hh in range(N_HEADS)])

    order = range(nc - 1, -1, -1) if reverse else range(nc)
    for c in order:
        rows = slice(c * CHUNK, (c + 1) * CHUNK)
        k = heads(k_ref, rows)
        kf = k.astype(F32)
        qf = heads(q_ref, rows).astype(F32)
        vf = heads(v_ref, rows).astype(F32)
        gblk = g_ref[rows, :]
        beta = jnp.stack([gblk[:, dir_off + hh:dir_off + hh + 1] for hh in range(N_HEADS)])
        gcol = jnp.stack([gblk[:, 2 * N_HEADS + dir_off + hh:2 * N_HEADS + dir_off + hh + 1]
                          for hh in range(N_HEADS)])
        grow = jnp.stack([gt_ref[dir_off + hh:dir_off + hh + 1, rows] for hh in range(N_HEADS)])
        glast = gcol[:, 0:1, :] if reverse else gcol[:, CHUNK - 1:CHUNK, :]

        decay = jnp.exp(jnp.where(incl, gcol - grow, NEG_BIG))
        kb = kf * beta
        kkqk = _bmm_nt(jnp.concatenate([kb, qf], axis=1).astype(BF16), k)
        lmat = jnp.where(strict, kkqk[:, :CHUNK, :] * decay, 0.0)
        qk = kkqk[:, CHUNK:, :] * decay
        tinv = _unit_tri_inverse(lmat, eye)

        eg = jnp.exp(gcol)
        rhs = jnp.concatenate([vf * beta, kb * eg], axis=2).astype(BF16)
        uw = _bmm(tinv.astype(BF16), rhs)
        u = uw[:, :, :HEAD_DIM]
        w = uw[:, :, HEAD_DIM:]
        qd = qf * eg
        kt = kf * jnp.exp(glast - gcol)

        state = s_ref[...]
        ws = _bmm(jnp.concatenate([w, qd], axis=1).astype(BF16), state.astype(BF16))
        v_new = (u - ws[:, :CHUNK, :]).astype(BF16)
        o = ws[:, CHUNK:, :] + _bmm(qk.astype(BF16), v_new)
        s_ref[...] = state * jnp.exp(glast) + _bmm_tn(kt.astype(BF16), v_new)
        for hh in range(N_HEADS):
            o_ref[rows, hh * HEAD_DIM:(hh + 1) * HEAD_DIM] = o[hh].astype(o_ref.dtype)


def _delta_rule(qkv, g, gt, bsz, seq, reverse):
    t = bsz * seq
    rows = min(DELTA_ROWS, seq)
    nblk = seq // rows

    def rb(b, i):
        return b * nblk + ((nblk - 1 - i) if reverse else i)

    return pl.pallas_call(
        functools.partial(_delta_kernel, reverse=reverse, nc=rows // CHUNK),
        out_shape=jax.ShapeDtypeStruct((t, WIDTH_A), F32),
        grid=(bsz, nblk),
        in_specs=[pl.BlockSpec((rows, WIDTH_A), lambda b, i: (rb(b, i), 0)),
                  pl.BlockSpec((rows, WIDTH_A), lambda b, i: (rb(b, i), 1)),
                  pl.BlockSpec((rows, WIDTH_A), lambda b, i: (rb(b, i), 2)),
                  pl.BlockSpec((rows, LANES), lambda b, i: (rb(b, i), 0)),
                  pl.BlockSpec((2 * N_HEADS, rows), lambda b, i: (0, rb(b, i)))],
        out_specs=pl.BlockSpec((rows, WIDTH_A), lambda b, i: (rb(b, i), 0)),
        scratch_shapes=[pltpu.VMEM((N_HEADS, HEAD_DIM, HEAD_DIM), F32)],
        compiler_params=_params("parallel", "arbitrary"),
        name="delta_bwd" if reverse else "delta_fwd",
    )(qkv, qkv, qkv, g, gt)


def _mix_kernel(of_ref, ob_ref, z_ref, yc_ref, gate_ref, x_ref,
                eg_ref, eb_ref, ng_ref, wao_ref, cg_ref, cb_ref, wbo_ref, bbo_ref,
                wout_ref, l1g_ref, l1b_ref, wr_ref, br_ref,
                h1_ref, logit_ref):
    o = of_ref[...] + ob_ref[...]
    z = z_ref[...].astype(F32)
    ng = ng_ref[...]
    parts = []
    for hh in range(N_HEADS):
        sl = slice(hh * HEAD_DIM, (hh + 1) * HEAD_DIM)
        oh = o[:, sl]
        zh = z[:, sl]
        inv = lax.rsqrt(jnp.mean(oh * oh, axis=-1, keepdims=True) + RMS_EPS)
        parts.append((oh * inv * ng * (zh * _sigmoid(zh))).astype(BF16))
    y_a = _dot(jnp.concatenate(parts, axis=1), wao_ref[...])

    yc = _layer_norm(yc_ref[...].astype(F32), cg_ref[...], cb_ref[...])
    y_b = _dot((yc * _sigmoid(yc)).astype(BF16), wbo_ref[...]) + bbo_ref[...]

    gates = gate_ref[...].astype(F32)
    mixed = gates[:, :D_MODEL] * y_a + gates[:, D_MODEL:] * y_b
    mix = _dot(mixed.astype(BF16), wout_ref[...])

    h0 = _layer_norm(x_ref[...], eg_ref[...], eb_ref[...])
    h1 = _layer_norm(DN_ALPHA * h0 + mix, l1g_ref[...], l1b_ref[...])
    h1_ref[...] = h1
    logit_ref[...] = _dot_hi(h1, wr_ref[...]) + br_ref[...]


def _mix(o_f, o_b, u_qkvz, yc, gates, x2, emb_g, emb_b, norm_g, w_a_o, cg, cb, w_b_o, b_b_o,
         w_out, l1g, l1b, w_router, b_router):
    t, d = x2.shape
    tm = min(256, t)
    row = lambda i: (i, 0)
    const = lambda i: (0, 0)
    wr = jnp.pad(w_router, ((0, 0), (0, LANES - N_EXPERTS)))
    br = jnp.pad(b_router.reshape(1, N_EXPERTS), ((0, 0), (0, LANES - N_EXPERTS)), constant_values=NEG_BIG)
    vec = lambda a: a.reshape(1, -1)
    return pl.pallas_call(
        _mix_kernel,
        out_shape=(jax.ShapeDtypeStruct((t, d), F32), jax.ShapeDtypeStruct((t, LANES), F32)),
        grid=(t // tm,),
        in_specs=[pl.BlockSpec((tm, d), row), pl.BlockSpec((tm, d), row),
                  pl.BlockSpec((tm, d), lambda i: (i, 3)),
                  pl.BlockSpec((tm, d), row), pl.BlockSpec((tm, 2 * d), row), pl.BlockSpec((tm, d), row),
                  pl.BlockSpec((1, d), const), pl.BlockSpec((1, d), const),
                  pl.BlockSpec((1, HEAD_DIM), const), pl.BlockSpec((d, d), const),
                  pl.BlockSpec((1, d), const), pl.BlockSpec((1, d), const),
                  pl.BlockSpec((d, d), const), pl.BlockSpec((1, d), const),
                  pl.BlockSpec((d, d), const), pl.BlockSpec((1, d), const), pl.BlockSpec((1, d), const),
                  pl.BlockSpec((d, LANES), const), pl.BlockSpec((1, LANES), const)],
        out_specs=(pl.BlockSpec((tm, d), row), pl.BlockSpec((tm, LANES), row)),
        compiler_params=_params("parallel"),
        name="mix",
    )(o_f, o_b, u_qkvz, yc, gates, x2, vec(emb_g), vec(emb_b), vec(norm_g), w_a_o, vec(cg), vec(cb),
      w_b_o, vec(b_b_o), w_out, vec(l1g), vec(l1b), wr, br)


def _route_kernel(logit_ref, gate_ref, eidx_ref, rank_ref, cnt_ref, base_ref):
    @pl.when(pl.program_id(0) == 0)
    def _():
        base_ref[...] = jnp.zeros_like(base_ref)

    x = logit_ref[...]
    tm = x.shape[0]
    lane = lax.broadcasted_iota(jnp.int32, x.shape, 1)
    lane_f = lane.astype(F32)
    row = lax.broadcasted_iota(jnp.int32, x.shape, 0)
    sel = jnp.zeros(x.shape, F32)
    vals, idxs = [], []
    for _ in range(TOP_K):
        m = jnp.max(x, axis=1, keepdims=True)
        idx = jnp.min(jnp.where(x == m, lane_f, float(LANES)), axis=1, keepdims=True).astype(jnp.int32)
        hit = lane == idx
        sel = sel + hit.astype(F32)
        x = jnp.where(hit, -3e38, x)
        vals.append(m)
        idxs.append(idx)

    exps = [jnp.exp(v - vals[0]) for v in vals]
    denom = exps[0]
    for e in exps[1:]:
        denom = denom + e

    csum = sel
    s = 1
    while s < tm:
        csum = csum + jnp.where(row >= s, pltpu.roll(csum, s, axis=0), 0.0)
        s *= 2
    before = base_ref[...] + csum - sel

    gate = jnp.zeros(x.shape, F32)
    eidx = jnp.zeros(x.shape, jnp.int32)
    rank = jnp.zeros(x.shape, jnp.int32)
    for k in range(TOP_K):
        rk = jnp.sum(jnp.where(lane == idxs[k], before, 0.0), axis=1, keepdims=True)
        gate = jnp.where(lane == k, exps[k] / denom, gate)
        eidx = jnp.where(lane == k, idxs[k], eidx)
        rank = jnp.where(lane == k, rk.astype(jnp.int32), rank)
    gate_ref[...] = gate
    eidx_ref[...] = eidx
    rank_ref[...] = rank
    total = base_ref[...] + csum[tm - 1:tm, :]
    base_ref[...] = total
    cnt_ref[...] = total


def _route(logits):
    t = logits.shape[0]
    tm = min(ROW_TILE, t)
    bm = EXPERT_ROWS
    row = lambda i: (i, 0)
    gate, eidx, rank, cnt = pl.pallas_call(
        _route_kernel,
        out_shape=(jax.ShapeDtypeStruct((t, LANES), F32), jax.ShapeDtypeStruct((t, LANES), jnp.int32),
                   jax.ShapeDtypeStruct((t, LANES), jnp.int32), jax.ShapeDtypeStruct((1, LANES), F32)),
        grid=(t // tm,),
        in_specs=[pl.BlockSpec((tm, LANES), row)],
        out_specs=(pl.BlockSpec((tm, LANES), row), pl.BlockSpec((tm, LANES), row),
                   pl.BlockSpec((tm, LANES), row), pl.BlockSpec((1, LANES), lambda i: (0, 0))),
        scratch_shapes=[pltpu.VMEM((1, LANES), F32)],
        compiler_params=_params("arbitrary"),
        name="moe_route",
    )(logits)
    counts = cnt[0, :N_EXPERTS].astype(jnp.int32)
    padded = (counts + bm - 1) // bm * bm
    pad_end = jnp.cumsum(padded)
    pad_start = pad_end - padded
    pos = pad_start[eidx[:, :TOP_K]] + rank[:, :TOP_K]
    n_blocks = -(-(t * TOP_K + N_EXPERTS * (bm - 1)) // bm)
    block_start = jnp.arange(n_blocks, dtype=jnp.int32) * bm
    block_exp = jnp.minimum(jnp.searchsorted(pad_end, block_start, side='right'),
                            N_EXPERTS - 1).astype(jnp.int32)
    n_used = (pad_end[-1] // bm).astype(jnp.int32).reshape(1)
    return gate[:, :TOP_K], pos.astype(jnp.int32), block_exp, n_used, n_blocks


def _gather_rows(idx_ref, src_hbm, buf_ref, sem, n):
    def issue(r, carry):
        pltpu.make_async_copy(src_hbm.at[pl.ds(idx_ref[0, 0, r], 1), :],
                              buf_ref.at[pl.ds(r, 1), :], sem).start()
        return carry

    lax.fori_loop(0, n, issue, 0, unroll=GATHER_UNROLL)
    pltpu.make_async_copy(src_hbm.at[pl.ds(0, n), :], buf_ref.at[pl.ds(0, n), :], sem).wait()


def _dispatch_kernel(idx_ref, h_ref, zero_hbm, xs_hbm, sem, *, tm):
    del zero_hbm

    def issue(r, carry):
        for k in range(TOP_K):
            pltpu.make_async_copy(h_ref.at[pl.ds(r, 1), :],
                                  xs_hbm.at[pl.ds(idx_ref[0, 0, r * TOP_K + k], 1), :], sem).start()
        return carry

    lax.fori_loop(0, tm, issue, 0, unroll=GATHER_UNROLL // 2)
    for _ in range(TOP_K):
        pltpu.make_async_copy(h_ref, xs_hbm.at[pl.ds(0, tm), :], sem).wait()


def _dispatch(h1, pos, n_blocks):
    t, d = h1.shape
    tm = min(COMBINE_ROWS, t)
    nt = t // tm
    n_slots = n_blocks * EXPERT_ROWS
    return pl.pallas_call(
        functools.partial(_dispatch_kernel, tm=tm),
        out_shape=jax.ShapeDtypeStruct((n_slots, d), F32),
        grid=(nt,),
        in_specs=[pl.BlockSpec((1, 1, TOP_K * tm), lambda i: (i, 0, 0), memory_space=pltpu.SMEM),
                  pl.BlockSpec((tm, d), lambda i: (i, 0)),
                  pl.BlockSpec(memory_space=pl.ANY)],
        out_specs=pl.BlockSpec(memory_space=pl.ANY),
        scratch_shapes=[pltpu.SemaphoreType.DMA(())],
        input_output_aliases={2: 0},
        compiler_params=_params("arbitrary"),
        name="moe_dispatch",
    )(pos.reshape(nt, 1, TOP_K * tm), h1, jnp.zeros((n_slots, d), F32))


def _expert_kernel(be_ref, nb_ref, x_ref, wgu_ref, bgu_ref, wd_ref, bd_ref, o_ref):
    @pl.when(pl.program_id(0) < nb_ref[0])
    def _():
        hgu = _dot(x_ref[...].astype(BF16), wgu_ref[0]) + bgu_ref[0]
        glu = jnp.minimum(hgu[:, :D_FF], SWIGLU_LIMIT)
        lin = jnp.clip(hgu[:, D_FF:], -SWIGLU_LIMIT, SWIGLU_LIMIT)
        act = glu * _sigmoid(SWIGLU_ALPHA * glu) * (lin + 1.0)
        o_ref[...] = _dot(act.astype(BF16), wd_ref[0]) + bd_ref[0]

    @pl.when(pl.program_id(0) >= nb_ref[0])
    def _():
        o_ref[...] = jnp.zeros_like(o_ref)


def _experts(xs, block_exp, n_used, w_gu, b_gu, w_down, b_down, n_blocks):
    d = xs.shape[1]
    bm = EXPERT_ROWS
    grid_spec = pltpu.PrefetchScalarGridSpec(
        num_scalar_prefetch=2,
        grid=(n_blocks,),
        in_specs=[pl.BlockSpec((bm, d), lambda i, be, nb: (i, 0)),
                  pl.BlockSpec((1, d, 2 * D_FF), lambda i, be, nb: (be[i], 0, 0)),
                  pl.BlockSpec((1, 1, 2 * D_FF), lambda i, be, nb: (be[i], 0, 0)),
                  pl.BlockSpec((1, D_FF, d), lambda i, be, nb: (be[i], 0, 0)),
                  pl.BlockSpec((1, 1, d), lambda i, be, nb: (be[i], 0, 0))],
        out_specs=pl.BlockSpec((bm, d), lambda i, be, nb: (i, 0)),
    )
    return pl.pallas_call(
        _expert_kernel,
        out_shape=jax.ShapeDtypeStruct((n_blocks * bm, d), F32),
        grid_spec=grid_spec,
        compiler_params=_params("arbitrary"),
        name="moe_experts",
    )(block_exp, n_used, xs, w_gu, b_gu.reshape(N_EXPERTS, 1, 2 * D_FF), w_down,
      b_down.reshape(N_EXPERTS, 1, d))


def _combine_kernel(idx_ref, ys_hbm, gate_ref, h_ref, g_ref, b_ref, o_ref, buf_ref, sem, *, tm):
    _gather_rows(idx_ref, ys_hbm, buf_ref, sem, TOP_K * tm)
    gate = gate_ref[...]
    f = gate[:, 0:1] * buf_ref[0:tm, :]
    for k in range(1, TOP_K):
        f = f + gate[:, k:k + 1] * buf_ref[k * tm:(k + 1) * tm, :]
    o_ref[...] = _layer_norm(DN_ALPHA * h_ref[...] + f, g_ref[...], b_ref[...])


def _combine(ys, pos, gate, h1, ln_g, ln_b):
    t, d = h1.shape
    tm = min(COMBINE_ROWS, t)
    nt = t // tm
    idx = pos.reshape(nt, tm, TOP_K).transpose(0, 2, 1).reshape(nt, 1, TOP_K * tm)
    return pl.pallas_call(
        functools.partial(_combine_kernel, tm=tm),
        out_shape=jax.ShapeDtypeStruct((t, d), F32),
        grid=(nt,),
        in_specs=[pl.BlockSpec((1, 1, TOP_K * tm), lambda i: (i, 0, 0), memory_space=pltpu.SMEM),
                  pl.BlockSpec(memory_space=pl.ANY),
                  pl.BlockSpec((tm, TOP_K), lambda i: (i, 0)),
                  pl.BlockSpec((tm, d), lambda i: (i, 0)),
                  pl.BlockSpec((1, d), lambda i: (0, 0)),
                  pl.BlockSpec((1, d), lambda i: (0, 0))],
        out_specs=pl.BlockSpec((tm, d), lambda i: (i, 0)),
        scratch_shapes=[pltpu.VMEM((TOP_K * tm, d), F32), pltpu.SemaphoreType.DMA(())],
        compiler_params=_params("arbitrary"),
        name="moe_combine",
    )(idx, ys, gate, h1, ln_g.reshape(1, d), ln_b.reshape(1, d))


def kernel(x, emb_ln_g, emb_ln_b, w_in, conv_qkv, a_log, dt_bias, dn_norm_g, w_a_o, b_glu, conv_dw, b_dw, conv_ln_g, conv_ln_b, w_b_o, b_b_o, b_gate, w_out, ln1_g, ln1_b, w_router, b_router, w_gu, b_gu, w_down, b_down, ln2_g, ln2_b):
    bsz, seq, d = x.shape
    t = bsz * seq
    x2 = x.reshape(t, d)
    wi = w_in[0]
    c0 = 4 * WIDTH_A
    c1 = c0 + 4 * N_HEADS
    c2 = c1 + 2 * WIDTH_B

    h0 = _ln0(x2, emb_ln_g, emb_ln_b)
    u_qkvz = _proj(h0, wi[:, :c0].astype(BF16))
    g, gt = _dn_gates(h0, wi[:, c0:c1].astype(BF16), a_log[0], dt_bias[0])
    glu = _proj_glu(h0, wi[:, c1:c1 + WIDTH_B].astype(BF16), wi[:, c1 + WIDTH_B:c2].astype(BF16),
                    b_glu[0, :WIDTH_B], b_glu[0, WIDTH_B:])
    gates = _proj_gate(h0, wi[:, c2:].astype(BF16), b_gate[0])

    qkv = _qkv_conv(u_qkvz, conv_qkv[0], bsz, seq)
    o_f = _delta_rule(qkv, g, gt, bsz, seq, reverse=False)
    o_b = _delta_rule(qkv, g, gt, bsz, seq, reverse=True)
    yc = _dw_conv(glu, conv_dw[0], b_dw[0], bsz, seq)

    h1, logits = _mix(o_f, o_b, u_qkvz, yc, gates, x2, emb_ln_g, emb_ln_b, dn_norm_g[0],
                      w_a_o[0].astype(BF16), conv_ln_g[0], conv_ln_b[0], w_b_o[0].astype(BF16), b_b_o[0],
                      w_out[0].astype(BF16), ln1_g[0], ln1_b[0], w_router[0], b_router[0])

    gate, pos, block_exp, n_used, n_blocks = _route(logits)
    xs = _dispatch(h1, pos, n_blocks)
    ys = _experts(xs, block_exp, n_used, w_gu[0].astype(BF16), b_gu[0], w_down[0].astype(BF16), b_down[0],
                  n_blocks)
    out = _combine(ys, pos, gate, h1, ln2_g[0], ln2_b[0])
    return out.reshape(bsz, seq, d)
```

```python
import functools

import jax
import jax.numpy as jnp
from jax import lax
from jax.experimental import pallas as pl
from jax.experimental.pallas import tpu as pltpu

F32 = jnp.float32
BF16 = jnp.bfloat16

D_MODEL = 1024
N_HEADS = 8
HEAD_DIM = 128
WIDTH_A = N_HEADS * HEAD_DIM
SHORT_CONV = 5
CHUNK = 64
WIDTH_B = D_MODEL
DW_CONV = 31
N_EXPERTS = 32
TOP_K = 4
D_FF = D_MODEL
SWIGLU_ALPHA = 1.702
SWIGLU_LIMIT = 7.0
DN_ALPHA = 2.0 ** 0.25
LN_EPS = 1e-5
RMS_EPS = 1e-6
L2_EPS = 1e-6
LANES = 128
NEG_BIG = -1e30

ROW_TILE = 512
DELTA_ROWS = 256
EXPERT_ROWS = 512
COMBINE_ROWS = 128
GATHER_UNROLL = 8
VMEM_LIMIT = 56 * 1024 * 1024


def _params(*sem):
    return pltpu.CompilerParams(dimension_semantics=sem, vmem_limit_bytes=VMEM_LIMIT)


def _layer_norm(x, g, b):
    mu = jnp.mean(x, axis=-1, keepdims=True)
    xc = x - mu
    var = jnp.mean(xc * xc, axis=-1, keepdims=True)
    return xc * lax.rsqrt(var + LN_EPS) * g + b


def _sigmoid(x):
    return 1.0 / (1.0 + jnp.exp(-x))


def _dot(a, b):
    return jnp.dot(a, b, preferred_element_type=F32)


def _split_bf16(a):
    hi = a.astype(BF16)
    return hi, (a - hi.astype(F32)).astype(BF16)


def _ln0_kernel(x_ref, g_ref, b_ref, o_ref):
    o_ref[...] = _layer_norm(x_ref[...], g_ref[...], b_ref[...]).astype(o_ref.dtype)


def _ln0(x2, g, b):
    t, d = x2.shape
    tm = min(ROW_TILE, t)
    return pl.pallas_call(
        _ln0_kernel,
        out_shape=jax.ShapeDtypeStruct((t, d), BF16),
        grid=(t // tm,),
        in_specs=[pl.BlockSpec((tm, d), lambda i: (i, 0)),
                  pl.BlockSpec((1, d), lambda i: (0, 0)),
                  pl.BlockSpec((1, d), lambda i: (0, 0))],
        out_specs=pl.BlockSpec((tm, d), lambda i: (i, 0)),
        compiler_params=_params("parallel"),
        name="ln0",
    )(x2, g.reshape(1, d), b.reshape(1, d))


def _proj_kernel(a_ref, w_ref, o_ref, *, tn):
    a = a_ref[...]
    for n0 in range(0, o_ref.shape[1], tn):
        o_ref[:, n0:n0 + tn] = _dot(a, w_ref[:, n0:n0 + tn]).astype(o_ref.dtype)


def _proj(h0, w):
    t, d = h0.shape
    n = w.shape[1]
    tm = min(ROW_TILE, t)
    return pl.pallas_call(
        functools.partial(_proj_kernel, tn=512),
        out_shape=jax.ShapeDtypeStruct((t, n), BF16),
        grid=(t // tm,),
        in_specs=[pl.BlockSpec((tm, d), lambda i: (i, 0)),
                  pl.BlockSpec((d, n), lambda i: (0, 0))],
        out_specs=pl.BlockSpec((tm, n), lambda i: (i, 0)),
        compiler_params=_params("parallel"),
        name="proj_qkvz",
    )(h0, w)


def _glu_kernel(a_ref, wa_ref, wb_ref, ba_ref, bb_ref, o_ref, *, tn):
    a = a_ref[...]
    for n0 in range(0, o_ref.shape[1], tn):
        lin = _dot(a, wa_ref[:, n0:n0 + tn]) + ba_ref[:, n0:n0 + tn]
        gate = _dot(a, wb_ref[:, n0:n0 + tn]) + bb_ref[:, n0:n0 + tn]
        o_ref[:, n0:n0 + tn] = (lin * _sigmoid(gate)).astype(o_ref.dtype)


def _proj_glu(h0, wa, wb, ba, bb):
    t, d = h0.shape
    n = wa.shape[1]
    tm = min(ROW_TILE, t)
    return pl.pallas_call(
        functools.partial(_glu_kernel, tn=512),
        out_shape=jax.ShapeDtypeStruct((t, n), BF16),
        grid=(t // tm,),
        in_specs=[pl.BlockSpec((tm, d), lambda i: (i, 0)),
                  pl.BlockSpec((d, n), lambda i: (0, 0)),
                  pl.BlockSpec((d, n), lambda i: (0, 0)),
                  pl.BlockSpec((1, n), lambda i: (0, 0)),
                  pl.BlockSpec((1, n), lambda i: (0, 0))],
        out_specs=pl.BlockSpec((tm, n), lambda i: (i, 0)),
        compiler_params=_params("parallel"),
        name="proj_glu",
    )(h0, wa, wb, ba.reshape(1, n), bb.reshape(1, n))


def _gate_kernel(a_ref, w_ref, b_ref, o_ref, *, tn):
    a = a_ref[...]
    for n0 in range(0, o_ref.shape[1], tn):
        o_ref[:, n0:n0 + tn] = _sigmoid(_dot(a, w_ref[:, n0:n0 + tn]) + b_ref[:, n0:n0 + tn]).astype(o_ref.dtype)


def _proj_gate(h0, w, b):
    t, d = h0.shape
    n = w.shape[1]
    tm = min(ROW_TILE, t)
    return pl.pallas_call(
        functools.partial(_gate_kernel, tn=512),
        out_shape=jax.ShapeDtypeStruct((t, n), BF16),
        grid=(t // tm,),
        in_specs=[pl.BlockSpec((tm, d), lambda i: (i, 0)),
                  pl.BlockSpec((d, n), lambda i: (0, 0)),
                  pl.BlockSpec((1, n), lambda i: (0, 0))],
        out_specs=pl.BlockSpec((tm, n), lambda i: (i, 0)),
        compiler_params=_params("parallel"),
        name="proj_gate",
    )(h0, w, b.reshape(1, n))


def _chunk_cumsum(x, reverse):
    rows = x.shape[0]
    pos = lax.broadcasted_iota(jnp.int32, x.shape, 0) % CHUNK
    s = 1
    while s < CHUNK:
        if reverse:
            shifted = pltpu.roll(x, rows - s, axis=0)
            x = x + jnp.where(pos < CHUNK - s, shifted, 0.0)
        else:
            shifted = pltpu.roll(x, s, axis=0)
            x = x + jnp.where(pos >= s, shifted, 0.0)
        s *= 2
    return x


def _dn_gates_kernel(a_ref, w_ref, alog_ref, dtb_ref, g_ref, gt_ref):
    u = _dot(a_ref[...], w_ref[...])
    lane = lax.broadcasted_iota(jnp.int32, u.shape, 1)
    beta = _sigmoid(u)
    x = u + dtb_ref[...]
    softplus = jnp.maximum(x, 0.0) + jnp.log(1.0 + jnp.exp(-jnp.abs(x)))
    log_a = -jnp.exp(alog_ref[...]) * softplus
    g_fwd = _chunk_cumsum(log_a, reverse=False)
    g_bwd = _chunk_cumsum(log_a, reverse=True)
    out = jnp.where(lane < 2 * N_HEADS, beta, jnp.where(lane < 3 * N_HEADS, g_fwd, g_bwd))
    g_ref[...] = out
    gt_ref[...] = out.T[2 * N_HEADS:4 * N_HEADS, :]


def _dn_gates(h0, w_small, a_log, dt_bias):
    t, d = h0.shape
    tm = min(ROW_TILE, t)
    pad = LANES - 4 * N_HEADS
    w = jnp.pad(w_small, ((0, 0), (0, pad)))
    alog = jnp.pad(a_log.reshape(1, 2 * N_HEADS), ((0, 0), (2 * N_HEADS, LANES - 4 * N_HEADS)))
    dtb = jnp.pad(dt_bias.reshape(1, 2 * N_HEADS), ((0, 0), (2 * N_HEADS, LANES - 4 * N_HEADS)))
    return pl.pallas_call(
        _dn_gates_kernel,
        out_shape=(jax.ShapeDtypeStruct((t, LANES), F32),
                   jax.ShapeDtypeStruct((2 * N_HEADS, t), F32)),
        grid=(t // tm,),
        in_specs=[pl.BlockSpec((tm, d), lambda i: (i, 0)),
                  pl.BlockSpec((d, LANES), lambda i: (0, 0)),
                  pl.BlockSpec((1, LANES), lambda i: (0, 0)),
                  pl.BlockSpec((1, LANES), lambda i: (0, 0))],
        out_specs=(pl.BlockSpec((tm, LANES), lambda i: (i, 0)),
                   pl.BlockSpec((2 * N_HEADS, tm), lambda i: (0, i))),
        compiler_params=_params("parallel"),
        name="dn_gates",
    )(h0, w, alog, dtb)


def _conv_rows(xp_ref, w, taps, base, r0, rows):
    acc = xp_ref[base + r0:base + r0 + rows, :] * w[0:1, :]
    for k in range(1, taps):
        acc = acc + xp_ref[base + k + r0:base + k + r0 + rows, :] * w[k:k + 1, :]
    return acc


def _fill_padded(xp_ref, x_ref, pad, seq):
    zeros = jnp.zeros((pad, xp_ref.shape[1]), F32)
    xp_ref[0:pad, :] = zeros
    xp_ref[pad + seq:pad + seq + pad, :] = zeros
    xp_ref[pad:pad + seq, :] = x_ref[...].astype(F32)


def _qkv_conv_kernel(u_ref, w_ref, o_ref, xp_ref, *, seq, rows):
    pad = 8
    j = pl.program_id(1)
    _fill_padded(xp_ref, u_ref, pad, seq)
    w = w_ref[...]
    is_q = j < N_HEADS
    is_qk = j < 2 * N_HEADS
    for r0 in range(0, seq, rows):
        y = _conv_rows(xp_ref, w, SHORT_CONV, pad - SHORT_CONV // 2, r0, rows)
        y = y * _sigmoid(y)
        inv = lax.rsqrt(jnp.sum(y * y, axis=-1, keepdims=True) + L2_EPS)
        scale = jnp.where(is_q, inv * (HEAD_DIM ** -0.5), jnp.where(is_qk, inv, 1.0))
        o_ref[r0:r0 + rows, :] = (y * scale).astype(o_ref.dtype)


def _qkv_conv(u_qkvz, conv_w, bsz, seq):
    t = bsz * seq
    ncol = 3 * N_HEADS
    rows = min(256, seq)
    return pl.pallas_call(
        functools.partial(_qkv_conv_kernel, seq=seq, rows=rows),
        out_shape=jax.ShapeDtypeStruct((t, 3 * WIDTH_A), BF16),
        grid=(bsz, ncol),
        in_specs=[pl.BlockSpec((seq, HEAD_DIM), lambda b, j: (b, j)),
                  pl.BlockSpec((SHORT_CONV, HEAD_DIM), lambda b, j: (0, j))],
        out_specs=pl.BlockSpec((seq, HEAD_DIM), lambda b, j: (b, j)),
        scratch_shapes=[pltpu.VMEM((seq + 16, HEAD_DIM), F32)],
        compiler_params=_params("parallel", "parallel"),
        name="qkv_conv",
    )(u_qkvz, conv_w)


def _dw_conv_kernel(x_ref, w_ref, b_ref, o_ref, xp_ref, *, seq, rows):
    pad = 16
    _fill_padded(xp_ref, x_ref, pad, seq)
    w = w_ref[...]
    for r0 in range(0, seq, rows):
        y = _conv_rows(xp_ref, w, DW_CONV, pad - DW_CONV // 2, r0, rows) + b_ref[...]
        o_ref[r0:r0 + rows, :] = y.astype(o_ref.dtype)


def _dw_conv(glu, conv_w, b_dw, bsz, seq):
    t = bsz * seq
    rows = min(256, seq)
    return pl.pallas_call(
        functools.partial(_dw_conv_kernel, seq=seq, rows=rows),
        out_shape=jax.ShapeDtypeStruct((t, WIDTH_B), BF16),
        grid=(bsz, WIDTH_B // LANES),
        in_specs=[pl.BlockSpec((seq, LANES), lambda b, j: (b, j)),
                  pl.BlockSpec((DW_CONV, LANES), lambda b, j: (0, j)),
                  pl.BlockSpec((1, LANES), lambda b, j: (0, j))],
        out_specs=pl.BlockSpec((seq, LANES), lambda b, j: (b, j)),
        scratch_shapes=[pltpu.VMEM((seq + 32, LANES), F32)],
        compiler_params=_params("parallel", "parallel"),
        name="dw_conv",
    )(glu, conv_w, b_dw.reshape(1, WIDTH_B))


def _bmm(a, b):
    return lax.dot_general(a, b, (((2,), (1,)), ((0,), (0,))), preferred_element_type=F32)


def _bmm_nt(a, b):
    return lax.dot_general(a, b, (((2,), (2,)), ((0,), (0,))), preferred_element_type=F32)


def _bmm_tn(a, b):
    return lax.dot_general(a, b, (((1,), (1,)), ((0,), (0,))), preferred_element_type=F32)


def _unit_tri_inverse(lmat, eye):
    lb = lmat.astype(BF16)
    x = eye - lmat
    p = _bmm(lb, lb)
    s = 2
    while 2 * s < CHUNK:
        pb = p.astype(BF16)
        xp = _bmm(jnp.concatenate([x.astype(BF16), pb], axis=1), pb)
        x = x + xp[:, :CHUNK]
        p = xp[:, CHUNK:]
        s *= 2
    return x + _bmm(x.astype(BF16), p.astype(BF16))


def _delta_kernel(q_ref, k_ref, v_ref, g_ref, gt_ref, o_ref, s_ref, *, reverse, nc):
    @pl.when(pl.program_id(1) == 0)
    def _():
        s_ref[...] = jnp.zeros_like(s_ref)

    dir_off = N_HEADS if reverse else 0
    ri = lax.broadcasted_iota(jnp.int32, (CHUNK, CHUNK), 0)
    ci = lax.broadcasted_iota(jnp.int32, (CHUNK, CHUNK), 1)
    incl = (ri <= ci) if reverse else (ri >= ci)
    strict = (ri < ci) if reverse else (ri > ci)
    eye = (ri == ci).astype(F32)

    def heads(ref, rows):
        return jnp.stack([ref[rows, hh * HEAD_DIM:(hh + 1) * HEAD_DIM] for hh in range(N_HEADS)])

    order = range(nc - 1, -1, -1) if reverse else range(nc)
    for c in order:
        rows = slice(c * CHUNK, (c + 1) * CHUNK)
        k = heads(k_ref, rows)
        kf = k.astype(F32)
        qf = heads(q_ref, rows).astype(F32)
        vf = heads(v_ref, rows).astype(F32)
        gblk = g_ref[rows, :]
        beta = jnp.stack([gblk[:, dir_off + hh:dir_off + hh + 1] for hh in range(N_HEADS)])
        gcol = jnp.stack([gblk[:, 2 * N_HEADS + dir_off + hh:2 * N_HEADS + dir_off + hh + 1]
                          for hh in range(N_HEADS)])
        grow = jnp.stack([gt_ref[dir_off + hh:dir_off + hh + 1, rows] for hh in range(N_HEADS)])
        glast = gcol[:, 0:1, :] if reverse else gcol[:, CHUNK - 1:CHUNK, :]

        decay = jnp.exp(jnp.where(incl, gcol - grow, NEG_BIG))
        kb = kf * beta
        kkqk = _bmm_nt(jnp.concatenate([kb, qf], axis=1).astype(BF16), k)
        lmat = jnp.where(strict, kkqk[:, :CHUNK, :] * decay, 0.0)
        qk = kkqk[:, CHUNK:, :] * decay
        tinv = _unit_tri_inverse(lmat, eye)

        eg = jnp.exp(gcol)
        rhs = jnp.concatenate([vf * beta, kb * eg], axis=2).astype(BF16)
        uw = _bmm(tinv.astype(BF16), rhs)
        u = uw[:, :, :HEAD_DIM]
        w = uw[:, :, HEAD_DIM:]
        qd = qf * eg
        kt = kf * jnp.exp(glast - gcol)

        state = s_ref[...]
        ws = _bmm(jnp.concatenate([w, qd], axis=1).astype(BF16), state.astype(BF16))
        v_new = (u - ws[:, :CHUNK, :]).astype(BF16)
        o = ws[:, CHUNK:, :] + _bmm(qk.astype(BF16), v_new)
        s_ref[...] = state * jnp.exp(glast) + _bmm_tn(kt.astype(BF16), v_new)
        for hh in range(N_HEADS):
            o_ref[rows, hh * HEAD_DIM:(hh + 1) * HEAD_DIM] = o[hh].astype(o_ref.dtype)


def _delta_rule(qkv, g, gt, bsz, seq, reverse):
    t = bsz * seq
    rows = min(DELTA_ROWS, seq)
    nblk = seq // rows

    def rb(b, i):
        return b * nblk + ((nblk - 1 - i) if reverse else i)

    return pl.pallas_call(
        functools.partial(_delta_kernel, reverse=reverse, nc=rows // CHUNK),
        out_shape=jax.ShapeDtypeStruct((t, WIDTH_A), BF16),
        grid=(bsz, nblk),
        in_specs=[pl.BlockSpec((rows, WIDTH_A), lambda b, i: (rb(b, i), 0)),
                  pl.BlockSpec((rows, WIDTH_A), lambda b, i: (rb(b, i), 1)),
                  pl.BlockSpec((rows, WIDTH_A), lambda b, i: (rb(b, i), 2)),
                  pl.BlockSpec((rows, LANES), lambda b, i: (rb(b, i), 0)),
                  pl.BlockSpec((2 * N_HEADS, rows), lambda b, i: (0, rb(b, i)))],
        out_specs=pl.BlockSpec((rows, WIDTH_A), lambda b, i: (rb(b, i), 0)),
        scratch_shapes=[pltpu.VMEM((N_HEADS, HEAD_DIM, HEAD_DIM), F32)],
        compiler_params=_params("parallel", "arbitrary"),
        name="delta_bwd" if reverse else "delta_fwd",
    )(qkv, qkv, qkv, g, gt)


def _mix_kernel(of_ref, ob_ref, z_ref, yc_ref, gate_ref, x_ref,
                eg_ref, eb_ref, ng_ref, wao_ref, cg_ref, cb_ref, wbo_ref, bbo_ref,
                wout_ref, l1g_ref, l1b_ref, wr_ref, br_ref,
                h1_ref, logit_ref):
    o = of_ref[...].astype(F32) + ob_ref[...].astype(F32)
    z = z_ref[...].astype(F32)
    ng = ng_ref[...]
    parts = []
    for hh in range(N_HEADS):
        sl = slice(hh * HEAD_DIM, (hh + 1) * HEAD_DIM)
        oh = o[:, sl]
        zh = z[:, sl]
        inv = lax.rsqrt(jnp.mean(oh * oh, axis=-1, keepdims=True) + RMS_EPS)
        parts.append((oh * inv * ng * (zh * _sigmoid(zh))).astype(BF16))
    y_a = _dot(jnp.concatenate(parts, axis=1), wao_ref[...])

    yc = _layer_norm(yc_ref[...].astype(F32), cg_ref[...], cb_ref[...])
    y_b = _dot((yc * _sigmoid(yc)).astype(BF16), wbo_ref[...]) + bbo_ref[...]

    gates = gate_ref[...].astype(F32)
    mixed = gates[:, :D_MODEL] * y_a + gates[:, D_MODEL:] * y_b
    mix = _dot(mixed.astype(BF16), wout_ref[...])

    h0 = _layer_norm(x_ref[...], eg_ref[...], eb_ref[...])
    h1 = _layer_norm(DN_ALPHA * h0 + mix, l1g_ref[...], l1b_ref[...])
    h1_ref[...] = h1
    h_hi, h_lo = _split_bf16(h1)
    p = _dot(h_hi, wr_ref[...])
    logit_ref[...] = p[:, :LANES] + p[:, LANES:] + _dot(h_lo, wr_ref[:, :LANES]) + br_ref[...]


def _mix(o_f, o_b, u_qkvz, yc, gates, x2, emb_g, emb_b, norm_g, w_a_o, cg, cb, w_b_o, b_b_o,
         w_out, l1g, l1b, w_router, b_router):
    t, d = x2.shape
    tm = min(ROW_TILE, t)
    row = lambda i: (i, 0)
    const = lambda i: (0, 0)
    wr = jnp.concatenate(_split_bf16(jnp.pad(w_router, ((0, 0), (0, LANES - N_EXPERTS)))), axis=1)
    br = jnp.pad(b_router.reshape(1, N_EXPERTS), ((0, 0), (0, LANES - N_EXPERTS)), constant_values=NEG_BIG)
    vec = lambda a: a.reshape(1, -1)
    return pl.pallas_call(
        _mix_kernel,
        out_shape=(jax.ShapeDtypeStruct((t, d), F32), jax.ShapeDtypeStruct((t, LANES), F32)),
        grid=(t // tm,),
        in_specs=[pl.BlockSpec((tm, d), row), pl.BlockSpec((tm, d), row),
                  pl.BlockSpec((tm, d), lambda i: (i, 3)),
                  pl.BlockSpec((tm, d), row), pl.BlockSpec((tm, 2 * d), row), pl.BlockSpec((tm, d), row),
                  pl.BlockSpec((1, d), const), pl.BlockSpec((1, d), const),
                  pl.BlockSpec((1, HEAD_DIM), const), pl.BlockSpec((d, d), const),
                  pl.BlockSpec((1, d), const), pl.BlockSpec((1, d), const),
                  pl.BlockSpec((d, d), const), pl.BlockSpec((1, d), const),
                  pl.BlockSpec((d, d), const), pl.BlockSpec((1, d), const), pl.BlockSpec((1, d), const),
                  pl.BlockSpec((d, 2 * LANES), const), pl.BlockSpec((1, LANES), const)],
        out_specs=(pl.BlockSpec((tm, d), row), pl.BlockSpec((tm, LANES), row)),
        compiler_params=_params("parallel"),
        name="mix",
    )(o_f, o_b, u_qkvz, yc, gates, x2, vec(emb_g), vec(emb_b), vec(norm_g), w_a_o, vec(cg), vec(cb),
      w_b_o, vec(b_b_o), w_out, vec(l1g), vec(l1b), wr, br)


def _route_kernel(logit_ref, gate_ref, eidx_ref, rank_ref, cnt_ref, base_ref):
    @pl.when(pl.program_id(0) == 0)
    def _():
        base_ref[...] = jnp.zeros_like(base_ref)

    x = logit_ref[...]
    tm = x.shape[0]
    lane = lax.broadcasted_iota(jnp.int32, x.shape, 1)
    lane_f = lane.astype(F32)
    row = lax.broadcasted_iota(jnp.int32, x.shape, 0)
    sel = jnp.zeros(x.shape, F32)
    vals, idxs = [], []
    for _ in range(TOP_K):
        m = jnp.max(x, axis=1, keepdims=True)
        idx = jnp.min(jnp.where(x == m, lane_f, float(LANES)), axis=1, keepdims=True).astype(jnp.int32)
        hit = lane == idx
        sel = sel + hit.astype(F32)
        x = jnp.where(hit, -3e38, x)
        vals.append(m)
        idxs.append(idx)

    exps = [jnp.exp(v - vals[0]) for v in vals]
    denom = exps[0]
    for e in exps[1:]:
        denom = denom + e

    csum = sel
    s = 1
    while s < tm:
        csum = csum + jnp.where(row >= s, pltpu.roll(csum, s, axis=0), 0.0)
        s *= 2
    before = base_ref[...] + csum - sel

    gate = jnp.zeros(x.shape, F32)
    eidx = jnp.zeros(x.shape, jnp.int32)
    rank = jnp.zeros(x.shape, jnp.int32)
    for k in range(TOP_K):
        rk = jnp.sum(jnp.where(lane == idxs[k], before, 0.0), axis=1, keepdims=True)
        gate = jnp.where(lane == k, exps[k] / denom, gate)
        eidx = jnp.where(lane == k, idxs[k], eidx)
        rank = jnp.where(lane == k, rk.astype(jnp.int32), rank)
    gate_ref[...] = gate
    eidx_ref[...] = eidx
    rank_ref[...] = rank
    total = base_ref[...] + csum[tm - 1:tm, :]
    base_ref[...] = total
    cnt_ref[...] = total


def _route(logits):
    t = logits.shape[0]
    tm = min(ROW_TILE, t)
    bm = EXPERT_ROWS
    row = lambda i: (i, 0)
    gate, eidx, rank, cnt = pl.pallas_call(
        _route_kernel,
        out_shape=(jax.ShapeDtypeStruct((t, LANES), F32), jax.ShapeDtypeStruct((t, LANES), jnp.int32),
                   jax.ShapeDtypeStruct((t, LANES), jnp.int32), jax.ShapeDtypeStruct((1, LANES), F32)),
        grid=(t // tm,),
        in_specs=[pl.BlockSpec((tm, LANES), row)],
        out_specs=(pl.BlockSpec((tm, LANES), row), pl.BlockSpec((tm, LANES), row),
                   pl.BlockSpec((tm, LANES), row), pl.BlockSpec((1, LANES), lambda i: (0, 0))),
        scratch_shapes=[pltpu.VMEM((1, LANES), F32)],
        compiler_params=_params("arbitrary"),
        name="moe_route",
    )(logits)
    counts = cnt[0, :N_EXPERTS].astype(jnp.int32)
    padded = (counts + bm - 1) // bm * bm
    pad_end = jnp.cumsum(padded)
    pad_start = pad_end - padded
    pos = pad_start[eidx[:, :TOP_K]] + rank[:, :TOP_K]
    n_blocks = -(-(t * TOP_K + N_EXPERTS * (bm - 1)) // bm)
    block_start = jnp.arange(n_blocks, dtype=jnp.int32) * bm
    block_exp = jnp.minimum(jnp.sum((block_start[:, None] >= pad_end[None, :]).astype(jnp.int32), axis=1),
                            N_EXPERTS - 1)
    n_used = (pad_end[-1] // bm).astype(jnp.int32).reshape(1)
    return gate[:, :TOP_K], pos.astype(jnp.int32), block_exp, pad_start, pad_end, n_used, n_blocks


def _dispatch_kernel(ps_ref, pe_ref, nb_ref, idx_ref, h_ref, xs_hbm, zero_ref, sem, zsem, *, tm, n_blocks):
    bm = EXPERT_ROWS

    @pl.when(pl.program_id(0) == 0)
    def _():
        zero_ref[...] = jnp.zeros_like(zero_ref)

        def zero_block(start):
            return pltpu.make_async_copy(zero_ref, xs_hbm.at[pl.ds(pl.multiple_of(start, bm), bm), :], zsem)

        for e in range(N_EXPERTS):
            @pl.when(pe_ref[e] > ps_ref[e])
            def _():
                zero_block(pe_ref[e] - bm).start()
        for e in range(N_EXPERTS):
            @pl.when(pe_ref[e] > ps_ref[e])
            def _():
                zero_block(pe_ref[e] - bm).wait()

        def tail_start(b, carry):
            zero_block(b * bm).start()
            return carry

        def tail_wait(b, carry):
            zero_block(b * bm).wait()
            return carry

        lax.fori_loop(nb_ref[0], n_blocks, tail_start, 0)
        lax.fori_loop(nb_ref[0], n_blocks, tail_wait, 0)

    def issue(r, carry):
        for k in range(TOP_K):
            pltpu.make_async_copy(h_ref.at[pl.ds(r, 1), :],
                                  xs_hbm.at[pl.ds(idx_ref[0, 0, r * TOP_K + k], 1), :],
                                  sem).start(priority=k % 2)
        return carry

    lax.fori_loop(0, tm, issue, 0, unroll=GATHER_UNROLL // 2)
    for _ in range(TOP_K):
        pltpu.make_async_copy(h_ref, xs_hbm.at[pl.ds(0, tm), :], sem).wait()


def _dispatch(h1, pos, pad_start, pad_end, n_used, n_blocks):
    t, d = h1.shape
    tm = min(ROW_TILE, t)
    nt = t // tm
    n_slots = n_blocks * EXPERT_ROWS
    grid_spec = pltpu.PrefetchScalarGridSpec(
        num_scalar_prefetch=3,
        grid=(nt,),
        in_specs=[pl.BlockSpec((1, 1, TOP_K * tm), lambda i, *_: (i, 0, 0), memory_space=pltpu.SMEM),
                  pl.BlockSpec((tm, d), lambda i, *_: (i, 0))],
        out_specs=pl.BlockSpec(memory_space=pl.ANY),
        scratch_shapes=[pltpu.VMEM((EXPERT_ROWS, d), F32), pltpu.SemaphoreType.DMA(()),
                        pltpu.SemaphoreType.DMA(())],
    )
    return pl.pallas_call(
        functools.partial(_dispatch_kernel, tm=tm, n_blocks=n_blocks),
        out_shape=jax.ShapeDtypeStruct((n_slots, d), F32),
        grid_spec=grid_spec,
        compiler_params=_params("arbitrary"),
        name="moe_dispatch",
    )(pad_start, pad_end, n_used, pos.reshape(nt, 1, TOP_K * tm), h1)


def _expert_kernel(be_ref, nb_ref, x_ref, wgu_ref, bgu_ref, wd_ref, bd_ref, o_ref):
    @pl.when(pl.program_id(0) < nb_ref[0])
    def _():
        hgu = _dot(x_ref[...].astype(BF16), wgu_ref[0]) + bgu_ref[0]
        glu = jnp.minimum(hgu[:, :D_FF], SWIGLU_LIMIT)
        lin = jnp.clip(hgu[:, D_FF:], -SWIGLU_LIMIT, SWIGLU_LIMIT)
        act = glu * _sigmoid(SWIGLU_ALPHA * glu) * (lin + 1.0)
        o_ref[...] = _dot(act.astype(BF16), wd_ref[0]) + bd_ref[0]

    @pl.when(pl.program_id(0) >= nb_ref[0])
    def _():
        o_ref[...] = jnp.zeros_like(o_ref)


def _experts(xs, block_exp, n_used, w_gu, b_gu, w_down, b_down, n_blocks):
    d = xs.shape[1]
    bm = EXPERT_ROWS
    grid_spec = pltpu.PrefetchScalarGridSpec(
        num_scalar_prefetch=2,
        grid=(n_blocks,),
        in_specs=[pl.BlockSpec((bm, d), lambda i, be, nb: (i, 0)),
                  pl.BlockSpec((1, d, 2 * D_FF), lambda i, be, nb: (be[i], 0, 0)),
                  pl.BlockSpec((1, 1, 2 * D_FF), lambda i, be, nb: (be[i], 0, 0)),
                  pl.BlockSpec((1, D_FF, d), lambda i, be, nb: (be[i], 0, 0)),
                  pl.BlockSpec((1, 1, d), lambda i, be, nb: (be[i], 0, 0))],
        out_specs=pl.BlockSpec((bm, d), lambda i, be, nb: (i, 0)),
    )
    return pl.pallas_call(
        _expert_kernel,
        out_shape=jax.ShapeDtypeStruct((n_blocks * bm, d), F32),
        grid_spec=grid_spec,
        compiler_params=_params("arbitrary"),
        name="moe_experts",
    )(block_exp, n_used, xs, w_gu, b_gu.reshape(N_EXPERTS, 1, 2 * D_FF), w_down,
      b_down.reshape(N_EXPERTS, 1, d))


def _combine_kernel(idx_ref, idx_next_ref, ys_hbm, gate_ref, h_ref, g_ref, b_ref, o_ref, buf_ref, sem, *, tm, nt):
    i = pl.program_id(0)
    n = TOP_K * tm
    slot = i % 2

    def gather(idx, s):
        def issue(r, carry):
            for j in range(2):
                rr = 2 * r + j
                pltpu.make_async_copy(ys_hbm.at[pl.ds(idx[0, 0, rr], 1), :],
                                      buf_ref.at[s, pl.ds(rr, 1), :], sem.at[s]).start(priority=j)
            return carry

        lax.fori_loop(0, n // 2, issue, 0, unroll=GATHER_UNROLL // 2)

    @pl.when(i == 0)
    def _():
        gather(idx_ref, 0)

    @pl.when(i + 1 < nt)
    def _():
        gather(idx_next_ref, 1 - slot)

    pltpu.make_async_copy(ys_hbm.at[pl.ds(0, n), :], buf_ref.at[slot], sem.at[slot]).wait()
    gate = gate_ref[...]
    f = gate[:, 0:1] * buf_ref[slot, 0:tm, :]
    for k in range(1, TOP_K):
        f = f + gate[:, k:k + 1] * buf_ref[slot, k * tm:(k + 1) * tm, :]
    o_ref[...] = _layer_norm(DN_ALPHA * h_ref[...] + f, g_ref[...], b_ref[...])


def _combine(ys, pos, gate, h1, ln_g, ln_b):
    t, d = h1.shape
    tm = min(COMBINE_ROWS, t)
    nt = t // tm
    idx = pos.reshape(nt, tm, TOP_K).transpose(0, 2, 1).reshape(nt, 1, TOP_K * tm)
    return pl.pallas_call(
        functools.partial(_combine_kernel, tm=tm, nt=nt),
        out_shape=jax.ShapeDtypeStruct((t, d), F32),
        grid=(nt,),
        in_specs=[pl.BlockSpec((1, 1, TOP_K * tm), lambda i: (i, 0, 0), memory_space=pltpu.SMEM),
                  pl.BlockSpec((1, 1, TOP_K * tm), lambda i: (jnp.minimum(i + 1, nt - 1), 0, 0),
                               memory_space=pltpu.SMEM),
                  pl.BlockSpec(memory_space=pl.ANY),
                  pl.BlockSpec((tm, TOP_K), lambda i: (i, 0)),
                  pl.BlockSpec((tm, d), lambda i: (i, 0)),
                  pl.BlockSpec((1, d), lambda i: (0, 0)),
                  pl.BlockSpec((1, d), lambda i: (0, 0))],
        out_specs=pl.BlockSpec((tm, d), lambda i: (i, 0)),
        scratch_shapes=[pltpu.VMEM((2, TOP_K * tm, d), F32), pltpu.SemaphoreType.DMA((2,))],
        compiler_params=_params("arbitrary"),
        name="moe_combine",
    )(idx, idx, ys, gate, h1, ln_g.reshape(1, d), ln_b.reshape(1, d))


def kernel(x, emb_ln_g, emb_ln_b, w_in, conv_qkv, a_log, dt_bias, dn_norm_g, w_a_o, b_glu, conv_dw, b_dw, conv_ln_g, conv_ln_b, w_b_o, b_b_o, b_gate, w_out, ln1_g, ln1_b, w_router, b_router, w_gu, b_gu, w_down, b_down, ln2_g, ln2_b):
    bsz, seq, d = x.shape
    t = bsz * seq
    x2 = x.reshape(t, d)
    wi = w_in[0]
    c0 = 4 * WIDTH_A
    c1 = c0 + 4 * N_HEADS
    c2 = c1 + 2 * WIDTH_B

    h0 = _ln0(x2, emb_ln_g, emb_ln_b)
    u_qkvz = _proj(h0, wi[:, :c0].astype(BF16))
    g, gt = _dn_gates(h0, wi[:, c0:c1].astype(BF16), a_log[0], dt_bias[0])
    glu = _proj_glu(h0, wi[:, c1:c1 + WIDTH_B].astype(BF16), wi[:, c1 + WIDTH_B:c2].astype(BF16),
                    b_glu[0, :WIDTH_B], b_glu[0, WIDTH_B:])
    gates = _proj_gate(h0, wi[:, c2:].astype(BF16), b_gate[0])

    qkv = _qkv_conv(u_qkvz, conv_qkv[0], bsz, seq)
    o_f = _delta_rule(qkv, g, gt, bsz, seq, reverse=False)
    o_b = _delta_rule(qkv, g, gt, bsz, seq, reverse=True)
    yc = _dw_conv(glu, conv_dw[0], b_dw[0], bsz, seq)

    h1, logits = _mix(o_f, o_b, u_qkvz, yc, gates, x2, emb_ln_g, emb_ln_b, dn_norm_g[0],
                      w_a_o[0].astype(BF16), conv_ln_g[0], conv_ln_b[0], w_b_o[0].astype(BF16), b_b_o[0],
                      w_out[0].astype(BF16), ln1_g[0], ln1_b[0], w_router[0], b_router[0])

    gate, pos, block_exp, pad_start, pad_end, n_used, n_blocks = _route(logits)
    xs = _dispatch(h1, pos, pad_start, pad_end, n_used, n_blocks)
    ys = _experts(xs, block_exp, n_used, w_gu[0].astype(BF16), b_gu[0], w_down[0].astype(BF16), b_down[0],
                  n_blocks)
    out = _combine(ys, pos, gate, h1, ln2_g[0], ln2_b[0])
    return out.reshape(bsz, seq, d)
```

```python
import functools

import jax
import jax.numpy as jnp
from jax import lax
from jax.experimental import pallas as pl
from jax.experimental.pallas import tpu as pltpu

F32 = jnp.float32
BF16 = jnp.bfloat16

D_MODEL = 1024
N_HEADS = 8
HEAD_DIM = 128
WIDTH_A = N_HEADS * HEAD_DIM
SHORT_CONV = 5
CHUNK = 64
WIDTH_B = D_MODEL
DW_CONV = 31
N_EXPERTS = 32
TOP_K = 4
D_FF = D_MODEL
SWIGLU_ALPHA = 1.702
SWIGLU_LIMIT = 7.0
DN_ALPHA = 2.0 ** 0.25
LN_EPS = 1e-5
RMS_EPS = 1e-6
L2_EPS = 1e-6
LANES = 128
NEG_BIG = -1e30

ROW_TILE = 512
DELTA_ROWS = 256
EXPERT_ROWS = 512
COMBINE_ROWS = 128
GATHER_UNROLL = 8
VMEM_LIMIT = 56 * 1024 * 1024


def _params(*sem):
    return pltpu.CompilerParams(dimension_semantics=sem, vmem_limit_bytes=VMEM_LIMIT)


def _layer_norm(x, g, b):
    mu = jnp.mean(x, axis=-1, keepdims=True)
    xc = x - mu
    var = jnp.mean(xc * xc, axis=-1, keepdims=True)
    return xc * lax.rsqrt(var + LN_EPS) * g + b


def _sigmoid(x):
    return 1.0 / (1.0 + jnp.exp(-x))


def _dot(a, b):
    return jnp.dot(a, b, preferred_element_type=F32)


def _pack_bf16_pair(a, b):
    ua = lax.bitcast_convert_type(a.astype(BF16).astype(F32), jnp.uint32)
    ub = lax.bitcast_convert_type(b.astype(BF16).astype(F32), jnp.uint32)
    return (ua >> 16) | ub


def _unpack_bf16_pair(p):
    a = lax.bitcast_convert_type(p << 16, F32)
    b = lax.bitcast_convert_type(p & jnp.uint32(0xFFFF0000), F32)
    return a, b


def _split_bf16(a):
    hi = a.astype(BF16)
    return hi, (a - hi.astype(F32)).astype(BF16)


def _ln0_kernel(x_ref, g_ref, b_ref, o_ref):
    o_ref[...] = _layer_norm(x_ref[...], g_ref[...], b_ref[...]).astype(o_ref.dtype)


def _ln0(x2, g, b):
    t, d = x2.shape
    tm = min(ROW_TILE, t)
    return pl.pallas_call(
        _ln0_kernel,
        out_shape=jax.ShapeDtypeStruct((t, d), BF16),
        grid=(t // tm,),
        in_specs=[pl.BlockSpec((tm, d), lambda i: (i, 0)),
                  pl.BlockSpec((1, d), lambda i: (0, 0)),
                  pl.BlockSpec((1, d), lambda i: (0, 0))],
        out_specs=pl.BlockSpec((tm, d), lambda i: (i, 0)),
        compiler_params=_params("parallel"),
        name="ln0",
    )(x2, g.reshape(1, d), b.reshape(1, d))


def _proj_kernel(a_ref, w_ref, o_ref, *, tn):
    a = a_ref[...]
    for n0 in range(0, o_ref.shape[1], tn):
        o_ref[:, n0:n0 + tn] = _dot(a, w_ref[:, n0:n0 + tn]).astype(o_ref.dtype)


def _proj(h0, w):
    t, d = h0.shape
    n = w.shape[1]
    tm = min(ROW_TILE, t)
    return pl.pallas_call(
        functools.partial(_proj_kernel, tn=512),
        out_shape=jax.ShapeDtypeStruct((t, n), BF16),
        grid=(t // tm,),
        in_specs=[pl.BlockSpec((tm, d), lambda i: (i, 0)),
                  pl.BlockSpec((d, n), lambda i: (0, 0))],
        out_specs=pl.BlockSpec((tm, n), lambda i: (i, 0)),
        compiler_params=_params("parallel"),
        name="proj_qkvz",
    )(h0, w)


def _glu_kernel(a_ref, wa_ref, wb_ref, ba_ref, bb_ref, o_ref, *, tn):
    a = a_ref[...]
    for n0 in range(0, o_ref.shape[1], tn):
        lin = _dot(a, wa_ref[:, n0:n0 + tn]) + ba_ref[:, n0:n0 + tn]
        gate = _dot(a, wb_ref[:, n0:n0 + tn]) + bb_ref[:, n0:n0 + tn]
        o_ref[:, n0:n0 + tn] = (lin * _sigmoid(gate)).astype(o_ref.dtype)


def _proj_glu(h0, wa, wb, ba, bb):
    t, d = h0.shape
    n = wa.shape[1]
    tm = min(ROW_TILE, t)
    return pl.pallas_call(
        functools.partial(_glu_kernel, tn=512),
        out_shape=jax.ShapeDtypeStruct((t, n), BF16),
        grid=(t // tm,),
        in_specs=[pl.BlockSpec((tm, d), lambda i: (i, 0)),
                  pl.BlockSpec((d, n), lambda i: (0, 0)),
                  pl.BlockSpec((d, n), lambda i: (0, 0)),
                  pl.BlockSpec((1, n), lambda i: (0, 0)),
                  pl.BlockSpec((1, n), lambda i: (0, 0))],
        out_specs=pl.BlockSpec((tm, n), lambda i: (i, 0)),
        compiler_params=_params("parallel"),
        name="proj_glu",
    )(h0, wa, wb, ba.reshape(1, n), bb.reshape(1, n))


def _gate_kernel(a_ref, w_ref, b_ref, o_ref, *, tn):
    a = a_ref[...]
    for n0 in range(0, o_ref.shape[1], tn):
        o_ref[:, n0:n0 + tn] = _sigmoid(_dot(a, w_ref[:, n0:n0 + tn]) + b_ref[:, n0:n0 + tn]).astype(o_ref.dtype)


def _proj_gate(h0, w, b):
    t, d = h0.shape
    n = w.shape[1]
    tm = min(ROW_TILE, t)
    return pl.pallas_call(
        functools.partial(_gate_kernel, tn=512),
        out_shape=jax.ShapeDtypeStruct((t, n), BF16),
        grid=(t // tm,),
        in_specs=[pl.BlockSpec((tm, d), lambda i: (i, 0)),
                  pl.BlockSpec((d, n), lambda i: (0, 0)),
                  pl.BlockSpec((1, n), lambda i: (0, 0))],
        out_specs=pl.BlockSpec((tm, n), lambda i: (i, 0)),
        compiler_params=_params("parallel"),
        name="proj_gate",
    )(h0, w, b.reshape(1, n))


def _chunk_cumsum(x, reverse):
    rows = x.shape[0]
    pos = lax.broadcasted_iota(jnp.int32, x.shape, 0) % CHUNK
    s = 1
    while s < CHUNK:
        if reverse:
            shifted = pltpu.roll(x, rows - s, axis=0)
            x = x + jnp.where(pos < CHUNK - s, shifted, 0.0)
        else:
            shifted = pltpu.roll(x, s, axis=0)
            x = x + jnp.where(pos >= s, shifted, 0.0)
        s *= 2
    return x


def _dn_gates_kernel(a_ref, w_ref, alog_ref, dtb_ref, g_ref, gt_ref):
    u = _dot(a_ref[...], w_ref[...])
    lane = lax.broadcasted_iota(jnp.int32, u.shape, 1)
    beta = _sigmoid(u)
    x = u + dtb_ref[...]
    softplus = jnp.maximum(x, 0.0) + jnp.log(1.0 + jnp.exp(-jnp.abs(x)))
    log_a = -jnp.exp(alog_ref[...]) * softplus
    g_fwd = _chunk_cumsum(log_a, reverse=False)
    g_bwd = _chunk_cumsum(log_a, reverse=True)
    out = jnp.where(lane < 2 * N_HEADS, beta, jnp.where(lane < 3 * N_HEADS, g_fwd, g_bwd))
    g_ref[...] = out
    gt_ref[...] = out.T[2 * N_HEADS:4 * N_HEADS, :]


def _dn_gates(h0, w_small, a_log, dt_bias):
    t, d = h0.shape
    tm = min(ROW_TILE, t)
    pad = LANES - 4 * N_HEADS
    w = jnp.pad(w_small, ((0, 0), (0, pad)))
    alog = jnp.pad(a_log.reshape(1, 2 * N_HEADS), ((0, 0), (2 * N_HEADS, LANES - 4 * N_HEADS)))
    dtb = jnp.pad(dt_bias.reshape(1, 2 * N_HEADS), ((0, 0), (2 * N_HEADS, LANES - 4 * N_HEADS)))
    return pl.pallas_call(
        _dn_gates_kernel,
        out_shape=(jax.ShapeDtypeStruct((t, LANES), F32),
                   jax.ShapeDtypeStruct((2 * N_HEADS, t), F32)),
        grid=(t // tm,),
        in_specs=[pl.BlockSpec((tm, d), lambda i: (i, 0)),
                  pl.BlockSpec((d, LANES), lambda i: (0, 0)),
                  pl.BlockSpec((1, LANES), lambda i: (0, 0)),
                  pl.BlockSpec((1, LANES), lambda i: (0, 0))],
        out_specs=(pl.BlockSpec((tm, LANES), lambda i: (i, 0)),
                   pl.BlockSpec((2 * N_HEADS, tm), lambda i: (0, i))),
        compiler_params=_params("parallel"),
        name="dn_gates",
    )(h0, w, alog, dtb)


def _conv_rows(xp_ref, w, taps, base, r0, rows):
    acc = xp_ref[base + r0:base + r0 + rows, :] * w[0:1, :]
    for k in range(1, taps):
        acc = acc + xp_ref[base + k + r0:base + k + r0 + rows, :] * w[k:k + 1, :]
    return acc


def _fill_padded(xp_ref, x_ref, pad, seq):
    zeros = jnp.zeros((pad, xp_ref.shape[1]), F32)
    xp_ref[0:pad, :] = zeros
    xp_ref[pad + seq:pad + seq + pad, :] = zeros
    xp_ref[pad:pad + seq, :] = x_ref[...].astype(F32)


def _qkv_conv_kernel(u_ref, w_ref, o_ref, xp_ref, *, seq, rows):
    pad = 8
    j = pl.program_id(1)
    _fill_padded(xp_ref, u_ref, pad, seq)
    w = w_ref[...]
    is_q = j < N_HEADS
    is_qk = j < 2 * N_HEADS
    for r0 in range(0, seq, rows):
        y = _conv_rows(xp_ref, w, SHORT_CONV, pad - SHORT_CONV // 2, r0, rows)
        y = y * _sigmoid(y)
        inv = lax.rsqrt(jnp.sum(y * y, axis=-1, keepdims=True) + L2_EPS)
        scale = jnp.where(is_q, inv * (HEAD_DIM ** -0.5), jnp.where(is_qk, inv, 1.0))
        o_ref[r0:r0 + rows, :] = (y * scale).astype(o_ref.dtype)


def _qkv_conv(u_qkvz, conv_w, bsz, seq):
    t = bsz * seq
    ncol = 3 * N_HEADS
    rows = min(256, seq)
    return pl.pallas_call(
        functools.partial(_qkv_conv_kernel, seq=seq, rows=rows),
        out_shape=jax.ShapeDtypeStruct((t, 3 * WIDTH_A), BF16),
        grid=(bsz, ncol),
        in_specs=[pl.BlockSpec((seq, HEAD_DIM), lambda b, j: (b, j)),
                  pl.BlockSpec((SHORT_CONV, HEAD_DIM), lambda b, j: (0, j))],
        out_specs=pl.BlockSpec((seq, HEAD_DIM), lambda b, j: (b, j)),
        scratch_shapes=[pltpu.VMEM((seq + 16, HEAD_DIM), F32)],
        compiler_params=_params("parallel", "parallel"),
        name="qkv_conv",
    )(u_qkvz, conv_w)


def _dw_conv_kernel(x_ref, w_ref, b_ref, o_ref, xp_ref, *, seq, rows):
    pad = 16
    _fill_padded(xp_ref, x_ref, pad, seq)
    w = w_ref[...]
    for r0 in range(0, seq, rows):
        y = _conv_rows(xp_ref, w, DW_CONV, pad - DW_CONV // 2, r0, rows) + b_ref[...]
        o_ref[r0:r0 + rows, :] = y.astype(o_ref.dtype)


def _dw_conv(glu, conv_w, b_dw, bsz, seq):
    t = bsz * seq
    rows = min(256, seq)
    return pl.pallas_call(
        functools.partial(_dw_conv_kernel, seq=seq, rows=rows),
        out_shape=jax.ShapeDtypeStruct((t, WIDTH_B), BF16),
        grid=(bsz, WIDTH_B // LANES),
        in_specs=[pl.BlockSpec((seq, LANES), lambda b, j: (b, j)),
                  pl.BlockSpec((DW_CONV, LANES), lambda b, j: (0, j)),
                  pl.BlockSpec((1, LANES), lambda b, j: (0, j))],
        out_specs=pl.BlockSpec((seq, LANES), lambda b, j: (b, j)),
        scratch_shapes=[pltpu.VMEM((seq + 32, LANES), F32)],
        compiler_params=_params("parallel", "parallel"),
        name="dw_conv",
    )(glu, conv_w, b_dw.reshape(1, WIDTH_B))


def _bmm(a, b):
    return lax.dot_general(a, b, (((2,), (1,)), ((0,), (0,))), preferred_element_type=F32)


def _bmm_nt(a, b):
    return lax.dot_general(a, b, (((2,), (2,)), ((0,), (0,))), preferred_element_type=F32)


def _bmm_tn(a, b):
    return lax.dot_general(a, b, (((1,), (1,)), ((0,), (0,))), preferred_element_type=F32)


def _block_diag_rows(x, half):
    lane = lax.broadcasted_iota(jnp.int32, x.shape, 2)
    return jnp.concatenate([jnp.where(lane < half, x, 0.0), jnp.where(lane >= half, x, 0.0)], axis=1)


def _unit_tri_inverse(lmat, eye):
    def rhs(p):
        return _block_diag_rows(p, CHUNK).astype(BF16)

    x = eye - lmat
    p = _bmm(lmat.astype(BF16), rhs(lmat))
    s = 2
    while 2 * s < CHUNK:
        xp = _bmm(jnp.concatenate([x, p], axis=1).astype(BF16), rhs(p))
        x = x + xp[:, :CHUNK]
        p = xp[:, CHUNK:]
        s *= 2
    return x + _bmm(x.astype(BF16), rhs(p))


def _delta_kernel(q_ref, k_ref, v_ref, g_ref, gtp_ref, o_ref, s_ref, *, reverse, nc):
    @pl.when(pl.program_id(1) == 0)
    def _():
        s_ref[...] = jnp.zeros_like(s_ref)

    n_pairs = N_HEADS // 2
    pw = 2 * HEAD_DIM
    dir_off = N_HEADS if reverse else 0
    ri = lax.broadcasted_iota(jnp.int32, (CHUNK, 2 * CHUNK), 0)
    ci = jnp.bitwise_and(lax.broadcasted_iota(jnp.int32, (CHUNK, 2 * CHUNK), 1), CHUNK - 1)
    incl = (ri <= ci) if reverse else (ri >= ci)
    strict = (ri < ci) if reverse else (ri > ci)
    eye = (ri == ci).astype(F32)
    last = 0 if reverse else CHUNK - 1

    order = list(range(nc - 1, -1, -1) if reverse else range(nc))

    def pairs(ref):
        return jnp.stack([ref[c * CHUNK:(c + 1) * CHUNK, p * pw:(p + 1) * pw]
                          for c in order for p in range(n_pairs)]).astype(F32)

    def pair_bcast(cols, width):
        return jnp.stack([jnp.concatenate([jnp.broadcast_to(cc[2 * p], (CHUNK, width)),
                                           jnp.broadcast_to(cc[2 * p + 1], (CHUNK, width))], axis=1)
                          for cc in cols for p in range(n_pairs)])

    kf = pairs(k_ref)
    qf = pairs(q_ref)
    vf = pairs(v_ref)
    gblks = [g_ref[c * CHUNK:(c + 1) * CHUNK, :] for c in order]
    beta_c = [[gb[:, dir_off + hh:dir_off + hh + 1] for hh in range(N_HEADS)] for gb in gblks]
    g_c = [[gb[:, 2 * N_HEADS + dir_off + hh:2 * N_HEADS + dir_off + hh + 1] for hh in range(N_HEADS)]
           for gb in gblks]
    glast_c = [[gc[last:last + 1, :] for gc in gcs] for gcs in g_c]
    beta = pair_bcast(beta_c, HEAD_DIM)
    eg = pair_bcast([[jnp.exp(gc) for gc in gcs] for gcs in g_c], HEAD_DIM)
    tail = pair_bcast([[jnp.exp(gl - gc) for gl, gc in zip(gls, gcs)] for gls, gcs in zip(glast_c, g_c)],
                      HEAD_DIM)
    gcol = pair_bcast(g_c, CHUNK)
    grow = jnp.stack([gtp_ref[dir_off // 2 + p:dir_off // 2 + p + 1, 2 * c * CHUNK:2 * (c + 1) * CHUNK]
                      for c in order for p in range(n_pairs)])

    decay = jnp.exp(jnp.where(incl, gcol - grow, NEG_BIG))
    kb = kf * beta
    kkqk = _bmm_nt(jnp.concatenate([kb, qf], axis=1).astype(BF16),
                   _block_diag_rows(kf, HEAD_DIM).astype(BF16))
    lmat = jnp.where(strict, kkqk[:, :CHUNK, :] * decay, 0.0)
    qk = (kkqk[:, CHUNK:, :] * decay).astype(BF16)
    tinv = _unit_tri_inverse(lmat, eye)
    rhs = jnp.concatenate([_block_diag_rows(vf * beta, HEAD_DIM),
                           _block_diag_rows(kb * eg, HEAD_DIM)], axis=2).astype(BF16)
    uw = _bmm(tinv.astype(BF16), rhs)
    u = uw[:, :, :pw]
    wq = jnp.concatenate([uw[:, :, pw:], qf * eg], axis=1).astype(BF16)
    kt = (kf * tail).astype(BF16)

    for i, c in enumerate(order):
        sl = slice(i * n_pairs, (i + 1) * n_pairs)
        s_a = s_ref[:, 0]
        s_b = s_ref[:, 1]
        zero = jnp.zeros_like(s_a)
        s_bd = jnp.concatenate([jnp.concatenate([s_a, zero], axis=2),
                                jnp.concatenate([zero, s_b], axis=2)], axis=1).astype(BF16)
        ws = _bmm(wq[sl], s_bd)
        v_new = u[sl] - ws[:, :CHUNK, :]
        o = ws[:, CHUNK:, :] + _bmm(qk[sl], _block_diag_rows(v_new, HEAD_DIM).astype(BF16))
        v16 = v_new.astype(BF16)
        cd_a = jnp.stack([jnp.exp(glast_c[i][2 * p]) for p in range(n_pairs)])
        cd_b = jnp.stack([jnp.exp(glast_c[i][2 * p + 1]) for p in range(n_pairs)])
        s_ref[:, 0] = s_a * cd_a + _bmm_tn(kt[sl, :, :HEAD_DIM], v16[:, :, :HEAD_DIM])
        s_ref[:, 1] = s_b * cd_b + _bmm_tn(kt[sl, :, HEAD_DIM:], v16[:, :, HEAD_DIM:])
        for p in range(n_pairs):
            o_ref[c * CHUNK:(c + 1) * CHUNK, p * pw:(p + 1) * pw] = o[p].astype(o_ref.dtype)


def _delta_rule(qkv, g, gtp, bsz, seq, reverse):
    t = bsz * seq
    rows = min(DELTA_ROWS, seq)
    nblk = seq // rows

    def rb(b, i):
        return b * nblk + ((nblk - 1 - i) if reverse else i)

    return pl.pallas_call(
        functools.partial(_delta_kernel, reverse=reverse, nc=rows // CHUNK),
        out_shape=jax.ShapeDtypeStruct((t, WIDTH_A), BF16),
        grid=(bsz, nblk),
        in_specs=[pl.BlockSpec((rows, WIDTH_A), lambda b, i: (rb(b, i), 0)),
                  pl.BlockSpec((rows, WIDTH_A), lambda b, i: (rb(b, i), 1)),
                  pl.BlockSpec((rows, WIDTH_A), lambda b, i: (rb(b, i), 2)),
                  pl.BlockSpec((rows, LANES), lambda b, i: (rb(b, i), 0)),
                  pl.BlockSpec((N_HEADS, 2 * rows), lambda b, i: (0, rb(b, i)))],
        out_specs=pl.BlockSpec((rows, WIDTH_A), lambda b, i: (rb(b, i), 0)),
        scratch_shapes=[pltpu.VMEM((N_HEADS // 2, 2, HEAD_DIM, HEAD_DIM), F32)],
        compiler_params=_params("parallel", "arbitrary"),
        name="delta_bwd" if reverse else "delta_fwd",
    )(qkv, qkv, qkv, g, gtp)


def _mix_kernel(of_ref, ob_ref, z_ref, yc_ref, gate_ref, x_ref,
                eg_ref, eb_ref, ng_ref, wao_ref, cg_ref, cb_ref, wbo_ref, bbo_ref,
                wout_ref, l1g_ref, l1b_ref, wr_ref, br_ref,
                h1_ref, h1p_ref, logit_ref):
    o = of_ref[...].astype(F32) + ob_ref[...].astype(F32)
    z = z_ref[...].astype(F32)
    ng = ng_ref[...]
    parts = []
    for hh in range(N_HEADS):
        sl = slice(hh * HEAD_DIM, (hh + 1) * HEAD_DIM)
        oh = o[:, sl]
        zh = z[:, sl]
        inv = lax.rsqrt(jnp.mean(oh * oh, axis=-1, keepdims=True) + RMS_EPS)
        parts.append((oh * inv * ng * (zh * _sigmoid(zh))).astype(BF16))
    y_a = _dot(jnp.concatenate(parts, axis=1), wao_ref[...])

    yc = _layer_norm(yc_ref[...].astype(F32), cg_ref[...], cb_ref[...])
    y_b = _dot((yc * _sigmoid(yc)).astype(BF16), wbo_ref[...]) + bbo_ref[...]

    gates = gate_ref[...].astype(F32)
    mixed = gates[:, :D_MODEL] * y_a + gates[:, D_MODEL:] * y_b
    mix = _dot(mixed.astype(BF16), wout_ref[...])

    h0 = _layer_norm(x_ref[...], eg_ref[...], eb_ref[...])
    h1 = _layer_norm(DN_ALPHA * h0 + mix, l1g_ref[...], l1b_ref[...])
    h1_ref[...] = h1
    h1p_ref[...] = _pack_bf16_pair(h1[:, :D_MODEL // 2], h1[:, D_MODEL // 2:])
    h_hi, h_lo = _split_bf16(h1)
    p = _dot(h_hi, wr_ref[...])
    logit_ref[...] = p[:, :LANES] + p[:, LANES:] + _dot(h_lo, wr_ref[:, :LANES]) + br_ref[...]


def _mix(o_f, o_b, u_qkvz, yc, gates, x2, emb_g, emb_b, norm_g, w_a_o, cg, cb, w_b_o, b_b_o,
         w_out, l1g, l1b, w_router, b_router):
    t, d = x2.shape
    tm = min(ROW_TILE, t)
    row = lambda i: (i, 0)
    const = lambda i: (0, 0)
    wr = jnp.concatenate(_split_bf16(jnp.pad(w_router, ((0, 0), (0, LANES - N_EXPERTS)))), axis=1)
    br = jnp.pad(b_router.reshape(1, N_EXPERTS), ((0, 0), (0, LANES - N_EXPERTS)), constant_values=NEG_BIG)
    vec = lambda a: a.reshape(1, -1)
    return pl.pallas_call(
        _mix_kernel,
        out_shape=(jax.ShapeDtypeStruct((t, d), F32), jax.ShapeDtypeStruct((t, d // 2), jnp.uint32),
                   jax.ShapeDtypeStruct((t, LANES), F32)),
        grid=(t // tm,),
        in_specs=[pl.BlockSpec((tm, d), row), pl.BlockSpec((tm, d), row),
                  pl.BlockSpec((tm, d), lambda i: (i, 3)),
                  pl.BlockSpec((tm, d), row), pl.BlockSpec((tm, 2 * d), row), pl.BlockSpec((tm, d), row),
                  pl.BlockSpec((1, d), const), pl.BlockSpec((1, d), const),
                  pl.BlockSpec((1, HEAD_DIM), const), pl.BlockSpec((d, d), const),
                  pl.BlockSpec((1, d), const), pl.BlockSpec((1, d), const),
                  pl.BlockSpec((d, d), const), pl.BlockSpec((1, d), const),
                  pl.BlockSpec((d, d), const), pl.BlockSpec((1, d), const), pl.BlockSpec((1, d), const),
                  pl.BlockSpec((d, 2 * LANES), const), pl.BlockSpec((1, LANES), const)],
        out_specs=(pl.BlockSpec((tm, d), row), pl.BlockSpec((tm, d // 2), row), pl.BlockSpec((tm, LANES), row)),
        compiler_params=_params("parallel"),
        name="mix",
    )(o_f, o_b, u_qkvz, yc, gates, x2, vec(emb_g), vec(emb_b), vec(norm_g), w_a_o, vec(cg), vec(cb),
      w_b_o, vec(b_b_o), w_out, vec(l1g), vec(l1b), wr, br)


def _route_kernel(logit_ref, gate_ref, eidx_ref, rank_ref, cnt_ref, base_ref):
    @pl.when(pl.program_id(0) == 0)
    def _():
        base_ref[...] = jnp.zeros_like(base_ref)

    x = logit_ref[...]
    tm = x.shape[0]
    lane = lax.broadcasted_iota(jnp.int32, x.shape, 1)
    lane_f = lane.astype(F32)
    row = lax.broadcasted_iota(jnp.int32, x.shape, 0)
    sel = jnp.zeros(x.shape, F32)
    vals, idxs = [], []
    for _ in range(TOP_K):
        m = jnp.max(x, axis=1, keepdims=True)
        idx = jnp.min(jnp.where(x == m, lane_f, float(LANES)), axis=1, keepdims=True).astype(jnp.int32)
        hit = lane == idx
        sel = sel + hit.astype(F32)
        x = jnp.where(hit, -3e38, x)
        vals.append(m)
        idxs.append(idx)

    exps = [jnp.exp(v - vals[0]) for v in vals]
    denom = exps[0]
    for e in exps[1:]:
        denom = denom + e

    csum = sel
    s = 1
    while s < tm:
        csum = csum + jnp.where(row >= s, pltpu.roll(csum, s, axis=0), 0.0)
        s *= 2
    before = base_ref[...] + csum - sel

    gate = jnp.zeros(x.shape, F32)
    eidx = jnp.zeros(x.shape, jnp.int32)
    rank = jnp.zeros(x.shape, jnp.int32)
    for k in range(TOP_K):
        rk = jnp.sum(jnp.where(lane == idxs[k], before, 0.0), axis=1, keepdims=True)
        gate = jnp.where(lane == k, exps[k] / denom, gate)
        eidx = jnp.where(lane == k, idxs[k], eidx)
        rank = jnp.where(lane == k, rk.astype(jnp.int32), rank)
    gate_ref[...] = gate
    eidx_ref[...] = eidx
    rank_ref[...] = rank
    total = base_ref[...] + csum[tm - 1:tm, :]
    base_ref[...] = total
    cnt_ref[...] = total


def _route(logits):
    t = logits.shape[0]
    tm = min(ROW_TILE, t)
    bm = EXPERT_ROWS
    row = lambda i: (i, 0)
    gate, eidx, rank, cnt = pl.pallas_call(
        _route_kernel,
        out_shape=(jax.ShapeDtypeStruct((t, LANES), F32), jax.ShapeDtypeStruct((t, LANES), jnp.int32),
                   jax.ShapeDtypeStruct((t, LANES), jnp.int32), jax.ShapeDtypeStruct((1, LANES), F32)),
        grid=(t // tm,),
        in_specs=[pl.BlockSpec((tm, LANES), row)],
        out_specs=(pl.BlockSpec((tm, LANES), row), pl.BlockSpec((tm, LANES), row),
                   pl.BlockSpec((tm, LANES), row), pl.BlockSpec((1, LANES), lambda i: (0, 0))),
        scratch_shapes=[pltpu.VMEM((1, LANES), F32)],
        compiler_params=_params("arbitrary"),
        name="moe_route",
    )(logits)
    counts = cnt[0, :N_EXPERTS].astype(jnp.int32)
    padded = (counts + bm - 1) // bm * bm
    pad_end = jnp.cumsum(padded)
    pad_start = pad_end - padded
    pos = pad_start[eidx[:, :TOP_K]] + rank[:, :TOP_K]
    n_blocks = -(-(t * TOP_K + N_EXPERTS * (bm - 1)) // bm)
    block_start = jnp.arange(n_blocks, dtype=jnp.int32) * bm
    block_exp = jnp.minimum(jnp.sum((block_start[:, None] >= pad_end[None, :]).astype(jnp.int32), axis=1),
                            N_EXPERTS - 1)
    n_used = (pad_end[-1] // bm).astype(jnp.int32).reshape(1)
    return gate[:, :TOP_K], pos.astype(jnp.int32), block_exp, pad_start, pad_end, n_used, n_blocks


def _dispatch_kernel(ps_ref, pe_ref, nb_ref, idx_ref, h_ref, xs_hbm, zero_ref, sem, zsem, *, tm, n_blocks):
    bm = EXPERT_ROWS

    @pl.when(pl.program_id(0) == 0)
    def _():
        zero_ref[...] = jnp.zeros_like(zero_ref)

        def zero_block(start):
            return pltpu.make_async_copy(zero_ref, xs_hbm.at[pl.ds(pl.multiple_of(start, bm), bm), :], zsem)

        for e in range(N_EXPERTS):
            @pl.when(pe_ref[e] > ps_ref[e])
            def _():
                zero_block(pe_ref[e] - bm).start()
        for e in range(N_EXPERTS):
            @pl.when(pe_ref[e] > ps_ref[e])
            def _():
                zero_block(pe_ref[e] - bm).wait()

        def tail_start(b, carry):
            zero_block(b * bm).start()
            return carry

        def tail_wait(b, carry):
            zero_block(b * bm).wait()
            return carry

        lax.fori_loop(nb_ref[0], n_blocks, tail_start, 0)
        lax.fori_loop(nb_ref[0], n_blocks, tail_wait, 0)

    def issue(r, carry):
        for k in range(TOP_K):
            pltpu.make_async_copy(h_ref.at[pl.ds(r, 1), :],
                                  xs_hbm.at[pl.ds(idx_ref[0, 0, r * TOP_K + k], 1), :],
                                  sem).start(priority=k % 2)
        return carry

    lax.fori_loop(0, tm, issue, 0, unroll=GATHER_UNROLL // 2)
    for _ in range(TOP_K):
        pltpu.make_async_copy(h_ref, xs_hbm.at[pl.ds(0, tm), :], sem).wait()


def _dispatch(h1, pos, pad_start, pad_end, n_used, n_blocks):
    t, d = h1.shape
    tm = min(ROW_TILE, t)
    nt = t // tm
    n_slots = n_blocks * EXPERT_ROWS
    grid_spec = pltpu.PrefetchScalarGridSpec(
        num_scalar_prefetch=3,
        grid=(nt,),
        in_specs=[pl.BlockSpec((1, 1, TOP_K * tm), lambda i, *_: (i, 0, 0), memory_space=pltpu.SMEM),
                  pl.BlockSpec((tm, d), lambda i, *_: (i, 0))],
        out_specs=pl.BlockSpec(memory_space=pl.ANY),
        scratch_shapes=[pltpu.VMEM((EXPERT_ROWS, d), h1.dtype), pltpu.SemaphoreType.DMA(()),
                        pltpu.SemaphoreType.DMA(())],
    )
    return pl.pallas_call(
        functools.partial(_dispatch_kernel, tm=tm, n_blocks=n_blocks),
        out_shape=jax.ShapeDtypeStruct((n_slots, d), h1.dtype),
        grid_spec=grid_spec,
        compiler_params=_params("arbitrary"),
        name="moe_dispatch",
    )(pad_start, pad_end, n_used, pos.reshape(nt, 1, TOP_K * tm), h1)


def _expert_kernel(be_ref, nb_ref, x_ref, wgu_ref, bgu_ref, wd_ref, bd_ref, o_ref):
    @pl.when(pl.program_id(0) < nb_ref[0])
    def _():
        half = D_MODEL // 2
        x_lo, x_hi = _unpack_bf16_pair(x_ref[...])
        hgu = (_dot(x_lo.astype(BF16), wgu_ref[0, :half, :]) + _dot(x_hi.astype(BF16), wgu_ref[0, half:, :])
               + bgu_ref[0])
        glu = jnp.minimum(hgu[:, :D_FF], SWIGLU_LIMIT)
        lin = jnp.clip(hgu[:, D_FF:], -SWIGLU_LIMIT, SWIGLU_LIMIT)
        act = glu * _sigmoid(SWIGLU_ALPHA * glu) * (lin + 1.0)
        y = _dot(act.astype(BF16), wd_ref[0]) + bd_ref[0]
        o_ref[...] = _pack_bf16_pair(y[:, :half], y[:, half:])

    @pl.when(pl.program_id(0) >= nb_ref[0])
    def _():
        o_ref[...] = jnp.zeros_like(o_ref)


def _experts(xs, block_exp, n_used, w_gu, b_gu, w_down, b_down, n_blocks):
    d = D_MODEL
    dp = xs.shape[1]
    bm = EXPERT_ROWS
    grid_spec = pltpu.PrefetchScalarGridSpec(
        num_scalar_prefetch=2,
        grid=(n_blocks,),
        in_specs=[pl.BlockSpec((bm, dp), lambda i, be, nb: (i, 0)),
                  pl.BlockSpec((1, d, 2 * D_FF), lambda i, be, nb: (be[i], 0, 0)),
                  pl.BlockSpec((1, 1, 2 * D_FF), lambda i, be, nb: (be[i], 0, 0)),
                  pl.BlockSpec((1, D_FF, d), lambda i, be, nb: (be[i], 0, 0)),
                  pl.BlockSpec((1, 1, d), lambda i, be, nb: (be[i], 0, 0))],
        out_specs=pl.BlockSpec((bm, dp), lambda i, be, nb: (i, 0)),
    )
    return pl.pallas_call(
        _expert_kernel,
        out_shape=jax.ShapeDtypeStruct((n_blocks * bm, dp), jnp.uint32),
        grid_spec=grid_spec,
        compiler_params=_params("arbitrary"),
        name="moe_experts",
    )(block_exp, n_used, xs, w_gu, b_gu.reshape(N_EXPERTS, 1, 2 * D_FF), w_down,
      b_down.reshape(N_EXPERTS, 1, d))


def _combine_kernel(idx_ref, idx_next_ref, ys_hbm, gate_ref, h_ref, g_ref, b_ref, o_ref, buf_ref, sem, *, tm, nt):
    i = pl.program_id(0)
    n = TOP_K * tm
    slot = i % 2

    def gather(idx, s):
        def issue(r, carry):
            for j in range(2):
                rr = 2 * r + j
                pltpu.make_async_copy(ys_hbm.at[pl.ds(idx[0, 0, rr], 1), :],
                                      buf_ref.at[s, pl.ds(rr, 1), :], sem.at[s]).start(priority=j)
            return carry

        lax.fori_loop(0, n // 2, issue, 0, unroll=GATHER_UNROLL // 2)

    @pl.when(i == 0)
    def _():
        gather(idx_ref, 0)

    @pl.when(i + 1 < nt)
    def _():
        gather(idx_next_ref, 1 - slot)

    pltpu.make_async_copy(ys_hbm.at[pl.ds(0, n), :], buf_ref.at[slot], sem.at[slot]).wait()
    gate = gate_ref[...]
    f_lo = f_hi = None
    for k in range(TOP_K):
        y_lo, y_hi = _unpack_bf16_pair(buf_ref[slot, k * tm:(k + 1) * tm, :])
        gk = gate[:, k:k + 1]
        f_lo = gk * y_lo if f_lo is None else f_lo + gk * y_lo
        f_hi = gk * y_hi if f_hi is None else f_hi + gk * y_hi
    f = jnp.concatenate([f_lo, f_hi], axis=1)
    o_ref[...] = _layer_norm(DN_ALPHA * h_ref[...] + f, g_ref[...], b_ref[...])


def _combine(ys, pos, gate, h1, ln_g, ln_b):
    t, d = h1.shape
    tm = min(COMBINE_ROWS, t)
    nt = t // tm
    idx = pos.reshape(nt, tm, TOP_K).transpose(0, 2, 1).reshape(nt, 1, TOP_K * tm)
    return pl.pallas_call(
        functools.partial(_combine_kernel, tm=tm, nt=nt),
        out_shape=jax.ShapeDtypeStruct((t, d), F32),
        grid=(nt,),
        in_specs=[pl.BlockSpec((1, 1, TOP_K * tm), lambda i: (i, 0, 0), memory_space=pltpu.SMEM),
                  pl.BlockSpec((1, 1, TOP_K * tm), lambda i: (jnp.minimum(i + 1, nt - 1), 0, 0),
                               memory_space=pltpu.SMEM),
                  pl.BlockSpec(memory_space=pl.ANY),
                  pl.BlockSpec((tm, TOP_K), lambda i: (i, 0)),
                  pl.BlockSpec((tm, d), lambda i: (i, 0)),
                  pl.BlockSpec((1, d), lambda i: (0, 0)),
                  pl.BlockSpec((1, d), lambda i: (0, 0))],
        out_specs=pl.BlockSpec((tm, d), lambda i: (i, 0)),
        scratch_shapes=[pltpu.VMEM((2, TOP_K * tm, ys.shape[1]), ys.dtype), pltpu.SemaphoreType.DMA((2,))],
        compiler_params=_params("arbitrary"),
        name="moe_combine",
    )(idx, idx, ys, gate, h1, ln_g.reshape(1, d), ln_b.reshape(1, d))


def kernel(x, emb_ln_g, emb_ln_b, w_in, conv_qkv, a_log, dt_bias, dn_norm_g, w_a_o, b_glu, conv_dw, b_dw, conv_ln_g, conv_ln_b, w_b_o, b_b_o, b_gate, w_out, ln1_g, ln1_b, w_router, b_router, w_gu, b_gu, w_down, b_down, ln2_g, ln2_b):
    bsz, seq, d = x.shape
    t = bsz * seq
    x2 = x.reshape(t, d)
    wi = w_in[0]
    c0 = 4 * WIDTH_A
    c1 = c0 + 4 * N_HEADS
    c2 = c1 + 2 * WIDTH_B

    h0 = _ln0(x2, emb_ln_g, emb_ln_b)
    u_qkvz = _proj(h0, wi[:, :c0].astype(BF16))
    g, gt = _dn_gates(h0, wi[:, c0:c1].astype(BF16), a_log[0], dt_bias[0])
    glu = _proj_glu(h0, wi[:, c1:c1 + WIDTH_B].astype(BF16), wi[:, c1 + WIDTH_B:c2].astype(BF16),
                    b_glu[0, :WIDTH_B], b_glu[0, WIDTH_B:])
    gates = _proj_gate(h0, wi[:, c2:].astype(BF16), b_gate[0])

    qkv = _qkv_conv(u_qkvz, conv_qkv[0], bsz, seq)
    gtp = gt.reshape(2, N_HEADS // 2, 2, t // CHUNK, CHUNK).transpose(0, 1, 3, 2, 4).reshape(N_HEADS, 2 * t)
    o_f = _delta_rule(qkv, g, gtp, bsz, seq, reverse=False)
    o_b = _delta_rule(qkv, g, gtp, bsz, seq, reverse=True)
    yc = _dw_conv(glu, conv_dw[0], b_dw[0], bsz, seq)

    h1, h1p, logits = _mix(o_f, o_b, u_qkvz, yc, gates, x2, emb_ln_g, emb_ln_b, dn_norm_g[0],
                      w_a_o[0].astype(BF16), conv_ln_g[0], conv_ln_b[0], w_b_o[0].astype(BF16), b_b_o[0],
                      w_out[0].astype(BF16), ln1_g[0], ln1_b[0], w_router[0], b_router[0])

    gate, pos, block_exp, pad_start, pad_end, n_used, n_blocks = _route(logits)
    xs = _dispatch(h1p, pos, pad_start, pad_end, n_used, n_blocks)
    ys = _experts(xs, block_exp, n_used, w_gu[0].astype(BF16), b_gu[0], w_down[0].astype(BF16), b_down[0],
                  n_blocks)
    out = _combine(ys, pos, gate, h1, ln2_g[0], ln2_b[0])
    return out.reshape(bsz, seq, d)
```

```python
import functools

import jax
import jax.numpy as jnp
from jax import lax
from jax.experimental import pallas as pl
from jax.experimental.pallas import tpu as pltpu

F32 = jnp.float32
BF16 = jnp.bfloat16

D_MODEL = 1024
N_HEADS = 8
HEAD_DIM = 128
WIDTH_A = N_HEADS * HEAD_DIM
SHORT_CONV = 5
CHUNK = 64
WIDTH_B = D_MODEL
DW_CONV = 31
N_EXPERTS = 32
TOP_K = 4
D_FF = D_MODEL
SWIGLU_ALPHA = 1.702
SWIGLU_LIMIT = 7.0
DN_ALPHA = 2.0 ** 0.25
LN_EPS = 1e-5
RMS_EPS = 1e-6
L2_EPS = 1e-6
LANES = 128
NEG_BIG = -1e30

ROW_TILE = 512
DELTA_ROWS = 256
EXPERT_ROWS = 512
COMBINE_ROWS = 128
GATHER_UNROLL = 8
VMEM_LIMIT = 56 * 1024 * 1024


def _params(*sem):
    return pltpu.CompilerParams(dimension_semantics=sem, vmem_limit_bytes=VMEM_LIMIT)


def _layer_norm(x, g, b):
    mu = jnp.mean(x, axis=-1, keepdims=True)
    xc = x - mu
    var = jnp.mean(xc * xc, axis=-1, keepdims=True)
    return xc * lax.rsqrt(var + LN_EPS) * g + b


def _sigmoid(x):
    return 1.0 / (1.0 + jnp.exp(-x))


def _dot(a, b):
    return jnp.dot(a, b, preferred_element_type=F32)


def _pack_bf16_pair(a, b):
    ua = lax.bitcast_convert_type(a.astype(BF16).astype(F32), jnp.uint32)
    ub = lax.bitcast_convert_type(b.astype(BF16).astype(F32), jnp.uint32)
    return (ua >> 16) | ub


def _unpack_bf16_pair(p):
    a = lax.bitcast_convert_type(p << 16, F32)
    b = lax.bitcast_convert_type(p & jnp.uint32(0xFFFF0000), F32)
    return a, b


def _split_bf16(a):
    hi = a.astype(BF16)
    return hi, (a - hi.astype(F32)).astype(BF16)


def _chunk_cumsum(x, reverse):
    rows = x.shape[0]
    pos = lax.broadcasted_iota(jnp.int32, x.shape, 0) % CHUNK
    s = 1
    while s < CHUNK:
        if reverse:
            shifted = pltpu.roll(x, rows - s, axis=0)
            x = x + jnp.where(pos < CHUNK - s, shifted, 0.0)
        else:
            shifted = pltpu.roll(x, s, axis=0)
            x = x + jnp.where(pos >= s, shifted, 0.0)
        s *= 2
    return x


def _inproj_kernel(x_ref, eg_ref, eb_ref, wq_ref, ws_ref, wga_ref, wgb_ref, bga_ref, bgb_ref, wgt_ref, bgt_ref,
                   alog_ref, dtb_ref, u_ref, g_ref, gt_ref, glu_ref, gate_ref, *, tn):
    h = _layer_norm(x_ref[...], eg_ref[...], eb_ref[...]).astype(BF16)

    for n0 in range(0, u_ref.shape[1], tn):
        u_ref[:, n0:n0 + tn] = _dot(h, wq_ref[:, n0:n0 + tn]).astype(u_ref.dtype)

    us = _dot(h, ws_ref[...])
    lane = lax.broadcasted_iota(jnp.int32, us.shape, 1)
    beta = _sigmoid(us)
    xs = us + dtb_ref[...]
    softplus = jnp.maximum(xs, 0.0) + jnp.log(1.0 + jnp.exp(-jnp.abs(xs)))
    log_a = -jnp.exp(alog_ref[...]) * softplus
    g_fwd = _chunk_cumsum(log_a, reverse=False)
    g_bwd = _chunk_cumsum(log_a, reverse=True)
    gates = jnp.where(lane < 2 * N_HEADS, beta, jnp.where(lane < 3 * N_HEADS, g_fwd, g_bwd))
    g_ref[...] = gates
    gt_ref[...] = gates.T[2 * N_HEADS:4 * N_HEADS, :]

    for n0 in range(0, glu_ref.shape[1], tn):
        lin = _dot(h, wga_ref[:, n0:n0 + tn]) + bga_ref[:, n0:n0 + tn]
        gt = _dot(h, wgb_ref[:, n0:n0 + tn]) + bgb_ref[:, n0:n0 + tn]
        glu_ref[:, n0:n0 + tn] = (lin * _sigmoid(gt)).astype(glu_ref.dtype)

    for n0 in range(0, gate_ref.shape[1], tn):
        gate_ref[:, n0:n0 + tn] = _sigmoid(_dot(h, wgt_ref[:, n0:n0 + tn])
                                           + bgt_ref[:, n0:n0 + tn]).astype(gate_ref.dtype)


def _inproj(x2, emb_g, emb_b, w_in, b_glu, b_gate, a_log, dt_bias):
    t, d = x2.shape
    tm = min(ROW_TILE, t)
    c0 = 4 * WIDTH_A
    c1 = c0 + 4 * N_HEADS
    c2 = c1 + 2 * WIDTH_B
    wb = w_in.astype(BF16)
    pad = LANES - 4 * N_HEADS
    w_small = jnp.pad(wb[:, c0:c1], ((0, 0), (0, pad)))
    alog = jnp.pad(a_log.reshape(1, 2 * N_HEADS), ((0, 0), (2 * N_HEADS, pad)))
    dtb = jnp.pad(dt_bias.reshape(1, 2 * N_HEADS), ((0, 0), (2 * N_HEADS, pad)))
    row = lambda i: (i, 0)

    def const(shape):
        return pl.BlockSpec(shape, lambda i: (0, 0), pipeline_mode=pl.Buffered(1))

    vec = lambda a: a.reshape(1, -1)
    bf = lambda n: jax.ShapeDtypeStruct((t, n), BF16)
    return pl.pallas_call(
        functools.partial(_inproj_kernel, tn=512),
        out_shape=(bf(c0), jax.ShapeDtypeStruct((t, LANES), F32), jax.ShapeDtypeStruct((2 * N_HEADS, t), F32),
                   bf(WIDTH_B), bf(2 * d)),
        grid=(t // tm,),
        in_specs=[pl.BlockSpec((tm, d), row), const((1, d)), const((1, d)),
                  const((d, c0)), const((d, LANES)),
                  const((d, WIDTH_B)), const((d, WIDTH_B)), const((1, WIDTH_B)), const((1, WIDTH_B)),
                  const((d, 2 * d)), const((1, 2 * d)), const((1, LANES)), const((1, LANES))],
        out_specs=(pl.BlockSpec((tm, c0), row), pl.BlockSpec((tm, LANES), row),
                   pl.BlockSpec((2 * N_HEADS, tm), lambda i: (0, i)),
                   pl.BlockSpec((tm, WIDTH_B), row), pl.BlockSpec((tm, 2 * d), row)),
        compiler_params=_params("parallel"),
        name="inproj",
    )(x2, vec(emb_g), vec(emb_b), wb[:, :c0], w_small, wb[:, c1:c1 + WIDTH_B], wb[:, c1 + WIDTH_B:c2],
      vec(b_glu[:WIDTH_B]), vec(b_glu[WIDTH_B:]), wb[:, c2:], vec(b_gate), alog, dtb)


def _conv_rows(xp_ref, w, taps, base, r0, rows):
    acc = xp_ref[base + r0:base + r0 + rows, :] * w[0:1, :]
    for k in range(1, taps):
        acc = acc + xp_ref[base + k + r0:base + k + r0 + rows, :] * w[k:k + 1, :]
    return acc


def _fill_padded(xp_ref, x_ref, pad, seq):
    zeros = jnp.zeros((pad, xp_ref.shape[1]), F32)
    xp_ref[0:pad, :] = zeros
    xp_ref[pad + seq:pad + seq + pad, :] = zeros
    xp_ref[pad:pad + seq, :] = x_ref[...].astype(F32)


def _qkv_conv_kernel(u_ref, w_ref, o_ref, xp_ref, *, seq, rows):
    pad = 8
    j = pl.program_id(1)
    _fill_padded(xp_ref, u_ref, pad, seq)
    w = w_ref[...]
    is_q = j < N_HEADS
    is_qk = j < 2 * N_HEADS
    for r0 in range(0, seq, rows):
        y = _conv_rows(xp_ref, w, SHORT_CONV, pad - SHORT_CONV // 2, r0, rows)
        y = y * _sigmoid(y)
        inv = lax.rsqrt(jnp.sum(y * y, axis=-1, keepdims=True) + L2_EPS)
        scale = jnp.where(is_q, inv * (HEAD_DIM ** -0.5), jnp.where(is_qk, inv, 1.0))
        o_ref[r0:r0 + rows, :] = (y * scale).astype(o_ref.dtype)


def _qkv_conv(u_qkvz, conv_w, bsz, seq):
    t = bsz * seq
    ncol = 3 * N_HEADS
    rows = min(256, seq)
    return pl.pallas_call(
        functools.partial(_qkv_conv_kernel, seq=seq, rows=rows),
        out_shape=jax.ShapeDtypeStruct((t, 3 * WIDTH_A), BF16),
        grid=(bsz, ncol),
        in_specs=[pl.BlockSpec((seq, HEAD_DIM), lambda b, j: (b, j)),
                  pl.BlockSpec((SHORT_CONV, HEAD_DIM), lambda b, j: (0, j))],
        out_specs=pl.BlockSpec((seq, HEAD_DIM), lambda b, j: (b, j)),
        scratch_shapes=[pltpu.VMEM((seq + 16, HEAD_DIM), F32)],
        compiler_params=_params("parallel", "parallel"),
        name="qkv_conv",
    )(u_qkvz, conv_w)


def _dw_conv_kernel(x_ref, w_ref, b_ref, o_ref, xp_ref, *, seq, rows):
    pad = 16
    _fill_padded(xp_ref, x_ref, pad, seq)
    w = w_ref[...]
    for r0 in range(0, seq, rows):
        y = _conv_rows(xp_ref, w, DW_CONV, pad - DW_CONV // 2, r0, rows) + b_ref[...]
        o_ref[r0:r0 + rows, :] = y.astype(o_ref.dtype)


def _dw_conv(glu, conv_w, b_dw, bsz, seq):
    t = bsz * seq
    rows = min(256, seq)
    return pl.pallas_call(
        functools.partial(_dw_conv_kernel, seq=seq, rows=rows),
        out_shape=jax.ShapeDtypeStruct((t, WIDTH_B), BF16),
        grid=(bsz, WIDTH_B // LANES),
        in_specs=[pl.BlockSpec((seq, LANES), lambda b, j: (b, j)),
                  pl.BlockSpec((DW_CONV, LANES), lambda b, j: (0, j)),
                  pl.BlockSpec((1, LANES), lambda b, j: (0, j))],
        out_specs=pl.BlockSpec((seq, LANES), lambda b, j: (b, j)),
        scratch_shapes=[pltpu.VMEM((seq + 32, LANES), F32)],
        compiler_params=_params("parallel", "parallel"),
        name="dw_conv",
    )(glu, conv_w, b_dw.reshape(1, WIDTH_B))


def _bmm(a, b):
    return lax.dot_general(a, b, (((2,), (1,)), ((0,), (0,))), preferred_element_type=F32)


def _bmm_nt(a, b):
    return lax.dot_general(a, b, (((2,), (2,)), ((0,), (0,))), preferred_element_type=F32)


def _bmm_tn(a, b):
    return lax.dot_general(a, b, (((1,), (1,)), ((0,), (0,))), preferred_element_type=F32)


def _block_diag_rows(x, half):
    lane = lax.broadcasted_iota(jnp.int32, x.shape, 2)
    return jnp.concatenate([jnp.where(lane < half, x, 0.0), jnp.where(lane >= half, x, 0.0)], axis=1)


def _unit_tri_inverse(lmat, eye):
    def rhs(p):
        return _block_diag_rows(p, CHUNK).astype(BF16)

    x = eye - lmat
    p = _bmm(lmat.astype(BF16), rhs(lmat))
    s = 2
    while 2 * s < CHUNK:
        xp = _bmm(jnp.concatenate([x, p], axis=1).astype(BF16), rhs(p))
        x = x + xp[:, :CHUNK]
        p = xp[:, CHUNK:]
        s *= 2
    return x + _bmm(x.astype(BF16), rhs(p))


def _delta_kernel(qf_ref, kf_ref, vf_ref, gf_ref, gtpf_ref, qb_ref, kb_ref, vb_ref, gb_ref, gtpb_ref,
                  of_ref, ob_ref, s_ref, *, nc):
    @pl.when(pl.program_id(1) == 0)
    def _():
        s_ref[...] = jnp.zeros_like(s_ref)

    n_pairs = N_HEADS // 2
    n_inst = 2 * n_pairs
    pw = 2 * HEAD_DIM
    dirs = ((qf_ref, kf_ref, vf_ref, gf_ref, gtpf_ref, of_ref, False),
            (qb_ref, kb_ref, vb_ref, gb_ref, gtpb_ref, ob_ref, True))
    steps = [[d + ((nc - 1 - i) if d[6] else i,) for d in dirs] for i in range(nc)]

    ri = lax.broadcasted_iota(jnp.int32, (CHUNK, 2 * CHUNK), 0)
    ci = jnp.bitwise_and(lax.broadcasted_iota(jnp.int32, (CHUNK, 2 * CHUNK), 1), CHUNK - 1)
    inst = lax.broadcasted_iota(jnp.int32, (nc * n_inst, 1, 1), 0)
    sign = 1 - 2 * jnp.bitwise_and(jnp.right_shift(inst, n_pairs.bit_length() - 1), 1)
    rel = (ri - ci)[None] * sign
    incl = rel >= 0
    strict = rel > 0
    eye = (ri == ci).astype(F32)

    def pairs(which):
        return jnp.stack([d[which][d[7] * CHUNK:(d[7] + 1) * CHUNK, p * pw:(p + 1) * pw]
                          for st in steps for d in st for p in range(n_pairs)]).astype(F32)

    def pair_bcast(cols, width):
        return jnp.stack([jnp.concatenate([jnp.broadcast_to(cc[2 * p], (CHUNK, width)),
                                           jnp.broadcast_to(cc[2 * p + 1], (CHUNK, width))], axis=1)
                          for cc in cols for p in range(n_pairs)])

    qf = pairs(0)
    kf = pairs(1)
    vf = pairs(2)
    beta_c, g_c, glast_c = [], [], []
    for st in steps:
        for d in st:
            gblk = d[3][d[7] * CHUNK:(d[7] + 1) * CHUNK, :]
            off = N_HEADS if d[6] else 0
            last = 0 if d[6] else CHUNK - 1
            beta_c.append([gblk[:, off + hh:off + hh + 1] for hh in range(N_HEADS)])
            g_c.append([gblk[:, 2 * N_HEADS + off + hh:2 * N_HEADS + off + hh + 1] for hh in range(N_HEADS)])
            glast_c.append([gc[last:last + 1, :] for gc in g_c[-1]])
    beta = pair_bcast(beta_c, HEAD_DIM)
    eg = pair_bcast([[jnp.exp(gc) for gc in gcs] for gcs in g_c], HEAD_DIM)
    tail = pair_bcast([[jnp.exp(gl - gc) for gl, gc in zip(gls, gcs)] for gls, gcs in zip(glast_c, g_c)],
                      HEAD_DIM)
    gcol = pair_bcast(g_c, CHUNK)
    grow = jnp.stack([d[4][(n_pairs if d[6] else 0) + p:(n_pairs if d[6] else 0) + p + 1,
                           2 * d[7] * CHUNK:2 * (d[7] + 1) * CHUNK]
                      for st in steps for d in st for p in range(n_pairs)])

    decay = jnp.exp(jnp.where(incl, gcol - grow, NEG_BIG))
    kb = kf * beta
    kkqk = _bmm_nt(jnp.concatenate([kb, qf], axis=1).astype(BF16),
                   _block_diag_rows(kf, HEAD_DIM).astype(BF16))
    lmat = jnp.where(strict, kkqk[:, :CHUNK, :] * decay, 0.0)
    qk = (kkqk[:, CHUNK:, :] * decay).astype(BF16)
    tinv = _unit_tri_inverse(lmat, eye)
    rhs = jnp.concatenate([_block_diag_rows(vf * beta, HEAD_DIM),
                           _block_diag_rows(kb * eg, HEAD_DIM)], axis=2).astype(BF16)
    uw = _bmm(tinv.astype(BF16), rhs)
    u = uw[:, :, :pw]
    wq = jnp.concatenate([uw[:, :, pw:], qf * eg], axis=1).astype(BF16)
    kt = (kf * tail).astype(BF16)

    for i, st in enumerate(steps):
        sl = slice(i * n_inst, (i + 1) * n_inst)
        s_a = s_ref[:, 0]
        s_b = s_ref[:, 1]
        zero = jnp.zeros_like(s_a)
        s_bd = jnp.concatenate([jnp.concatenate([s_a, zero], axis=2),
                                jnp.concatenate([zero, s_b], axis=2)], axis=1).astype(BF16)
        ws = _bmm(wq[sl], s_bd)
        v_new = u[sl] - ws[:, :CHUNK, :]
        o = ws[:, CHUNK:, :] + _bmm(qk[sl], _block_diag_rows(v_new, HEAD_DIM).astype(BF16))
        v16 = v_new.astype(BF16)
        gl = [glast_c[2 * i + dd] for dd in range(2)]
        cd_a = jnp.stack([jnp.exp(gl[dd][2 * p]) for dd in range(2) for p in range(n_pairs)])
        cd_b = jnp.stack([jnp.exp(gl[dd][2 * p + 1]) for dd in range(2) for p in range(n_pairs)])
        s_ref[:, 0] = s_a * cd_a + _bmm_tn(kt[sl, :, :HEAD_DIM], v16[:, :, :HEAD_DIM])
        s_ref[:, 1] = s_b * cd_b + _bmm_tn(kt[sl, :, HEAD_DIM:], v16[:, :, HEAD_DIM:])
        for dd, d in enumerate(st):
            for p in range(n_pairs):
                d[5][d[7] * CHUNK:(d[7] + 1) * CHUNK, p * pw:(p + 1) * pw] = (
                    o[dd * n_pairs + p].astype(d[5].dtype))


def _delta_rule(qkv, g, gtp, bsz, seq):
    t = bsz * seq
    rows = min(DELTA_ROWS, seq)
    nblk = seq // rows

    def fwd(col):
        return lambda b, i: (b * nblk + i, col)

    def bwd(col):
        return lambda b, i: (b * nblk + nblk - 1 - i, col)

    def specs(m):
        return [pl.BlockSpec((rows, WIDTH_A), m(0)), pl.BlockSpec((rows, WIDTH_A), m(1)),
                pl.BlockSpec((rows, WIDTH_A), m(2)), pl.BlockSpec((rows, LANES), m(0)),
                pl.BlockSpec((N_HEADS, 2 * rows), lambda b, i, m=m: (0, m(0)(b, i)[0]))]

    out = jax.ShapeDtypeStruct((t, WIDTH_A), BF16)
    return pl.pallas_call(
        functools.partial(_delta_kernel, nc=rows // CHUNK),
        out_shape=(out, out),
        grid=(bsz, nblk),
        in_specs=specs(fwd) + specs(bwd),
        out_specs=(pl.BlockSpec((rows, WIDTH_A), fwd(0)), pl.BlockSpec((rows, WIDTH_A), bwd(0))),
        scratch_shapes=[pltpu.VMEM((N_HEADS, 2, HEAD_DIM, HEAD_DIM), F32)],
        compiler_params=_params("parallel", "arbitrary"),
        name="delta_rule",
    )(qkv, qkv, qkv, g, gtp, qkv, qkv, qkv, g, gtp)


def _mix_kernel(of_ref, ob_ref, z_ref, yc_ref, gate_ref, x_ref,
                eg_ref, eb_ref, ng_ref, wao_ref, cg_ref, cb_ref, wbo_ref, bbo_ref,
                wout_ref, l1g_ref, l1b_ref, wr_ref, br_ref,
                h1_ref, h1p_ref, logit_ref):
    o = of_ref[...].astype(F32) + ob_ref[...].astype(F32)
    z = z_ref[...].astype(F32)
    ng = ng_ref[...]
    parts = []
    for hh in range(N_HEADS):
        sl = slice(hh * HEAD_DIM, (hh + 1) * HEAD_DIM)
        oh = o[:, sl]
        zh = z[:, sl]
        inv = lax.rsqrt(jnp.mean(oh * oh, axis=-1, keepdims=True) + RMS_EPS)
        parts.append((oh * inv * ng * (zh * _sigmoid(zh))).astype(BF16))
    y_a = _dot(jnp.concatenate(parts, axis=1), wao_ref[...])

    yc = _layer_norm(yc_ref[...].astype(F32), cg_ref[...], cb_ref[...])
    y_b = _dot((yc * _sigmoid(yc)).astype(BF16), wbo_ref[...]) + bbo_ref[...]

    gates = gate_ref[...].astype(F32)
    mixed = gates[:, :D_MODEL] * y_a + gates[:, D_MODEL:] * y_b
    mix = _dot(mixed.astype(BF16), wout_ref[...])

    h0 = _layer_norm(x_ref[...], eg_ref[...], eb_ref[...])
    h1 = _layer_norm(DN_ALPHA * h0 + mix, l1g_ref[...], l1b_ref[...])
    h1_ref[...] = h1
    h1p_ref[...] = _pack_bf16_pair(h1[:, :D_MODEL // 2], h1[:, D_MODEL // 2:])
    h_hi, h_lo = _split_bf16(h1)
    p = _dot(h_hi, wr_ref[...])
    logit_ref[...] = p[:, :LANES] + p[:, LANES:] + _dot(h_lo, wr_ref[:, :LANES]) + br_ref[...]


def _mix(o_f, o_b, u_qkvz, yc, gates, x2, emb_g, emb_b, norm_g, w_a_o, cg, cb, w_b_o, b_b_o,
         w_out, l1g, l1b, w_router, b_router):
    t, d = x2.shape
    tm = min(ROW_TILE, t)
    row = lambda i: (i, 0)
    const = lambda i: (0, 0)
    wr = jnp.concatenate(_split_bf16(jnp.pad(w_router, ((0, 0), (0, LANES - N_EXPERTS)))), axis=1)
    br = jnp.pad(b_router.reshape(1, N_EXPERTS), ((0, 0), (0, LANES - N_EXPERTS)), constant_values=NEG_BIG)
    vec = lambda a: a.reshape(1, -1)
    return pl.pallas_call(
        _mix_kernel,
        out_shape=(jax.ShapeDtypeStruct((t, d), F32), jax.ShapeDtypeStruct((t, d // 2), jnp.uint32),
                   jax.ShapeDtypeStruct((t, LANES), F32)),
        grid=(t // tm,),
        in_specs=[pl.BlockSpec((tm, d), row), pl.BlockSpec((tm, d), row),
                  pl.BlockSpec((tm, d), lambda i: (i, 3)),
                  pl.BlockSpec((tm, d), row), pl.BlockSpec((tm, 2 * d), row), pl.BlockSpec((tm, d), row),
                  pl.BlockSpec((1, d), const), pl.BlockSpec((1, d), const),
                  pl.BlockSpec((1, HEAD_DIM), const), pl.BlockSpec((d, d), const),
                  pl.BlockSpec((1, d), const), pl.BlockSpec((1, d), const),
                  pl.BlockSpec((d, d), const), pl.BlockSpec((1, d), const),
                  pl.BlockSpec((d, d), const), pl.BlockSpec((1, d), const), pl.BlockSpec((1, d), const),
                  pl.BlockSpec((d, 2 * LANES), const), pl.BlockSpec((1, LANES), const)],
        out_specs=(pl.BlockSpec((tm, d), row), pl.BlockSpec((tm, d // 2), row), pl.BlockSpec((tm, LANES), row)),
        compiler_params=_params("parallel"),
        name="mix",
    )(o_f, o_b, u_qkvz, yc, gates, x2, vec(emb_g), vec(emb_b), vec(norm_g), w_a_o, vec(cg), vec(cb),
      w_b_o, vec(b_b_o), w_out, vec(l1g), vec(l1b), wr, br)


def _route_kernel(logit_ref, gate_ref, eidx_ref, rank_ref, cnt_ref, base_ref):
    @pl.when(pl.program_id(0) == 0)
    def _():
        base_ref[...] = jnp.zeros_like(base_ref)

    x = logit_ref[...]
    tm = x.shape[0]
    lane = lax.broadcasted_iota(jnp.int32, x.shape, 1)
    lane_f = lane.astype(F32)
    row = lax.broadcasted_iota(jnp.int32, x.shape, 0)
    sel = jnp.zeros(x.shape, F32)
    vals, idxs = [], []
    for _ in range(TOP_K):
        m = jnp.max(x, axis=1, keepdims=True)
        idx = jnp.min(jnp.where(x == m, lane_f, float(LANES)), axis=1, keepdims=True).astype(jnp.int32)
        hit = lane == idx
        sel = sel + hit.astype(F32)
        x = jnp.where(hit, -3e38, x)
        vals.append(m)
        idxs.append(idx)

    exps = [jnp.exp(v - vals[0]) for v in vals]
    denom = exps[0]
    for e in exps[1:]:
        denom = denom + e

    csum = sel
    s = 1
    while s < tm:
        csum = csum + jnp.where(row >= s, pltpu.roll(csum, s, axis=0), 0.0)
        s *= 2
    before = base_ref[...] + csum - sel

    gate = jnp.zeros(x.shape, F32)
    eidx = jnp.zeros(x.shape, jnp.int32)
    rank = jnp.zeros(x.shape, jnp.int32)
    for k in range(TOP_K):
        rk = jnp.sum(jnp.where(lane == idxs[k], before, 0.0), axis=1, keepdims=True)
        gate = jnp.where(lane == k, exps[k] / denom, gate)
        eidx = jnp.where(lane == k, idxs[k], eidx)
        rank = jnp.where(lane == k, rk.astype(jnp.int32), rank)
    gate_ref[...] = gate
    eidx_ref[...] = eidx
    rank_ref[...] = rank
    total = base_ref[...] + csum[tm - 1:tm, :]
    base_ref[...] = total
    cnt_ref[...] = total


def _route(logits):
    t = logits.shape[0]
    tm = min(ROW_TILE, t)
    bm = EXPERT_ROWS
    row = lambda i: (i, 0)
    gate, eidx, rank, cnt = pl.pallas_call(
        _route_kernel,
        out_shape=(jax.ShapeDtypeStruct((t, LANES), F32), jax.ShapeDtypeStruct((t, LANES), jnp.int32),
                   jax.ShapeDtypeStruct((t, LANES), jnp.int32), jax.ShapeDtypeStruct((1, LANES), F32)),
        grid=(t // tm,),
        in_specs=[pl.BlockSpec((tm, LANES), row)],
        out_specs=(pl.BlockSpec((tm, LANES), row), pl.BlockSpec((tm, LANES), row),
                   pl.BlockSpec((tm, LANES), row), pl.BlockSpec((1, LANES), lambda i: (0, 0))),
        scratch_shapes=[pltpu.VMEM((1, LANES), F32)],
        compiler_params=_params("arbitrary"),
        name="moe_route",
    )(logits)
    counts = cnt[0, :N_EXPERTS].astype(jnp.int32)
    padded = (counts + bm - 1) // bm * bm
    pad_end = jnp.cumsum(padded)
    pad_start = pad_end - padded
    pos = pad_start[eidx[:, :TOP_K]] + rank[:, :TOP_K]
    n_blocks = -(-(t * TOP_K + N_EXPERTS * (bm - 1)) // bm)
    block_start = jnp.arange(n_blocks, dtype=jnp.int32) * bm
    block_exp = jnp.minimum(jnp.sum((block_start[:, None] >= pad_end[None, :]).astype(jnp.int32), axis=1),
                            N_EXPERTS - 1)
    n_used = (pad_end[-1] // bm).astype(jnp.int32).reshape(1)
    return gate[:, :TOP_K], pos.astype(jnp.int32), block_exp, pad_start, pad_end, n_used, n_blocks


def _dispatch_kernel(ps_ref, pe_ref, nb_ref, idx_ref, h_ref, xs_hbm, zero_ref, sem, zsem, *, tm, n_blocks):
    bm = EXPERT_ROWS

    @pl.when(pl.program_id(0) == 0)
    def _():
        zero_ref[...] = jnp.zeros_like(zero_ref)

        def zero_block(start):
            return pltpu.make_async_copy(zero_ref, xs_hbm.at[pl.ds(pl.multiple_of(start, bm), bm), :], zsem)

        for e in range(N_EXPERTS):
            @pl.when(pe_ref[e] > ps_ref[e])
            def _():
                zero_block(pe_ref[e] - bm).start()
        for e in range(N_EXPERTS):
            @pl.when(pe_ref[e] > ps_ref[e])
            def _():
                zero_block(pe_ref[e] - bm).wait()

        def tail_start(b, carry):
            zero_block(b * bm).start()
            return carry

        def tail_wait(b, carry):
            zero_block(b * bm).wait()
            return carry

        lax.fori_loop(nb_ref[0], n_blocks, tail_start, 0)
        lax.fori_loop(nb_ref[0], n_blocks, tail_wait, 0)

    def issue(r, carry):
        for k in range(TOP_K):
            pltpu.make_async_copy(h_ref.at[pl.ds(r, 1), :],
                                  xs_hbm.at[pl.ds(idx_ref[0, 0, r * TOP_K + k], 1), :],
                                  sem).start(priority=k % 2)
        return carry

    lax.fori_loop(0, tm, issue, 0, unroll=GATHER_UNROLL // 2)
    for _ in range(TOP_K):
        pltpu.make_async_copy(h_ref, xs_hbm.at[pl.ds(0, tm), :], sem).wait()


def _dispatch(h1, pos, pad_start, pad_end, n_used, n_blocks):
    t, d = h1.shape
    tm = min(ROW_TILE, t)
    nt = t // tm
    n_slots = n_blocks * EXPERT_ROWS
    grid_spec = pltpu.PrefetchScalarGridSpec(
        num_scalar_prefetch=3,
        grid=(nt,),
        in_specs=[pl.BlockSpec((1, 1, TOP_K * tm), lambda i, *_: (i, 0, 0), memory_space=pltpu.SMEM),
                  pl.BlockSpec((tm, d), lambda i, *_: (i, 0))],
        out_specs=pl.BlockSpec(memory_space=pl.ANY),
        scratch_shapes=[pltpu.VMEM((EXPERT_ROWS, d), h1.dtype), pltpu.SemaphoreType.DMA(()),
                        pltpu.SemaphoreType.DMA(())],
    )
    return pl.pallas_call(
        functools.partial(_dispatch_kernel, tm=tm, n_blocks=n_blocks),
        out_shape=jax.ShapeDtypeStruct((n_slots, d), h1.dtype),
        grid_spec=grid_spec,
        compiler_params=_params("arbitrary"),
        name="moe_dispatch",
    )(pad_start, pad_end, n_used, pos.reshape(nt, 1, TOP_K * tm), h1)


def _expert_kernel(be_ref, nb_ref, x_ref, wgu_ref, bgu_ref, wd_ref, bd_ref, o_ref, wgu16_ref, wd16_ref):
    i = pl.program_id(0)
    active = i < nb_ref[0]
    new_expert = jnp.logical_or(i == 0, be_ref[i] != be_ref[jnp.maximum(i - 1, 0)])

    @pl.when(jnp.logical_and(active, new_expert))
    def _():
        wgu16_ref[...] = wgu_ref[0].astype(BF16)
        wd16_ref[...] = wd_ref[0].astype(BF16)

    @pl.when(active)
    def _():
        half = D_MODEL // 2
        x_lo, x_hi = _unpack_bf16_pair(x_ref[...])
        hgu = (_dot(x_lo.astype(BF16), wgu16_ref[:half, :]) + _dot(x_hi.astype(BF16), wgu16_ref[half:, :])
               + bgu_ref[0])
        glu = jnp.minimum(hgu[:, :D_FF], SWIGLU_LIMIT)
        lin = jnp.clip(hgu[:, D_FF:], -SWIGLU_LIMIT, SWIGLU_LIMIT)
        act = glu * _sigmoid(SWIGLU_ALPHA * glu) * (lin + 1.0)
        y = _dot(act.astype(BF16), wd16_ref[...]) + bd_ref[0]
        o_ref[...] = _pack_bf16_pair(y[:, :half], y[:, half:])

    @pl.when(jnp.logical_not(active))
    def _():
        o_ref[...] = jnp.zeros_like(o_ref)


def _experts(xs, block_exp, n_used, w_gu, b_gu, w_down, b_down, n_blocks):
    d = D_MODEL
    dp = xs.shape[1]
    bm = EXPERT_ROWS
    grid_spec = pltpu.PrefetchScalarGridSpec(
        num_scalar_prefetch=2,
        grid=(n_blocks,),
        in_specs=[pl.BlockSpec((bm, dp), lambda i, be, nb: (i, 0)),
                  pl.BlockSpec((1, d, 2 * D_FF), lambda i, be, nb: (be[i], 0, 0)),
                  pl.BlockSpec((1, 1, 2 * D_FF), lambda i, be, nb: (be[i], 0, 0)),
                  pl.BlockSpec((1, D_FF, d), lambda i, be, nb: (be[i], 0, 0)),
                  pl.BlockSpec((1, 1, d), lambda i, be, nb: (be[i], 0, 0))],
        out_specs=pl.BlockSpec((bm, dp), lambda i, be, nb: (i, 0)),
        scratch_shapes=[pltpu.VMEM((d, 2 * D_FF), BF16), pltpu.VMEM((D_FF, d), BF16)],
    )
    return pl.pallas_call(
        _expert_kernel,
        out_shape=jax.ShapeDtypeStruct((n_blocks * bm, dp), jnp.uint32),
        grid_spec=grid_spec,
        compiler_params=_params("arbitrary"),
        name="moe_experts",
    )(block_exp, n_used, xs, w_gu, b_gu.reshape(N_EXPERTS, 1, 2 * D_FF), w_down,
      b_down.reshape(N_EXPERTS, 1, d))


def _combine_kernel(idx_ref, idx_next_ref, ys_hbm, gate_ref, h_ref, g_ref, b_ref, o_ref, buf_ref, sem, *, tm, nt):
    i = pl.program_id(0)
    n = TOP_K * tm
    slot = i % 2

    def gather(idx, s):
        def issue(r, carry):
            for j in range(2):
                rr = 2 * r + j
                pltpu.make_async_copy(ys_hbm.at[pl.ds(idx[0, 0, rr], 1), :],
                                      buf_ref.at[s, pl.ds(rr, 1), :], sem.at[s]).start(priority=j)
            return carry

        lax.fori_loop(0, n // 2, issue, 0, unroll=GATHER_UNROLL // 2)

    @pl.when(i == 0)
    def _():
        gather(idx_ref, 0)

    @pl.when(i + 1 < nt)
    def _():
        gather(idx_next_ref, 1 - slot)

    pltpu.make_async_copy(ys_hbm.at[pl.ds(0, n), :], buf_ref.at[slot], sem.at[slot]).wait()
    gate = gate_ref[...]
    f_lo = f_hi = None
    for k in range(TOP_K):
        y_lo, y_hi = _unpack_bf16_pair(buf_ref[slot, k * tm:(k + 1) * tm, :])
        gk = gate[:, k:k + 1]
        f_lo = gk * y_lo if f_lo is None else f_lo + gk * y_lo
        f_hi = gk * y_hi if f_hi is None else f_hi + gk * y_hi
    f = jnp.concatenate([f_lo, f_hi], axis=1)
    o_ref[...] = _layer_norm(DN_ALPHA * h_ref[...] + f, g_ref[...], b_ref[...])


def _combine(ys, pos, gate, h1, ln_g, ln_b):
    t, d = h1.shape
    tm = min(COMBINE_ROWS, t)
    nt = t // tm
    idx = pos.reshape(nt, tm, TOP_K).transpose(0, 2, 1).reshape(nt, 1, TOP_K * tm)
    return pl.pallas_call(
        functools.partial(_combine_kernel, tm=tm, nt=nt),
        out_shape=jax.ShapeDtypeStruct((t, d), F32),
        grid=(nt,),
        in_specs=[pl.BlockSpec((1, 1, TOP_K * tm), lambda i: (i, 0, 0), memory_space=pltpu.SMEM),
                  pl.BlockSpec((1, 1, TOP_K * tm), lambda i: (jnp.minimum(i + 1, nt - 1), 0, 0),
                               memory_space=pltpu.SMEM),
                  pl.BlockSpec(memory_space=pl.ANY),
                  pl.BlockSpec((tm, TOP_K), lambda i: (i, 0)),
                  pl.BlockSpec((tm, d), lambda i: (i, 0)),
                  pl.BlockSpec((1, d), lambda i: (0, 0)),
                  pl.BlockSpec((1, d), lambda i: (0, 0))],
        out_specs=pl.BlockSpec((tm, d), lambda i: (i, 0)),
        scratch_shapes=[pltpu.VMEM((2, TOP_K * tm, ys.shape[1]), ys.dtype), pltpu.SemaphoreType.DMA((2,))],
        compiler_params=_params("arbitrary"),
        name="moe_combine",
    )(idx, idx, ys, gate, h1, ln_g.reshape(1, d), ln_b.reshape(1, d))


def kernel(x, emb_ln_g, emb_ln_b, w_in, conv_qkv, a_log, dt_bias, dn_norm_g, w_a_o, b_glu, conv_dw, b_dw, conv_ln_g, conv_ln_b, w_b_o, b_b_o, b_gate, w_out, ln1_g, ln1_b, w_router, b_router, w_gu, b_gu, w_down, b_down, ln2_g, ln2_b):
    bsz, seq, d = x.shape
    t = bsz * seq
    x2 = x.reshape(t, d)
    u_qkvz, g, gt, glu, gates = _inproj(x2, emb_ln_g, emb_ln_b, w_in[0], b_glu[0], b_gate[0], a_log[0], dt_bias[0])

    qkv = _qkv_conv(u_qkvz, conv_qkv[0], bsz, seq)
    gtp = gt.reshape(2, N_HEADS // 2, 2, t // CHUNK, CHUNK).transpose(0, 1, 3, 2, 4).reshape(N_HEADS, 2 * t)
    o_f, o_b = _delta_rule(qkv, g, gtp, bsz, seq)
    yc = _dw_conv(glu, conv_dw[0], b_dw[0], bsz, seq)

    h1, h1p, logits = _mix(o_f, o_b, u_qkvz, yc, gates, x2, emb_ln_g, emb_ln_b, dn_norm_g[0],
                           w_a_o[0].astype(BF16), conv_ln_g[0], conv_ln_b[0], w_b_o[0].astype(BF16), b_b_o[0],
                           w_out[0].astype(BF16), ln1_g[0], ln1_b[0], w_router[0], b_router[0])

    gate, pos, block_exp, pad_start, pad_end, n_used, n_blocks = _route(logits)
    xs = _dispatch(h1p, pos, pad_start, pad_end, n_used, n_blocks)
    ys = _experts(xs, block_exp, n_used, w_gu[0], b_gu[0], w_down[0], b_down[0], n_blocks)
    out = _combine(ys, pos, gate, h1, ln2_g[0], ln2_b[0])
    return out.reshape(bsz, seq, d)
```

```python
import functools

import jax
import jax.numpy as jnp
from jax import lax
from jax.experimental import pallas as pl
from jax.experimental.pallas import tpu as pltpu
from jax.experimental.pallas import tpu_sc as plsc

F32 = jnp.float32
BF16 = jnp.bfloat16

D_MODEL = 1024
N_HEADS = 8
HEAD_DIM = 128
WIDTH_A = N_HEADS * HEAD_DIM
SHORT_CONV = 5
CHUNK = 64
WIDTH_B = D_MODEL
DW_CONV = 31
N_EXPERTS = 32
TOP_K = 4
D_FF = D_MODEL
SWIGLU_ALPHA = 1.702
SWIGLU_LIMIT = 7.0
DN_ALPHA = 2.0 ** 0.25
LN_EPS = 1e-5
RMS_EPS = 1e-6
L2_EPS = 1e-6
LANES = 128
NEG_BIG = -1e30

ROW_TILE = 512
DELTA_ROWS = 256
EXPERT_ROWS = 512
COMBINE_ROWS = 128
GATHER_UNROLL = 8
SC_CORES = 2
SC_SUBCORES = 16
SC_GATHER_ROWS = 64
VMEM_LIMIT = 56 * 1024 * 1024


def _params(*sem):
    return pltpu.CompilerParams(dimension_semantics=sem, vmem_limit_bytes=VMEM_LIMIT)


def _layer_norm(x, g, b):
    mu = jnp.mean(x, axis=-1, keepdims=True)
    xc = x - mu
    var = jnp.mean(xc * xc, axis=-1, keepdims=True)
    return xc * lax.rsqrt(var + LN_EPS) * g + b


def _sigmoid(x):
    return 1.0 / (1.0 + jnp.exp(-x))


def _dot(a, b):
    return jnp.dot(a, b, preferred_element_type=F32)


def _pack_bf16_pair(a, b):
    ua = lax.bitcast_convert_type(a.astype(BF16).astype(F32), jnp.uint32)
    ub = lax.bitcast_convert_type(b.astype(BF16).astype(F32), jnp.uint32)
    return (ua >> 16) | ub


def _unpack_bf16_pair(p):
    a = lax.bitcast_convert_type(p << 16, F32)
    b = lax.bitcast_convert_type(p & jnp.uint32(0xFFFF0000), F32)
    return a, b


def _split_bf16(a):
    hi = a.astype(BF16)
    return hi, (a - hi.astype(F32)).astype(BF16)


def _chunk_cumsum(x, reverse):
    rows = x.shape[0]
    pos = lax.broadcasted_iota(jnp.int32, x.shape, 0) % CHUNK
    s = 1
    while s < CHUNK:
        if reverse:
            shifted = pltpu.roll(x, rows - s, axis=0)
            x = x + jnp.where(pos < CHUNK - s, shifted, 0.0)
        else:
            shifted = pltpu.roll(x, s, axis=0)
            x = x + jnp.where(pos >= s, shifted, 0.0)
        s *= 2
    return x


def _inproj_kernel(x_ref, eg_ref, eb_ref, wq_ref, ws_ref, wga_ref, wgb_ref, bga_ref, bgb_ref, wgt_ref, bgt_ref,
                   alog_ref, dtb_ref, u_ref, g_ref, gt_ref, glu_ref, gate_ref, *, tn):
    h = _layer_norm(x_ref[...], eg_ref[...], eb_ref[...]).astype(BF16)

    for n0 in range(0, u_ref.shape[1], tn):
        u_ref[:, n0:n0 + tn] = _dot(h, wq_ref[:, n0:n0 + tn]).astype(u_ref.dtype)

    us = _dot(h, ws_ref[...])
    lane = lax.broadcasted_iota(jnp.int32, us.shape, 1)
    beta = _sigmoid(us)
    xs = us + dtb_ref[...]
    softplus = jnp.maximum(xs, 0.0) + jnp.log(1.0 + jnp.exp(-jnp.abs(xs)))
    log_a = -jnp.exp(alog_ref[...]) * softplus
    g_fwd = _chunk_cumsum(log_a, reverse=False)
    g_bwd = _chunk_cumsum(log_a, reverse=True)
    gates = jnp.where(lane < 2 * N_HEADS, beta, jnp.where(lane < 3 * N_HEADS, g_fwd, g_bwd))
    g_ref[...] = gates
    gt_ref[...] = gates.T[2 * N_HEADS:4 * N_HEADS, :]

    for n0 in range(0, glu_ref.shape[1], tn):
        lin = _dot(h, wga_ref[:, n0:n0 + tn]) + bga_ref[:, n0:n0 + tn]
        gt = _dot(h, wgb_ref[:, n0:n0 + tn]) + bgb_ref[:, n0:n0 + tn]
        glu_ref[:, n0:n0 + tn] = (lin * _sigmoid(gt)).astype(glu_ref.dtype)

    for n0 in range(0, gate_ref.shape[1], tn):
        gate_ref[:, n0:n0 + tn] = _sigmoid(_dot(h, wgt_ref[:, n0:n0 + tn])
                                           + bgt_ref[:, n0:n0 + tn]).astype(gate_ref.dtype)


def _inproj(x2, emb_g, emb_b, w_in, b_glu, b_gate, a_log, dt_bias):
    t, d = x2.shape
    tm = min(ROW_TILE, t)
    c0 = 4 * WIDTH_A
    c1 = c0 + 4 * N_HEADS
    c2 = c1 + 2 * WIDTH_B
    wb = w_in.astype(BF16)
    pad = LANES - 4 * N_HEADS
    w_small = jnp.pad(wb[:, c0:c1], ((0, 0), (0, pad)))
    alog = jnp.pad(a_log.reshape(1, 2 * N_HEADS), ((0, 0), (2 * N_HEADS, pad)))
    dtb = jnp.pad(dt_bias.reshape(1, 2 * N_HEADS), ((0, 0), (2 * N_HEADS, pad)))
    row = lambda i: (i, 0)

    def const(shape):
        return pl.BlockSpec(shape, lambda i: (0, 0), pipeline_mode=pl.Buffered(1))

    vec = lambda a: a.reshape(1, -1)
    bf = lambda n: jax.ShapeDtypeStruct((t, n), BF16)
    return pl.pallas_call(
        functools.partial(_inproj_kernel, tn=512),
        out_shape=(bf(c0), jax.ShapeDtypeStruct((t, LANES), F32), jax.ShapeDtypeStruct((2 * N_HEADS, t), F32),
                   bf(WIDTH_B), bf(2 * d)),
        grid=(t // tm,),
        in_specs=[pl.BlockSpec((tm, d), row), const((1, d)), const((1, d)),
                  const((d, c0)), const((d, LANES)),
                  const((d, WIDTH_B)), const((d, WIDTH_B)), const((1, WIDTH_B)), const((1, WIDTH_B)),
                  const((d, 2 * d)), const((1, 2 * d)), const((1, LANES)), const((1, LANES))],
        out_specs=(pl.BlockSpec((tm, c0), row), pl.BlockSpec((tm, LANES), row),
                   pl.BlockSpec((2 * N_HEADS, tm), lambda i: (0, i)),
                   pl.BlockSpec((tm, WIDTH_B), row), pl.BlockSpec((tm, 2 * d), row)),
        compiler_params=_params("parallel"),
        name="inproj",
    )(x2, vec(emb_g), vec(emb_b), wb[:, :c0], w_small, wb[:, c1:c1 + WIDTH_B], wb[:, c1 + WIDTH_B:c2],
      vec(b_glu[:WIDTH_B]), vec(b_glu[WIDTH_B:]), wb[:, c2:], vec(b_gate), alog, dtb)


def _conv_rows(xp_ref, w, taps, base, r0, rows):
    acc = xp_ref[base + r0:base + r0 + rows, :] * w[0:1, :]
    for k in range(1, taps):
        acc = acc + xp_ref[base + k + r0:base + k + r0 + rows, :] * w[k:k + 1, :]
    return acc


def _fill_padded(xp_ref, x_ref, pad, seq):
    zeros = jnp.zeros((pad, xp_ref.shape[1]), F32)
    xp_ref[0:pad, :] = zeros
    xp_ref[pad + seq:pad + seq + pad, :] = zeros
    xp_ref[pad:pad + seq, :] = x_ref[...].astype(F32)


def _qkv_conv_kernel(u_ref, w_ref, o_ref, xp_ref, *, seq, rows):
    pad = 8
    j = pl.program_id(1)
    _fill_padded(xp_ref, u_ref, pad, seq)
    w = w_ref[...]
    is_q = j < N_HEADS
    is_qk = j < 2 * N_HEADS
    for r0 in range(0, seq, rows):
        y = _conv_rows(xp_ref, w, SHORT_CONV, pad - SHORT_CONV // 2, r0, rows)
        y = y * _sigmoid(y)
        inv = lax.rsqrt(jnp.sum(y * y, axis=-1, keepdims=True) + L2_EPS)
        scale = jnp.where(is_q, inv * (HEAD_DIM ** -0.5), jnp.where(is_qk, inv, 1.0))
        o_ref[r0:r0 + rows, :] = (y * scale).astype(o_ref.dtype)


def _qkv_conv(u_qkvz, conv_w, bsz, seq):
    t = bsz * seq
    ncol = 3 * N_HEADS
    rows = min(256, seq)
    return pl.pallas_call(
        functools.partial(_qkv_conv_kernel, seq=seq, rows=rows),
        out_shape=jax.ShapeDtypeStruct((t, 3 * WIDTH_A), BF16),
        grid=(bsz, ncol),
        in_specs=[pl.BlockSpec((seq, HEAD_DIM), lambda b, j: (b, j)),
                  pl.BlockSpec((SHORT_CONV, HEAD_DIM), lambda b, j: (0, j))],
        out_specs=pl.BlockSpec((seq, HEAD_DIM), lambda b, j: (b, j)),
        scratch_shapes=[pltpu.VMEM((seq + 16, HEAD_DIM), F32)],
        compiler_params=_params("parallel", "parallel"),
        name="qkv_conv",
    )(u_qkvz, conv_w)


def _dw_conv_kernel(x_ref, w_ref, b_ref, o_ref, xp_ref, *, seq, rows):
    pad = 16
    _fill_padded(xp_ref, x_ref, pad, seq)
    w = w_ref[...]
    for r0 in range(0, seq, rows):
        y = _conv_rows(xp_ref, w, DW_CONV, pad - DW_CONV // 2, r0, rows) + b_ref[...]
        o_ref[r0:r0 + rows, :] = y.astype(o_ref.dtype)


def _dw_conv(glu, conv_w, b_dw, bsz, seq):
    t = bsz * seq
    rows = min(256, seq)
    return pl.pallas_call(
        functools.partial(_dw_conv_kernel, seq=seq, rows=rows),
        out_shape=jax.ShapeDtypeStruct((t, WIDTH_B), BF16),
        grid=(bsz, WIDTH_B // LANES),
        in_specs=[pl.BlockSpec((seq, LANES), lambda b, j: (b, j)),
                  pl.BlockSpec((DW_CONV, LANES), lambda b, j: (0, j)),
                  pl.BlockSpec((1, LANES), lambda b, j: (0, j))],
        out_specs=pl.BlockSpec((seq, LANES), lambda b, j: (b, j)),
        scratch_shapes=[pltpu.VMEM((seq + 32, LANES), F32)],
        compiler_params=_params("parallel", "parallel"),
        name="dw_conv",
    )(glu, conv_w, b_dw.reshape(1, WIDTH_B))


def _bmm(a, b):
    return lax.dot_general(a, b, (((2,), (1,)), ((0,), (0,))), preferred_element_type=F32)


def _bmm_nt(a, b):
    return lax.dot_general(a, b, (((2,), (2,)), ((0,), (0,))), preferred_element_type=F32)


def _bmm_tn(a, b):
    return lax.dot_general(a, b, (((1,), (1,)), ((0,), (0,))), preferred_element_type=F32)


def _block_diag_rows(x, half):
    lane = lax.broadcasted_iota(jnp.int32, x.shape, 2)
    return jnp.concatenate([jnp.where(lane < half, x, 0.0), jnp.where(lane >= half, x, 0.0)], axis=1)


def _unit_tri_inverse(lmat, eye):
    def rhs(p):
        return _block_diag_rows(p, CHUNK).astype(BF16)

    x = eye - lmat
    p = _bmm(lmat.astype(BF16), rhs(lmat))
    s = 2
    while 2 * s < CHUNK:
        xp = _bmm(jnp.concatenate([x, p], axis=1).astype(BF16), rhs(p))
        x = x + xp[:, :CHUNK]
        p = xp[:, CHUNK:]
        s *= 2
    return x + _bmm(x.astype(BF16), rhs(p))


def _delta_kernel(qf_ref, kf_ref, vf_ref, gf_ref, gtpf_ref, qb_ref, kb_ref, vb_ref, gb_ref, gtpb_ref,
                  of_ref, ob_ref, s_ref, *, nc):
    @pl.when(pl.program_id(1) == 0)
    def _():
        s_ref[...] = jnp.zeros_like(s_ref)

    n_pairs = N_HEADS // 2
    n_inst = 2 * n_pairs
    pw = 2 * HEAD_DIM
    dirs = ((qf_ref, kf_ref, vf_ref, gf_ref, gtpf_ref, of_ref, False),
            (qb_ref, kb_ref, vb_ref, gb_ref, gtpb_ref, ob_ref, True))
    steps = [[d + ((nc - 1 - i) if d[6] else i,) for d in dirs] for i in range(nc)]

    ri = lax.broadcasted_iota(jnp.int32, (CHUNK, 2 * CHUNK), 0)
    ci = jnp.bitwise_and(lax.broadcasted_iota(jnp.int32, (CHUNK, 2 * CHUNK), 1), CHUNK - 1)
    inst = lax.broadcasted_iota(jnp.int32, (nc * n_inst, 1, 1), 0)
    sign = 1 - 2 * jnp.bitwise_and(jnp.right_shift(inst, n_pairs.bit_length() - 1), 1)
    rel = (ri - ci)[None] * sign
    incl = rel >= 0
    strict = rel > 0
    eye = (ri == ci).astype(F32)

    def pairs(which):
        return jnp.stack([d[which][d[7] * CHUNK:(d[7] + 1) * CHUNK, p * pw:(p + 1) * pw]
                          for st in steps for d in st for p in range(n_pairs)]).astype(F32)

    def pair_bcast(cols, width):
        return jnp.stack([jnp.concatenate([jnp.broadcast_to(cc[2 * p], (CHUNK, width)),
                                           jnp.broadcast_to(cc[2 * p + 1], (CHUNK, width))], axis=1)
                          for cc in cols for p in range(n_pairs)])

    qf = pairs(0)
    kf = pairs(1)
    vf = pairs(2)
    beta_c, g_c, glast_c = [], [], []
    for st in steps:
        for d in st:
            gblk = d[3][d[7] * CHUNK:(d[7] + 1) * CHUNK, :]
            off = N_HEADS if d[6] else 0
            last = 0 if d[6] else CHUNK - 1
            beta_c.append([gblk[:, off + hh:off + hh + 1] for hh in range(N_HEADS)])
            g_c.append([gblk[:, 2 * N_HEADS + off + hh:2 * N_HEADS + off + hh + 1] for hh in range(N_HEADS)])
            glast_c.append([gc[last:last + 1, :] for gc in g_c[-1]])
    beta = pair_bcast(beta_c, HEAD_DIM)
    eg = pair_bcast([[jnp.exp(gc) for gc in gcs] for gcs in g_c], HEAD_DIM)
    tail = pair_bcast([[jnp.exp(gl - gc) for gl, gc in zip(gls, gcs)] for gls, gcs in zip(glast_c, g_c)],
                      HEAD_DIM)
    gcol = pair_bcast(g_c, CHUNK)
    grow = jnp.stack([d[4][(n_pairs if d[6] else 0) + p:(n_pairs if d[6] else 0) + p + 1,
                           2 * d[7] * CHUNK:2 * (d[7] + 1) * CHUNK]
                      for st in steps for d in st for p in range(n_pairs)])

    decay = jnp.exp(jnp.where(incl, gcol - grow, NEG_BIG))
    kb = kf * beta
    kkqk = _bmm_nt(jnp.concatenate([kb, qf], axis=1).astype(BF16),
                   _block_diag_rows(kf, HEAD_DIM).astype(BF16))
    lmat = jnp.where(strict, kkqk[:, :CHUNK, :] * decay, 0.0)
    qk = (kkqk[:, CHUNK:, :] * decay).astype(BF16)
    tinv = _unit_tri_inverse(lmat, eye)
    rhs = jnp.concatenate([_block_diag_rows(vf * beta, HEAD_DIM),
                           _block_diag_rows(kb * eg, HEAD_DIM)], axis=2).astype(BF16)
    uw = _bmm(tinv.astype(BF16), rhs)
    u = uw[:, :, :pw]
    wq = jnp.concatenate([uw[:, :, pw:], qf * eg], axis=1).astype(BF16)
    kt = (kf * tail).astype(BF16)

    for i, st in enumerate(steps):
        sl = slice(i * n_inst, (i + 1) * n_inst)
        s_a = s_ref[:, 0]
        s_b = s_ref[:, 1]
        zero = jnp.zeros_like(s_a)
        s_bd = jnp.concatenate([jnp.concatenate([s_a, zero], axis=2),
                                jnp.concatenate([zero, s_b], axis=2)], axis=1).astype(BF16)
        ws = _bmm(wq[sl], s_bd)
        v_new = u[sl] - ws[:, :CHUNK, :]
        o = ws[:, CHUNK:, :] + _bmm(qk[sl], _block_diag_rows(v_new, HEAD_DIM).astype(BF16))
        v16 = v_new.astype(BF16)
        gl = [glast_c[2 * i + dd] for dd in range(2)]
        cd_a = jnp.stack([jnp.exp(gl[dd][2 * p]) for dd in range(2) for p in range(n_pairs)])
        cd_b = jnp.stack([jnp.exp(gl[dd][2 * p + 1]) for dd in range(2) for p in range(n_pairs)])
        s_ref[:, 0] = s_a * cd_a + _bmm_tn(kt[sl, :, :HEAD_DIM], v16[:, :, :HEAD_DIM])
        s_ref[:, 1] = s_b * cd_b + _bmm_tn(kt[sl, :, HEAD_DIM:], v16[:, :, HEAD_DIM:])
        for dd, d in enumerate(st):
            for p in range(n_pairs):
                d[5][d[7] * CHUNK:(d[7] + 1) * CHUNK, p * pw:(p + 1) * pw] = (
                    o[dd * n_pairs + p].astype(d[5].dtype))


def _delta_rule(qkv, g, gtp, bsz, seq):
    t = bsz * seq
    rows = min(DELTA_ROWS, seq)
    nblk = seq // rows

    def fwd(col):
        return lambda b, i: (b * nblk + i, col)

    def bwd(col):
        return lambda b, i: (b * nblk + nblk - 1 - i, col)

    def specs(m):
        return [pl.BlockSpec((rows, WIDTH_A), m(0)), pl.BlockSpec((rows, WIDTH_A), m(1)),
                pl.BlockSpec((rows, WIDTH_A), m(2)), pl.BlockSpec((rows, LANES), m(0)),
                pl.BlockSpec((N_HEADS, 2 * rows), lambda b, i, m=m: (0, m(0)(b, i)[0]))]

    out = jax.ShapeDtypeStruct((t, WIDTH_A), BF16)
    return pl.pallas_call(
        functools.partial(_delta_kernel, nc=rows // CHUNK),
        out_shape=(out, out),
        grid=(bsz, nblk),
        in_specs=specs(fwd) + specs(bwd),
        out_specs=(pl.BlockSpec((rows, WIDTH_A), fwd(0)), pl.BlockSpec((rows, WIDTH_A), bwd(0))),
        scratch_shapes=[pltpu.VMEM((N_HEADS, 2, HEAD_DIM, HEAD_DIM), F32)],
        compiler_params=_params("parallel", "arbitrary"),
        name="delta_rule",
    )(qkv, qkv, qkv, g, gtp, qkv, qkv, qkv, g, gtp)


def _mix_kernel(of_ref, ob_ref, z_ref, yc_ref, gate_ref, x_ref,
                eg_ref, eb_ref, ng_ref, wao_ref, cg_ref, cb_ref, wbo_ref, bbo_ref,
                wout_ref, l1g_ref, l1b_ref, wr_ref, br_ref,
                h1_ref, h1p_ref, logit_ref):
    o = of_ref[...].astype(F32) + ob_ref[...].astype(F32)
    z = z_ref[...].astype(F32)
    ng = ng_ref[...]
    parts = []
    for hh in range(N_HEADS):
        sl = slice(hh * HEAD_DIM, (hh + 1) * HEAD_DIM)
        oh = o[:, sl]
        zh = z[:, sl]
        inv = lax.rsqrt(jnp.mean(oh * oh, axis=-1, keepdims=True) + RMS_EPS)
        parts.append((oh * inv * ng * (zh * _sigmoid(zh))).astype(BF16))
    y_a = _dot(jnp.concatenate(parts, axis=1), wao_ref[...])

    yc = _layer_norm(yc_ref[...].astype(F32), cg_ref[...], cb_ref[...])
    y_b = _dot((yc * _sigmoid(yc)).astype(BF16), wbo_ref[...]) + bbo_ref[...]

    gates = gate_ref[...].astype(F32)
    mixed = gates[:, :D_MODEL] * y_a + gates[:, D_MODEL:] * y_b
    mix = _dot(mixed.astype(BF16), wout_ref[...])

    h0 = _layer_norm(x_ref[...], eg_ref[...], eb_ref[...])
    h1 = _layer_norm(DN_ALPHA * h0 + mix, l1g_ref[...], l1b_ref[...])
    h1_ref[...] = h1
    h1p_ref[...] = _pack_bf16_pair(h1[:, :D_MODEL // 2], h1[:, D_MODEL // 2:])
    h_hi, h_lo = _split_bf16(h1)
    p = _dot(h_hi, wr_ref[...])
    logit_ref[...] = p[:, :LANES] + p[:, LANES:] + _dot(h_lo, wr_ref[:, :LANES]) + br_ref[...]


def _mix(o_f, o_b, u_qkvz, yc, gates, x2, emb_g, emb_b, norm_g, w_a_o, cg, cb, w_b_o, b_b_o,
         w_out, l1g, l1b, w_router, b_router):
    t, d = x2.shape
    tm = min(ROW_TILE, t)
    row = lambda i: (i, 0)
    const = lambda i: (0, 0)
    wr = jnp.concatenate(_split_bf16(jnp.pad(w_router, ((0, 0), (0, LANES - N_EXPERTS)))), axis=1)
    br = jnp.pad(b_router.reshape(1, N_EXPERTS), ((0, 0), (0, LANES - N_EXPERTS)), constant_values=NEG_BIG)
    vec = lambda a: a.reshape(1, -1)
    return pl.pallas_call(
        _mix_kernel,
        out_shape=(jax.ShapeDtypeStruct((t, d), F32), jax.ShapeDtypeStruct((t, d // 2), jnp.uint32),
                   jax.ShapeDtypeStruct((t, LANES), F32)),
        grid=(t // tm,),
        in_specs=[pl.BlockSpec((tm, d), row), pl.BlockSpec((tm, d), row),
                  pl.BlockSpec((tm, d), lambda i: (i, 3)),
                  pl.BlockSpec((tm, d), row), pl.BlockSpec((tm, 2 * d), row), pl.BlockSpec((tm, d), row),
                  pl.BlockSpec((1, d), const), pl.BlockSpec((1, d), const),
                  pl.BlockSpec((1, HEAD_DIM), const), pl.BlockSpec((d, d), const),
                  pl.BlockSpec((1, d), const), pl.BlockSpec((1, d), const),
                  pl.BlockSpec((d, d), const), pl.BlockSpec((1, d), const),
                  pl.BlockSpec((d, d), const), pl.BlockSpec((1, d), const), pl.BlockSpec((1, d), const),
                  pl.BlockSpec((d, 2 * LANES), const), pl.BlockSpec((1, LANES), const)],
        out_specs=(pl.BlockSpec((tm, d), row), pl.BlockSpec((tm, d // 2), row), pl.BlockSpec((tm, LANES), row)),
        compiler_params=_params("parallel"),
        name="mix",
    )(o_f, o_b, u_qkvz, yc, gates, x2, vec(emb_g), vec(emb_b), vec(norm_g), w_a_o, vec(cg), vec(cb),
      w_b_o, vec(b_b_o), w_out, vec(l1g), vec(l1b), wr, br)


def _route_kernel(logit_ref, gate_ref, eidx_ref, rank_ref, cnt_ref, base_ref):
    @pl.when(pl.program_id(0) == 0)
    def _():
        base_ref[...] = jnp.zeros_like(base_ref)

    x = logit_ref[...]
    tm = x.shape[0]
    lane = lax.broadcasted_iota(jnp.int32, x.shape, 1)
    lane_f = lane.astype(F32)
    row = lax.broadcasted_iota(jnp.int32, x.shape, 0)
    sel = jnp.zeros(x.shape, F32)
    vals, idxs = [], []
    for _ in range(TOP_K):
        m = jnp.max(x, axis=1, keepdims=True)
        idx = jnp.min(jnp.where(x == m, lane_f, float(LANES)), axis=1, keepdims=True).astype(jnp.int32)
        hit = lane == idx
        sel = sel + hit.astype(F32)
        x = jnp.where(hit, -3e38, x)
        vals.append(m)
        idxs.append(idx)

    exps = [jnp.exp(v - vals[0]) for v in vals]
    denom = exps[0]
    for e in exps[1:]:
        denom = denom + e

    csum = sel
    s = 1
    while s < tm:
        csum = csum + jnp.where(row >= s, pltpu.roll(csum, s, axis=0), 0.0)
        s *= 2
    before = base_ref[...] + csum - sel

    gate = jnp.zeros(x.shape, F32)
    eidx = jnp.zeros(x.shape, jnp.int32)
    rank = jnp.zeros(x.shape, jnp.int32)
    for k in range(TOP_K):
        rk = jnp.sum(jnp.where(lane == idxs[k], before, 0.0), axis=1, keepdims=True)
        gate = jnp.where(lane == k, exps[k] / denom, gate)
        eidx = jnp.where(lane == k, idxs[k], eidx)
        rank = jnp.where(lane == k, rk.astype(jnp.int32), rank)
    gate_ref[...] = gate
    eidx_ref[...] = eidx
    rank_ref[...] = rank
    total = base_ref[...] + csum[tm - 1:tm, :]
    base_ref[...] = total
    cnt_ref[...] = total


def _route(logits):
    t = logits.shape[0]
    tm = min(ROW_TILE, t)
    bm = EXPERT_ROWS
    row = lambda i: (i, 0)
    gate, eidx, rank, cnt = pl.pallas_call(
        _route_kernel,
        out_shape=(jax.ShapeDtypeStruct((t, LANES), F32), jax.ShapeDtypeStruct((t, LANES), jnp.int32),
                   jax.ShapeDtypeStruct((t, LANES), jnp.int32), jax.ShapeDtypeStruct((1, LANES), F32)),
        grid=(t // tm,),
        in_specs=[pl.BlockSpec((tm, LANES), row)],
        out_specs=(pl.BlockSpec((tm, LANES), row), pl.BlockSpec((tm, LANES), row),
                   pl.BlockSpec((tm, LANES), row), pl.BlockSpec((1, LANES), lambda i: (0, 0))),
        scratch_shapes=[pltpu.VMEM((1, LANES), F32)],
        compiler_params=_params("arbitrary"),
        name="moe_route",
    )(logits)
    counts = cnt[0, :N_EXPERTS].astype(jnp.int32)
    padded = (counts + bm - 1) // bm * bm
    pad_end = jnp.cumsum(padded)
    pad_start = pad_end - padded
    pos = pad_start[eidx[:, :TOP_K]] + rank[:, :TOP_K]
    n_blocks = -(-(t * TOP_K + N_EXPERTS * (bm - 1)) // bm)
    block_start = jnp.arange(n_blocks, dtype=jnp.int32) * bm
    block_exp = jnp.minimum(jnp.sum((block_start[:, None] >= pad_end[None, :]).astype(jnp.int32), axis=1),
                            N_EXPERTS - 1)
    n_used = (pad_end[-1] // bm).astype(jnp.int32).reshape(1)
    return gate[:, :TOP_K], pos.astype(jnp.int32), block_exp, pad_start, pad_end, n_used, n_blocks


def _dispatch_kernel(ps_ref, pe_ref, nb_ref, idx_ref, h_ref, xs_hbm, zero_ref, sem, zsem, *, tm, n_blocks):
    bm = EXPERT_ROWS

    @pl.when(pl.program_id(0) == 0)
    def _():
        zero_ref[...] = jnp.zeros_like(zero_ref)

        def zero_block(start):
            return pltpu.make_async_copy(zero_ref, xs_hbm.at[pl.ds(pl.multiple_of(start, bm), bm), :], zsem)

        for e in range(N_EXPERTS):
            @pl.when(pe_ref[e] > ps_ref[e])
            def _():
                zero_block(pe_ref[e] - bm).start()
        for e in range(N_EXPERTS):
            @pl.when(pe_ref[e] > ps_ref[e])
            def _():
                zero_block(pe_ref[e] - bm).wait()

        def tail_start(b, carry):
            zero_block(b * bm).start()
            return carry

        def tail_wait(b, carry):
            zero_block(b * bm).wait()
            return carry

        lax.fori_loop(nb_ref[0], n_blocks, tail_start, 0)
        lax.fori_loop(nb_ref[0], n_blocks, tail_wait, 0)

    def issue(r, carry):
        for k in range(TOP_K):
            pltpu.make_async_copy(h_ref.at[pl.ds(r, 1), :],
                                  xs_hbm.at[pl.ds(idx_ref[0, 0, r * TOP_K + k], 1), :],
                                  sem).start(priority=k % 2)
        return carry

    lax.fori_loop(0, tm, issue, 0, unroll=GATHER_UNROLL // 2)
    for _ in range(TOP_K):
        pltpu.make_async_copy(h_ref, xs_hbm.at[pl.ds(0, tm), :], sem).wait()


def _dispatch(h1, pos, pad_start, pad_end, n_used, n_blocks):
    t, d = h1.shape
    tm = min(ROW_TILE, t)
    nt = t // tm
    n_slots = n_blocks * EXPERT_ROWS
    grid_spec = pltpu.PrefetchScalarGridSpec(
        num_scalar_prefetch=3,
        grid=(nt,),
        in_specs=[pl.BlockSpec((1, 1, TOP_K * tm), lambda i, *_: (i, 0, 0), memory_space=pltpu.SMEM),
                  pl.BlockSpec((tm, d), lambda i, *_: (i, 0))],
        out_specs=pl.BlockSpec(memory_space=pl.ANY),
        scratch_shapes=[pltpu.VMEM((EXPERT_ROWS, d), h1.dtype), pltpu.SemaphoreType.DMA(()),
                        pltpu.SemaphoreType.DMA(())],
    )
    return pl.pallas_call(
        functools.partial(_dispatch_kernel, tm=tm, n_blocks=n_blocks),
        out_shape=jax.ShapeDtypeStruct((n_slots, d), h1.dtype),
        grid_spec=grid_spec,
        compiler_params=_params("arbitrary"),
        name="moe_dispatch",
    )(pad_start, pad_end, n_used, pos.reshape(nt, 1, TOP_K * tm), h1)


def _expert_kernel(be_ref, nb_ref, x_ref, wgu_ref, bgu_ref, wd_ref, bd_ref, o_ref, wgu16_ref, wd16_ref):
    i = pl.program_id(0)
    active = i < nb_ref[0]
    new_expert = jnp.logical_or(i == 0, be_ref[i] != be_ref[jnp.maximum(i - 1, 0)])

    @pl.when(jnp.logical_and(active, new_expert))
    def _():
        wgu16_ref[...] = wgu_ref[0].astype(BF16)
        wd16_ref[...] = wd_ref[0].astype(BF16)

    @pl.when(active)
    def _():
        half = D_MODEL // 2
        x_lo, x_hi = _unpack_bf16_pair(x_ref[...])
        hgu = (_dot(x_lo.astype(BF16), wgu16_ref[:half, :]) + _dot(x_hi.astype(BF16), wgu16_ref[half:, :])
               + bgu_ref[0])
        glu = jnp.minimum(hgu[:, :D_FF], SWIGLU_LIMIT)
        lin = jnp.clip(hgu[:, D_FF:], -SWIGLU_LIMIT, SWIGLU_LIMIT)
        act = glu * _sigmoid(SWIGLU_ALPHA * glu) * (lin + 1.0)
        y = _dot(act.astype(BF16), wd16_ref[...]) + bd_ref[0]
        o_ref[...] = _pack_bf16_pair(y[:, :half], y[:, half:])

    @pl.when(jnp.logical_not(active))
    def _():
        o_ref[...] = jnp.zeros_like(o_ref)


def _experts(xs, block_exp, n_used, w_gu, b_gu, w_down, b_down, n_blocks):
    d = D_MODEL
    dp = xs.shape[1]
    bm = EXPERT_ROWS
    grid_spec = pltpu.PrefetchScalarGridSpec(
        num_scalar_prefetch=2,
        grid=(n_blocks,),
        in_specs=[pl.BlockSpec((bm, dp), lambda i, be, nb: (i, 0)),
                  pl.BlockSpec((1, d, 2 * D_FF), lambda i, be, nb: (be[i], 0, 0)),
                  pl.BlockSpec((1, 1, 2 * D_FF), lambda i, be, nb: (be[i], 0, 0)),
                  pl.BlockSpec((1, D_FF, d), lambda i, be, nb: (be[i], 0, 0)),
                  pl.BlockSpec((1, 1, d), lambda i, be, nb: (be[i], 0, 0))],
        out_specs=pl.BlockSpec((bm, dp), lambda i, be, nb: (i, 0)),
        scratch_shapes=[pltpu.VMEM((d, 2 * D_FF), BF16), pltpu.VMEM((D_FF, d), BF16)],
    )
    return pl.pallas_call(
        _expert_kernel,
        out_shape=jax.ShapeDtypeStruct((n_blocks * bm, dp), jnp.uint32),
        grid_spec=grid_spec,
        compiler_params=_params("arbitrary"),
        name="moe_experts",
    )(block_exp, n_used, xs, w_gu, b_gu.reshape(N_EXPERTS, 1, 2 * D_FF), w_down,
      b_down.reshape(N_EXPERTS, 1, d))


def _sc_gather_rows(table, idx):
    m = idx.shape[0]
    d = table.shape[1]
    n_workers = SC_CORES * SC_SUBCORES
    per_worker = m // n_workers
    n_iter = per_worker // SC_GATHER_ROWS
    mesh = plsc.VectorSubcoreMesh(core_axis_name="c", subcore_axis_name="s")

    @functools.partial(
        pl.kernel, mesh=mesh,
        out_type=jax.ShapeDtypeStruct((m, d), table.dtype),
        scratch_types=[pltpu.VMEM((SC_GATHER_ROWS,), jnp.int32),
                       pltpu.VMEM((SC_GATHER_ROWS, d), table.dtype),
                       pltpu.SemaphoreType.DMA],
        name="sc_gather_rows",
    )
    def gather(table_hbm, idx_hbm, out_hbm, idx_v, rows_v, sem):
        wid = lax.axis_index("s") * SC_CORES + lax.axis_index("c")
        base = wid * per_worker

        @pl.loop(0, n_iter)
        def _(j):
            off = pl.multiple_of(base + j * SC_GATHER_ROWS, SC_GATHER_ROWS)
            pltpu.sync_copy(idx_hbm.at[pl.ds(off, SC_GATHER_ROWS)], idx_v)
            pltpu.async_copy(table_hbm.at[idx_v], rows_v, sem).wait()
            pltpu.sync_copy(rows_v, out_hbm.at[pl.ds(off, SC_GATHER_ROWS)])

    return gather(table, idx)


def _combine_dense_kernel(y0_ref, y1_ref, y2_ref, y3_ref, gate_ref, h_ref, g_ref, b_ref, o_ref):
    gate = gate_ref[...]
    f_lo = f_hi = None
    for k, y_ref in enumerate((y0_ref, y1_ref, y2_ref, y3_ref)):
        y_lo, y_hi = _unpack_bf16_pair(y_ref[...])
        gk = gate[:, k:k + 1]
        f_lo = gk * y_lo if f_lo is None else f_lo + gk * y_lo
        f_hi = gk * y_hi if f_hi is None else f_hi + gk * y_hi
    f = jnp.concatenate([f_lo, f_hi], axis=1)
    o_ref[...] = _layer_norm(DN_ALPHA * h_ref[...] + f, g_ref[...], b_ref[...])


def _combine_dense(yg, gate, h1, ln_g, ln_b):
    t, d = h1.shape
    tm = min(ROW_TILE, t)
    nt = t // tm
    dp = yg.shape[1]
    slab = lambda k: pl.BlockSpec((tm, dp), lambda i, k=k: (k * nt + i, 0))
    return pl.pallas_call(
        _combine_dense_kernel,
        out_shape=jax.ShapeDtypeStruct((t, d), F32),
        grid=(nt,),
        in_specs=[slab(0), slab(1), slab(2), slab(3),
                  pl.BlockSpec((tm, TOP_K), lambda i: (i, 0)),
                  pl.BlockSpec((tm, d), lambda i: (i, 0)),
                  pl.BlockSpec((1, d), lambda i: (0, 0)),
                  pl.BlockSpec((1, d), lambda i: (0, 0))],
        out_specs=pl.BlockSpec((tm, d), lambda i: (i, 0)),
        compiler_params=_params("parallel"),
        name="moe_combine",
    )(yg, yg, yg, yg, gate, h1, ln_g.reshape(1, d), ln_b.reshape(1, d))


def _combine_kernel(idx_ref, idx_next_ref, ys_hbm, gate_ref, h_ref, g_ref, b_ref, o_ref, buf_ref, sem, *, tm, nt):
    i = pl.program_id(0)
    n = TOP_K * tm
    slot = i % 2

    def gather(idx, s):
        def issue(r, carry):
            for j in range(2):
                rr = 2 * r + j
                pltpu.make_async_copy(ys_hbm.at[pl.ds(idx[0, 0, rr], 1), :],
                                      buf_ref.at[s, pl.ds(rr, 1), :], sem.at[s]).start(priority=j)
            return carry

        lax.fori_loop(0, n // 2, issue, 0, unroll=GATHER_UNROLL // 2)

    @pl.when(i == 0)
    def _():
        gather(idx_ref, 0)

    @pl.when(i + 1 < nt)
    def _():
        gather(idx_next_ref, 1 - slot)

    pltpu.make_async_copy(ys_hbm.at[pl.ds(0, n), :], buf_ref.at[slot], sem.at[slot]).wait()
    gate = gate_ref[...]
    f_lo = f_hi = None
    for k in range(TOP_K):
        y_lo, y_hi = _unpack_bf16_pair(buf_ref[slot, k * tm:(k + 1) * tm, :])
        gk = gate[:, k:k + 1]
        f_lo = gk * y_lo if f_lo is None else f_lo + gk * y_lo
        f_hi = gk * y_hi if f_hi is None else f_hi + gk * y_hi
    f = jnp.concatenate([f_lo, f_hi], axis=1)
    o_ref[...] = _layer_norm(DN_ALPHA * h_ref[...] + f, g_ref[...], b_ref[...])


def _combine(ys, pos, gate, h1, ln_g, ln_b):
    t, d = h1.shape
    tm = min(COMBINE_ROWS, t)
    nt = t // tm
    idx = pos.reshape(nt, tm, TOP_K).transpose(0, 2, 1).reshape(nt, 1, TOP_K * tm)
    return pl.pallas_call(
        functools.partial(_combine_kernel, tm=tm, nt=nt),
        out_shape=jax.ShapeDtypeStruct((t, d), F32),
        grid=(nt,),
        in_specs=[pl.BlockSpec((1, 1, TOP_K * tm), lambda i: (i, 0, 0), memory_space=pltpu.SMEM),
                  pl.BlockSpec((1, 1, TOP_K * tm), lambda i: (jnp.minimum(i + 1, nt - 1), 0, 0),
                               memory_space=pltpu.SMEM),
                  pl.BlockSpec(memory_space=pl.ANY),
                  pl.BlockSpec((tm, TOP_K), lambda i: (i, 0)),
                  pl.BlockSpec((tm, d), lambda i: (i, 0)),
                  pl.BlockSpec((1, d), lambda i: (0, 0)),
                  pl.BlockSpec((1, d), lambda i: (0, 0))],
        out_specs=pl.BlockSpec((tm, d), lambda i: (i, 0)),
        scratch_shapes=[pltpu.VMEM((2, TOP_K * tm, ys.shape[1]), ys.dtype), pltpu.SemaphoreType.DMA((2,))],
        compiler_params=_params("arbitrary"),
        name="moe_combine",
    )(idx, idx, ys, gate, h1, ln_g.reshape(1, d), ln_b.reshape(1, d))


def kernel(x, emb_ln_g, emb_ln_b, w_in, conv_qkv, a_log, dt_bias, dn_norm_g, w_a_o, b_glu, conv_dw, b_dw, conv_ln_g, conv_ln_b, w_b_o, b_b_o, b_gate, w_out, ln1_g, ln1_b, w_router, b_router, w_gu, b_gu, w_down, b_down, ln2_g, ln2_b):
    bsz, seq, d = x.shape
    t = bsz * seq
    x2 = x.reshape(t, d)
    u_qkvz, g, gt, glu, gates = _inproj(x2, emb_ln_g, emb_ln_b, w_in[0], b_glu[0], b_gate[0], a_log[0], dt_bias[0])

    qkv = _qkv_conv(u_qkvz, conv_qkv[0], bsz, seq)
    gtp = gt.reshape(2, N_HEADS // 2, 2, t // CHUNK, CHUNK).transpose(0, 1, 3, 2, 4).reshape(N_HEADS, 2 * t)
    o_f, o_b = _delta_rule(qkv, g, gtp, bsz, seq)
    yc = _dw_conv(glu, conv_dw[0], b_dw[0], bsz, seq)

    h1, h1p, logits = _mix(o_f, o_b, u_qkvz, yc, gates, x2, emb_ln_g, emb_ln_b, dn_norm_g[0],
                           w_a_o[0].astype(BF16), conv_ln_g[0], conv_ln_b[0], w_b_o[0].astype(BF16), b_b_o[0],
                           w_out[0].astype(BF16), ln1_g[0], ln1_b[0], w_router[0], b_router[0])

    gate, pos, block_exp, pad_start, pad_end, n_used, n_blocks = _route(logits)
    xs = _dispatch(h1p, pos, pad_start, pad_end, n_used, n_blocks)
    ys = _experts(xs, block_exp, n_used, w_gu[0], b_gu[0], w_down[0], b_down[0], n_blocks)
    yg = _sc_gather_rows(ys, pos.T.reshape(-1))
    out = _combine_dense(yg, gate, h1, ln2_g[0], ln2_b[0])
    return out.reshape(bsz, seq, d)
```

```python
import functools

import jax
import jax.numpy as jnp
from jax import lax
from jax.experimental import pallas as pl
from jax.experimental.pallas import tpu as pltpu
from jax.experimental.pallas import tpu_sc as plsc

F32 = jnp.float32
BF16 = jnp.bfloat16

D_MODEL = 1024
N_HEADS = 8
HEAD_DIM = 128
WIDTH_A = N_HEADS * HEAD_DIM
SHORT_CONV = 5
CHUNK = 64
WIDTH_B = D_MODEL
DW_CONV = 31
N_EXPERTS = 32
TOP_K = 4
D_FF = D_MODEL
SWIGLU_ALPHA = 1.702
SWIGLU_LIMIT = 7.0
DN_ALPHA = 2.0 ** 0.25
LN_EPS = 1e-5
RMS_EPS = 1e-6
L2_EPS = 1e-6
LANES = 128
NEG_BIG = -1e30

ROW_TILE = 512
DELTA_ROWS = 256
EXPERT_ROWS = 512
SC_CORES = 2
SC_SUBCORES = 16
SC_GATHER_ROWS = 64
VMEM_LIMIT = 56 * 1024 * 1024


def _params(*sem):
    return pltpu.CompilerParams(dimension_semantics=sem, vmem_limit_bytes=VMEM_LIMIT)


def _layer_norm(x, g, b):
    mu = jnp.mean(x, axis=-1, keepdims=True)
    xc = x - mu
    var = jnp.mean(xc * xc, axis=-1, keepdims=True)
    return xc * lax.rsqrt(var + LN_EPS) * g + b


def _sigmoid(x):
    return 1.0 / (1.0 + jnp.exp(-x))


def _dot(a, b):
    return jnp.dot(a, b, preferred_element_type=F32)


def _pack_bf16_pair(a, b):
    ua = lax.bitcast_convert_type(a.astype(BF16).astype(F32), jnp.uint32)
    ub = lax.bitcast_convert_type(b.astype(BF16).astype(F32), jnp.uint32)
    return (ua >> 16) | ub


def _unpack_bf16_pair(p):
    a = lax.bitcast_convert_type(p << 16, F32)
    b = lax.bitcast_convert_type(p & jnp.uint32(0xFFFF0000), F32)
    return a, b


def _split_bf16(a):
    hi = a.astype(BF16)
    return hi, (a - hi.astype(F32)).astype(BF16)


def _chunk_cumsum(x, reverse):
    rows = x.shape[0]
    pos = lax.broadcasted_iota(jnp.int32, x.shape, 0) % CHUNK
    s = 1
    while s < CHUNK:
        if reverse:
            shifted = pltpu.roll(x, rows - s, axis=0)
            x = x + jnp.where(pos < CHUNK - s, shifted, 0.0)
        else:
            shifted = pltpu.roll(x, s, axis=0)
            x = x + jnp.where(pos >= s, shifted, 0.0)
        s *= 2
    return x


def _inproj_kernel(x_ref, eg_ref, eb_ref, wq_ref, ws_ref, wga_ref, wgb_ref, bga_ref, bgb_ref, wgt_ref, bgt_ref,
                   alog_ref, dtb_ref, u_ref, g_ref, gt_ref, glu_ref, gate_ref, *, tn):
    h = _layer_norm(x_ref[...], eg_ref[...], eb_ref[...]).astype(BF16)

    for n0 in range(0, u_ref.shape[1], tn):
        u_ref[:, n0:n0 + tn] = _dot(h, wq_ref[:, n0:n0 + tn]).astype(u_ref.dtype)

    us = _dot(h, ws_ref[...])
    lane = lax.broadcasted_iota(jnp.int32, us.shape, 1)
    beta = _sigmoid(us)
    xs = us + dtb_ref[...]
    softplus = jnp.maximum(xs, 0.0) + jnp.log(1.0 + jnp.exp(-jnp.abs(xs)))
    log_a = -jnp.exp(alog_ref[...]) * softplus
    g_fwd = _chunk_cumsum(log_a, reverse=False)
    g_bwd = _chunk_cumsum(log_a, reverse=True)
    gates = jnp.where(lane < 2 * N_HEADS, beta, jnp.where(lane < 3 * N_HEADS, g_fwd, g_bwd))
    g_ref[...] = gates
    gt_ref[...] = gates.T[2 * N_HEADS:4 * N_HEADS, :]

    for n0 in range(0, glu_ref.shape[1], tn):
        lin = _dot(h, wga_ref[:, n0:n0 + tn]) + bga_ref[:, n0:n0 + tn]
        gt = _dot(h, wgb_ref[:, n0:n0 + tn]) + bgb_ref[:, n0:n0 + tn]
        glu_ref[:, n0:n0 + tn] = (lin * _sigmoid(gt)).astype(glu_ref.dtype)

    for n0 in range(0, gate_ref.shape[1], tn):
        gate_ref[:, n0:n0 + tn] = _sigmoid(_dot(h, wgt_ref[:, n0:n0 + tn])
                                           + bgt_ref[:, n0:n0 + tn]).astype(gate_ref.dtype)


def _inproj(x2, emb_g, emb_b, w_in, b_glu, b_gate, a_log, dt_bias):
    t, d = x2.shape
    tm = min(ROW_TILE, t)
    c0 = 4 * WIDTH_A
    c1 = c0 + 4 * N_HEADS
    c2 = c1 + 2 * WIDTH_B
    wb = w_in.astype(BF16)
    pad = LANES - 4 * N_HEADS
    w_small = jnp.pad(wb[:, c0:c1], ((0, 0), (0, pad)))
    alog = jnp.pad(a_log.reshape(1, 2 * N_HEADS), ((0, 0), (2 * N_HEADS, pad)))
    dtb = jnp.pad(dt_bias.reshape(1, 2 * N_HEADS), ((0, 0), (2 * N_HEADS, pad)))
    row = lambda i: (i, 0)

    def const(shape):
        return pl.BlockSpec(shape, lambda i: (0, 0), pipeline_mode=pl.Buffered(1))

    vec = lambda a: a.reshape(1, -1)
    bf = lambda n: jax.ShapeDtypeStruct((t, n), BF16)
    return pl.pallas_call(
        functools.partial(_inproj_kernel, tn=512),
        out_shape=(bf(c0), jax.ShapeDtypeStruct((t, LANES), F32), jax.ShapeDtypeStruct((2 * N_HEADS, t), F32),
                   bf(WIDTH_B), bf(2 * d)),
        grid=(t // tm,),
        in_specs=[pl.BlockSpec((tm, d), row), const((1, d)), const((1, d)),
                  const((d, c0)), const((d, LANES)),
                  const((d, WIDTH_B)), const((d, WIDTH_B)), const((1, WIDTH_B)), const((1, WIDTH_B)),
                  const((d, 2 * d)), const((1, 2 * d)), const((1, LANES)), const((1, LANES))],
        out_specs=(pl.BlockSpec((tm, c0), row), pl.BlockSpec((tm, LANES), row),
                   pl.BlockSpec((2 * N_HEADS, tm), lambda i: (0, i)),
                   pl.BlockSpec((tm, WIDTH_B), row), pl.BlockSpec((tm, 2 * d), row)),
        compiler_params=_params("parallel"),
        name="inproj",
    )(x2, vec(emb_g), vec(emb_b), wb[:, :c0], w_small, wb[:, c1:c1 + WIDTH_B], wb[:, c1 + WIDTH_B:c2],
      vec(b_glu[:WIDTH_B]), vec(b_glu[WIDTH_B:]), wb[:, c2:], vec(b_gate), alog, dtb)


def _conv_rows(xp_ref, w, taps, base, r0, rows):
    acc = xp_ref[base + r0:base + r0 + rows, :] * w[0:1, :]
    for k in range(1, taps):
        acc = acc + xp_ref[base + k + r0:base + k + r0 + rows, :] * w[k:k + 1, :]
    return acc


def _fill_padded(xp_ref, x_ref, pad, seq):
    zeros = jnp.zeros((pad, xp_ref.shape[1]), F32)
    xp_ref[0:pad, :] = zeros
    xp_ref[pad + seq:pad + seq + pad, :] = zeros
    xp_ref[pad:pad + seq, :] = x_ref[...].astype(F32)


def _qkv_conv_kernel(u_ref, w_ref, o_ref, xp_ref, *, seq, rows):
    pad = 8
    j = pl.program_id(1)
    _fill_padded(xp_ref, u_ref, pad, seq)
    w = w_ref[...]
    is_q = j < N_HEADS
    is_qk = j < 2 * N_HEADS
    for r0 in range(0, seq, rows):
        y = _conv_rows(xp_ref, w, SHORT_CONV, pad - SHORT_CONV // 2, r0, rows)
        y = y * _sigmoid(y)
        inv = lax.rsqrt(jnp.sum(y * y, axis=-1, keepdims=True) + L2_EPS)
        scale = jnp.where(is_q, inv * (HEAD_DIM ** -0.5), jnp.where(is_qk, inv, 1.0))
        o_ref[r0:r0 + rows, :] = (y * scale).astype(o_ref.dtype)


def _qkv_conv(u_qkvz, conv_w, bsz, seq):
    t = bsz * seq
    ncol = 3 * N_HEADS
    rows = min(256, seq)
    return pl.pallas_call(
        functools.partial(_qkv_conv_kernel, seq=seq, rows=rows),
        out_shape=jax.ShapeDtypeStruct((t, 3 * WIDTH_A), BF16),
        grid=(bsz, ncol),
        in_specs=[pl.BlockSpec((seq, HEAD_DIM), lambda b, j: (b, j)),
                  pl.BlockSpec((SHORT_CONV, HEAD_DIM), lambda b, j: (0, j))],
        out_specs=pl.BlockSpec((seq, HEAD_DIM), lambda b, j: (b, j)),
        scratch_shapes=[pltpu.VMEM((seq + 16, HEAD_DIM), F32)],
        compiler_params=_params("parallel", "parallel"),
        name="qkv_conv",
    )(u_qkvz, conv_w)


def _dw_conv_kernel(x_ref, w_ref, b_ref, o_ref, xp_ref, *, seq, rows):
    pad = 16
    _fill_padded(xp_ref, x_ref, pad, seq)
    w = w_ref[...]
    for r0 in range(0, seq, rows):
        y = _conv_rows(xp_ref, w, DW_CONV, pad - DW_CONV // 2, r0, rows) + b_ref[...]
        o_ref[r0:r0 + rows, :] = y.astype(o_ref.dtype)


def _dw_conv(glu, conv_w, b_dw, bsz, seq):
    t = bsz * seq
    rows = min(256, seq)
    return pl.pallas_call(
        functools.partial(_dw_conv_kernel, seq=seq, rows=rows),
        out_shape=jax.ShapeDtypeStruct((t, WIDTH_B), BF16),
        grid=(bsz, WIDTH_B // LANES),
        in_specs=[pl.BlockSpec((seq, LANES), lambda b, j: (b, j)),
                  pl.BlockSpec((DW_CONV, LANES), lambda b, j: (0, j)),
                  pl.BlockSpec((1, LANES), lambda b, j: (0, j))],
        out_specs=pl.BlockSpec((seq, LANES), lambda b, j: (b, j)),
        scratch_shapes=[pltpu.VMEM((seq + 32, LANES), F32)],
        compiler_params=_params("parallel", "parallel"),
        name="dw_conv",
    )(glu, conv_w, b_dw.reshape(1, WIDTH_B))


def _bmm(a, b):
    return lax.dot_general(a, b, (((2,), (1,)), ((0,), (0,))), preferred_element_type=F32)


def _bmm_nt(a, b):
    return lax.dot_general(a, b, (((2,), (2,)), ((0,), (0,))), preferred_element_type=F32)


def _bmm_tn(a, b):
    return lax.dot_general(a, b, (((1,), (1,)), ((0,), (0,))), preferred_element_type=F32)


def _block_diag_rows(x, half):
    lane = lax.broadcasted_iota(jnp.int32, x.shape, 2)
    return jnp.concatenate([jnp.where(lane < half, x, 0.0), jnp.where(lane >= half, x, 0.0)], axis=1)


def _unit_tri_inverse(lmat, eye):
    def rhs(p):
        return _block_diag_rows(p, CHUNK).astype(BF16)

    x = eye - lmat
    p = _bmm(lmat.astype(BF16), rhs(lmat))
    s = 2
    while 2 * s < CHUNK:
        xp = _bmm(jnp.concatenate([x, p], axis=1).astype(BF16), rhs(p))
        x = x + xp[:, :CHUNK]
        p = xp[:, CHUNK:]
        s *= 2
    return x + _bmm(x.astype(BF16), rhs(p))


def _delta_kernel(qf_ref, kf_ref, vf_ref, gf_ref, gtpf_ref, qb_ref, kb_ref, vb_ref, gb_ref, gtpb_ref,
                  of_ref, ob_ref, s_ref, *, nc):
    @pl.when(pl.program_id(1) == 0)
    def _():
        s_ref[...] = jnp.zeros_like(s_ref)

    n_pairs = N_HEADS // 2
    n_inst = 2 * n_pairs
    pw = 2 * HEAD_DIM
    dirs = ((qf_ref, kf_ref, vf_ref, gf_ref, gtpf_ref, of_ref, False),
            (qb_ref, kb_ref, vb_ref, gb_ref, gtpb_ref, ob_ref, True))
    steps = [[d + ((nc - 1 - i) if d[6] else i,) for d in dirs] for i in range(nc)]

    ri = lax.broadcasted_iota(jnp.int32, (CHUNK, 2 * CHUNK), 0)
    ci = jnp.bitwise_and(lax.broadcasted_iota(jnp.int32, (CHUNK, 2 * CHUNK), 1), CHUNK - 1)
    inst = lax.broadcasted_iota(jnp.int32, (nc * n_inst, 1, 1), 0)
    sign = 1 - 2 * jnp.bitwise_and(jnp.right_shift(inst, n_pairs.bit_length() - 1), 1)
    rel = (ri - ci)[None] * sign
    incl = rel >= 0
    strict = rel > 0
    eye = (ri == ci).astype(F32)

    def pairs(which):
        return jnp.stack([d[which][d[7] * CHUNK:(d[7] + 1) * CHUNK, p * pw:(p + 1) * pw]
                          for st in steps for d in st for p in range(n_pairs)]).astype(F32)

    def pair_bcast(cols, width):
        return jnp.stack([jnp.concatenate([jnp.broadcast_to(cc[2 * p], (CHUNK, width)),
                                           jnp.broadcast_to(cc[2 * p + 1], (CHUNK, width))], axis=1)
                          for cc in cols for p in range(n_pairs)])

    qf = pairs(0)
    kf = pairs(1)
    vf = pairs(2)
    beta_c, g_c, glast_c = [], [], []
    for st in steps:
        for d in st:
            gblk = d[3][d[7] * CHUNK:(d[7] + 1) * CHUNK, :]
            off = N_HEADS if d[6] else 0
            last = 0 if d[6] else CHUNK - 1
            beta_c.append([gblk[:, off + hh:off + hh + 1] for hh in range(N_HEADS)])
            g_c.append([gblk[:, 2 * N_HEADS + off + hh:2 * N_HEADS + off + hh + 1] for hh in range(N_HEADS)])
            glast_c.append([gc[last:last + 1, :] for gc in g_c[-1]])
    beta = pair_bcast(beta_c, HEAD_DIM)
    eg = pair_bcast([[jnp.exp(gc) for gc in gcs] for gcs in g_c], HEAD_DIM)
    tail = pair_bcast([[jnp.exp(gl - gc) for gl, gc in zip(gls, gcs)] for gls, gcs in zip(glast_c, g_c)],
                      HEAD_DIM)
    gcol = pair_bcast(g_c, CHUNK)
    grow = jnp.stack([d[4][(n_pairs if d[6] else 0) + p:(n_pairs if d[6] else 0) + p + 1,
                           2 * d[7] * CHUNK:2 * (d[7] + 1) * CHUNK]
                      for st in steps for d in st for p in range(n_pairs)])

    decay = jnp.exp(jnp.where(incl, gcol - grow, NEG_BIG))
    kb = kf * beta
    kkqk = _bmm_nt(jnp.concatenate([kb, qf], axis=1).astype(BF16),
                   _block_diag_rows(kf, HEAD_DIM).astype(BF16))
    lmat = jnp.where(strict, kkqk[:, :CHUNK, :] * decay, 0.0)
    qk = (kkqk[:, CHUNK:, :] * decay).astype(BF16)
    tinv = _unit_tri_inverse(lmat, eye)
    rhs = jnp.concatenate([_block_diag_rows(vf * beta, HEAD_DIM),
                           _block_diag_rows(kb * eg, HEAD_DIM)], axis=2).astype(BF16)
    uw = _bmm(tinv.astype(BF16), rhs)
    u = uw[:, :, :pw]
    wq = jnp.concatenate([uw[:, :, pw:], qf * eg], axis=1).astype(BF16)
    kt = (kf * tail).astype(BF16)

    for i, st in enumerate(steps):
        sl = slice(i * n_inst, (i + 1) * n_inst)
        s_a = s_ref[:, 0]
        s_b = s_ref[:, 1]
        zero = jnp.zeros_like(s_a)
        s_bd = jnp.concatenate([jnp.concatenate([s_a, zero], axis=2),
                                jnp.concatenate([zero, s_b], axis=2)], axis=1).astype(BF16)
        ws = _bmm(wq[sl], s_bd)
        v_new = u[sl] - ws[:, :CHUNK, :]
        o = ws[:, CHUNK:, :] + _bmm(qk[sl], _block_diag_rows(v_new, HEAD_DIM).astype(BF16))
        v16 = v_new.astype(BF16)
        gl = [glast_c[2 * i + dd] for dd in range(2)]
        cd_a = jnp.stack([jnp.exp(gl[dd][2 * p]) for dd in range(2) for p in range(n_pairs)])
        cd_b = jnp.stack([jnp.exp(gl[dd][2 * p + 1]) for dd in range(2) for p in range(n_pairs)])
        s_ref[:, 0] = s_a * cd_a + _bmm_tn(kt[sl, :, :HEAD_DIM], v16[:, :, :HEAD_DIM])
        s_ref[:, 1] = s_b * cd_b + _bmm_tn(kt[sl, :, HEAD_DIM:], v16[:, :, HEAD_DIM:])
        for dd, d in enumerate(st):
            for p in range(n_pairs):
                d[5][d[7] * CHUNK:(d[7] + 1) * CHUNK, p * pw:(p + 1) * pw] = (
                    o[dd * n_pairs + p].astype(d[5].dtype))


def _delta_rule(qkv, g, gtp, bsz, seq):
    t = bsz * seq
    rows = min(DELTA_ROWS, seq)
    nblk = seq // rows

    def fwd(col):
        return lambda b, i: (b * nblk + i, col)

    def bwd(col):
        return lambda b, i: (b * nblk + nblk - 1 - i, col)

    def specs(m):
        return [pl.BlockSpec((rows, WIDTH_A), m(0)), pl.BlockSpec((rows, WIDTH_A), m(1)),
                pl.BlockSpec((rows, WIDTH_A), m(2)), pl.BlockSpec((rows, LANES), m(0)),
                pl.BlockSpec((N_HEADS, 2 * rows), lambda b, i, m=m: (0, m(0)(b, i)[0]))]

    out = jax.ShapeDtypeStruct((t, WIDTH_A), BF16)
    return pl.pallas_call(
        functools.partial(_delta_kernel, nc=rows // CHUNK),
        out_shape=(out, out),
        grid=(bsz, nblk),
        in_specs=specs(fwd) + specs(bwd),
        out_specs=(pl.BlockSpec((rows, WIDTH_A), fwd(0)), pl.BlockSpec((rows, WIDTH_A), bwd(0))),
        scratch_shapes=[pltpu.VMEM((N_HEADS, 2, HEAD_DIM, HEAD_DIM), F32)],
        compiler_params=_params("parallel", "arbitrary"),
        name="delta_rule",
    )(qkv, qkv, qkv, g, gtp, qkv, qkv, qkv, g, gtp)


def _mix_kernel(of_ref, ob_ref, z_ref, yc_ref, gate_ref, x_ref,
                eg_ref, eb_ref, ng_ref, wao_ref, cg_ref, cb_ref, wbo_ref, bbo_ref,
                wout_ref, l1g_ref, l1b_ref, wr_ref, br_ref,
                h1_ref, h1p_ref, logit_ref):
    o = of_ref[...].astype(F32) + ob_ref[...].astype(F32)
    z = z_ref[...].astype(F32)
    ng = ng_ref[...]
    parts = []
    for hh in range(N_HEADS):
        sl = slice(hh * HEAD_DIM, (hh + 1) * HEAD_DIM)
        oh = o[:, sl]
        zh = z[:, sl]
        inv = lax.rsqrt(jnp.mean(oh * oh, axis=-1, keepdims=True) + RMS_EPS)
        parts.append((oh * inv * ng * (zh * _sigmoid(zh))).astype(BF16))
    y_a = _dot(jnp.concatenate(parts, axis=1), wao_ref[...])

    yc = _layer_norm(yc_ref[...].astype(F32), cg_ref[...], cb_ref[...])
    y_b = _dot((yc * _sigmoid(yc)).astype(BF16), wbo_ref[...]) + bbo_ref[...]

    gates = gate_ref[...].astype(F32)
    mixed = gates[:, :D_MODEL] * y_a + gates[:, D_MODEL:] * y_b
    mix = _dot(mixed.astype(BF16), wout_ref[...])

    h0 = _layer_norm(x_ref[...], eg_ref[...], eb_ref[...])
    h1 = _layer_norm(DN_ALPHA * h0 + mix, l1g_ref[...], l1b_ref[...])
    h1_ref[...] = h1
    h1p_ref[...] = _pack_bf16_pair(h1[:, :D_MODEL // 2], h1[:, D_MODEL // 2:])
    h_hi, h_lo = _split_bf16(h1)
    p = _dot(h_hi, wr_ref[...])
    logit_ref[...] = p[:, :LANES] + p[:, LANES:] + _dot(h_lo, wr_ref[:, :LANES]) + br_ref[...]


def _mix(o_f, o_b, u_qkvz, yc, gates, x2, emb_g, emb_b, norm_g, w_a_o, cg, cb, w_b_o, b_b_o,
         w_out, l1g, l1b, w_router, b_router):
    t, d = x2.shape
    tm = min(ROW_TILE, t)
    row = lambda i: (i, 0)
    const = lambda i: (0, 0)
    wr = jnp.concatenate(_split_bf16(jnp.pad(w_router, ((0, 0), (0, LANES - N_EXPERTS)))), axis=1)
    br = jnp.pad(b_router.reshape(1, N_EXPERTS), ((0, 0), (0, LANES - N_EXPERTS)), constant_values=NEG_BIG)
    vec = lambda a: a.reshape(1, -1)
    return pl.pallas_call(
        _mix_kernel,
        out_shape=(jax.ShapeDtypeStruct((t, d), F32), jax.ShapeDtypeStruct((t, d // 2), jnp.uint32),
                   jax.ShapeDtypeStruct((t, LANES), F32)),
        grid=(t // tm,),
        in_specs=[pl.BlockSpec((tm, d), row), pl.BlockSpec((tm, d), row),
                  pl.BlockSpec((tm, d), lambda i: (i, 3)),
                  pl.BlockSpec((tm, d), row), pl.BlockSpec((tm, 2 * d), row), pl.BlockSpec((tm, d), row),
                  pl.BlockSpec((1, d), const), pl.BlockSpec((1, d), const),
                  pl.BlockSpec((1, HEAD_DIM), const), pl.BlockSpec((d, d), const),
                  pl.BlockSpec((1, d), const), pl.BlockSpec((1, d), const),
                  pl.BlockSpec((d, d), const), pl.BlockSpec((1, d), const),
                  pl.BlockSpec((d, d), const), pl.BlockSpec((1, d), const), pl.BlockSpec((1, d), const),
                  pl.BlockSpec((d, 2 * LANES), const), pl.BlockSpec((1, LANES), const)],
        out_specs=(pl.BlockSpec((tm, d), row), pl.BlockSpec((tm, d // 2), row), pl.BlockSpec((tm, LANES), row)),
        compiler_params=_params("parallel"),
        name="mix",
    )(o_f, o_b, u_qkvz, yc, gates, x2, vec(emb_g), vec(emb_b), vec(norm_g), w_a_o, vec(cg), vec(cb),
      w_b_o, vec(b_b_o), w_out, vec(l1g), vec(l1b), wr, br)


def _route_kernel(logit_ref, gate_ref, eidx_ref, rank_ref, cnt_ref, base_ref):
    @pl.when(pl.program_id(0) == 0)
    def _():
        base_ref[...] = jnp.zeros_like(base_ref)

    x = logit_ref[...]
    tm = x.shape[0]
    lane = lax.broadcasted_iota(jnp.int32, x.shape, 1)
    lane_f = lane.astype(F32)
    row = lax.broadcasted_iota(jnp.int32, x.shape, 0)
    sel = jnp.zeros(x.shape, F32)
    vals, idxs = [], []
    for _ in range(TOP_K):
        m = jnp.max(x, axis=1, keepdims=True)
        idx = jnp.min(jnp.where(x == m, lane_f, float(LANES)), axis=1, keepdims=True).astype(jnp.int32)
        hit = lane == idx
        sel = sel + hit.astype(F32)
        x = jnp.where(hit, -3e38, x)
        vals.append(m)
        idxs.append(idx)

    exps = [jnp.exp(v - vals[0]) for v in vals]
    denom = exps[0]
    for e in exps[1:]:
        denom = denom + e

    csum = sel
    s = 1
    while s < tm:
        csum = csum + jnp.where(row >= s, pltpu.roll(csum, s, axis=0), 0.0)
        s *= 2
    before = base_ref[...] + csum - sel

    gate = jnp.zeros(x.shape, F32)
    eidx = jnp.zeros(x.shape, jnp.int32)
    rank = jnp.zeros(x.shape, jnp.int32)
    for k in range(TOP_K):
        rk = jnp.sum(jnp.where(lane == idxs[k], before, 0.0), axis=1, keepdims=True)
        gate = jnp.where(lane == k, exps[k] / denom, gate)
        eidx = jnp.where(lane == k, idxs[k], eidx)
        rank = jnp.where(lane == k, rk.astype(jnp.int32), rank)
    gate_ref[...] = gate
    eidx_ref[...] = eidx
    rank_ref[...] = rank
    total = base_ref[...] + csum[tm - 1:tm, :]
    base_ref[...] = total
    cnt_ref[...] = total


def _route(logits):
    t = logits.shape[0]
    tm = min(ROW_TILE, t)
    bm = EXPERT_ROWS
    row = lambda i: (i, 0)
    gate, eidx, rank, cnt = pl.pallas_call(
        _route_kernel,
        out_shape=(jax.ShapeDtypeStruct((t, LANES), F32), jax.ShapeDtypeStruct((t, LANES), jnp.int32),
                   jax.ShapeDtypeStruct((t, LANES), jnp.int32), jax.ShapeDtypeStruct((1, LANES), F32)),
        grid=(t // tm,),
        in_specs=[pl.BlockSpec((tm, LANES), row)],
        out_specs=(pl.BlockSpec((tm, LANES), row), pl.BlockSpec((tm, LANES), row),
                   pl.BlockSpec((tm, LANES), row), pl.BlockSpec((1, LANES), lambda i: (0, 0))),
        scratch_shapes=[pltpu.VMEM((1, LANES), F32)],
        compiler_params=_params("arbitrary"),
        name="moe_route",
    )(logits)
    counts = cnt[0, :N_EXPERTS].astype(jnp.int32)
    padded = (counts + bm - 1) // bm * bm
    pad_end = jnp.cumsum(padded)
    pad_start = pad_end - padded
    pos = pad_start[eidx[:, :TOP_K]] + rank[:, :TOP_K]
    n_blocks = -(-(t * TOP_K + N_EXPERTS * (bm - 1)) // bm)
    block_start = jnp.arange(n_blocks, dtype=jnp.int32) * bm
    block_exp = jnp.minimum(jnp.sum((block_start[:, None] >= pad_end[None, :]).astype(jnp.int32), axis=1),
                            N_EXPERTS - 1)
    block_rows = jnp.clip((pad_start + counts)[block_exp] - block_start, 0, bm).astype(jnp.int32)
    return gate[:, :TOP_K], pos.astype(jnp.int32), block_exp, block_rows, n_blocks


def _sc_worker_range(n_rows):
    per_worker = n_rows // (SC_CORES * SC_SUBCORES)
    wid = lax.axis_index("s") * SC_CORES + lax.axis_index("c")
    return wid * per_worker, per_worker


def _sc_scatter_rows(src, idx, n_out):
    t, d = src.shape
    n_copies = idx.shape[0] // t
    mesh = plsc.VectorSubcoreMesh(core_axis_name="c", subcore_axis_name="s")

    @functools.partial(
        pl.kernel, mesh=mesh,
        out_type=jax.ShapeDtypeStruct((n_out, d), src.dtype),
        scratch_types=[pltpu.VMEM((SC_GATHER_ROWS,), jnp.int32),
                       pltpu.VMEM((SC_GATHER_ROWS, d), src.dtype),
                       pltpu.SemaphoreType.DMA],
        name="sc_scatter_rows",
    )
    def scatter(src_hbm, idx_hbm, out_hbm, idx_v, rows_v, sem):
        base, per_worker = _sc_worker_range(t)

        @pl.loop(0, per_worker // SC_GATHER_ROWS)
        def _(j):
            off = pl.multiple_of(base + j * SC_GATHER_ROWS, SC_GATHER_ROWS)
            pltpu.sync_copy(src_hbm.at[pl.ds(off, SC_GATHER_ROWS)], rows_v)
            for k in range(n_copies):
                pltpu.sync_copy(idx_hbm.at[pl.ds(k * t + off, SC_GATHER_ROWS)], idx_v)
                pltpu.async_copy(rows_v, out_hbm.at[idx_v], sem).wait()

    return scatter(src, idx)


def _expert_kernel(be_ref, nr_ref, x_ref, wgu_ref, bgu_ref, wd_ref, bd_ref, o_ref, wgu16_ref, wd16_ref):
    i = pl.program_id(0)
    active = nr_ref[i] > 0
    new_expert = jnp.logical_or(i == 0, be_ref[i] != be_ref[jnp.maximum(i - 1, 0)])

    @pl.when(jnp.logical_and(active, new_expert))
    def _():
        wgu16_ref[...] = wgu_ref[0].astype(BF16)
        wd16_ref[...] = wd_ref[0].astype(BF16)

    @pl.when(active)
    def _():
        half = D_MODEL // 2
        row = lax.broadcasted_iota(jnp.int32, x_ref.shape, 0)
        x = jnp.where(row < nr_ref[i], x_ref[...], jnp.uint32(0))
        x_lo, x_hi = _unpack_bf16_pair(x)
        hgu = (_dot(x_lo.astype(BF16), wgu16_ref[:half, :]) + _dot(x_hi.astype(BF16), wgu16_ref[half:, :])
               + bgu_ref[0])
        glu = jnp.minimum(hgu[:, :D_FF], SWIGLU_LIMIT)
        lin = jnp.clip(hgu[:, D_FF:], -SWIGLU_LIMIT, SWIGLU_LIMIT)
        act = glu * _sigmoid(SWIGLU_ALPHA * glu) * (lin + 1.0)
        y = _dot(act.astype(BF16), wd16_ref[...]) + bd_ref[0]
        o_ref[...] = _pack_bf16_pair(y[:, :half], y[:, half:])

    @pl.when(jnp.logical_not(active))
    def _():
        o_ref[...] = jnp.zeros_like(o_ref)


def _experts(xs, block_exp, block_rows, w_gu, b_gu, w_down, b_down, n_blocks):
    d = D_MODEL
    dp = xs.shape[1]
    bm = EXPERT_ROWS
    grid_spec = pltpu.PrefetchScalarGridSpec(
        num_scalar_prefetch=2,
        grid=(n_blocks,),
        in_specs=[pl.BlockSpec((bm, dp), lambda i, be, nb: (i, 0)),
                  pl.BlockSpec((1, d, 2 * D_FF), lambda i, be, nb: (be[i], 0, 0)),
                  pl.BlockSpec((1, 1, 2 * D_FF), lambda i, be, nb: (be[i], 0, 0)),
                  pl.BlockSpec((1, D_FF, d), lambda i, be, nb: (be[i], 0, 0)),
                  pl.BlockSpec((1, 1, d), lambda i, be, nb: (be[i], 0, 0))],
        out_specs=pl.BlockSpec((bm, dp), lambda i, be, nb: (i, 0)),
        scratch_shapes=[pltpu.VMEM((d, 2 * D_FF), BF16), pltpu.VMEM((D_FF, d), BF16)],
    )
    return pl.pallas_call(
        _expert_kernel,
        out_shape=jax.ShapeDtypeStruct((n_blocks * bm, dp), jnp.uint32),
        grid_spec=grid_spec,
        compiler_params=_params("arbitrary"),
        name="moe_experts",
    )(block_exp, block_rows, xs, w_gu, b_gu.reshape(N_EXPERTS, 1, 2 * D_FF), w_down,
      b_down.reshape(N_EXPERTS, 1, d))


def _sc_gather_rows(table, idx):
    m = idx.shape[0]
    d = table.shape[1]
    mesh = plsc.VectorSubcoreMesh(core_axis_name="c", subcore_axis_name="s")

    @functools.partial(
        pl.kernel, mesh=mesh,
        out_type=jax.ShapeDtypeStruct((m, d), table.dtype),
        scratch_types=[pltpu.VMEM((SC_GATHER_ROWS,), jnp.int32),
                       pltpu.VMEM((SC_GATHER_ROWS, d), table.dtype),
                       pltpu.SemaphoreType.DMA],
        name="sc_gather_rows",
    )
    def gather(table_hbm, idx_hbm, out_hbm, idx_v, rows_v, sem):
        base, per_worker = _sc_worker_range(m)

        @pl.loop(0, per_worker // SC_GATHER_ROWS)
        def _(j):
            off = pl.multiple_of(base + j * SC_GATHER_ROWS, SC_GATHER_ROWS)
            pltpu.sync_copy(idx_hbm.at[pl.ds(off, SC_GATHER_ROWS)], idx_v)
            pltpu.async_copy(table_hbm.at[idx_v], rows_v, sem).wait()
            pltpu.sync_copy(rows_v, out_hbm.at[pl.ds(off, SC_GATHER_ROWS)])

    return gather(table, idx)


def _combine_dense_kernel(y0_ref, y1_ref, y2_ref, y3_ref, gate_ref, h_ref, g_ref, b_ref, o_ref):
    gate = gate_ref[...]
    f_lo = f_hi = None
    for k, y_ref in enumerate((y0_ref, y1_ref, y2_ref, y3_ref)):
        y_lo, y_hi = _unpack_bf16_pair(y_ref[...])
        gk = gate[:, k:k + 1]
        f_lo = gk * y_lo if f_lo is None else f_lo + gk * y_lo
        f_hi = gk * y_hi if f_hi is None else f_hi + gk * y_hi
    f = jnp.concatenate([f_lo, f_hi], axis=1)
    o_ref[...] = _layer_norm(DN_ALPHA * h_ref[...] + f, g_ref[...], b_ref[...])


def _combine_dense(yg, gate, h1, ln_g, ln_b):
    t, d = h1.shape
    tm = min(ROW_TILE, t)
    nt = t // tm
    dp = yg.shape[1]
    slab = lambda k: pl.BlockSpec((tm, dp), lambda i, k=k: (k * nt + i, 0))
    return pl.pallas_call(
        _combine_dense_kernel,
        out_shape=jax.ShapeDtypeStruct((t, d), F32),
        grid=(nt,),
        in_specs=[slab(0), slab(1), slab(2), slab(3),
                  pl.BlockSpec((tm, TOP_K), lambda i: (i, 0)),
                  pl.BlockSpec((tm, d), lambda i: (i, 0)),
                  pl.BlockSpec((1, d), lambda i: (0, 0)),
                  pl.BlockSpec((1, d), lambda i: (0, 0))],
        out_specs=pl.BlockSpec((tm, d), lambda i: (i, 0)),
        compiler_params=_params("parallel"),
        name="moe_combine",
    )(yg, yg, yg, yg, gate, h1, ln_g.reshape(1, d), ln_b.reshape(1, d))


def kernel(x, emb_ln_g, emb_ln_b, w_in, conv_qkv, a_log, dt_bias, dn_norm_g, w_a_o, b_glu, conv_dw, b_dw, conv_ln_g, conv_ln_b, w_b_o, b_b_o, b_gate, w_out, ln1_g, ln1_b, w_router, b_router, w_gu, b_gu, w_down, b_down, ln2_g, ln2_b):
    bsz, seq, d = x.shape
    t = bsz * seq
    x2 = x.reshape(t, d)
    u_qkvz, g, gt, glu, gates = _inproj(x2, emb_ln_g, emb_ln_b, w_in[0], b_glu[0], b_gate[0], a_log[0], dt_bias[0])

    qkv = _qkv_conv(u_qkvz, conv_qkv[0], bsz, seq)
    gtp = gt.reshape(2, N_HEADS // 2, 2, t // CHUNK, CHUNK).transpose(0, 1, 3, 2, 4).reshape(N_HEADS, 2 * t)
    o_f, o_b = _delta_rule(qkv, g, gtp, bsz, seq)
    yc = _dw_conv(glu, conv_dw[0], b_dw[0], bsz, seq)

    h1, h1p, logits = _mix(o_f, o_b, u_qkvz, yc, gates, x2, emb_ln_g, emb_ln_b, dn_norm_g[0],
                           w_a_o[0].astype(BF16), conv_ln_g[0], conv_ln_b[0], w_b_o[0].astype(BF16), b_b_o[0],
                           w_out[0].astype(BF16), ln1_g[0], ln1_b[0], w_router[0], b_router[0])

    gate, pos, block_exp, block_rows, n_blocks = _route(logits)
    pos_kmajor = pos.T.reshape(-1)
    xs = _sc_scatter_rows(h1p, pos_kmajor, n_blocks * EXPERT_ROWS)
    ys = _experts(xs, block_exp, block_rows, w_gu[0], b_gu[0], w_down[0], b_down[0], n_blocks)
    yg = _sc_gather_rows(ys, pos_kmajor)
    out = _combine_dense(yg, gate, h1, ln2_g[0], ln2_b[0])
    return out.reshape(bsz, seq, d)
```

```python
import functools

import jax
import jax.numpy as jnp
from jax import lax
from jax.experimental import pallas as pl
from jax.experimental.pallas import tpu as pltpu
from jax.experimental.pallas import tpu_sc as plsc

F32 = jnp.float32
BF16 = jnp.bfloat16

D_MODEL = 1024
N_HEADS = 8
HEAD_DIM = 128
WIDTH_A = N_HEADS * HEAD_DIM
SHORT_CONV = 5
CHUNK = 64
WIDTH_B = D_MODEL
DW_CONV = 31
N_EXPERTS = 32
TOP_K = 4
D_FF = D_MODEL
SWIGLU_ALPHA = 1.702
SWIGLU_LIMIT = 7.0
DN_ALPHA = 2.0 ** 0.25
LN_EPS = 1e-5
RMS_EPS = 1e-6
L2_EPS = 1e-6
LANES = 128
NEG_BIG = -1e30

ROW_TILE = 512
DELTA_ROWS = 512
EXPERT_ROWS = 512
SC_CORES = 2
SC_SUBCORES = 16
SC_GATHER_ROWS = 64
VMEM_LIMIT = 56 * 1024 * 1024


def _params(*sem):
    return pltpu.CompilerParams(dimension_semantics=sem, vmem_limit_bytes=VMEM_LIMIT)


def _layer_norm(x, g, b):
    mu = jnp.mean(x, axis=-1, keepdims=True)
    xc = x - mu
    var = jnp.mean(xc * xc, axis=-1, keepdims=True)
    return xc * lax.rsqrt(var + LN_EPS) * g + b


def _sigmoid(x):
    return 1.0 / (1.0 + jnp.exp(-x))


def _dot(a, b):
    return jnp.dot(a, b, preferred_element_type=F32)


def _pack_bf16_pair(a, b):
    ua = lax.bitcast_convert_type(a.astype(BF16).astype(F32), jnp.uint32)
    ub = lax.bitcast_convert_type(b.astype(BF16).astype(F32), jnp.uint32)
    return (ua >> 16) | ub


def _unpack_bf16_pair(p):
    a = lax.bitcast_convert_type(p << 16, F32)
    b = lax.bitcast_convert_type(p & jnp.uint32(0xFFFF0000), F32)
    return a, b


def _split_bf16(a):
    hi = a.astype(BF16)
    return hi, (a - hi.astype(F32)).astype(BF16)


def _chunk_cumsum(x, reverse):
    rows = x.shape[0]
    pos = lax.broadcasted_iota(jnp.int32, x.shape, 0) % CHUNK
    s = 1
    while s < CHUNK:
        if reverse:
            shifted = pltpu.roll(x, rows - s, axis=0)
            x = x + jnp.where(pos < CHUNK - s, shifted, 0.0)
        else:
            shifted = pltpu.roll(x, s, axis=0)
            x = x + jnp.where(pos >= s, shifted, 0.0)
        s *= 2
    return x


def _inproj_kernel(x_ref, eg_ref, eb_ref, wq_ref, ws_ref, wga_ref, wgb_ref, bga_ref, bgb_ref, wgt_ref, bgt_ref,
                   alog_ref, dtb_ref, u_ref, g_ref, gt_ref, glu_ref, gate_ref, *, tn):
    h = _layer_norm(x_ref[...], eg_ref[...], eb_ref[...]).astype(BF16)

    for n0 in range(0, u_ref.shape[1], tn):
        u_ref[:, n0:n0 + tn] = _dot(h, wq_ref[:, n0:n0 + tn]).astype(u_ref.dtype)

    us = _dot(h, ws_ref[...])
    lane = lax.broadcasted_iota(jnp.int32, us.shape, 1)
    beta = _sigmoid(us)
    xs = us + dtb_ref[...]
    softplus = jnp.maximum(xs, 0.0) + jnp.log(1.0 + jnp.exp(-jnp.abs(xs)))
    log_a = -jnp.exp(alog_ref[...]) * softplus
    g_fwd = _chunk_cumsum(log_a, reverse=False)
    g_bwd = _chunk_cumsum(log_a, reverse=True)
    gates = jnp.where(lane < 2 * N_HEADS, beta, jnp.where(lane < 3 * N_HEADS, g_fwd, g_bwd))
    g_ref[...] = gates
    gt_ref[...] = gates.T[2 * N_HEADS:4 * N_HEADS, :]

    for n0 in range(0, glu_ref.shape[1], tn):
        lin = _dot(h, wga_ref[:, n0:n0 + tn]) + bga_ref[:, n0:n0 + tn]
        gt = _dot(h, wgb_ref[:, n0:n0 + tn]) + bgb_ref[:, n0:n0 + tn]
        glu_ref[:, n0:n0 + tn] = (lin * _sigmoid(gt)).astype(glu_ref.dtype)

    for n0 in range(0, gate_ref.shape[1], tn):
        gate_ref[:, n0:n0 + tn] = _sigmoid(_dot(h, wgt_ref[:, n0:n0 + tn])
                                           + bgt_ref[:, n0:n0 + tn]).astype(gate_ref.dtype)


def _inproj(x2, emb_g, emb_b, w_in, b_glu, b_gate, a_log, dt_bias):
    t, d = x2.shape
    tm = min(ROW_TILE, t)
    c0 = 4 * WIDTH_A
    c1 = c0 + 4 * N_HEADS
    c2 = c1 + 2 * WIDTH_B
    wb = w_in.astype(BF16)
    pad = LANES - 4 * N_HEADS
    w_small = jnp.pad(wb[:, c0:c1], ((0, 0), (0, pad)))
    alog = jnp.pad(a_log.reshape(1, 2 * N_HEADS), ((0, 0), (2 * N_HEADS, pad)))
    dtb = jnp.pad(dt_bias.reshape(1, 2 * N_HEADS), ((0, 0), (2 * N_HEADS, pad)))
    row = lambda i: (i, 0)

    def const(shape):
        return pl.BlockSpec(shape, lambda i: (0, 0), pipeline_mode=pl.Buffered(1))

    vec = lambda a: a.reshape(1, -1)
    bf = lambda n: jax.ShapeDtypeStruct((t, n), BF16)
    return pl.pallas_call(
        functools.partial(_inproj_kernel, tn=512),
        out_shape=(bf(c0), jax.ShapeDtypeStruct((t, LANES), F32), jax.ShapeDtypeStruct((2 * N_HEADS, t), F32),
                   bf(WIDTH_B), bf(2 * d)),
        grid=(t // tm,),
        in_specs=[pl.BlockSpec((tm, d), row), const((1, d)), const((1, d)),
                  const((d, c0)), const((d, LANES)),
                  const((d, WIDTH_B)), const((d, WIDTH_B)), const((1, WIDTH_B)), const((1, WIDTH_B)),
                  const((d, 2 * d)), const((1, 2 * d)), const((1, LANES)), const((1, LANES))],
        out_specs=(pl.BlockSpec((tm, c0), row), pl.BlockSpec((tm, LANES), row),
                   pl.BlockSpec((2 * N_HEADS, tm), lambda i: (0, i)),
                   pl.BlockSpec((tm, WIDTH_B), row), pl.BlockSpec((tm, 2 * d), row)),
        compiler_params=_params("parallel"),
        name="inproj",
    )(x2, vec(emb_g), vec(emb_b), wb[:, :c0], w_small, wb[:, c1:c1 + WIDTH_B], wb[:, c1 + WIDTH_B:c2],
      vec(b_glu[:WIDTH_B]), vec(b_glu[WIDTH_B:]), wb[:, c2:], vec(b_gate), alog, dtb)


def _conv_rows(xp_ref, w, taps, base, r0, rows):
    acc = xp_ref[base + r0:base + r0 + rows, :] * w[0:1, :]
    for k in range(1, taps):
        acc = acc + xp_ref[base + k + r0:base + k + r0 + rows, :] * w[k:k + 1, :]
    return acc


def _fill_padded(xp_ref, x_ref, pad, seq):
    zeros = jnp.zeros((pad, xp_ref.shape[1]), F32)
    xp_ref[0:pad, :] = zeros
    xp_ref[pad + seq:pad + seq + pad, :] = zeros
    xp_ref[pad:pad + seq, :] = x_ref[...].astype(F32)


def _qkv_conv_kernel(u_ref, w_ref, o_ref, xp_ref, *, seq, rows):
    pad = 8
    j = pl.program_id(1)
    _fill_padded(xp_ref, u_ref, pad, seq)
    w = w_ref[...]
    is_q = j < N_HEADS
    is_qk = j < 2 * N_HEADS
    for r0 in range(0, seq, rows):
        y = _conv_rows(xp_ref, w, SHORT_CONV, pad - SHORT_CONV // 2, r0, rows)
        y = y * _sigmoid(y)
        inv = lax.rsqrt(jnp.sum(y * y, axis=-1, keepdims=True) + L2_EPS)
        scale = jnp.where(is_q, inv * (HEAD_DIM ** -0.5), jnp.where(is_qk, inv, 1.0))
        o_ref[r0:r0 + rows, :] = (y * scale).astype(o_ref.dtype)


def _qkv_conv(u_qkvz, conv_w, bsz, seq):
    t = bsz * seq
    ncol = 3 * N_HEADS
    rows = min(256, seq)
    return pl.pallas_call(
        functools.partial(_qkv_conv_kernel, seq=seq, rows=rows),
        out_shape=jax.ShapeDtypeStruct((t, 3 * WIDTH_A), BF16),
        grid=(bsz, ncol),
        in_specs=[pl.BlockSpec((seq, HEAD_DIM), lambda b, j: (b, j)),
                  pl.BlockSpec((SHORT_CONV, HEAD_DIM), lambda b, j: (0, j))],
        out_specs=pl.BlockSpec((seq, HEAD_DIM), lambda b, j: (b, j)),
        scratch_shapes=[pltpu.VMEM((seq + 16, HEAD_DIM), F32)],
        compiler_params=_params("parallel", "parallel"),
        name="qkv_conv",
    )(u_qkvz, conv_w)


def _dw_conv_kernel(x_ref, w_ref, b_ref, o_ref, xp_ref, *, seq, rows):
    pad = 16
    _fill_padded(xp_ref, x_ref, pad, seq)
    w = w_ref[...]
    for r0 in range(0, seq, rows):
        y = _conv_rows(xp_ref, w, DW_CONV, pad - DW_CONV // 2, r0, rows) + b_ref[...]
        o_ref[r0:r0 + rows, :] = y.astype(o_ref.dtype)


def _dw_conv(glu, conv_w, b_dw, bsz, seq):
    t = bsz * seq
    rows = min(256, seq)
    return pl.pallas_call(
        functools.partial(_dw_conv_kernel, seq=seq, rows=rows),
        out_shape=jax.ShapeDtypeStruct((t, WIDTH_B), BF16),
        grid=(bsz, WIDTH_B // LANES),
        in_specs=[pl.BlockSpec((seq, LANES), lambda b, j: (b, j)),
                  pl.BlockSpec((DW_CONV, LANES), lambda b, j: (0, j)),
                  pl.BlockSpec((1, LANES), lambda b, j: (0, j))],
        out_specs=pl.BlockSpec((seq, LANES), lambda b, j: (b, j)),
        scratch_shapes=[pltpu.VMEM((seq + 32, LANES), F32)],
        compiler_params=_params("parallel", "parallel"),
        name="dw_conv",
    )(glu, conv_w, b_dw.reshape(1, WIDTH_B))


def _bmm(a, b):
    return lax.dot_general(a, b, (((2,), (1,)), ((0,), (0,))), preferred_element_type=F32)


def _bmm_nt(a, b):
    return lax.dot_general(a, b, (((2,), (2,)), ((0,), (0,))), preferred_element_type=F32)


def _bmm_tn(a, b):
    return lax.dot_general(a, b, (((1,), (1,)), ((0,), (0,))), preferred_element_type=F32)


def _block_diag_rows(x, half):
    lane = lax.broadcasted_iota(jnp.int32, x.shape, 2)
    return jnp.concatenate([jnp.where(lane < half, x, 0.0), jnp.where(lane >= half, x, 0.0)], axis=1)


def _unit_tri_inverse(lmat, eye):
    def rhs(p):
        return _block_diag_rows(p, CHUNK).astype(BF16)

    x = eye - lmat
    p = _bmm(lmat.astype(BF16), rhs(lmat))
    s = 2
    while 2 * s < CHUNK:
        xp = _bmm(jnp.concatenate([x, p], axis=1).astype(BF16), rhs(p))
        x = x + xp[:, :CHUNK]
        p = xp[:, CHUNK:]
        s *= 2
    return x + _bmm(x.astype(BF16), rhs(p))


def _delta_kernel(qf_ref, kf_ref, vf_ref, gf_ref, gtpf_ref, qb_ref, kb_ref, vb_ref, gb_ref, gtpb_ref,
                  of_ref, ob_ref, s_ref, *, nc):
    @pl.when(pl.program_id(1) == 0)
    def _():
        s_ref[...] = jnp.zeros_like(s_ref)

    n_pairs = N_HEADS // 2
    n_inst = 2 * n_pairs
    pw = 2 * HEAD_DIM
    dirs = ((qf_ref, kf_ref, vf_ref, gf_ref, gtpf_ref, of_ref, False),
            (qb_ref, kb_ref, vb_ref, gb_ref, gtpb_ref, ob_ref, True))
    steps = [[d + ((nc - 1 - i) if d[6] else i,) for d in dirs] for i in range(nc)]

    ri = lax.broadcasted_iota(jnp.int32, (CHUNK, 2 * CHUNK), 0)
    ci = jnp.bitwise_and(lax.broadcasted_iota(jnp.int32, (CHUNK, 2 * CHUNK), 1), CHUNK - 1)
    inst = lax.broadcasted_iota(jnp.int32, (nc * n_inst, 1, 1), 0)
    sign = 1 - 2 * jnp.bitwise_and(jnp.right_shift(inst, n_pairs.bit_length() - 1), 1)
    rel = (ri - ci)[None] * sign
    incl = rel >= 0
    strict = rel > 0
    eye = (ri == ci).astype(F32)

    def pairs(which):
        return jnp.stack([d[which][d[7] * CHUNK:(d[7] + 1) * CHUNK, p * pw:(p + 1) * pw]
                          for st in steps for d in st for p in range(n_pairs)]).astype(F32)

    def pair_bcast(cols, width):
        return jnp.stack([jnp.concatenate([jnp.broadcast_to(cc[2 * p], (CHUNK, width)),
                                           jnp.broadcast_to(cc[2 * p + 1], (CHUNK, width))], axis=1)
                          for cc in cols for p in range(n_pairs)])

    qf = pairs(0)
    kf = pairs(1)
    vf = pairs(2)
    beta_c, g_c, glast_c = [], [], []
    for st in steps:
        for d in st:
            gblk = d[3][d[7] * CHUNK:(d[7] + 1) * CHUNK, :]
            off = N_HEADS if d[6] else 0
            last = 0 if d[6] else CHUNK - 1
            beta_c.append([gblk[:, off + hh:off + hh + 1] for hh in range(N_HEADS)])
            g_c.append([gblk[:, 2 * N_HEADS + off + hh:2 * N_HEADS + off + hh + 1] for hh in range(N_HEADS)])
            glast_c.append([gc[last:last + 1, :] for gc in g_c[-1]])
    beta = pair_bcast(beta_c, HEAD_DIM)
    eg = pair_bcast([[jnp.exp(gc) for gc in gcs] for gcs in g_c], HEAD_DIM)
    tail = pair_bcast([[jnp.exp(gl - gc) for gl, gc in zip(gls, gcs)] for gls, gcs in zip(glast_c, g_c)],
                      HEAD_DIM)
    gcol = pair_bcast(g_c, CHUNK)
    grow = jnp.stack([d[4][(n_pairs if d[6] else 0) + p:(n_pairs if d[6] else 0) + p + 1,
                           2 * d[7] * CHUNK:2 * (d[7] + 1) * CHUNK]
                      for st in steps for d in st for p in range(n_pairs)])

    decay = jnp.exp(jnp.where(incl, gcol - grow, NEG_BIG))
    kb = kf * beta
    kkqk = _bmm_nt(jnp.concatenate([kb, qf], axis=1).astype(BF16),
                   _block_diag_rows(kf, HEAD_DIM).astype(BF16))
    lmat = jnp.where(strict, kkqk[:, :CHUNK, :] * decay, 0.0)
    qk = (kkqk[:, CHUNK:, :] * decay).astype(BF16)
    tinv = _unit_tri_inverse(lmat, eye)
    rhs = jnp.concatenate([_block_diag_rows(vf * beta, HEAD_DIM),
                           _block_diag_rows(kb * eg, HEAD_DIM)], axis=2).astype(BF16)
    uw = _bmm(tinv.astype(BF16), rhs)
    u = uw[:, :, :pw]
    wq = jnp.concatenate([uw[:, :, pw:], qf * eg], axis=1).astype(BF16)
    kt = (kf * tail).astype(BF16)

    for i, st in enumerate(steps):
        sl = slice(i * n_inst, (i + 1) * n_inst)
        s_a = s_ref[:, 0]
        s_b = s_ref[:, 1]
        zero = jnp.zeros_like(s_a)
        s_bd = jnp.concatenate([jnp.concatenate([s_a, zero], axis=2),
                                jnp.concatenate([zero, s_b], axis=2)], axis=1).astype(BF16)
        ws = _bmm(wq[sl], s_bd)
        v_new = u[sl] - ws[:, :CHUNK, :]
        o = ws[:, CHUNK:, :] + _bmm(qk[sl], _block_diag_rows(v_new, HEAD_DIM).astype(BF16))
        v16 = v_new.astype(BF16)
        gl = [glast_c[2 * i + dd] for dd in range(2)]
        cd_a = jnp.stack([jnp.exp(gl[dd][2 * p]) for dd in range(2) for p in range(n_pairs)])
        cd_b = jnp.stack([jnp.exp(gl[dd][2 * p + 1]) for dd in range(2) for p in range(n_pairs)])
        s_ref[:, 0] = s_a * cd_a + _bmm_tn(kt[sl, :, :HEAD_DIM], v16[:, :, :HEAD_DIM])
        s_ref[:, 1] = s_b * cd_b + _bmm_tn(kt[sl, :, HEAD_DIM:], v16[:, :, HEAD_DIM:])
        for dd, d in enumerate(st):
            for p in range(n_pairs):
                d[5][d[7] * CHUNK:(d[7] + 1) * CHUNK, p * pw:(p + 1) * pw] = (
                    o[dd * n_pairs + p].astype(d[5].dtype))


def _delta_rule(qkv, g, gtp, bsz, seq):
    t = bsz * seq
    rows = min(DELTA_ROWS, seq)
    nblk = seq // rows

    def fwd(col):
        return lambda b, i: (b * nblk + i, col)

    def bwd(col):
        return lambda b, i: (b * nblk + nblk - 1 - i, col)

    def specs(m):
        return [pl.BlockSpec((rows, WIDTH_A), m(0)), pl.BlockSpec((rows, WIDTH_A), m(1)),
                pl.BlockSpec((rows, WIDTH_A), m(2)), pl.BlockSpec((rows, LANES), m(0)),
                pl.BlockSpec((N_HEADS, 2 * rows), lambda b, i, m=m: (0, m(0)(b, i)[0]))]

    out = jax.ShapeDtypeStruct((t, WIDTH_A), BF16)
    return pl.pallas_call(
        functools.partial(_delta_kernel, nc=rows // CHUNK),
        out_shape=(out, out),
        grid=(bsz, nblk),
        in_specs=specs(fwd) + specs(bwd),
        out_specs=(pl.BlockSpec((rows, WIDTH_A), fwd(0)), pl.BlockSpec((rows, WIDTH_A), bwd(0))),
        scratch_shapes=[pltpu.VMEM((N_HEADS, 2, HEAD_DIM, HEAD_DIM), F32)],
        compiler_params=_params("parallel", "arbitrary"),
        name="delta_rule",
    )(qkv, qkv, qkv, g, gtp, qkv, qkv, qkv, g, gtp)


def _mix_kernel(of_ref, ob_ref, z_ref, yc_ref, gate_ref, x_ref,
                eg_ref, eb_ref, ng_ref, wao_ref, cg_ref, cb_ref, wbo_ref, bbo_ref,
                wout_ref, l1g_ref, l1b_ref, wr_ref, br_ref,
                h1_ref, h1p_ref, logit_ref):
    o = of_ref[...].astype(F32) + ob_ref[...].astype(F32)
    z = z_ref[...].astype(F32)
    ng = ng_ref[...]
    parts = []
    for hh in range(N_HEADS):
        sl = slice(hh * HEAD_DIM, (hh + 1) * HEAD_DIM)
        oh = o[:, sl]
        zh = z[:, sl]
        inv = lax.rsqrt(jnp.mean(oh * oh, axis=-1, keepdims=True) + RMS_EPS)
        parts.append((oh * inv * ng * (zh * _sigmoid(zh))).astype(BF16))
    y_a = _dot(jnp.concatenate(parts, axis=1), wao_ref[...])

    yc = _layer_norm(yc_ref[...].astype(F32), cg_ref[...], cb_ref[...])
    y_b = _dot((yc * _sigmoid(yc)).astype(BF16), wbo_ref[...]) + bbo_ref[...]

    gates = gate_ref[...].astype(F32)
    mixed = gates[:, :D_MODEL] * y_a + gates[:, D_MODEL:] * y_b
    mix = _dot(mixed.astype(BF16), wout_ref[...])

    h0 = _layer_norm(x_ref[...], eg_ref[...], eb_ref[...])
    h1 = _layer_norm(DN_ALPHA * h0 + mix, l1g_ref[...], l1b_ref[...])
    h1_ref[...] = h1
    h1p_ref[...] = _pack_bf16_pair(h1[:, :D_MODEL // 2], h1[:, D_MODEL // 2:])
    h_hi, h_lo = _split_bf16(h1)
    p = _dot(h_hi, wr_ref[...])
    logit_ref[...] = p[:, :LANES] + p[:, LANES:] + _dot(h_lo, wr_ref[:, :LANES]) + br_ref[...]


def _mix(o_f, o_b, u_qkvz, yc, gates, x2, emb_g, emb_b, norm_g, w_a_o, cg, cb, w_b_o, b_b_o,
         w_out, l1g, l1b, w_router, b_router):
    t, d = x2.shape
    tm = min(ROW_TILE, t)
    row = lambda i: (i, 0)
    const = lambda i: (0, 0)
    wr = jnp.concatenate(_split_bf16(jnp.pad(w_router, ((0, 0), (0, LANES - N_EXPERTS)))), axis=1)
    br = jnp.pad(b_router.reshape(1, N_EXPERTS), ((0, 0), (0, LANES - N_EXPERTS)), constant_values=NEG_BIG)
    vec = lambda a: a.reshape(1, -1)
    return pl.pallas_call(
        _mix_kernel,
        out_shape=(jax.ShapeDtypeStruct((t, d), F32), jax.ShapeDtypeStruct((t, d // 2), jnp.uint32),
                   jax.ShapeDtypeStruct((t, LANES), F32)),
        grid=(t // tm,),
        in_specs=[pl.BlockSpec((tm, d), row), pl.BlockSpec((tm, d), row),
                  pl.BlockSpec((tm, d), lambda i: (i, 3)),
                  pl.BlockSpec((tm, d), row), pl.BlockSpec((tm, 2 * d), row), pl.BlockSpec((tm, d), row),
                  pl.BlockSpec((1, d), const), pl.BlockSpec((1, d), const),
                  pl.BlockSpec((1, HEAD_DIM), const), pl.BlockSpec((d, d), const),
                  pl.BlockSpec((1, d), const), pl.BlockSpec((1, d), const),
                  pl.BlockSpec((d, d), const), pl.BlockSpec((1, d), const),
                  pl.BlockSpec((d, d), const), pl.BlockSpec((1, d), const), pl.BlockSpec((1, d), const),
                  pl.BlockSpec((d, 2 * LANES), const), pl.BlockSpec((1, LANES), const)],
        out_specs=(pl.BlockSpec((tm, d), row), pl.BlockSpec((tm, d // 2), row), pl.BlockSpec((tm, LANES), row)),
        compiler_params=_params("parallel"),
        name="mix",
    )(o_f, o_b, u_qkvz, yc, gates, x2, vec(emb_g), vec(emb_b), vec(norm_g), w_a_o, vec(cg), vec(cb),
      w_b_o, vec(b_b_o), w_out, vec(l1g), vec(l1b), wr, br)


def _route_kernel(logit_ref, gate_ref, eidx_ref, rank_ref, cnt_ref, base_ref):
    @pl.when(pl.program_id(0) == 0)
    def _():
        base_ref[...] = jnp.zeros_like(base_ref)

    x = logit_ref[...]
    tm = x.shape[0]
    lane = lax.broadcasted_iota(jnp.int32, x.shape, 1)
    lane_f = lane.astype(F32)
    row = lax.broadcasted_iota(jnp.int32, x.shape, 0)
    sel = jnp.zeros(x.shape, F32)
    vals, idxs = [], []
    for _ in range(TOP_K):
        m = jnp.max(x, axis=1, keepdims=True)
        idx = jnp.min(jnp.where(x == m, lane_f, float(LANES)), axis=1, keepdims=True).astype(jnp.int32)
        hit = lane == idx
        sel = sel + hit.astype(F32)
        x = jnp.where(hit, -3e38, x)
        vals.append(m)
        idxs.append(idx)

    exps = [jnp.exp(v - vals[0]) for v in vals]
    denom = exps[0]
    for e in exps[1:]:
        denom = denom + e

    csum = sel
    s = 1
    while s < tm:
        csum = csum + jnp.where(row >= s, pltpu.roll(csum, s, axis=0), 0.0)
        s *= 2
    before = base_ref[...] + csum - sel

    gate = jnp.zeros(x.shape, F32)
    eidx = jnp.zeros(x.shape, F32)
    rank = jnp.zeros(x.shape, F32)
    for k in range(TOP_K):
        rk = jnp.sum(jnp.where(lane == idxs[k], before, 0.0), axis=1, keepdims=True)
        gate = jnp.where(lane == k, exps[k] / denom, gate)
        eidx = jnp.where(lane == k, idxs[k].astype(F32), eidx)
        rank = jnp.where(lane == k, rk, rank)
    gate_ref[...] = gate
    eidx_ref[...] = eidx.T[:8, :].astype(jnp.int32)
    rank_ref[...] = rank.T[:8, :].astype(jnp.int32)
    total = base_ref[...] + csum[tm - 1:tm, :]
    base_ref[...] = total
    cnt_ref[...] = total


def _route(logits):
    t = logits.shape[0]
    tm = min(ROW_TILE, t)
    bm = EXPERT_ROWS
    row = lambda i: (i, 0)
    gate, eidx, rank, cnt = pl.pallas_call(
        _route_kernel,
        out_shape=(jax.ShapeDtypeStruct((t, LANES), F32), jax.ShapeDtypeStruct((8, t), jnp.int32),
                   jax.ShapeDtypeStruct((8, t), jnp.int32), jax.ShapeDtypeStruct((1, LANES), F32)),
        grid=(t // tm,),
        in_specs=[pl.BlockSpec((tm, LANES), row)],
        out_specs=(pl.BlockSpec((tm, LANES), row), pl.BlockSpec((8, tm), lambda i: (0, i)),
                   pl.BlockSpec((8, tm), lambda i: (0, i)), pl.BlockSpec((1, LANES), lambda i: (0, 0))),
        scratch_shapes=[pltpu.VMEM((1, LANES), F32)],
        compiler_params=_params("arbitrary"),
        name="moe_route",
    )(logits)
    counts = cnt[0, :N_EXPERTS].astype(jnp.int32)
    padded = (counts + bm - 1) // bm * bm
    pad_end = jnp.cumsum(padded)
    pad_start = pad_end - padded
    onehot = eidx[:TOP_K, :, None] == jnp.arange(N_EXPERTS, dtype=jnp.int32)
    pos_kmajor = (jnp.sum(jnp.where(onehot, pad_start, 0), axis=-1) + rank[:TOP_K]).reshape(-1)
    n_blocks = -(-(t * TOP_K + N_EXPERTS * (bm - 1)) // bm)
    block_start = jnp.arange(n_blocks, dtype=jnp.int32) * bm
    block_exp = jnp.minimum(jnp.sum((block_start[:, None] >= pad_end[None, :]).astype(jnp.int32), axis=1),
                            N_EXPERTS - 1)
    block_rows = jnp.clip((pad_start + counts)[block_exp] - block_start, 0, bm).astype(jnp.int32)
    return gate, pos_kmajor.astype(jnp.int32), block_exp, block_rows, n_blocks


def _sc_worker_range(n_rows):
    per_worker = n_rows // (SC_CORES * SC_SUBCORES)
    wid = lax.axis_index("s") * SC_CORES + lax.axis_index("c")
    return wid * per_worker, per_worker


def _sc_scatter_rows(src, idx, n_out):
    t, d = src.shape
    n_copies = idx.shape[0] // t
    mesh = plsc.VectorSubcoreMesh(core_axis_name="c", subcore_axis_name="s")

    @functools.partial(
        pl.kernel, mesh=mesh,
        out_type=jax.ShapeDtypeStruct((n_out, d), src.dtype),
        scratch_types=([pltpu.VMEM((SC_GATHER_ROWS,), jnp.int32)] * n_copies
                       + [pltpu.VMEM((SC_GATHER_ROWS, d), src.dtype)]
                       + [pltpu.SemaphoreType.DMA] * n_copies),
        name="sc_scatter_rows",
    )
    def scatter(src_hbm, idx_hbm, out_hbm, *scratch):
        idx_v = scratch[:n_copies]
        rows_v = scratch[n_copies]
        sems = scratch[n_copies + 1:]
        base, per_worker = _sc_worker_range(t)

        @pl.loop(0, per_worker // SC_GATHER_ROWS)
        def _(j):
            off = pl.multiple_of(base + j * SC_GATHER_ROWS, SC_GATHER_ROWS)
            for k in range(n_copies):
                pltpu.sync_copy(idx_hbm.at[pl.ds(k * t + off, SC_GATHER_ROWS)], idx_v[k])
            pltpu.sync_copy(src_hbm.at[pl.ds(off, SC_GATHER_ROWS)], rows_v)
            copies = [pltpu.async_copy(rows_v, out_hbm.at[idx_v[k]], sems[k]) for k in range(n_copies)]
            for c in copies:
                c.wait()

    return scatter(src, idx)


def _expert_kernel(be_ref, nr_ref, x_ref, wgu_ref, bgu_ref, wd_ref, bd_ref, o_ref, wgu16_ref, wd16_ref):
    i = pl.program_id(0)
    active = nr_ref[i] > 0
    new_expert = jnp.logical_or(i == 0, be_ref[i] != be_ref[jnp.maximum(i - 1, 0)])

    @pl.when(jnp.logical_and(active, new_expert))
    def _():
        wgu16_ref[...] = wgu_ref[0].astype(BF16)
        wd16_ref[...] = wd_ref[0].astype(BF16)

    @pl.when(active)
    def _():
        half = D_MODEL // 2
        row = lax.broadcasted_iota(jnp.int32, x_ref.shape, 0)
        x = jnp.where(row < nr_ref[i], x_ref[...], jnp.uint32(0))
        x_lo, x_hi = _unpack_bf16_pair(x)
        hgu = (_dot(x_lo.astype(BF16), wgu16_ref[:half, :]) + _dot(x_hi.astype(BF16), wgu16_ref[half:, :])
               + bgu_ref[0])
        glu = jnp.minimum(hgu[:, :D_FF], SWIGLU_LIMIT)
        lin = jnp.clip(hgu[:, D_FF:], -SWIGLU_LIMIT, SWIGLU_LIMIT)
        act = glu * _sigmoid(SWIGLU_ALPHA * glu) * (lin + 1.0)
        y = _dot(act.astype(BF16), wd16_ref[...]) + bd_ref[0]
        o_ref[...] = _pack_bf16_pair(y[:, :half], y[:, half:])

    @pl.when(jnp.logical_not(active))
    def _():
        o_ref[...] = jnp.zeros_like(o_ref)


def _experts(xs, block_exp, block_rows, w_gu, b_gu, w_down, b_down, n_blocks):
    d = D_MODEL
    dp = xs.shape[1]
    bm = EXPERT_ROWS
    grid_spec = pltpu.PrefetchScalarGridSpec(
        num_scalar_prefetch=2,
        grid=(n_blocks,),
        in_specs=[pl.BlockSpec((bm, dp), lambda i, be, nb: (i, 0)),
                  pl.BlockSpec((1, d, 2 * D_FF), lambda i, be, nb: (be[i], 0, 0)),
                  pl.BlockSpec((1, 1, 2 * D_FF), lambda i, be, nb: (be[i], 0, 0)),
                  pl.BlockSpec((1, D_FF, d), lambda i, be, nb: (be[i], 0, 0)),
                  pl.BlockSpec((1, 1, d), lambda i, be, nb: (be[i], 0, 0))],
        out_specs=pl.BlockSpec((bm, dp), lambda i, be, nb: (i, 0)),
        scratch_shapes=[pltpu.VMEM((d, 2 * D_FF), BF16), pltpu.VMEM((D_FF, d), BF16)],
    )
    return pl.pallas_call(
        _expert_kernel,
        out_shape=jax.ShapeDtypeStruct((n_blocks * bm, dp), jnp.uint32),
        grid_spec=grid_spec,
        compiler_params=_params("arbitrary"),
        name="moe_experts",
    )(block_exp, block_rows, xs, w_gu, b_gu.reshape(N_EXPERTS, 1, 2 * D_FF), w_down,
      b_down.reshape(N_EXPERTS, 1, d))


def _sc_gather_rows(table, idx):
    m = idx.shape[0]
    d = table.shape[1]
    mesh = plsc.VectorSubcoreMesh(core_axis_name="c", subcore_axis_name="s")

    @functools.partial(
        pl.kernel, mesh=mesh,
        out_type=jax.ShapeDtypeStruct((m, d), table.dtype),
        scratch_types=([pltpu.VMEM((SC_GATHER_ROWS,), jnp.int32)] * 2
                       + [pltpu.VMEM((SC_GATHER_ROWS, d), table.dtype)] * 2
                       + [pltpu.SemaphoreType.DMA] * 4),
        name="sc_gather_rows",
    )
    def gather(table_hbm, idx_hbm, out_hbm, idx0, idx1, rows0, rows1, g0, g1, w0, w1):
        base, per_worker = _sc_worker_range(m)

        @pl.loop(0, per_worker // (2 * SC_GATHER_ROWS))
        def _(j):
            off0 = pl.multiple_of(base + 2 * j * SC_GATHER_ROWS, SC_GATHER_ROWS)
            off1 = off0 + SC_GATHER_ROWS
            pltpu.sync_copy(idx_hbm.at[pl.ds(off0, SC_GATHER_ROWS)], idx0)
            gather0 = pltpu.async_copy(table_hbm.at[idx0], rows0, g0)
            pltpu.sync_copy(idx_hbm.at[pl.ds(off1, SC_GATHER_ROWS)], idx1)
            gather1 = pltpu.async_copy(table_hbm.at[idx1], rows1, g1)
            gather0.wait()
            write0 = pltpu.async_copy(rows0, out_hbm.at[pl.ds(off0, SC_GATHER_ROWS)], w0)
            gather1.wait()
            write1 = pltpu.async_copy(rows1, out_hbm.at[pl.ds(off1, SC_GATHER_ROWS)], w1)
            write0.wait()
            write1.wait()

    return gather(table, idx)


def _combine_dense_kernel(y0_ref, y1_ref, y2_ref, y3_ref, gate_ref, h_ref, g_ref, b_ref, o_ref):
    gate = gate_ref[...]
    f_lo = f_hi = None
    for k, y_ref in enumerate((y0_ref, y1_ref, y2_ref, y3_ref)):
        y_lo, y_hi = _unpack_bf16_pair(y_ref[...])
        gk = gate[:, k:k + 1]
        f_lo = gk * y_lo if f_lo is None else f_lo + gk * y_lo
        f_hi = gk * y_hi if f_hi is None else f_hi + gk * y_hi
    f = jnp.concatenate([f_lo, f_hi], axis=1)
    o_ref[...] = _layer_norm(DN_ALPHA * h_ref[...] + f, g_ref[...], b_ref[...])


def _combine_dense(yg, gate, h1, ln_g, ln_b):
    t, d = h1.shape
    tm = min(ROW_TILE, t)
    nt = t // tm
    dp = yg.shape[1]
    slab = lambda k: pl.BlockSpec((tm, dp), lambda i, k=k: (k * nt + i, 0))
    return pl.pallas_call(
        _combine_dense_kernel,
        out_shape=jax.ShapeDtypeStruct((t, d), F32),
        grid=(nt,),
        in_specs=[slab(0), slab(1), slab(2), slab(3),
                  pl.BlockSpec((tm, LANES), lambda i: (i, 0)),
                  pl.BlockSpec((tm, d), lambda i: (i, 0)),
                  pl.BlockSpec((1, d), lambda i: (0, 0)),
                  pl.BlockSpec((1, d), lambda i: (0, 0))],
        out_specs=pl.BlockSpec((tm, d), lambda i: (i, 0)),
        compiler_params=_params("parallel"),
        name="moe_combine",
    )(yg, yg, yg, yg, gate, h1, ln_g.reshape(1, d), ln_b.reshape(1, d))


def kernel(x, emb_ln_g, emb_ln_b, w_in, conv_qkv, a_log, dt_bias, dn_norm_g, w_a_o, b_glu, conv_dw, b_dw, conv_ln_g, conv_ln_b, w_b_o, b_b_o, b_gate, w_out, ln1_g, ln1_b, w_router, b_router, w_gu, b_gu, w_down, b_down, ln2_g, ln2_b):
    bsz, seq, d = x.shape
    t = bsz * seq
    x2 = x.reshape(t, d)
    u_qkvz, g, gt, glu, gates = _inproj(x2, emb_ln_g, emb_ln_b, w_in[0], b_glu[0], b_gate[0], a_log[0], dt_bias[0])

    qkv = _qkv_conv(u_qkvz, conv_qkv[0], bsz, seq)
    gtp = gt.reshape(2, N_HEADS // 2, 2, t // CHUNK, CHUNK).transpose(0, 1, 3, 2, 4).reshape(N_HEADS, 2 * t)
    o_f, o_b = _delta_rule(qkv, g, gtp, bsz, seq)
    yc = _dw_conv(glu, conv_dw[0], b_dw[0], bsz, seq)

    h1, h1p, logits = _mix(o_f, o_b, u_qkvz, yc, gates, x2, emb_ln_g, emb_ln_b, dn_norm_g[0],
                           w_a_o[0].astype(BF16), conv_ln_g[0], conv_ln_b[0], w_b_o[0].astype(BF16), b_b_o[0],
                           w_out[0].astype(BF16), ln1_g[0], ln1_b[0], w_router[0], b_router[0])

    gate, pos_kmajor, block_exp, block_rows, n_blocks = _route(logits)
    xs = _sc_scatter_rows(h1p, pos_kmajor, n_blocks * EXPERT_ROWS)
    ys = _experts(xs, block_exp, block_rows, w_gu[0], b_gu[0], w_down[0], b_down[0], n_blocks)
    yg = _sc_gather_rows(ys, pos_kmajor)
    out = _combine_dense(yg, gate, h1, ln2_g[0], ln2_b[0])
    return out.reshape(bsz, seq, d)
```

```python
import functools

import jax
import jax.numpy as jnp
from jax import lax
from jax.experimental import pallas as pl
from jax.experimental.pallas import tpu as pltpu
from jax.experimental.pallas import tpu_sc as plsc

F32 = jnp.float32
BF16 = jnp.bfloat16

D_MODEL = 1024
N_HEADS = 8
HEAD_DIM = 128
WIDTH_A = N_HEADS * HEAD_DIM
SHORT_CONV = 5
CHUNK = 64
WIDTH_B = D_MODEL
DW_CONV = 31
N_EXPERTS = 32
TOP_K = 4
D_FF = D_MODEL
SWIGLU_ALPHA = 1.702
SWIGLU_LIMIT = 7.0
DN_ALPHA = 2.0 ** 0.25
LN_EPS = 1e-5
RMS_EPS = 1e-6
L2_EPS = 1e-6
LANES = 128
NEG_BIG = -1e30

ROW_TILE = 512
DELTA_ROWS = 512
EXPERT_ROWS = 512
EXPERT_SLAB = 512
SC_CORES = 2
SC_SUBCORES = 16
SC_GATHER_ROWS = 64
VMEM_LIMIT = 56 * 1024 * 1024


def _params(*sem):
    return pltpu.CompilerParams(dimension_semantics=sem, vmem_limit_bytes=VMEM_LIMIT)


def _layer_norm(x, g, b):
    mu = jnp.mean(x, axis=-1, keepdims=True)
    xc = x - mu
    var = jnp.mean(xc * xc, axis=-1, keepdims=True)
    return xc * lax.rsqrt(var + LN_EPS) * g + b


def _sigmoid(x):
    return 0.5 * jnp.tanh(0.5 * x) + 0.5


def _silu(x):
    h = 0.5 * x
    return h + h * jnp.tanh(h)


def _dot(a, b):
    return jnp.dot(a, b, preferred_element_type=F32)


def _pack_bf16_pair(a, b):
    ua = lax.bitcast_convert_type(a.astype(BF16).astype(F32), jnp.uint32)
    ub = lax.bitcast_convert_type(b.astype(BF16).astype(F32), jnp.uint32)
    return (ua >> 16) | ub


def _unpack_bf16_pair(p):
    a = lax.bitcast_convert_type(p << 16, F32)
    b = lax.bitcast_convert_type(p & jnp.uint32(0xFFFF0000), F32)
    return a, b


def _split_bf16(a):
    hi = a.astype(BF16)
    return hi, (a - hi.astype(F32)).astype(BF16)


def _chunk_cumsum(x, reverse):
    rows = x.shape[0]
    pos = lax.broadcasted_iota(jnp.int32, x.shape, 0) % CHUNK
    s = 1
    while s < CHUNK:
        if reverse:
            shifted = pltpu.roll(x, rows - s, axis=0)
            x = x + jnp.where(pos < CHUNK - s, shifted, 0.0)
        else:
            shifted = pltpu.roll(x, s, axis=0)
            x = x + jnp.where(pos >= s, shifted, 0.0)
        s *= 2
    return x


def _inproj_kernel(x_ref, eg_ref, eb_ref, wq_ref, ws_ref, wga_ref, wgb_ref, bga_ref, bgb_ref, wgt_ref, bgt_ref,
                   alog_ref, dtb_ref, u_ref, g_ref, gt_ref, glu_ref, gate_ref, *, tn):
    h = _layer_norm(x_ref[...], eg_ref[...], eb_ref[...]).astype(BF16)

    for n0 in range(0, u_ref.shape[1], tn):
        u_ref[:, n0:n0 + tn] = _dot(h, wq_ref[:, n0:n0 + tn]).astype(u_ref.dtype)

    us = _dot(h, ws_ref[...])
    lane = lax.broadcasted_iota(jnp.int32, us.shape, 1)
    beta = _sigmoid(us)
    xs = us + dtb_ref[...]
    softplus = jnp.maximum(xs, 0.0) + jnp.log(1.0 + jnp.exp(-jnp.abs(xs)))
    log_a = -jnp.exp(alog_ref[...]) * softplus
    g_fwd = _chunk_cumsum(log_a, reverse=False)
    g_bwd = _chunk_cumsum(log_a, reverse=True)
    gates = jnp.where(lane < 2 * N_HEADS, beta, jnp.where(lane < 3 * N_HEADS, g_fwd, g_bwd))
    g_ref[...] = gates
    gt_ref[...] = gates.T[2 * N_HEADS:4 * N_HEADS, :]

    for n0 in range(0, glu_ref.shape[1], tn):
        lin = _dot(h, wga_ref[:, n0:n0 + tn]) + bga_ref[:, n0:n0 + tn]
        gt = _dot(h, wgb_ref[:, n0:n0 + tn]) + bgb_ref[:, n0:n0 + tn]
        glu_ref[:, n0:n0 + tn] = (lin * _sigmoid(gt)).astype(glu_ref.dtype)

    for n0 in range(0, gate_ref.shape[1], tn):
        gate_ref[:, n0:n0 + tn] = _sigmoid(_dot(h, wgt_ref[:, n0:n0 + tn])
                                           + bgt_ref[:, n0:n0 + tn]).astype(gate_ref.dtype)


def _inproj(x2, emb_g, emb_b, w_in, b_glu, b_gate, a_log, dt_bias):
    t, d = x2.shape
    tm = min(ROW_TILE, t)
    c0 = 4 * WIDTH_A
    c1 = c0 + 4 * N_HEADS
    c2 = c1 + 2 * WIDTH_B
    wb = w_in.astype(BF16)
    pad = LANES - 4 * N_HEADS
    w_small = jnp.pad(wb[:, c0:c1], ((0, 0), (0, pad)))
    alog = jnp.pad(a_log.reshape(1, 2 * N_HEADS), ((0, 0), (2 * N_HEADS, pad)))
    dtb = jnp.pad(dt_bias.reshape(1, 2 * N_HEADS), ((0, 0), (2 * N_HEADS, pad)))
    row = lambda i: (i, 0)

    def const(shape):
        return pl.BlockSpec(shape, lambda i: (0, 0), pipeline_mode=pl.Buffered(1))

    vec = lambda a: a.reshape(1, -1)
    bf = lambda n: jax.ShapeDtypeStruct((t, n), BF16)
    return pl.pallas_call(
        functools.partial(_inproj_kernel, tn=512),
        out_shape=(bf(c0), jax.ShapeDtypeStruct((t, LANES), F32), jax.ShapeDtypeStruct((2 * N_HEADS, t), F32),
                   bf(WIDTH_B), bf(2 * d)),
        grid=(t // tm,),
        in_specs=[pl.BlockSpec((tm, d), row), const((1, d)), const((1, d)),
                  const((d, c0)), const((d, LANES)),
                  const((d, WIDTH_B)), const((d, WIDTH_B)), const((1, WIDTH_B)), const((1, WIDTH_B)),
                  const((d, 2 * d)), const((1, 2 * d)), const((1, LANES)), const((1, LANES))],
        out_specs=(pl.BlockSpec((tm, c0), row), pl.BlockSpec((tm, LANES), row),
                   pl.BlockSpec((2 * N_HEADS, tm), lambda i: (0, i)),
                   pl.BlockSpec((tm, WIDTH_B), row), pl.BlockSpec((tm, 2 * d), row)),
        compiler_params=_params("parallel"),
        name="inproj",
    )(x2, vec(emb_g), vec(emb_b), wb[:, :c0], w_small, wb[:, c1:c1 + WIDTH_B], wb[:, c1 + WIDTH_B:c2],
      vec(b_glu[:WIDTH_B]), vec(b_glu[WIDTH_B:]), wb[:, c2:], vec(b_gate), alog, dtb)


def _conv_rows(xp_ref, w, taps, base, r0, rows):
    acc = xp_ref[base + r0:base + r0 + rows, :] * w[0:1, :]
    for k in range(1, taps):
        acc = acc + xp_ref[base + k + r0:base + k + r0 + rows, :] * w[k:k + 1, :]
    return acc


def _fill_padded(xp_ref, x_ref, pad, seq):
    zeros = jnp.zeros((pad, xp_ref.shape[1]), F32)
    xp_ref[0:pad, :] = zeros
    xp_ref[pad + seq:pad + seq + pad, :] = zeros
    xp_ref[pad:pad + seq, :] = x_ref[...].astype(F32)


def _qkv_conv_kernel(u_ref, w_ref, o_ref, xp_ref, *, seq, rows):
    pad = 8
    j = pl.program_id(1)
    _fill_padded(xp_ref, u_ref, pad, seq)
    w = w_ref[...]
    is_qk = j < 2 * N_HEADS
    head_scale = jnp.where(j < N_HEADS, HEAD_DIM ** -0.5, 1.0)
    for r0 in range(0, seq, rows):
        y = _silu(_conv_rows(xp_ref, w, SHORT_CONV, pad - SHORT_CONV // 2, r0, rows))
        inv = lax.rsqrt(jnp.sum(y * y, axis=-1, keepdims=True) + L2_EPS)
        scale = jnp.where(is_qk, inv * head_scale, 1.0)
        o_ref[r0:r0 + rows, :] = (y * scale).astype(o_ref.dtype)


def _qkv_conv(u_qkvz, conv_w, bsz, seq):
    t = bsz * seq
    ncol = 3 * N_HEADS
    rows = min(256, seq)
    return pl.pallas_call(
        functools.partial(_qkv_conv_kernel, seq=seq, rows=rows),
        out_shape=jax.ShapeDtypeStruct((t, 3 * WIDTH_A), BF16),
        grid=(bsz, ncol),
        in_specs=[pl.BlockSpec((seq, HEAD_DIM), lambda b, j: (b, j)),
                  pl.BlockSpec((SHORT_CONV, HEAD_DIM), lambda b, j: (0, j))],
        out_specs=pl.BlockSpec((seq, HEAD_DIM), lambda b, j: (b, j)),
        scratch_shapes=[pltpu.VMEM((seq + 16, HEAD_DIM), F32)],
        compiler_params=_params("parallel", "parallel"),
        name="qkv_conv",
    )(u_qkvz, conv_w)


def _dw_conv_kernel(x_ref, w_ref, b_ref, o_ref, xp_ref, *, seq, rows):
    pad = 16
    _fill_padded(xp_ref, x_ref, pad, seq)
    w = w_ref[...]
    for r0 in range(0, seq, rows):
        y = _conv_rows(xp_ref, w, DW_CONV, pad - DW_CONV // 2, r0, rows) + b_ref[...]
        o_ref[r0:r0 + rows, :] = y.astype(o_ref.dtype)


def _dw_conv(glu, conv_w, b_dw, bsz, seq):
    t = bsz * seq
    rows = min(256, seq)
    return pl.pallas_call(
        functools.partial(_dw_conv_kernel, seq=seq, rows=rows),
        out_shape=jax.ShapeDtypeStruct((t, WIDTH_B), BF16),
        grid=(bsz, WIDTH_B // LANES),
        in_specs=[pl.BlockSpec((seq, LANES), lambda b, j: (b, j)),
                  pl.BlockSpec((DW_CONV, LANES), lambda b, j: (0, j)),
                  pl.BlockSpec((1, LANES), lambda b, j: (0, j))],
        out_specs=pl.BlockSpec((seq, LANES), lambda b, j: (b, j)),
        scratch_shapes=[pltpu.VMEM((seq + 32, LANES), F32)],
        compiler_params=_params("parallel", "parallel"),
        name="dw_conv",
    )(glu, conv_w, b_dw.reshape(1, WIDTH_B))


def _bmm(a, b):
    return lax.dot_general(a, b, (((2,), (1,)), ((0,), (0,))), preferred_element_type=F32)


def _bmm_nt(a, b):
    return lax.dot_general(a, b, (((2,), (2,)), ((0,), (0,))), preferred_element_type=F32)


def _bmm_tn(a, b):
    return lax.dot_general(a, b, (((1,), (1,)), ((0,), (0,))), preferred_element_type=F32)


def _block_diag_rows(x, half):
    lane = lax.broadcasted_iota(jnp.int32, x.shape, 2)
    return jnp.concatenate([jnp.where(lane < half, x, 0.0), jnp.where(lane >= half, x, 0.0)], axis=1)


def _unit_tri_inverse(lmat, eye):
    def rhs(p):
        return _block_diag_rows(p, CHUNK).astype(BF16)

    x = eye - lmat
    p = _bmm(lmat.astype(BF16), rhs(lmat))
    s = 2
    while 2 * s < CHUNK:
        xp = _bmm(jnp.concatenate([x, p], axis=1).astype(BF16), rhs(p))
        x = x + xp[:, :CHUNK]
        p = xp[:, CHUNK:]
        s *= 2
    return x + _bmm(x.astype(BF16), rhs(p))


def _delta_kernel(qf_ref, kf_ref, vf_ref, gf_ref, gtpf_ref, qb_ref, kb_ref, vb_ref, gb_ref, gtpb_ref,
                  of_ref, ob_ref, s_ref, *, nc):
    @pl.when(pl.program_id(1) == 0)
    def _():
        s_ref[...] = jnp.zeros_like(s_ref)

    n_pairs = N_HEADS // 2
    n_inst = 2 * n_pairs
    pw = 2 * HEAD_DIM
    dirs = ((qf_ref, kf_ref, vf_ref, gf_ref, gtpf_ref, of_ref, False),
            (qb_ref, kb_ref, vb_ref, gb_ref, gtpb_ref, ob_ref, True))
    steps = [[d + ((nc - 1 - i) if d[6] else i,) for d in dirs] for i in range(nc)]

    ri = lax.broadcasted_iota(jnp.int32, (CHUNK, 2 * CHUNK), 0)
    ci = jnp.bitwise_and(lax.broadcasted_iota(jnp.int32, (CHUNK, 2 * CHUNK), 1), CHUNK - 1)
    inst = lax.broadcasted_iota(jnp.int32, (nc * n_inst, 1, 1), 0)
    sign = 1 - 2 * jnp.bitwise_and(jnp.right_shift(inst, n_pairs.bit_length() - 1), 1)
    rel = (ri - ci)[None] * sign
    incl = rel >= 0
    strict = rel > 0
    eye = (ri == ci).astype(F32)

    def pairs(which):
        return jnp.stack([d[which][d[7] * CHUNK:(d[7] + 1) * CHUNK, p * pw:(p + 1) * pw]
                          for st in steps for d in st for p in range(n_pairs)]).astype(F32)

    def pair_bcast(cols, width):
        return jnp.stack([jnp.concatenate([jnp.broadcast_to(cc[2 * p], (CHUNK, width)),
                                           jnp.broadcast_to(cc[2 * p + 1], (CHUNK, width))], axis=1)
                          for cc in cols for p in range(n_pairs)])

    qf = pairs(0)
    kf = pairs(1)
    vf = pairs(2)
    beta_c, g_c, glast_c = [], [], []
    for st in steps:
        for d in st:
            gblk = d[3][d[7] * CHUNK:(d[7] + 1) * CHUNK, :]
            off = N_HEADS if d[6] else 0
            last = 0 if d[6] else CHUNK - 1
            beta_c.append([gblk[:, off + hh:off + hh + 1] for hh in range(N_HEADS)])
            g_c.append([gblk[:, 2 * N_HEADS + off + hh:2 * N_HEADS + off + hh + 1] for hh in range(N_HEADS)])
            glast_c.append([gc[last:last + 1, :] for gc in g_c[-1]])
    beta = pair_bcast(beta_c, HEAD_DIM)
    eg = pair_bcast([[jnp.exp(gc) for gc in gcs] for gcs in g_c], HEAD_DIM)
    tail = pair_bcast([[jnp.exp(gl - gc) for gl, gc in zip(gls, gcs)] for gls, gcs in zip(glast_c, g_c)],
                      HEAD_DIM)
    gcol = pair_bcast(g_c, CHUNK)
    grow = jnp.stack([d[4][(n_pairs if d[6] else 0) + p:(n_pairs if d[6] else 0) + p + 1,
                           2 * d[7] * CHUNK:2 * (d[7] + 1) * CHUNK]
                      for st in steps for d in st for p in range(n_pairs)])

    decay = jnp.exp(jnp.where(incl, gcol - grow, NEG_BIG))
    kb = kf * beta
    kkqk = _bmm_nt(jnp.concatenate([kb, qf], axis=1).astype(BF16),
                   _block_diag_rows(kf, HEAD_DIM).astype(BF16))
    lmat = jnp.where(strict, kkqk[:, :CHUNK, :] * decay, 0.0)
    qk = (kkqk[:, CHUNK:, :] * decay).astype(BF16)
    tinv = _unit_tri_inverse(lmat, eye)
    rhs = jnp.concatenate([_block_diag_rows(vf * beta, HEAD_DIM),
                           _block_diag_rows(kb * eg, HEAD_DIM)], axis=2).astype(BF16)
    uw = _bmm(tinv.astype(BF16), rhs)
    u = uw[:, :, :pw]
    wq = jnp.concatenate([uw[:, :, pw:], qf * eg], axis=1).astype(BF16)
    kt = (kf * tail).astype(BF16)

    for i, st in enumerate(steps):
        sl = slice(i * n_inst, (i + 1) * n_inst)
        s_a = s_ref[:, 0]
        s_b = s_ref[:, 1]
        zero = jnp.zeros_like(s_a)
        s_bd = jnp.concatenate([jnp.concatenate([s_a, zero], axis=2),
                                jnp.concatenate([zero, s_b], axis=2)], axis=1).astype(BF16)
        ws = _bmm(wq[sl], s_bd)
        v_new = u[sl] - ws[:, :CHUNK, :]
        o = ws[:, CHUNK:, :] + _bmm(qk[sl], _block_diag_rows(v_new, HEAD_DIM).astype(BF16))
        v16 = v_new.astype(BF16)
        gl = [glast_c[2 * i + dd] for dd in range(2)]
        cd_a = jnp.stack([jnp.exp(gl[dd][2 * p]) for dd in range(2) for p in range(n_pairs)])
        cd_b = jnp.stack([jnp.exp(gl[dd][2 * p + 1]) for dd in range(2) for p in range(n_pairs)])
        s_ref[:, 0] = s_a * cd_a + _bmm_tn(kt[sl, :, :HEAD_DIM], v16[:, :, :HEAD_DIM])
        s_ref[:, 1] = s_b * cd_b + _bmm_tn(kt[sl, :, HEAD_DIM:], v16[:, :, HEAD_DIM:])
        for dd, d in enumerate(st):
            for p in range(n_pairs):
                d[5][d[7] * CHUNK:(d[7] + 1) * CHUNK, p * pw:(p + 1) * pw] = (
                    o[dd * n_pairs + p].astype(d[5].dtype))


def _delta_rule(qkv, g, gtp, bsz, seq):
    t = bsz * seq
    rows = min(DELTA_ROWS, seq)
    nblk = seq // rows

    def fwd(col):
        return lambda b, i: (b * nblk + i, col)

    def bwd(col):
        return lambda b, i: (b * nblk + nblk - 1 - i, col)

    def specs(m):
        return [pl.BlockSpec((rows, WIDTH_A), m(0)), pl.BlockSpec((rows, WIDTH_A), m(1)),
                pl.BlockSpec((rows, WIDTH_A), m(2)), pl.BlockSpec((rows, LANES), m(0)),
                pl.BlockSpec((N_HEADS, 2 * rows), lambda b, i, m=m: (0, m(0)(b, i)[0]))]

    out = jax.ShapeDtypeStruct((t, WIDTH_A), BF16)
    return pl.pallas_call(
        functools.partial(_delta_kernel, nc=rows // CHUNK),
        out_shape=(out, out),
        grid=(bsz, nblk),
        in_specs=specs(fwd) + specs(bwd),
        out_specs=(pl.BlockSpec((rows, WIDTH_A), fwd(0)), pl.BlockSpec((rows, WIDTH_A), bwd(0))),
        scratch_shapes=[pltpu.VMEM((N_HEADS, 2, HEAD_DIM, HEAD_DIM), F32)],
        compiler_params=_params("parallel", "arbitrary"),
        name="delta_rule",
    )(qkv, qkv, qkv, g, gtp, qkv, qkv, qkv, g, gtp)


def _mix_kernel(of_ref, ob_ref, z_ref, yc_ref, gate_ref, x_ref,
                eg_ref, eb_ref, ng_ref, wao_ref, cg_ref, cb_ref, wbo_ref, bbo_ref,
                wout_ref, l1g_ref, l1b_ref, wr_ref, br_ref,
                h1_ref, h1p_ref, logit_ref):
    o = of_ref[...].astype(F32) + ob_ref[...].astype(F32)
    z = z_ref[...].astype(F32)
    ng = ng_ref[...]
    parts = []
    for hh in range(N_HEADS):
        sl = slice(hh * HEAD_DIM, (hh + 1) * HEAD_DIM)
        oh = o[:, sl]
        zh = z[:, sl]
        inv = lax.rsqrt(jnp.mean(oh * oh, axis=-1, keepdims=True) + RMS_EPS)
        parts.append((oh * inv * ng * _silu(zh)).astype(BF16))
    y_a = _dot(jnp.concatenate(parts, axis=1), wao_ref[...])

    yc = _layer_norm(yc_ref[...].astype(F32), cg_ref[...], cb_ref[...])
    y_b = _dot(_silu(yc).astype(BF16), wbo_ref[...]) + bbo_ref[...]

    gates = gate_ref[...].astype(F32)
    mixed = gates[:, :D_MODEL] * y_a + gates[:, D_MODEL:] * y_b
    mix = _dot(mixed.astype(BF16), wout_ref[...])

    h0 = _layer_norm(x_ref[...], eg_ref[...], eb_ref[...])
    h1 = _layer_norm(DN_ALPHA * h0 + mix, l1g_ref[...], l1b_ref[...])
    h1_ref[...] = h1
    h1p_ref[...] = _pack_bf16_pair(h1[:, :D_MODEL // 2], h1[:, D_MODEL // 2:])
    h_hi, h_lo = _split_bf16(h1)
    p = _dot(h_hi, wr_ref[...])
    logit_ref[...] = p[:, :LANES] + p[:, LANES:] + _dot(h_lo, wr_ref[:, :LANES]) + br_ref[...]


def _mix(o_f, o_b, u_qkvz, yc, gates, x2, emb_g, emb_b, norm_g, w_a_o, cg, cb, w_b_o, b_b_o,
         w_out, l1g, l1b, w_router, b_router):
    t, d = x2.shape
    tm = min(ROW_TILE, t)
    row = lambda i: (i, 0)
    const = lambda i: (0, 0)
    wr = jnp.concatenate(_split_bf16(jnp.pad(w_router, ((0, 0), (0, LANES - N_EXPERTS)))), axis=1)
    br = jnp.pad(b_router.reshape(1, N_EXPERTS), ((0, 0), (0, LANES - N_EXPERTS)), constant_values=NEG_BIG)
    vec = lambda a: a.reshape(1, -1)
    return pl.pallas_call(
        _mix_kernel,
        out_shape=(jax.ShapeDtypeStruct((t, d), F32), jax.ShapeDtypeStruct((t, d // 2), jnp.uint32),
                   jax.ShapeDtypeStruct((t, LANES), F32)),
        grid=(t // tm,),
        in_specs=[pl.BlockSpec((tm, d), row), pl.BlockSpec((tm, d), row),
                  pl.BlockSpec((tm, d), lambda i: (i, 3)),
                  pl.BlockSpec((tm, d), row), pl.BlockSpec((tm, 2 * d), row), pl.BlockSpec((tm, d), row),
                  pl.BlockSpec((1, d), const), pl.BlockSpec((1, d), const),
                  pl.BlockSpec((1, HEAD_DIM), const), pl.BlockSpec((d, d), const),
                  pl.BlockSpec((1, d), const), pl.BlockSpec((1, d), const),
                  pl.BlockSpec((d, d), const), pl.BlockSpec((1, d), const),
                  pl.BlockSpec((d, d), const), pl.BlockSpec((1, d), const), pl.BlockSpec((1, d), const),
                  pl.BlockSpec((d, 2 * LANES), const), pl.BlockSpec((1, LANES), const)],
        out_specs=(pl.BlockSpec((tm, d), row), pl.BlockSpec((tm, d // 2), row), pl.BlockSpec((tm, LANES), row)),
        compiler_params=_params("parallel"),
        name="mix",
    )(o_f, o_b, u_qkvz, yc, gates, x2, vec(emb_g), vec(emb_b), vec(norm_g), w_a_o, vec(cg), vec(cb),
      w_b_o, vec(b_b_o), w_out, vec(l1g), vec(l1b), wr, br)


def _route_kernel(logit_ref, gate_ref, eidx_ref, rank_ref, cnt_ref, base_ref):
    @pl.when(pl.program_id(0) == 0)
    def _():
        base_ref[...] = jnp.zeros_like(base_ref)

    x = logit_ref[...]
    tm = x.shape[0]
    lane = lax.broadcasted_iota(jnp.int32, x.shape, 1)
    lane_f = lane.astype(F32)
    row = lax.broadcasted_iota(jnp.int32, x.shape, 0)
    sel = jnp.zeros(x.shape, F32)
    vals, idxs = [], []
    for _ in range(TOP_K):
        m = jnp.max(x, axis=1, keepdims=True)
        idx = jnp.min(jnp.where(x == m, lane_f, float(LANES)), axis=1, keepdims=True).astype(jnp.int32)
        hit = lane == idx
        sel = sel + hit.astype(F32)
        x = jnp.where(hit, -3e38, x)
        vals.append(m)
        idxs.append(idx)

    exps = [jnp.exp(v - vals[0]) for v in vals]
    denom = exps[0]
    for e in exps[1:]:
        denom = denom + e

    csum = sel
    s = 1
    while s < tm:
        csum = csum + jnp.where(row >= s, pltpu.roll(csum, s, axis=0), 0.0)
        s *= 2
    before = base_ref[...] + csum - sel

    gate = jnp.zeros(x.shape, F32)
    eidx = jnp.zeros(x.shape, F32)
    rank = jnp.zeros(x.shape, F32)
    for k in range(TOP_K):
        rk = jnp.sum(jnp.where(lane == idxs[k], before, 0.0), axis=1, keepdims=True)
        gate = jnp.where(lane == k, exps[k] / denom, gate)
        eidx = jnp.where(lane == k, idxs[k].astype(F32), eidx)
        rank = jnp.where(lane == k, rk, rank)
    gate_ref[...] = gate
    eidx_ref[...] = eidx.T[:8, :].astype(jnp.int32)
    rank_ref[...] = rank.T[:8, :].astype(jnp.int32)
    total = base_ref[...] + csum[tm - 1:tm, :]
    base_ref[...] = total
    cnt_ref[...] = total


def _route(logits):
    t = logits.shape[0]
    tm = min(ROW_TILE, t)
    bm = EXPERT_ROWS
    row = lambda i: (i, 0)
    gate, eidx, rank, cnt = pl.pallas_call(
        _route_kernel,
        out_shape=(jax.ShapeDtypeStruct((t, LANES), F32), jax.ShapeDtypeStruct((8, t), jnp.int32),
                   jax.ShapeDtypeStruct((8, t), jnp.int32), jax.ShapeDtypeStruct((1, LANES), F32)),
        grid=(t // tm,),
        in_specs=[pl.BlockSpec((tm, LANES), row)],
        out_specs=(pl.BlockSpec((tm, LANES), row), pl.BlockSpec((8, tm), lambda i: (0, i)),
                   pl.BlockSpec((8, tm), lambda i: (0, i)), pl.BlockSpec((1, LANES), lambda i: (0, 0))),
        scratch_shapes=[pltpu.VMEM((1, LANES), F32)],
        compiler_params=_params("arbitrary"),
        name="moe_route",
    )(logits)
    counts = cnt[0, :N_EXPERTS].astype(jnp.int32)
    padded = (counts + bm - 1) // bm * bm
    pad_end = jnp.cumsum(padded)
    pad_start = pad_end - padded
    onehot = eidx[:TOP_K, :, None] == jnp.arange(N_EXPERTS, dtype=jnp.int32)
    pos_kmajor = (jnp.sum(jnp.where(onehot, pad_start, 0), axis=-1) + rank[:TOP_K]).reshape(-1)
    n_blocks = -(-(t * TOP_K + N_EXPERTS * (bm - 1)) // bm)
    block_start = jnp.arange(n_blocks, dtype=jnp.int32) * bm
    block_exp = jnp.minimum(jnp.sum((block_start[:, None] >= pad_end[None, :]).astype(jnp.int32), axis=1),
                            N_EXPERTS - 1)
    block_rows = jnp.clip((pad_start + counts)[block_exp] - block_start, 0, bm).astype(jnp.int32)
    return gate, pos_kmajor.astype(jnp.int32), block_exp, block_rows, n_blocks


def _sc_worker_range(n_rows):
    per_worker = n_rows // (SC_CORES * SC_SUBCORES)
    wid = lax.axis_index("s") * SC_CORES + lax.axis_index("c")
    return wid * per_worker, per_worker


def _sc_scatter_rows(src, idx, n_out):
    t, d = src.shape
    n_copies = idx.shape[0] // t
    mesh = plsc.VectorSubcoreMesh(core_axis_name="c", subcore_axis_name="s")

    @functools.partial(
        pl.kernel, mesh=mesh,
        out_type=jax.ShapeDtypeStruct((n_out, d), src.dtype),
        scratch_types=([pltpu.VMEM((SC_GATHER_ROWS,), jnp.int32)] * n_copies
                       + [pltpu.VMEM((SC_GATHER_ROWS, d), src.dtype)]
                       + [pltpu.SemaphoreType.DMA] * n_copies),
        name="sc_scatter_rows",
    )
    def scatter(src_hbm, idx_hbm, out_hbm, *scratch):
        idx_v = scratch[:n_copies]
        rows_v = scratch[n_copies]
        sems = scratch[n_copies + 1:]
        base, per_worker = _sc_worker_range(t)

        @pl.loop(0, per_worker // SC_GATHER_ROWS)
        def _(j):
            off = pl.multiple_of(base + j * SC_GATHER_ROWS, SC_GATHER_ROWS)
            for k in range(n_copies):
                pltpu.sync_copy(idx_hbm.at[pl.ds(k * t + off, SC_GATHER_ROWS)], idx_v[k])
            pltpu.sync_copy(src_hbm.at[pl.ds(off, SC_GATHER_ROWS)], rows_v)
            copies = [pltpu.async_copy(rows_v, out_hbm.at[idx_v[k]], sems[k]) for k in range(n_copies)]
            for c in copies:
                c.wait()

    return scatter(src, idx)


def _expert_kernel(be_ref, nr_ref, x_ref, wgu_ref, bgu_ref, wd_ref, bd_ref, o_ref, wgu16_ref, wd16_ref, *, tn):
    i = pl.program_id(0)
    active = nr_ref[i] > 0
    new_expert = jnp.logical_or(i == 0, be_ref[i] != be_ref[jnp.maximum(i - 1, 0)])

    @pl.when(jnp.logical_and(active, new_expert))
    def _():
        wgu16_ref[...] = wgu_ref[0].astype(BF16)
        wd16_ref[...] = wd_ref[0].astype(BF16)

    @pl.when(active)
    def _():
        half = D_MODEL // 2
        row = lax.broadcasted_iota(jnp.int32, x_ref.shape, 0)
        x = jnp.where(row < nr_ref[i], x_ref[...], jnp.uint32(0))
        x_lo, x_hi = _unpack_bf16_pair(x)
        x_lo = x_lo.astype(BF16)
        x_hi = x_hi.astype(BF16)

        def up(c0):
            return (_dot(x_lo, wgu16_ref[:half, c0:c0 + tn]) + _dot(x_hi, wgu16_ref[half:, c0:c0 + tn])
                    + bgu_ref[0, :, c0:c0 + tn])

        y = bd_ref[0]
        for n0 in range(0, D_FF, tn):
            glu = jnp.minimum(up(n0), SWIGLU_LIMIT)
            lin = jnp.clip(up(D_FF + n0), -SWIGLU_LIMIT, SWIGLU_LIMIT)
            act = glu * _sigmoid(SWIGLU_ALPHA * glu) * (lin + 1.0)
            y = y + _dot(act.astype(BF16), wd16_ref[n0:n0 + tn, :])
        o_ref[...] = _pack_bf16_pair(y[:, :half], y[:, half:])

    @pl.when(jnp.logical_not(active))
    def _():
        o_ref[...] = jnp.zeros_like(o_ref)


def _experts(xs, block_exp, block_rows, w_gu, b_gu, w_down, b_down, n_blocks):
    d = D_MODEL
    dp = xs.shape[1]
    bm = EXPERT_ROWS
    grid_spec = pltpu.PrefetchScalarGridSpec(
        num_scalar_prefetch=2,
        grid=(n_blocks,),
        in_specs=[pl.BlockSpec((bm, dp), lambda i, be, nb: (i, 0)),
                  pl.BlockSpec((1, d, 2 * D_FF), lambda i, be, nb: (be[i], 0, 0)),
                  pl.BlockSpec((1, 1, 2 * D_FF), lambda i, be, nb: (be[i], 0, 0)),
                  pl.BlockSpec((1, D_FF, d), lambda i, be, nb: (be[i], 0, 0)),
                  pl.BlockSpec((1, 1, d), lambda i, be, nb: (be[i], 0, 0))],
        out_specs=pl.BlockSpec((bm, dp), lambda i, be, nb: (i, 0)),
        scratch_shapes=[pltpu.VMEM((d, 2 * D_FF), BF16), pltpu.VMEM((D_FF, d), BF16)],
    )
    return pl.pallas_call(
        functools.partial(_expert_kernel, tn=EXPERT_SLAB),
        out_shape=jax.ShapeDtypeStruct((n_blocks * bm, dp), jnp.uint32),
        grid_spec=grid_spec,
        compiler_params=_params("arbitrary"),
        name="moe_experts",
    )(block_exp, block_rows, xs, w_gu, b_gu.reshape(N_EXPERTS, 1, 2 * D_FF), w_down,
      b_down.reshape(N_EXPERTS, 1, d))


def _sc_gather_rows(table, idx):
    m = idx.shape[0]
    d = table.shape[1]
    mesh = plsc.VectorSubcoreMesh(core_axis_name="c", subcore_axis_name="s")

    @functools.partial(
        pl.kernel, mesh=mesh,
        out_type=jax.ShapeDtypeStruct((m, d), table.dtype),
        scratch_types=([pltpu.VMEM((SC_GATHER_ROWS,), jnp.int32)] * 2
                       + [pltpu.VMEM((SC_GATHER_ROWS, d), table.dtype)] * 2
                       + [pltpu.SemaphoreType.DMA] * 4),
        name="sc_gather_rows",
    )
    def gather(table_hbm, idx_hbm, out_hbm, idx0, idx1, rows0, rows1, g0, g1, w0, w1):
        base, per_worker = _sc_worker_range(m)

        @pl.loop(0, per_worker // (2 * SC_GATHER_ROWS))
        def _(j):
            off0 = pl.multiple_of(base + 2 * j * SC_GATHER_ROWS, SC_GATHER_ROWS)
            off1 = off0 + SC_GATHER_ROWS
            pltpu.sync_copy(idx_hbm.at[pl.ds(off0, SC_GATHER_ROWS)], idx0)
            gather0 = pltpu.async_copy(table_hbm.at[idx0], rows0, g0)
            pltpu.sync_copy(idx_hbm.at[pl.ds(off1, SC_GATHER_ROWS)], idx1)
            gather1 = pltpu.async_copy(table_hbm.at[idx1], rows1, g1)
            gather0.wait()
            write0 = pltpu.async_copy(rows0, out_hbm.at[pl.ds(off0, SC_GATHER_ROWS)], w0)
            gather1.wait()
            write1 = pltpu.async_copy(rows1, out_hbm.at[pl.ds(off1, SC_GATHER_ROWS)], w1)
            write0.wait()
            write1.wait()

    return gather(table, idx)


def _combine_dense_kernel(y0_ref, y1_ref, y2_ref, y3_ref, gate_ref, h_ref, g_ref, b_ref, o_ref):
    gate = gate_ref[...]
    f_lo = f_hi = None
    for k, y_ref in enumerate((y0_ref, y1_ref, y2_ref, y3_ref)):
        y_lo, y_hi = _unpack_bf16_pair(y_ref[...])
        gk = gate[:, k:k + 1]
        f_lo = gk * y_lo if f_lo is None else f_lo + gk * y_lo
        f_hi = gk * y_hi if f_hi is None else f_hi + gk * y_hi
    f = jnp.concatenate([f_lo, f_hi], axis=1)
    o_ref[...] = _layer_norm(DN_ALPHA * h_ref[...] + f, g_ref[...], b_ref[...])


def _combine_dense(yg, gate, h1, ln_g, ln_b):
    t, d = h1.shape
    tm = min(ROW_TILE, t)
    nt = t // tm
    dp = yg.shape[1]
    slab = lambda k: pl.BlockSpec((tm, dp), lambda i, k=k: (k * nt + i, 0))
    return pl.pallas_call(
        _combine_dense_kernel,
        out_shape=jax.ShapeDtypeStruct((t, d), F32),
        grid=(nt,),
        in_specs=[slab(0), slab(1), slab(2), slab(3),
                  pl.BlockSpec((tm, LANES), lambda i: (i, 0)),
                  pl.BlockSpec((tm, d), lambda i: (i, 0)),
                  pl.BlockSpec((1, d), lambda i: (0, 0)),
                  pl.BlockSpec((1, d), lambda i: (0, 0))],
        out_specs=pl.BlockSpec((tm, d), lambda i: (i, 0)),
        compiler_params=_params("parallel"),
        name="moe_combine",
    )(yg, yg, yg, yg, gate, h1, ln_g.reshape(1, d), ln_b.reshape(1, d))


def kernel(x, emb_ln_g, emb_ln_b, w_in, conv_qkv, a_log, dt_bias, dn_norm_g, w_a_o, b_glu, conv_dw, b_dw, conv_ln_g, conv_ln_b, w_b_o, b_b_o, b_gate, w_out, ln1_g, ln1_b, w_router, b_router, w_gu, b_gu, w_down, b_down, ln2_g, ln2_b):
    bsz, seq, d = x.shape
    t = bsz * seq
    x2 = x.reshape(t, d)
    u_qkvz, g, gt, glu, gates = _inproj(x2, emb_ln_g, emb_ln_b, w_in[0], b_glu[0], b_gate[0], a_log[0], dt_bias[0])

    qkv = _qkv_conv(u_qkvz, conv_qkv[0], bsz, seq)
    gtp = gt.reshape(2, N_HEADS // 2, 2, t // CHUNK, CHUNK).transpose(0, 1, 3, 2, 4).reshape(N_HEADS, 2 * t)
    o_f, o_b = _delta_rule(qkv, g, gtp, bsz, seq)
    yc = _dw_conv(glu, conv_dw[0], b_dw[0], bsz, seq)

    h1, h1p, logits = _mix(o_f, o_b, u_qkvz, yc, gates, x2, emb_ln_g, emb_ln_b, dn_norm_g[0],
                           w_a_o[0].astype(BF16), conv_ln_g[0], conv_ln_b[0], w_b_o[0].astype(BF16), b_b_o[0],
                           w_out[0].astype(BF16), ln1_g[0], ln1_b[0], w_router[0], b_router[0])

    gate, pos_kmajor, block_exp, block_rows, n_blocks = _route(logits)
    xs = _sc_scatter_rows(h1p, pos_kmajor, n_blocks * EXPERT_ROWS)
    ys = _experts(xs, block_exp, block_rows, w_gu[0], b_gu[0], w_down[0], b_down[0], n_blocks)
    yg = _sc_gather_rows(ys, pos_kmajor)
    out = _combine_dense(yg, gate, h1, ln2_g[0], ln2_b[0])
    return out.reshape(bsz, seq, d)
```

```python
import functools

import jax
import jax.numpy as jnp
from jax import lax
from jax.experimental import pallas as pl
from jax.experimental.pallas import tpu as pltpu
from jax.experimental.pallas import tpu_sc as plsc

F32 = jnp.float32
BF16 = jnp.bfloat16

D_MODEL = 1024
N_HEADS = 8
HEAD_DIM = 128
WIDTH_A = N_HEADS * HEAD_DIM
SHORT_CONV = 5
CHUNK = 64
WIDTH_B = D_MODEL
DW_CONV = 31
N_EXPERTS = 32
TOP_K = 4
D_FF = D_MODEL
SWIGLU_ALPHA = 1.702
SWIGLU_LIMIT = 7.0
DN_ALPHA = 2.0 ** 0.25
LN_EPS = 1e-5
RMS_EPS = 1e-6
L2_EPS = 1e-6
LANES = 128
NEG_BIG = -1e30

ROW_TILE = 512
DELTA_ROWS = 512
EXPERT_ROWS = 512
EXPERT_SLAB = 512
COMBINE_CHUNKS = 4
SC_CORES = 2
SC_SUBCORES = 16
SC_GATHER_ROWS = 64
VMEM_LIMIT = 56 * 1024 * 1024


def _params(*sem):
    return pltpu.CompilerParams(dimension_semantics=sem, vmem_limit_bytes=VMEM_LIMIT)


def _layer_norm(x, g, b):
    mu = jnp.mean(x, axis=-1, keepdims=True)
    xc = x - mu
    var = jnp.mean(xc * xc, axis=-1, keepdims=True)
    return xc * lax.rsqrt(var + LN_EPS) * g + b


def _sigmoid(x):
    return 0.5 * jnp.tanh(0.5 * x) + 0.5


def _silu(x):
    h = 0.5 * x
    return h + h * jnp.tanh(h)


def _dot(a, b):
    return jnp.dot(a, b, preferred_element_type=F32)


def _pack_bf16_pair(a, b):
    ua = lax.bitcast_convert_type(a.astype(BF16).astype(F32), jnp.uint32)
    ub = lax.bitcast_convert_type(b.astype(BF16).astype(F32), jnp.uint32)
    return (ua >> 16) | ub


def _unpack_bf16_pair(p):
    a = lax.bitcast_convert_type(p << 16, F32)
    b = lax.bitcast_convert_type(p & jnp.uint32(0xFFFF0000), F32)
    return a, b


def _split_bf16(a):
    hi = a.astype(BF16)
    return hi, (a - hi.astype(F32)).astype(BF16)


def _chunk_cumsum(x, reverse):
    rows = x.shape[0]
    pos = lax.broadcasted_iota(jnp.int32, x.shape, 0) % CHUNK
    s = 1
    while s < CHUNK:
        if reverse:
            shifted = pltpu.roll(x, rows - s, axis=0)
            x = x + jnp.where(pos < CHUNK - s, shifted, 0.0)
        else:
            shifted = pltpu.roll(x, s, axis=0)
            x = x + jnp.where(pos >= s, shifted, 0.0)
        s *= 2
    return x


def _inproj_kernel(x_ref, eg_ref, eb_ref, wq_ref, ws_ref, wga_ref, wgb_ref, bga_ref, bgb_ref, wgt_ref, bgt_ref,
                   alog_ref, dtb_ref, u_ref, g_ref, gt_ref, glu_ref, gate_ref, *, tn):
    h = _layer_norm(x_ref[...], eg_ref[...], eb_ref[...]).astype(BF16)

    for n0 in range(0, u_ref.shape[1], tn):
        u_ref[:, n0:n0 + tn] = _dot(h, wq_ref[:, n0:n0 + tn]).astype(u_ref.dtype)

    us = _dot(h, ws_ref[...])
    lane = lax.broadcasted_iota(jnp.int32, us.shape, 1)
    beta = _sigmoid(us)
    xs = us + dtb_ref[...]
    softplus = jnp.maximum(xs, 0.0) + jnp.log(1.0 + jnp.exp(-jnp.abs(xs)))
    log_a = -jnp.exp(alog_ref[...]) * softplus
    g_fwd = _chunk_cumsum(log_a, reverse=False)
    g_bwd = _chunk_cumsum(log_a, reverse=True)
    gates = jnp.where(lane < 2 * N_HEADS, beta, jnp.where(lane < 3 * N_HEADS, g_fwd, g_bwd))
    g_ref[...] = gates
    gt_ref[...] = gates.T[2 * N_HEADS:4 * N_HEADS, :]

    for n0 in range(0, glu_ref.shape[1], tn):
        lin = _dot(h, wga_ref[:, n0:n0 + tn]) + bga_ref[:, n0:n0 + tn]
        gt = _dot(h, wgb_ref[:, n0:n0 + tn]) + bgb_ref[:, n0:n0 + tn]
        glu_ref[:, n0:n0 + tn] = (lin * _sigmoid(gt)).astype(glu_ref.dtype)

    for n0 in range(0, gate_ref.shape[1], tn):
        gate_ref[:, n0:n0 + tn] = _sigmoid(_dot(h, wgt_ref[:, n0:n0 + tn])
                                           + bgt_ref[:, n0:n0 + tn]).astype(gate_ref.dtype)


def _inproj(x2, emb_g, emb_b, w_in, b_glu, b_gate, a_log, dt_bias):
    t, d = x2.shape
    tm = min(ROW_TILE, t)
    c0 = 4 * WIDTH_A
    c1 = c0 + 4 * N_HEADS
    c2 = c1 + 2 * WIDTH_B
    wb = w_in.astype(BF16)
    pad = LANES - 4 * N_HEADS
    w_small = jnp.pad(wb[:, c0:c1], ((0, 0), (0, pad)))
    alog = jnp.pad(a_log.reshape(1, 2 * N_HEADS), ((0, 0), (2 * N_HEADS, pad)))
    dtb = jnp.pad(dt_bias.reshape(1, 2 * N_HEADS), ((0, 0), (2 * N_HEADS, pad)))
    row = lambda i: (i, 0)

    def const(shape):
        return pl.BlockSpec(shape, lambda i: (0, 0), pipeline_mode=pl.Buffered(1))

    vec = lambda a: a.reshape(1, -1)
    bf = lambda n: jax.ShapeDtypeStruct((t, n), BF16)
    return pl.pallas_call(
        functools.partial(_inproj_kernel, tn=512),
        out_shape=(bf(c0), jax.ShapeDtypeStruct((t, LANES), F32), jax.ShapeDtypeStruct((2 * N_HEADS, t), F32),
                   bf(WIDTH_B), bf(2 * d)),
        grid=(t // tm,),
        in_specs=[pl.BlockSpec((tm, d), row), const((1, d)), const((1, d)),
                  const((d, c0)), const((d, LANES)),
                  const((d, WIDTH_B)), const((d, WIDTH_B)), const((1, WIDTH_B)), const((1, WIDTH_B)),
                  const((d, 2 * d)), const((1, 2 * d)), const((1, LANES)), const((1, LANES))],
        out_specs=(pl.BlockSpec((tm, c0), row), pl.BlockSpec((tm, LANES), row),
                   pl.BlockSpec((2 * N_HEADS, tm), lambda i: (0, i)),
                   pl.BlockSpec((tm, WIDTH_B), row), pl.BlockSpec((tm, 2 * d), row)),
        compiler_params=_params("parallel"),
        name="inproj",
    )(x2, vec(emb_g), vec(emb_b), wb[:, :c0], w_small, wb[:, c1:c1 + WIDTH_B], wb[:, c1 + WIDTH_B:c2],
      vec(b_glu[:WIDTH_B]), vec(b_glu[WIDTH_B:]), wb[:, c2:], vec(b_gate), alog, dtb)


def _conv_rows(xp_ref, w, taps, base, r0, rows):
    acc = xp_ref[base + r0:base + r0 + rows, :] * w[0:1, :]
    for k in range(1, taps):
        acc = acc + xp_ref[base + k + r0:base + k + r0 + rows, :] * w[k:k + 1, :]
    return acc


def _fill_padded(xp_ref, x_ref, pad, seq):
    zeros = jnp.zeros((pad, xp_ref.shape[1]), F32)
    xp_ref[0:pad, :] = zeros
    xp_ref[pad + seq:pad + seq + pad, :] = zeros
    xp_ref[pad:pad + seq, :] = x_ref[...].astype(F32)


def _qkv_conv_kernel(u_ref, w_ref, o_ref, xp_ref, *, seq, rows):
    pad = 8
    j = pl.program_id(1)
    _fill_padded(xp_ref, u_ref, pad, seq)
    w = w_ref[...]
    is_qk = j < 2 * N_HEADS
    head_scale = jnp.where(j < N_HEADS, HEAD_DIM ** -0.5, 1.0)
    for r0 in range(0, seq, rows):
        y = _silu(_conv_rows(xp_ref, w, SHORT_CONV, pad - SHORT_CONV // 2, r0, rows))
        inv = lax.rsqrt(jnp.sum(y * y, axis=-1, keepdims=True) + L2_EPS)
        scale = jnp.where(is_qk, inv * head_scale, 1.0)
        o_ref[r0:r0 + rows, :] = (y * scale).astype(o_ref.dtype)


def _qkv_conv(u_qkvz, conv_w, bsz, seq):
    t = bsz * seq
    ncol = 3 * N_HEADS
    rows = min(256, seq)
    return pl.pallas_call(
        functools.partial(_qkv_conv_kernel, seq=seq, rows=rows),
        out_shape=jax.ShapeDtypeStruct((t, 3 * WIDTH_A), BF16),
        grid=(bsz, ncol),
        in_specs=[pl.BlockSpec((seq, HEAD_DIM), lambda b, j: (b, j)),
                  pl.BlockSpec((SHORT_CONV, HEAD_DIM), lambda b, j: (0, j))],
        out_specs=pl.BlockSpec((seq, HEAD_DIM), lambda b, j: (b, j)),
        scratch_shapes=[pltpu.VMEM((seq + 16, HEAD_DIM), F32)],
        compiler_params=_params("parallel", "parallel"),
        name="qkv_conv",
    )(u_qkvz, conv_w)


def _dw_conv_kernel(x_ref, w_ref, b_ref, o_ref, xp_ref, *, seq, rows):
    pad = 16
    _fill_padded(xp_ref, x_ref, pad, seq)
    w = w_ref[...]
    for r0 in range(0, seq, rows):
        y = _conv_rows(xp_ref, w, DW_CONV, pad - DW_CONV // 2, r0, rows) + b_ref[...]
        o_ref[r0:r0 + rows, :] = y.astype(o_ref.dtype)


def _dw_conv(glu, conv_w, b_dw, bsz, seq):
    t = bsz * seq
    rows = min(256, seq)
    return pl.pallas_call(
        functools.partial(_dw_conv_kernel, seq=seq, rows=rows),
        out_shape=jax.ShapeDtypeStruct((t, WIDTH_B), BF16),
        grid=(bsz, WIDTH_B // LANES),
        in_specs=[pl.BlockSpec((seq, LANES), lambda b, j: (b, j)),
                  pl.BlockSpec((DW_CONV, LANES), lambda b, j: (0, j)),
                  pl.BlockSpec((1, LANES), lambda b, j: (0, j))],
        out_specs=pl.BlockSpec((seq, LANES), lambda b, j: (b, j)),
        scratch_shapes=[pltpu.VMEM((seq + 32, LANES), F32)],
        compiler_params=_params("parallel", "parallel"),
        name="dw_conv",
    )(glu, conv_w, b_dw.reshape(1, WIDTH_B))


def _bmm(a, b):
    return lax.dot_general(a, b, (((2,), (1,)), ((0,), (0,))), preferred_element_type=F32)


def _bmm_nt(a, b):
    return lax.dot_general(a, b, (((2,), (2,)), ((0,), (0,))), preferred_element_type=F32)


def _bmm_tn(a, b):
    return lax.dot_general(a, b, (((1,), (1,)), ((0,), (0,))), preferred_element_type=F32)


def _block_diag_rows(x, half):
    lane = lax.broadcasted_iota(jnp.int32, x.shape, 2)
    return jnp.concatenate([jnp.where(lane < half, x, 0.0), jnp.where(lane >= half, x, 0.0)], axis=1)


def _unit_tri_inverse(lmat, eye):
    def rhs(p):
        return _block_diag_rows(p, CHUNK).astype(BF16)

    x = eye - lmat
    p = _bmm(lmat.astype(BF16), rhs(lmat))
    s = 2
    while 2 * s < CHUNK:
        xp = _bmm(jnp.concatenate([x, p], axis=1).astype(BF16), rhs(p))
        x = x + xp[:, :CHUNK]
        p = xp[:, CHUNK:]
        s *= 2
    return x + _bmm(x.astype(BF16), rhs(p))


def _delta_kernel(qf_ref, kf_ref, vf_ref, gf_ref, gtpf_ref, qb_ref, kb_ref, vb_ref, gb_ref, gtpb_ref,
                  of_ref, ob_ref, s_ref, *, nc):
    @pl.when(pl.program_id(1) == 0)
    def _():
        s_ref[...] = jnp.zeros_like(s_ref)

    n_pairs = N_HEADS // 2
    n_inst = 2 * n_pairs
    pw = 2 * HEAD_DIM
    dirs = ((qf_ref, kf_ref, vf_ref, gf_ref, gtpf_ref, of_ref, False),
            (qb_ref, kb_ref, vb_ref, gb_ref, gtpb_ref, ob_ref, True))
    steps = [[d + ((nc - 1 - i) if d[6] else i,) for d in dirs] for i in range(nc)]

    ri = lax.broadcasted_iota(jnp.int32, (CHUNK, 2 * CHUNK), 0)
    ci = jnp.bitwise_and(lax.broadcasted_iota(jnp.int32, (CHUNK, 2 * CHUNK), 1), CHUNK - 1)
    inst = lax.broadcasted_iota(jnp.int32, (nc * n_inst, 1, 1), 0)
    sign = 1 - 2 * jnp.bitwise_and(jnp.right_shift(inst, n_pairs.bit_length() - 1), 1)
    rel = (ri - ci)[None] * sign
    incl = rel >= 0
    strict = rel > 0
    eye = (ri == ci).astype(F32)

    def pairs(which):
        return jnp.stack([d[which][d[7] * CHUNK:(d[7] + 1) * CHUNK, p * pw:(p + 1) * pw]
                          for st in steps for d in st for p in range(n_pairs)]).astype(F32)

    def pair_bcast(cols, width):
        return jnp.stack([jnp.concatenate([jnp.broadcast_to(cc[2 * p], (CHUNK, width)),
                                           jnp.broadcast_to(cc[2 * p + 1], (CHUNK, width))], axis=1)
                          for cc in cols for p in range(n_pairs)])

    qf = pairs(0)
    kf = pairs(1)
    vf = pairs(2)
    beta_c, g_c, glast_c = [], [], []
    for st in steps:
        for d in st:
            gblk = d[3][d[7] * CHUNK:(d[7] + 1) * CHUNK, :]
            off = N_HEADS if d[6] else 0
            last = 0 if d[6] else CHUNK - 1
            beta_c.append([gblk[:, off + hh:off + hh + 1] for hh in range(N_HEADS)])
            g_c.append([gblk[:, 2 * N_HEADS + off + hh:2 * N_HEADS + off + hh + 1] for hh in range(N_HEADS)])
            glast_c.append([gc[last:last + 1, :] for gc in g_c[-1]])
    beta = pair_bcast(beta_c, HEAD_DIM)
    eg = pair_bcast([[jnp.exp(gc) for gc in gcs] for gcs in g_c], HEAD_DIM)
    tail = pair_bcast([[jnp.exp(gl - gc) for gl, gc in zip(gls, gcs)] for gls, gcs in zip(glast_c, g_c)],
                      HEAD_DIM)
    gcol = pair_bcast(g_c, CHUNK)
    grow = jnp.stack([d[4][(n_pairs if d[6] else 0) + p:(n_pairs if d[6] else 0) + p + 1,
                           2 * d[7] * CHUNK:2 * (d[7] + 1) * CHUNK]
                      for st in steps for d in st for p in range(n_pairs)])

    decay = jnp.exp(jnp.where(incl, gcol - grow, NEG_BIG))
    kb = kf * beta
    kkqk = _bmm_nt(jnp.concatenate([kb, qf], axis=1).astype(BF16),
                   _block_diag_rows(kf, HEAD_DIM).astype(BF16))
    lmat = jnp.where(strict, kkqk[:, :CHUNK, :] * decay, 0.0)
    qk = (kkqk[:, CHUNK:, :] * decay).astype(BF16)
    tinv = _unit_tri_inverse(lmat, eye)
    rhs = jnp.concatenate([_block_diag_rows(vf * beta, HEAD_DIM),
                           _block_diag_rows(kb * eg, HEAD_DIM)], axis=2).astype(BF16)
    uw = _bmm(tinv.astype(BF16), rhs)
    u = uw[:, :, :pw]
    wq = jnp.concatenate([uw[:, :, pw:], qf * eg], axis=1).astype(BF16)
    kt = (kf * tail).astype(BF16)

    for i, st in enumerate(steps):
        sl = slice(i * n_inst, (i + 1) * n_inst)
        s_a = s_ref[:, 0]
        s_b = s_ref[:, 1]
        zero = jnp.zeros_like(s_a)
        s_bd = jnp.concatenate([jnp.concatenate([s_a, zero], axis=2),
                                jnp.concatenate([zero, s_b], axis=2)], axis=1).astype(BF16)
        ws = _bmm(wq[sl], s_bd)
        v_new = u[sl] - ws[:, :CHUNK, :]
        o = ws[:, CHUNK:, :] + _bmm(qk[sl], _block_diag_rows(v_new, HEAD_DIM).astype(BF16))
        v16 = v_new.astype(BF16)
        gl = [glast_c[2 * i + dd] for dd in range(2)]
        cd_a = jnp.stack([jnp.exp(gl[dd][2 * p]) for dd in range(2) for p in range(n_pairs)])
        cd_b = jnp.stack([jnp.exp(gl[dd][2 * p + 1]) for dd in range(2) for p in range(n_pairs)])
        s_ref[:, 0] = s_a * cd_a + _bmm_tn(kt[sl, :, :HEAD_DIM], v16[:, :, :HEAD_DIM])
        s_ref[:, 1] = s_b * cd_b + _bmm_tn(kt[sl, :, HEAD_DIM:], v16[:, :, HEAD_DIM:])
        for dd, d in enumerate(st):
            for p in range(n_pairs):
                d[5][d[7] * CHUNK:(d[7] + 1) * CHUNK, p * pw:(p + 1) * pw] = (
                    o[dd * n_pairs + p].astype(d[5].dtype))


def _delta_rule(qkv, g, gtp, bsz, seq):
    t = bsz * seq
    rows = min(DELTA_ROWS, seq)
    nblk = seq // rows

    def fwd(col):
        return lambda b, i: (b * nblk + i, col)

    def bwd(col):
        return lambda b, i: (b * nblk + nblk - 1 - i, col)

    def specs(m):
        return [pl.BlockSpec((rows, WIDTH_A), m(0)), pl.BlockSpec((rows, WIDTH_A), m(1)),
                pl.BlockSpec((rows, WIDTH_A), m(2)), pl.BlockSpec((rows, LANES), m(0)),
                pl.BlockSpec((N_HEADS, 2 * rows), lambda b, i, m=m: (0, m(0)(b, i)[0]))]

    out = jax.ShapeDtypeStruct((t, WIDTH_A), BF16)
    return pl.pallas_call(
        functools.partial(_delta_kernel, nc=rows // CHUNK),
        out_shape=(out, out),
        grid=(bsz, nblk),
        in_specs=specs(fwd) + specs(bwd),
        out_specs=(pl.BlockSpec((rows, WIDTH_A), fwd(0)), pl.BlockSpec((rows, WIDTH_A), bwd(0))),
        scratch_shapes=[pltpu.VMEM((N_HEADS, 2, HEAD_DIM, HEAD_DIM), F32)],
        compiler_params=_params("parallel", "arbitrary"),
        name="delta_rule",
    )(qkv, qkv, qkv, g, gtp, qkv, qkv, qkv, g, gtp)


def _mix_kernel(of_ref, ob_ref, z_ref, yc_ref, gate_ref, x_ref,
                eg_ref, eb_ref, ng_ref, wao_ref, cg_ref, cb_ref, wbo_ref, bbo_ref,
                wout_ref, l1g_ref, l1b_ref, wr_ref, br_ref,
                h1_ref, h1p_ref, logit_ref):
    o = of_ref[...].astype(F32) + ob_ref[...].astype(F32)
    z = z_ref[...].astype(F32)
    ng = ng_ref[...]
    parts = []
    for hh in range(N_HEADS):
        sl = slice(hh * HEAD_DIM, (hh + 1) * HEAD_DIM)
        oh = o[:, sl]
        zh = z[:, sl]
        inv = lax.rsqrt(jnp.mean(oh * oh, axis=-1, keepdims=True) + RMS_EPS)
        parts.append((oh * inv * ng * _silu(zh)).astype(BF16))
    y_a = _dot(jnp.concatenate(parts, axis=1), wao_ref[...])

    yc = _layer_norm(yc_ref[...].astype(F32), cg_ref[...], cb_ref[...])
    y_b = _dot(_silu(yc).astype(BF16), wbo_ref[...]) + bbo_ref[...]

    gates = gate_ref[...].astype(F32)
    mixed = gates[:, :D_MODEL] * y_a + gates[:, D_MODEL:] * y_b
    mix = _dot(mixed.astype(BF16), wout_ref[...])

    h0 = _layer_norm(x_ref[...], eg_ref[...], eb_ref[...])
    h1 = _layer_norm(DN_ALPHA * h0 + mix, l1g_ref[...], l1b_ref[...])
    h1_ref[...] = h1
    h1p_ref[...] = _pack_bf16_pair(h1[:, :D_MODEL // 2], h1[:, D_MODEL // 2:])
    h_hi, h_lo = _split_bf16(h1)
    p = _dot(h_hi, wr_ref[...])
    logit_ref[...] = p[:, :LANES] + p[:, LANES:] + _dot(h_lo, wr_ref[:, :LANES]) + br_ref[...]


def _mix(o_f, o_b, u_qkvz, yc, gates, x2, emb_g, emb_b, norm_g, w_a_o, cg, cb, w_b_o, b_b_o,
         w_out, l1g, l1b, w_router, b_router):
    t, d = x2.shape
    tm = min(ROW_TILE, t)
    row = lambda i: (i, 0)
    const = lambda i: (0, 0)
    wr = jnp.concatenate(_split_bf16(jnp.pad(w_router, ((0, 0), (0, LANES - N_EXPERTS)))), axis=1)
    br = jnp.pad(b_router.reshape(1, N_EXPERTS), ((0, 0), (0, LANES - N_EXPERTS)), constant_values=NEG_BIG)
    vec = lambda a: a.reshape(1, -1)
    return pl.pallas_call(
        _mix_kernel,
        out_shape=(jax.ShapeDtypeStruct((t, d), F32), jax.ShapeDtypeStruct((t, d // 2), jnp.uint32),
                   jax.ShapeDtypeStruct((t, LANES), F32)),
        grid=(t // tm,),
        in_specs=[pl.BlockSpec((tm, d), row), pl.BlockSpec((tm, d), row),
                  pl.BlockSpec((tm, d), lambda i: (i, 3)),
                  pl.BlockSpec((tm, d), row), pl.BlockSpec((tm, 2 * d), row), pl.BlockSpec((tm, d), row),
                  pl.BlockSpec((1, d), const), pl.BlockSpec((1, d), const),
                  pl.BlockSpec((1, HEAD_DIM), const), pl.BlockSpec((d, d), const),
                  pl.BlockSpec((1, d), const), pl.BlockSpec((1, d), const),
                  pl.BlockSpec((d, d), const), pl.BlockSpec((1, d), const),
                  pl.BlockSpec((d, d), const), pl.BlockSpec((1, d), const), pl.BlockSpec((1, d), const),
                  pl.BlockSpec((d, 2 * LANES), const), pl.BlockSpec((1, LANES), const)],
        out_specs=(pl.BlockSpec((tm, d), row), pl.BlockSpec((tm, d // 2), row), pl.BlockSpec((tm, LANES), row)),
        compiler_params=_params("parallel"),
        name="mix",
    )(o_f, o_b, u_qkvz, yc, gates, x2, vec(emb_g), vec(emb_b), vec(norm_g), w_a_o, vec(cg), vec(cb),
      w_b_o, vec(b_b_o), w_out, vec(l1g), vec(l1b), wr, br)


def _route_kernel(logit_ref, gate_ref, eidx_ref, rank_ref, cnt_ref, base_ref):
    @pl.when(pl.program_id(0) == 0)
    def _():
        base_ref[...] = jnp.zeros_like(base_ref)

    x = logit_ref[...]
    tm = x.shape[0]
    lane = lax.broadcasted_iota(jnp.int32, x.shape, 1)
    lane_f = lane.astype(F32)
    row = lax.broadcasted_iota(jnp.int32, x.shape, 0)
    sel = jnp.zeros(x.shape, F32)
    vals, idxs = [], []
    for _ in range(TOP_K):
        m = jnp.max(x, axis=1, keepdims=True)
        idx = jnp.min(jnp.where(x == m, lane_f, float(LANES)), axis=1, keepdims=True).astype(jnp.int32)
        hit = lane == idx
        sel = sel + hit.astype(F32)
        x = jnp.where(hit, -3e38, x)
        vals.append(m)
        idxs.append(idx)

    exps = [jnp.exp(v - vals[0]) for v in vals]
    denom = exps[0]
    for e in exps[1:]:
        denom = denom + e

    csum = sel
    s = 1
    while s < tm:
        csum = csum + jnp.where(row >= s, pltpu.roll(csum, s, axis=0), 0.0)
        s *= 2
    before = base_ref[...] + csum - sel

    gate = jnp.zeros(x.shape, F32)
    eidx = jnp.zeros(x.shape, F32)
    rank = jnp.zeros(x.shape, F32)
    for k in range(TOP_K):
        rk = jnp.sum(jnp.where(lane == idxs[k], before, 0.0), axis=1, keepdims=True)
        gate = jnp.where(lane == k, exps[k] / denom, gate)
        eidx = jnp.where(lane == k, idxs[k].astype(F32), eidx)
        rank = jnp.where(lane == k, rk, rank)
    gate_ref[...] = gate
    eidx_ref[...] = eidx.T[:8, :].astype(jnp.int32)
    rank_ref[...] = rank.T[:8, :].astype(jnp.int32)
    total = base_ref[...] + csum[tm - 1:tm, :]
    base_ref[...] = total
    cnt_ref[...] = total


def _route(logits):
    t = logits.shape[0]
    tm = min(ROW_TILE, t)
    bm = EXPERT_ROWS
    row = lambda i: (i, 0)
    gate, eidx, rank, cnt = pl.pallas_call(
        _route_kernel,
        out_shape=(jax.ShapeDtypeStruct((t, LANES), F32), jax.ShapeDtypeStruct((8, t), jnp.int32),
                   jax.ShapeDtypeStruct((8, t), jnp.int32), jax.ShapeDtypeStruct((1, LANES), F32)),
        grid=(t // tm,),
        in_specs=[pl.BlockSpec((tm, LANES), row)],
        out_specs=(pl.BlockSpec((tm, LANES), row), pl.BlockSpec((8, tm), lambda i: (0, i)),
                   pl.BlockSpec((8, tm), lambda i: (0, i)), pl.BlockSpec((1, LANES), lambda i: (0, 0))),
        scratch_shapes=[pltpu.VMEM((1, LANES), F32)],
        compiler_params=_params("arbitrary"),
        name="moe_route",
    )(logits)
    counts = cnt[0, :N_EXPERTS].astype(jnp.int32)
    padded = (counts + bm - 1) // bm * bm
    pad_end = jnp.cumsum(padded)
    pad_start = pad_end - padded
    onehot = eidx[:TOP_K, :, None] == jnp.arange(N_EXPERTS, dtype=jnp.int32)
    pos_kmajor = (jnp.sum(jnp.where(onehot, pad_start, 0), axis=-1) + rank[:TOP_K]).reshape(-1)
    n_blocks = -(-(t * TOP_K + N_EXPERTS * (bm - 1)) // bm)
    block_start = jnp.arange(n_blocks, dtype=jnp.int32) * bm
    block_exp = jnp.minimum(jnp.sum((block_start[:, None] >= pad_end[None, :]).astype(jnp.int32), axis=1),
                            N_EXPERTS - 1)
    block_rows = jnp.clip((pad_start + counts)[block_exp] - block_start, 0, bm).astype(jnp.int32)
    return gate, pos_kmajor.astype(jnp.int32), block_exp, block_rows, n_blocks


def _sc_worker_range(n_rows):
    per_worker = n_rows // (SC_CORES * SC_SUBCORES)
    wid = lax.axis_index("s") * SC_CORES + lax.axis_index("c")
    return wid * per_worker, per_worker


def _sc_scatter_rows(src, idx, n_out):
    t, d = src.shape
    n_copies = idx.shape[0] // t
    mesh = plsc.VectorSubcoreMesh(core_axis_name="c", subcore_axis_name="s")

    @functools.partial(
        pl.kernel, mesh=mesh,
        out_type=jax.ShapeDtypeStruct((n_out, d), src.dtype),
        scratch_types=([pltpu.VMEM((SC_GATHER_ROWS,), jnp.int32)] * n_copies
                       + [pltpu.VMEM((SC_GATHER_ROWS, d), src.dtype)]
                       + [pltpu.SemaphoreType.DMA] * n_copies),
        name="sc_scatter_rows",
    )
    def scatter(src_hbm, idx_hbm, out_hbm, *scratch):
        idx_v = scratch[:n_copies]
        rows_v = scratch[n_copies]
        sems = scratch[n_copies + 1:]
        base, per_worker = _sc_worker_range(t)

        @pl.loop(0, per_worker // SC_GATHER_ROWS)
        def _(j):
            off = pl.multiple_of(base + j * SC_GATHER_ROWS, SC_GATHER_ROWS)
            for k in range(n_copies):
                pltpu.sync_copy(idx_hbm.at[pl.ds(k * t + off, SC_GATHER_ROWS)], idx_v[k])
            pltpu.sync_copy(src_hbm.at[pl.ds(off, SC_GATHER_ROWS)], rows_v)
            copies = [pltpu.async_copy(rows_v, out_hbm.at[idx_v[k]], sems[k]) for k in range(n_copies)]
            for c in copies:
                c.wait()

    return scatter(src, idx)


def _expert_kernel(be_ref, nr_ref, x_ref, wgu_ref, bgu_ref, wd_ref, bd_ref, o_ref, wgu16_ref, wd16_ref, *, tn):
    i = pl.program_id(0)
    active = nr_ref[i] > 0
    new_expert = jnp.logical_or(i == 0, be_ref[i] != be_ref[jnp.maximum(i - 1, 0)])

    @pl.when(jnp.logical_and(active, new_expert))
    def _():
        wgu16_ref[...] = wgu_ref[0].astype(BF16)
        wd16_ref[...] = wd_ref[0].astype(BF16)

    @pl.when(active)
    def _():
        half = D_MODEL // 2
        row = lax.broadcasted_iota(jnp.int32, x_ref.shape, 0)
        x = jnp.where(row < nr_ref[i], x_ref[...], jnp.uint32(0))
        x_lo, x_hi = _unpack_bf16_pair(x)
        x_lo = x_lo.astype(BF16)
        x_hi = x_hi.astype(BF16)

        def up(c0):
            return (_dot(x_lo, wgu16_ref[:half, c0:c0 + tn]) + _dot(x_hi, wgu16_ref[half:, c0:c0 + tn])
                    + bgu_ref[0, :, c0:c0 + tn])

        y = bd_ref[0]
        for n0 in range(0, D_FF, tn):
            glu = jnp.minimum(up(n0), SWIGLU_LIMIT)
            lin = jnp.clip(up(D_FF + n0), -SWIGLU_LIMIT, SWIGLU_LIMIT)
            act = glu * _sigmoid(SWIGLU_ALPHA * glu) * (lin + 1.0)
            y = y + _dot(act.astype(BF16), wd16_ref[n0:n0 + tn, :])
        o_ref[...] = _pack_bf16_pair(y[:, :half], y[:, half:])

    @pl.when(jnp.logical_not(active))
    def _():
        o_ref[...] = jnp.zeros_like(o_ref)


def _experts(xs, block_exp, block_rows, w_gu, b_gu, w_down, b_down, n_blocks):
    d = D_MODEL
    dp = xs.shape[1]
    bm = EXPERT_ROWS
    grid_spec = pltpu.PrefetchScalarGridSpec(
        num_scalar_prefetch=2,
        grid=(n_blocks,),
        in_specs=[pl.BlockSpec((bm, dp), lambda i, be, nb: (i, 0)),
                  pl.BlockSpec((1, d, 2 * D_FF), lambda i, be, nb: (be[i], 0, 0)),
                  pl.BlockSpec((1, 1, 2 * D_FF), lambda i, be, nb: (be[i], 0, 0)),
                  pl.BlockSpec((1, D_FF, d), lambda i, be, nb: (be[i], 0, 0)),
                  pl.BlockSpec((1, 1, d), lambda i, be, nb: (be[i], 0, 0))],
        out_specs=pl.BlockSpec((bm, dp), lambda i, be, nb: (i, 0)),
        scratch_shapes=[pltpu.VMEM((d, 2 * D_FF), BF16), pltpu.VMEM((D_FF, d), BF16)],
    )
    return pl.pallas_call(
        functools.partial(_expert_kernel, tn=EXPERT_SLAB),
        out_shape=jax.ShapeDtypeStruct((n_blocks * bm, dp), jnp.uint32),
        grid_spec=grid_spec,
        compiler_params=_params("arbitrary"),
        name="moe_experts",
    )(block_exp, block_rows, xs, w_gu, b_gu.reshape(N_EXPERTS, 1, 2 * D_FF), w_down,
      b_down.reshape(N_EXPERTS, 1, d))


def _sc_gather_rows(table, idx):
    m = idx.shape[0]
    d = table.shape[1]
    mesh = plsc.VectorSubcoreMesh(core_axis_name="c", subcore_axis_name="s")

    @functools.partial(
        pl.kernel, mesh=mesh,
        out_type=jax.ShapeDtypeStruct((m, d), table.dtype),
        scratch_types=([pltpu.VMEM((SC_GATHER_ROWS,), jnp.int32)] * 2
                       + [pltpu.VMEM((SC_GATHER_ROWS, d), table.dtype)] * 2
                       + [pltpu.SemaphoreType.DMA] * 4),
        name="sc_gather_rows",
    )
    def gather(table_hbm, idx_hbm, out_hbm, idx0, idx1, rows0, rows1, g0, g1, w0, w1):
        base, per_worker = _sc_worker_range(m)

        @pl.loop(0, per_worker // (2 * SC_GATHER_ROWS))
        def _(j):
            off0 = pl.multiple_of(base + 2 * j * SC_GATHER_ROWS, SC_GATHER_ROWS)
            off1 = off0 + SC_GATHER_ROWS
            pltpu.sync_copy(idx_hbm.at[pl.ds(off0, SC_GATHER_ROWS)], idx0)
            gather0 = pltpu.async_copy(table_hbm.at[idx0], rows0, g0)
            pltpu.sync_copy(idx_hbm.at[pl.ds(off1, SC_GATHER_ROWS)], idx1)
            gather1 = pltpu.async_copy(table_hbm.at[idx1], rows1, g1)
            gather0.wait()
            write0 = pltpu.async_copy(rows0, out_hbm.at[pl.ds(off0, SC_GATHER_ROWS)], w0)
            gather1.wait()
            write1 = pltpu.async_copy(rows1, out_hbm.at[pl.ds(off1, SC_GATHER_ROWS)], w1)
            write0.wait()
            write1.wait()

    return gather(table, idx)


def _combine_dense_kernel(y0_ref, y1_ref, y2_ref, y3_ref, gate_ref, h_ref, g_ref, b_ref, *rest):
    o_ref = rest[-1]
    gate = gate_ref[...]
    f_lo = f_hi = None
    for k, y_ref in enumerate((y0_ref, y1_ref, y2_ref, y3_ref)):
        y_lo, y_hi = _unpack_bf16_pair(y_ref[...])
        gk = gate[:, k:k + 1]
        f_lo = gk * y_lo if f_lo is None else f_lo + gk * y_lo
        f_hi = gk * y_hi if f_hi is None else f_hi + gk * y_hi
    f = jnp.concatenate([f_lo, f_hi], axis=1)
    o_ref[...] = _layer_norm(DN_ALPHA * h_ref[...] + f, g_ref[...], b_ref[...])


def _combine_dense(yg, gate, h1, ln_g, ln_b, chunk, n_chunks, prev_out):
    t, d = h1.shape
    tc = t // n_chunks
    tm = min(ROW_TILE, tc)
    nt = tc // tm
    dp = yg.shape[1]
    slab = lambda k: pl.BlockSpec((tm, dp), lambda i, k=k: (k * nt + i, 0))
    rows = lambda i: (chunk * nt + i, 0)
    in_specs = [slab(0), slab(1), slab(2), slab(3),
                pl.BlockSpec((tm, LANES), rows),
                pl.BlockSpec((tm, d), rows),
                pl.BlockSpec((1, d), lambda i: (0, 0)),
                pl.BlockSpec((1, d), lambda i: (0, 0))]
    args = [yg, yg, yg, yg, gate, h1, ln_g.reshape(1, d), ln_b.reshape(1, d)]
    aliases = {}
    if prev_out is not None:
        in_specs.append(pl.BlockSpec(memory_space=pl.ANY))
        args.append(prev_out)
        aliases = {len(args) - 1: 0}
    return pl.pallas_call(
        _combine_dense_kernel,
        out_shape=jax.ShapeDtypeStruct((t, d), F32),
        grid=(nt,),
        in_specs=in_specs,
        out_specs=pl.BlockSpec((tm, d), rows),
        input_output_aliases=aliases,
        compiler_params=_params("parallel"),
        name="moe_combine",
    )(*args)


def kernel(x, emb_ln_g, emb_ln_b, w_in, conv_qkv, a_log, dt_bias, dn_norm_g, w_a_o, b_glu, conv_dw, b_dw, conv_ln_g, conv_ln_b, w_b_o, b_b_o, b_gate, w_out, ln1_g, ln1_b, w_router, b_router, w_gu, b_gu, w_down, b_down, ln2_g, ln2_b):
    bsz, seq, d = x.shape
    t = bsz * seq
    x2 = x.reshape(t, d)
    u_qkvz, g, gt, glu, gates = _inproj(x2, emb_ln_g, emb_ln_b, w_in[0], b_glu[0], b_gate[0], a_log[0], dt_bias[0])

    qkv = _qkv_conv(u_qkvz, conv_qkv[0], bsz, seq)
    gtp = gt.reshape(2, N_HEADS // 2, 2, t // CHUNK, CHUNK).transpose(0, 1, 3, 2, 4).reshape(N_HEADS, 2 * t)
    o_f, o_b = _delta_rule(qkv, g, gtp, bsz, seq)
    yc = _dw_conv(glu, conv_dw[0], b_dw[0], bsz, seq)

    h1, h1p, logits = _mix(o_f, o_b, u_qkvz, yc, gates, x2, emb_ln_g, emb_ln_b, dn_norm_g[0],
                           w_a_o[0].astype(BF16), conv_ln_g[0], conv_ln_b[0], w_b_o[0].astype(BF16), b_b_o[0],
                           w_out[0].astype(BF16), ln1_g[0], ln1_b[0], w_router[0], b_router[0])

    gate, pos_kmajor, block_exp, block_rows, n_blocks = _route(logits)
    xs = _sc_scatter_rows(h1p, pos_kmajor, n_blocks * EXPERT_ROWS)
    ys = _experts(xs, block_exp, block_rows, w_gu[0], b_gu[0], w_down[0], b_down[0], n_blocks)
    tc = t // COMBINE_CHUNKS
    pos_chunks = pos_kmajor.reshape(TOP_K, COMBINE_CHUNKS, tc)
    out = None
    for c in range(COMBINE_CHUNKS):
        yg = _sc_gather_rows(ys, pos_chunks[:, c, :].reshape(-1))
        out = _combine_dense(yg, gate, h1, ln2_g[0], ln2_b[0], c, COMBINE_CHUNKS, out)
    return out.reshape(bsz, seq, d)
```

```python
import functools

import jax
import jax.numpy as jnp
from jax import lax
from jax.experimental import pallas as pl
from jax.experimental.pallas import tpu as pltpu
from jax.experimental.pallas import tpu_sc as plsc

F32 = jnp.float32
BF16 = jnp.bfloat16

D_MODEL = 1024
N_HEADS = 8
HEAD_DIM = 128
WIDTH_A = N_HEADS * HEAD_DIM
SHORT_CONV = 5
CHUNK = 64
WIDTH_B = D_MODEL
DW_CONV = 31
N_EXPERTS = 32
TOP_K = 4
D_FF = D_MODEL
SWIGLU_ALPHA = 1.702
SWIGLU_LIMIT = 7.0
DN_ALPHA = 2.0 ** 0.25
LN_EPS = 1e-5
RMS_EPS = 1e-6
L2_EPS = 1e-6
LANES = 128
NEG_BIG = -1e30

ROW_TILE = 512
DELTA_ROWS = 512
EXPERT_ROWS = 512
EXPERT_SLAB = 512
SC_CORES = 2
SC_SUBCORES = 16
SC_GATHER_ROWS = 64
VMEM_LIMIT = 56 * 1024 * 1024


def _params(*sem):
    return pltpu.CompilerParams(dimension_semantics=sem, vmem_limit_bytes=VMEM_LIMIT)


def _layer_norm(x, g, b):
    mu = jnp.mean(x, axis=-1, keepdims=True)
    xc = x - mu
    var = jnp.mean(xc * xc, axis=-1, keepdims=True)
    return xc * lax.rsqrt(var + LN_EPS) * g + b


def _sigmoid(x):
    return 0.5 * jnp.tanh(0.5 * x) + 0.5


def _silu(x):
    h = 0.5 * x
    return h + h * jnp.tanh(h)


def _dot(a, b):
    return jnp.dot(a, b, preferred_element_type=F32)


def _pack_bf16_pair(a, b):
    ua = lax.bitcast_convert_type(a.astype(BF16).astype(F32), jnp.uint32)
    ub = lax.bitcast_convert_type(b.astype(BF16).astype(F32), jnp.uint32)
    return (ua >> 16) | ub


def _unpack_bf16_pair(p):
    a = lax.bitcast_convert_type(p << 16, F32)
    b = lax.bitcast_convert_type(p & jnp.uint32(0xFFFF0000), F32)
    return a, b


def _split_bf16(a):
    hi = a.astype(BF16)
    return hi, (a - hi.astype(F32)).astype(BF16)


def _chunk_cumsum(x, reverse):
    rows = x.shape[0]
    pos = lax.broadcasted_iota(jnp.int32, x.shape, 0) % CHUNK
    s = 1
    while s < CHUNK:
        if reverse:
            shifted = pltpu.roll(x, rows - s, axis=0)
            x = x + jnp.where(pos < CHUNK - s, shifted, 0.0)
        else:
            shifted = pltpu.roll(x, s, axis=0)
            x = x + jnp.where(pos >= s, shifted, 0.0)
        s *= 2
    return x


def _inproj_kernel(x_ref, eg_ref, eb_ref, wq_ref, ws_ref, wga_ref, wgb_ref, bga_ref, bgb_ref, wgt_ref, bgt_ref,
                   alog_ref, dtb_ref, u_ref, g_ref, gt_ref, glu_ref, gate_ref, *, tn):
    h = _layer_norm(x_ref[...], eg_ref[...], eb_ref[...]).astype(BF16)

    for n0 in range(0, u_ref.shape[1], tn):
        u_ref[:, n0:n0 + tn] = _dot(h, wq_ref[:, n0:n0 + tn]).astype(u_ref.dtype)

    us = _dot(h, ws_ref[...])
    lane = lax.broadcasted_iota(jnp.int32, us.shape, 1)
    beta = _sigmoid(us)
    xs = us + dtb_ref[...]
    softplus = jnp.maximum(xs, 0.0) + jnp.log(1.0 + jnp.exp(-jnp.abs(xs)))
    log_a = -jnp.exp(alog_ref[...]) * softplus
    g_fwd = _chunk_cumsum(log_a, reverse=False)
    g_bwd = _chunk_cumsum(log_a, reverse=True)
    gates = jnp.where(lane < 2 * N_HEADS, beta, jnp.where(lane < 3 * N_HEADS, g_fwd, g_bwd))
    g_ref[...] = gates
    gt_ref[...] = gates.T[2 * N_HEADS:4 * N_HEADS, :]

    for n0 in range(0, glu_ref.shape[1], tn):
        lin = _dot(h, wga_ref[:, n0:n0 + tn]) + bga_ref[:, n0:n0 + tn]
        gt = _dot(h, wgb_ref[:, n0:n0 + tn]) + bgb_ref[:, n0:n0 + tn]
        glu_ref[:, n0:n0 + tn] = (lin * _sigmoid(gt)).astype(glu_ref.dtype)

    for n0 in range(0, gate_ref.shape[1], tn):
        gate_ref[:, n0:n0 + tn] = _sigmoid(_dot(h, wgt_ref[:, n0:n0 + tn])
                                           + bgt_ref[:, n0:n0 + tn]).astype(gate_ref.dtype)


def _inproj(x2, emb_g, emb_b, w_in, b_glu, b_gate, a_log, dt_bias):
    t, d = x2.shape
    tm = min(ROW_TILE, t)
    c0 = 4 * WIDTH_A
    c1 = c0 + 4 * N_HEADS
    c2 = c1 + 2 * WIDTH_B
    wb = w_in.astype(BF16)
    pad = LANES - 4 * N_HEADS
    w_small = jnp.pad(wb[:, c0:c1], ((0, 0), (0, pad)))
    alog = jnp.pad(a_log.reshape(1, 2 * N_HEADS), ((0, 0), (2 * N_HEADS, pad)))
    dtb = jnp.pad(dt_bias.reshape(1, 2 * N_HEADS), ((0, 0), (2 * N_HEADS, pad)))
    row = lambda i: (i, 0)

    def const(shape):
        return pl.BlockSpec(shape, lambda i: (0, 0), pipeline_mode=pl.Buffered(1))

    vec = lambda a: a.reshape(1, -1)
    bf = lambda n: jax.ShapeDtypeStruct((t, n), BF16)
    return pl.pallas_call(
        functools.partial(_inproj_kernel, tn=512),
        out_shape=(bf(c0), jax.ShapeDtypeStruct((t, LANES), F32), jax.ShapeDtypeStruct((2 * N_HEADS, t), F32),
                   bf(WIDTH_B), bf(2 * d)),
        grid=(t // tm,),
        in_specs=[pl.BlockSpec((tm, d), row), const((1, d)), const((1, d)),
                  const((d, c0)), const((d, LANES)),
                  const((d, WIDTH_B)), const((d, WIDTH_B)), const((1, WIDTH_B)), const((1, WIDTH_B)),
                  const((d, 2 * d)), const((1, 2 * d)), const((1, LANES)), const((1, LANES))],
        out_specs=(pl.BlockSpec((tm, c0), row), pl.BlockSpec((tm, LANES), row),
                   pl.BlockSpec((2 * N_HEADS, tm), lambda i: (0, i)),
                   pl.BlockSpec((tm, WIDTH_B), row), pl.BlockSpec((tm, 2 * d), row)),
        compiler_params=_params("parallel"),
        name="inproj",
    )(x2, vec(emb_g), vec(emb_b), wb[:, :c0], w_small, wb[:, c1:c1 + WIDTH_B], wb[:, c1 + WIDTH_B:c2],
      vec(b_glu[:WIDTH_B]), vec(b_glu[WIDTH_B:]), wb[:, c2:], vec(b_gate), alog, dtb)


def _conv_rows(xp_ref, w, taps, base, r0, rows):
    acc = xp_ref[base + r0:base + r0 + rows, :] * w[0:1, :]
    for k in range(1, taps):
        acc = acc + xp_ref[base + k + r0:base + k + r0 + rows, :] * w[k:k + 1, :]
    return acc


def _fill_padded(xp_ref, x_ref, pad, seq):
    zeros = jnp.zeros((pad, xp_ref.shape[1]), F32)
    xp_ref[0:pad, :] = zeros
    xp_ref[pad + seq:pad + seq + pad, :] = zeros
    xp_ref[pad:pad + seq, :] = x_ref[...].astype(F32)


def _qkv_conv_kernel(u_ref, w_ref, o_ref, xp_ref, *, seq, rows):
    pad = 8
    j = pl.program_id(1)
    _fill_padded(xp_ref, u_ref, pad, seq)
    w = w_ref[...]
    is_qk = j < 2 * N_HEADS
    head_scale = jnp.where(j < N_HEADS, HEAD_DIM ** -0.5, 1.0)
    for r0 in range(0, seq, rows):
        y = _silu(_conv_rows(xp_ref, w, SHORT_CONV, pad - SHORT_CONV // 2, r0, rows))
        inv = lax.rsqrt(jnp.sum(y * y, axis=-1, keepdims=True) + L2_EPS)
        scale = jnp.where(is_qk, inv * head_scale, 1.0)
        o_ref[r0:r0 + rows, :] = (y * scale).astype(o_ref.dtype)


def _qkv_conv(u_qkvz, conv_w, bsz, seq):
    t = bsz * seq
    ncol = 3 * N_HEADS
    rows = min(256, seq)
    return pl.pallas_call(
        functools.partial(_qkv_conv_kernel, seq=seq, rows=rows),
        out_shape=jax.ShapeDtypeStruct((t, 3 * WIDTH_A), BF16),
        grid=(bsz, ncol),
        in_specs=[pl.BlockSpec((seq, HEAD_DIM), lambda b, j: (b, j)),
                  pl.BlockSpec((SHORT_CONV, HEAD_DIM), lambda b, j: (0, j))],
        out_specs=pl.BlockSpec((seq, HEAD_DIM), lambda b, j: (b, j)),
        scratch_shapes=[pltpu.VMEM((seq + 16, HEAD_DIM), F32)],
        compiler_params=_params("parallel", "parallel"),
        name="qkv_conv",
    )(u_qkvz, conv_w)


def _dw_conv_kernel(x_ref, w_ref, b_ref, o_ref, xp_ref, *, seq, rows):
    pad = 16
    _fill_padded(xp_ref, x_ref, pad, seq)
    w = w_ref[...]
    for r0 in range(0, seq, rows):
        y = _conv_rows(xp_ref, w, DW_CONV, pad - DW_CONV // 2, r0, rows) + b_ref[...]
        o_ref[r0:r0 + rows, :] = y.astype(o_ref.dtype)


def _dw_conv(glu, conv_w, b_dw, bsz, seq):
    t = bsz * seq
    rows = min(256, seq)
    return pl.pallas_call(
        functools.partial(_dw_conv_kernel, seq=seq, rows=rows),
        out_shape=jax.ShapeDtypeStruct((t, WIDTH_B), BF16),
        grid=(bsz, WIDTH_B // LANES),
        in_specs=[pl.BlockSpec((seq, LANES), lambda b, j: (b, j)),
                  pl.BlockSpec((DW_CONV, LANES), lambda b, j: (0, j)),
                  pl.BlockSpec((1, LANES), lambda b, j: (0, j))],
        out_specs=pl.BlockSpec((seq, LANES), lambda b, j: (b, j)),
        scratch_shapes=[pltpu.VMEM((seq + 32, LANES), F32)],
        compiler_params=_params("parallel", "parallel"),
        name="dw_conv",
    )(glu, conv_w, b_dw.reshape(1, WIDTH_B))


def _bmm(a, b):
    return lax.dot_general(a, b, (((2,), (1,)), ((0,), (0,))), preferred_element_type=F32)


def _bmm_nt(a, b):
    return lax.dot_general(a, b, (((2,), (2,)), ((0,), (0,))), preferred_element_type=F32)


def _bmm_tn(a, b):
    return lax.dot_general(a, b, (((1,), (1,)), ((0,), (0,))), preferred_element_type=F32)


def _block_diag_rows(x, half):
    lane = lax.broadcasted_iota(jnp.int32, x.shape, 2)
    return jnp.concatenate([jnp.where(lane < half, x, 0.0), jnp.where(lane >= half, x, 0.0)], axis=1)


def _unit_tri_inverse(lmat, eye):
    def rhs(p):
        return _block_diag_rows(p, CHUNK).astype(BF16)

    x = eye - lmat
    p = _bmm(lmat.astype(BF16), rhs(lmat))
    s = 2
    while 2 * s < CHUNK:
        xp = _bmm(jnp.concatenate([x, p], axis=1).astype(BF16), rhs(p))
        x = x + xp[:, :CHUNK]
        p = xp[:, CHUNK:]
        s *= 2
    return x + _bmm(x.astype(BF16), rhs(p))


def _delta_kernel(qf_ref, kf_ref, vf_ref, gf_ref, gtpf_ref, qb_ref, kb_ref, vb_ref, gb_ref, gtpb_ref,
                  of_ref, ob_ref, s_ref, *, nc):
    @pl.when(pl.program_id(1) == 0)
    def _():
        s_ref[...] = jnp.zeros_like(s_ref)

    n_pairs = N_HEADS // 2
    n_inst = 2 * n_pairs
    pw = 2 * HEAD_DIM
    dirs = ((qf_ref, kf_ref, vf_ref, gf_ref, gtpf_ref, of_ref, False),
            (qb_ref, kb_ref, vb_ref, gb_ref, gtpb_ref, ob_ref, True))
    steps = [[d + ((nc - 1 - i) if d[6] else i,) for d in dirs] for i in range(nc)]

    ri = lax.broadcasted_iota(jnp.int32, (CHUNK, 2 * CHUNK), 0)
    ci = jnp.bitwise_and(lax.broadcasted_iota(jnp.int32, (CHUNK, 2 * CHUNK), 1), CHUNK - 1)
    inst = lax.broadcasted_iota(jnp.int32, (nc * n_inst, 1, 1), 0)
    sign = 1 - 2 * jnp.bitwise_and(jnp.right_shift(inst, n_pairs.bit_length() - 1), 1)
    rel = (ri - ci)[None] * sign
    incl = rel >= 0
    strict = rel > 0
    eye = (ri == ci).astype(F32)

    def pairs(which):
        return jnp.stack([d[which][d[7] * CHUNK:(d[7] + 1) * CHUNK, p * pw:(p + 1) * pw]
                          for st in steps for d in st for p in range(n_pairs)]).astype(F32)

    def pair_bcast(cols, width):
        return jnp.stack([jnp.concatenate([jnp.broadcast_to(cc[2 * p], (CHUNK, width)),
                                           jnp.broadcast_to(cc[2 * p + 1], (CHUNK, width))], axis=1)
                          for cc in cols for p in range(n_pairs)])

    qf = pairs(0)
    kf = pairs(1)
    vf = pairs(2)
    beta_c, g_c, glast_c = [], [], []
    for st in steps:
        for d in st:
            gblk = d[3][d[7] * CHUNK:(d[7] + 1) * CHUNK, :]
            off = N_HEADS if d[6] else 0
            last = 0 if d[6] else CHUNK - 1
            beta_c.append([gblk[:, off + hh:off + hh + 1] for hh in range(N_HEADS)])
            g_c.append([gblk[:, 2 * N_HEADS + off + hh:2 * N_HEADS + off + hh + 1] for hh in range(N_HEADS)])
            glast_c.append([gc[last:last + 1, :] for gc in g_c[-1]])
    beta = pair_bcast(beta_c, HEAD_DIM)
    eg = pair_bcast([[jnp.exp(gc) for gc in gcs] for gcs in g_c], HEAD_DIM)
    tail = pair_bcast([[jnp.exp(gl - gc) for gl, gc in zip(gls, gcs)] for gls, gcs in zip(glast_c, g_c)],
                      HEAD_DIM)
    gcol = pair_bcast(g_c, CHUNK)
    grow = jnp.stack([d[4][(n_pairs if d[6] else 0) + p:(n_pairs if d[6] else 0) + p + 1,
                           2 * d[7] * CHUNK:2 * (d[7] + 1) * CHUNK]
                      for st in steps for d in st for p in range(n_pairs)])

    decay = jnp.exp(jnp.where(incl, gcol - grow, NEG_BIG))
    kb = kf * beta
    kkqk = _bmm_nt(jnp.concatenate([kb, qf], axis=1).astype(BF16),
                   _block_diag_rows(kf, HEAD_DIM).astype(BF16))
    lmat = jnp.where(strict, kkqk[:, :CHUNK, :] * decay, 0.0)
    qk = (kkqk[:, CHUNK:, :] * decay).astype(BF16)
    tinv = _unit_tri_inverse(lmat, eye)
    rhs = jnp.concatenate([_block_diag_rows(vf * beta, HEAD_DIM),
                           _block_diag_rows(kb * eg, HEAD_DIM)], axis=2).astype(BF16)
    uw = _bmm(tinv.astype(BF16), rhs)
    u = uw[:, :, :pw]
    wq = jnp.concatenate([uw[:, :, pw:], qf * eg], axis=1).astype(BF16)
    kt = (kf * tail).astype(BF16)

    for i, st in enumerate(steps):
        sl = slice(i * n_inst, (i + 1) * n_inst)
        s_a = s_ref[:, 0]
        s_b = s_ref[:, 1]
        zero = jnp.zeros_like(s_a)
        s_bd = jnp.concatenate([jnp.concatenate([s_a, zero], axis=2),
                                jnp.concatenate([zero, s_b], axis=2)], axis=1).astype(BF16)
        ws = _bmm(wq[sl], s_bd)
        v_new = u[sl] - ws[:, :CHUNK, :]
        o = ws[:, CHUNK:, :] + _bmm(qk[sl], _block_diag_rows(v_new, HEAD_DIM).astype(BF16))
        v16 = v_new.astype(BF16)
        gl = [glast_c[2 * i + dd] for dd in range(2)]
        cd_a = jnp.stack([jnp.exp(gl[dd][2 * p]) for dd in range(2) for p in range(n_pairs)])
        cd_b = jnp.stack([jnp.exp(gl[dd][2 * p + 1]) for dd in range(2) for p in range(n_pairs)])
        s_ref[:, 0] = s_a * cd_a + _bmm_tn(kt[sl, :, :HEAD_DIM], v16[:, :, :HEAD_DIM])
        s_ref[:, 1] = s_b * cd_b + _bmm_tn(kt[sl, :, HEAD_DIM:], v16[:, :, HEAD_DIM:])
        for dd, d in enumerate(st):
            for p in range(n_pairs):
                d[5][d[7] * CHUNK:(d[7] + 1) * CHUNK, p * pw:(p + 1) * pw] = (
                    o[dd * n_pairs + p].astype(d[5].dtype))


def _delta_rule(qkv, g, gtp, bsz, seq):
    t = bsz * seq
    rows = min(DELTA_ROWS, seq)
    nblk = seq // rows

    def fwd(col):
        return lambda b, i: (b * nblk + i, col)

    def bwd(col):
        return lambda b, i: (b * nblk + nblk - 1 - i, col)

    def specs(m):
        return [pl.BlockSpec((rows, WIDTH_A), m(0)), pl.BlockSpec((rows, WIDTH_A), m(1)),
                pl.BlockSpec((rows, WIDTH_A), m(2)), pl.BlockSpec((rows, LANES), m(0)),
                pl.BlockSpec((N_HEADS, 2 * rows), lambda b, i, m=m: (0, m(0)(b, i)[0]))]

    out = jax.ShapeDtypeStruct((t, WIDTH_A), BF16)
    return pl.pallas_call(
        functools.partial(_delta_kernel, nc=rows // CHUNK),
        out_shape=(out, out),
        grid=(bsz, nblk),
        in_specs=specs(fwd) + specs(bwd),
        out_specs=(pl.BlockSpec((rows, WIDTH_A), fwd(0)), pl.BlockSpec((rows, WIDTH_A), bwd(0))),
        scratch_shapes=[pltpu.VMEM((N_HEADS, 2, HEAD_DIM, HEAD_DIM), F32)],
        compiler_params=_params("parallel", "arbitrary"),
        name="delta_rule",
    )(qkv, qkv, qkv, g, gtp, qkv, qkv, qkv, g, gtp)


def _mix_kernel(of_ref, ob_ref, z_ref, yc_ref, gate_ref, x_ref,
                eg_ref, eb_ref, ng_ref, wao_ref, cg_ref, cb_ref, wbo_ref, bbo_ref,
                wout_ref, l1g_ref, l1b_ref, wr_ref, br_ref,
                h1_ref, h1p_ref, logit_ref):
    o = of_ref[...].astype(F32) + ob_ref[...].astype(F32)
    z = z_ref[...].astype(F32)
    ng = ng_ref[...]
    parts = []
    for hh in range(N_HEADS):
        sl = slice(hh * HEAD_DIM, (hh + 1) * HEAD_DIM)
        oh = o[:, sl]
        zh = z[:, sl]
        inv = lax.rsqrt(jnp.mean(oh * oh, axis=-1, keepdims=True) + RMS_EPS)
        parts.append((oh * inv * ng * _silu(zh)).astype(BF16))
    y_a = _dot(jnp.concatenate(parts, axis=1), wao_ref[...])

    yc = _layer_norm(yc_ref[...].astype(F32), cg_ref[...], cb_ref[...])
    y_b = _dot(_silu(yc).astype(BF16), wbo_ref[...]) + bbo_ref[...]

    gates = gate_ref[...].astype(F32)
    mixed = gates[:, :D_MODEL] * y_a + gates[:, D_MODEL:] * y_b
    mix = _dot(mixed.astype(BF16), wout_ref[...])

    h0 = _layer_norm(x_ref[...], eg_ref[...], eb_ref[...])
    h1 = _layer_norm(DN_ALPHA * h0 + mix, l1g_ref[...], l1b_ref[...])
    h1_ref[...] = h1
    h1p_ref[...] = _pack_bf16_pair(h1[:, :D_MODEL // 2], h1[:, D_MODEL // 2:])
    h_hi, h_lo = _split_bf16(h1)
    p = _dot(h_hi, wr_ref[...])
    logit_ref[...] = p[:, :LANES] + p[:, LANES:] + _dot(h_lo, wr_ref[:, :LANES]) + br_ref[...]


def _mix(o_f, o_b, u_qkvz, yc, gates, x2, emb_g, emb_b, norm_g, w_a_o, cg, cb, w_b_o, b_b_o,
         w_out, l1g, l1b, w_router, b_router):
    t, d = x2.shape
    tm = min(ROW_TILE, t)
    row = lambda i: (i, 0)
    const = lambda i: (0, 0)
    wr = jnp.concatenate(_split_bf16(jnp.pad(w_router, ((0, 0), (0, LANES - N_EXPERTS)))), axis=1)
    br = jnp.pad(b_router.reshape(1, N_EXPERTS), ((0, 0), (0, LANES - N_EXPERTS)), constant_values=NEG_BIG)
    vec = lambda a: a.reshape(1, -1)
    return pl.pallas_call(
        _mix_kernel,
        out_shape=(jax.ShapeDtypeStruct((t, d), F32), jax.ShapeDtypeStruct((t, d // 2), jnp.uint32),
                   jax.ShapeDtypeStruct((t, LANES), F32)),
        grid=(t // tm,),
        in_specs=[pl.BlockSpec((tm, d), row), pl.BlockSpec((tm, d), row),
                  pl.BlockSpec((tm, d), lambda i: (i, 3)),
                  pl.BlockSpec((tm, d), row), pl.BlockSpec((tm, 2 * d), row), pl.BlockSpec((tm, d), row),
                  pl.BlockSpec((1, d), const), pl.BlockSpec((1, d), const),
                  pl.BlockSpec((1, HEAD_DIM), const), pl.BlockSpec((d, d), const),
                  pl.BlockSpec((1, d), const), pl.BlockSpec((1, d), const),
                  pl.BlockSpec((d, d), const), pl.BlockSpec((1, d), const),
                  pl.BlockSpec((d, d), const), pl.BlockSpec((1, d), const), pl.BlockSpec((1, d), const),
                  pl.BlockSpec((d, 2 * LANES), const), pl.BlockSpec((1, LANES), const)],
        out_specs=(pl.BlockSpec((tm, d), row), pl.BlockSpec((tm, d // 2), row), pl.BlockSpec((tm, LANES), row)),
        compiler_params=_params("parallel"),
        name="mix",
    )(o_f, o_b, u_qkvz, yc, gates, x2, vec(emb_g), vec(emb_b), vec(norm_g), w_a_o, vec(cg), vec(cb),
      w_b_o, vec(b_b_o), w_out, vec(l1g), vec(l1b), wr, br)


def _route_kernel(logit_ref, gate_ref, eidx_ref, rank_ref, cnt_ref, base_ref):
    @pl.when(pl.program_id(0) == 0)
    def _():
        base_ref[...] = jnp.zeros_like(base_ref)

    x = logit_ref[...]
    tm = x.shape[0]
    lane = lax.broadcasted_iota(jnp.int32, x.shape, 1)
    lane_f = lane.astype(F32)
    row = lax.broadcasted_iota(jnp.int32, x.shape, 0)
    sel = jnp.zeros(x.shape, F32)
    vals, idxs = [], []
    for _ in range(TOP_K):
        m = jnp.max(x, axis=1, keepdims=True)
        idx = jnp.min(jnp.where(x == m, lane_f, float(LANES)), axis=1, keepdims=True).astype(jnp.int32)
        hit = lane == idx
        sel = sel + hit.astype(F32)
        x = jnp.where(hit, -3e38, x)
        vals.append(m)
        idxs.append(idx)

    exps = [jnp.exp(v - vals[0]) for v in vals]
    denom = exps[0]
    for e in exps[1:]:
        denom = denom + e

    csum = sel
    s = 1
    while s < tm:
        csum = csum + jnp.where(row >= s, pltpu.roll(csum, s, axis=0), 0.0)
        s *= 2
    before = base_ref[...] + csum - sel

    gate = jnp.zeros(x.shape, F32)
    eidx = jnp.zeros(x.shape, F32)
    rank = jnp.zeros(x.shape, F32)
    for k in range(TOP_K):
        rk = jnp.sum(jnp.where(lane == idxs[k], before, 0.0), axis=1, keepdims=True)
        gate = jnp.where(lane == k, exps[k] / denom, gate)
        eidx = jnp.where(lane == k, idxs[k].astype(F32), eidx)
        rank = jnp.where(lane == k, rk, rank)
    gate_ref[...] = gate
    eidx_ref[...] = eidx.T[:8, :].astype(jnp.int32)
    rank_ref[...] = rank.T[:8, :].astype(jnp.int32)
    total = base_ref[...] + csum[tm - 1:tm, :]
    base_ref[...] = total
    cnt_ref[...] = total


def _route(logits):
    t = logits.shape[0]
    tm = min(ROW_TILE, t)
    bm = EXPERT_ROWS
    row = lambda i: (i, 0)
    gate, eidx, rank, cnt = pl.pallas_call(
        _route_kernel,
        out_shape=(jax.ShapeDtypeStruct((t, LANES), F32), jax.ShapeDtypeStruct((8, t), jnp.int32),
                   jax.ShapeDtypeStruct((8, t), jnp.int32), jax.ShapeDtypeStruct((1, LANES), F32)),
        grid=(t // tm,),
        in_specs=[pl.BlockSpec((tm, LANES), row)],
        out_specs=(pl.BlockSpec((tm, LANES), row), pl.BlockSpec((8, tm), lambda i: (0, i)),
                   pl.BlockSpec((8, tm), lambda i: (0, i)), pl.BlockSpec((1, LANES), lambda i: (0, 0))),
        scratch_shapes=[pltpu.VMEM((1, LANES), F32)],
        compiler_params=_params("arbitrary"),
        name="moe_route",
    )(logits)
    counts = cnt[0, :N_EXPERTS].astype(jnp.int32)
    padded = (counts + bm - 1) // bm * bm
    pad_end = jnp.cumsum(padded)
    pad_start = pad_end - padded
    onehot = eidx[:TOP_K, :, None] == jnp.arange(N_EXPERTS, dtype=jnp.int32)
    pos_kmajor = (jnp.sum(jnp.where(onehot, pad_start, 0), axis=-1) + rank[:TOP_K]).reshape(-1)
    n_blocks = -(-(t * TOP_K + N_EXPERTS * (bm - 1)) // bm)
    block_start = jnp.arange(n_blocks, dtype=jnp.int32) * bm
    block_exp = jnp.minimum(jnp.sum((block_start[:, None] >= pad_end[None, :]).astype(jnp.int32), axis=1),
                            N_EXPERTS - 1)
    block_rows = jnp.clip((pad_start + counts)[block_exp] - block_start, 0, bm).astype(jnp.int32)
    return gate, pos_kmajor.astype(jnp.int32), block_exp, block_rows, n_blocks


def _sc_worker_range(n_rows):
    per_worker = n_rows // (SC_CORES * SC_SUBCORES)
    wid = lax.axis_index("s") * SC_CORES + lax.axis_index("c")
    return wid * per_worker, per_worker


def _sc_scatter_rows(src, idx, n_out):
    t, d = src.shape
    n_copies = idx.shape[0] // t
    mesh = plsc.VectorSubcoreMesh(core_axis_name="c", subcore_axis_name="s")

    @functools.partial(
        pl.kernel, mesh=mesh,
        out_type=jax.ShapeDtypeStruct((n_out, d), src.dtype),
        scratch_types=([pltpu.VMEM((SC_GATHER_ROWS,), jnp.int32)] * n_copies
                       + [pltpu.VMEM((SC_GATHER_ROWS, d), src.dtype)]
                       + [pltpu.SemaphoreType.DMA] * n_copies),
        name="sc_scatter_rows",
    )
    def scatter(src_hbm, idx_hbm, out_hbm, *scratch):
        idx_v = scratch[:n_copies]
        rows_v = scratch[n_copies]
        sems = scratch[n_copies + 1:]
        base, per_worker = _sc_worker_range(t)

        @pl.loop(0, per_worker // SC_GATHER_ROWS)
        def _(j):
            off = pl.multiple_of(base + j * SC_GATHER_ROWS, SC_GATHER_ROWS)
            for k in range(n_copies):
                pltpu.sync_copy(idx_hbm.at[pl.ds(k * t + off, SC_GATHER_ROWS)], idx_v[k])
            pltpu.sync_copy(src_hbm.at[pl.ds(off, SC_GATHER_ROWS)], rows_v)
            copies = [pltpu.async_copy(rows_v, out_hbm.at[idx_v[k]], sems[k]) for k in range(n_copies)]
            for c in copies:
                c.wait()

    return scatter(src, idx)


def _expert_kernel(be_ref, nr_ref, x_ref, wgu_ref, bgu_ref, wd_ref, bd_ref, o_ref, *, tn):
    i = pl.program_id(0)
    active = nr_ref[i] > 0

    @pl.when(active)
    def _():
        half = D_MODEL // 2
        row = lax.broadcasted_iota(jnp.int32, x_ref.shape, 0)
        x = jnp.where(row < nr_ref[i], x_ref[...], jnp.uint32(0))
        x_lo, x_hi = _unpack_bf16_pair(x)
        x_lo = x_lo.astype(BF16)
        x_hi = x_hi.astype(BF16)

        def up(c0):
            return (_dot(x_lo, wgu_ref[0, :half, c0:c0 + tn]) + _dot(x_hi, wgu_ref[0, half:, c0:c0 + tn])
                    + bgu_ref[0, :, c0:c0 + tn])

        y = bd_ref[0]
        for n0 in range(0, D_FF, tn):
            glu = jnp.minimum(up(n0), SWIGLU_LIMIT)
            lin = jnp.clip(up(D_FF + n0), -SWIGLU_LIMIT, SWIGLU_LIMIT)
            act = glu * _sigmoid(SWIGLU_ALPHA * glu) * (lin + 1.0)
            y = y + _dot(act.astype(BF16), wd_ref[0, n0:n0 + tn, :])
        o_ref[...] = _pack_bf16_pair(y[:, :half], y[:, half:])

    @pl.when(jnp.logical_not(active))
    def _():
        o_ref[...] = jnp.zeros_like(o_ref)


def _experts(xs, block_exp, block_rows, w_gu, b_gu, w_down, b_down, n_blocks):
    d = D_MODEL
    dp = xs.shape[1]
    bm = EXPERT_ROWS
    grid_spec = pltpu.PrefetchScalarGridSpec(
        num_scalar_prefetch=2,
        grid=(n_blocks,),
        in_specs=[pl.BlockSpec((bm, dp), lambda i, be, nb: (i, 0)),
                  pl.BlockSpec((1, d, 2 * D_FF), lambda i, be, nb: (be[i], 0, 0)),
                  pl.BlockSpec((1, 1, 2 * D_FF), lambda i, be, nb: (be[i], 0, 0)),
                  pl.BlockSpec((1, D_FF, d), lambda i, be, nb: (be[i], 0, 0)),
                  pl.BlockSpec((1, 1, d), lambda i, be, nb: (be[i], 0, 0))],
        out_specs=pl.BlockSpec((bm, dp), lambda i, be, nb: (i, 0)),
    )
    return pl.pallas_call(
        functools.partial(_expert_kernel, tn=EXPERT_SLAB),
        out_shape=jax.ShapeDtypeStruct((n_blocks * bm, dp), jnp.uint32),
        grid_spec=grid_spec,
        compiler_params=_params("arbitrary"),
        name="moe_experts",
    )(block_exp, block_rows, xs, w_gu, b_gu.reshape(N_EXPERTS, 1, 2 * D_FF), w_down,
      b_down.reshape(N_EXPERTS, 1, d))


def _sc_gather_rows(table, idx):
    m = idx.shape[0]
    d = table.shape[1]
    mesh = plsc.VectorSubcoreMesh(core_axis_name="c", subcore_axis_name="s")

    @functools.partial(
        pl.kernel, mesh=mesh,
        out_type=jax.ShapeDtypeStruct((m, d), table.dtype),
        scratch_types=([pltpu.VMEM((SC_GATHER_ROWS,), jnp.int32)] * 2
                       + [pltpu.VMEM((SC_GATHER_ROWS, d), table.dtype)] * 2
                       + [pltpu.SemaphoreType.DMA] * 4),
        name="sc_gather_rows",
    )
    def gather(table_hbm, idx_hbm, out_hbm, idx0, idx1, rows0, rows1, g0, g1, w0, w1):
        base, per_worker = _sc_worker_range(m)

        @pl.loop(0, per_worker // (2 * SC_GATHER_ROWS))
        def _(j):
            off0 = pl.multiple_of(base + 2 * j * SC_GATHER_ROWS, SC_GATHER_ROWS)
            off1 = off0 + SC_GATHER_ROWS
            pltpu.sync_copy(idx_hbm.at[pl.ds(off0, SC_GATHER_ROWS)], idx0)
            gather0 = pltpu.async_copy(table_hbm.at[idx0], rows0, g0)
            pltpu.sync_copy(idx_hbm.at[pl.ds(off1, SC_GATHER_ROWS)], idx1)
            gather1 = pltpu.async_copy(table_hbm.at[idx1], rows1, g1)
            gather0.wait()
            write0 = pltpu.async_copy(rows0, out_hbm.at[pl.ds(off0, SC_GATHER_ROWS)], w0)
            gather1.wait()
            write1 = pltpu.async_copy(rows1, out_hbm.at[pl.ds(off1, SC_GATHER_ROWS)], w1)
            write0.wait()
            write1.wait()

    return gather(table, idx)


def _combine_dense_kernel(y0_ref, y1_ref, y2_ref, y3_ref, gate_ref, h_ref, g_ref, b_ref, o_ref):
    gate = gate_ref[...]
    f_lo = f_hi = None
    for k, y_ref in enumerate((y0_ref, y1_ref, y2_ref, y3_ref)):
        y_lo, y_hi = _unpack_bf16_pair(y_ref[...])
        gk = gate[:, k:k + 1]
        f_lo = gk * y_lo if f_lo is None else f_lo + gk * y_lo
        f_hi = gk * y_hi if f_hi is None else f_hi + gk * y_hi
    f = jnp.concatenate([f_lo, f_hi], axis=1)
    o_ref[...] = _layer_norm(DN_ALPHA * h_ref[...] + f, g_ref[...], b_ref[...])


def _combine_dense(yg, gate, h1, ln_g, ln_b):
    t, d = h1.shape
    tm = min(ROW_TILE, t)
    nt = t // tm
    dp = yg.shape[1]
    slab = lambda k: pl.BlockSpec((tm, dp), lambda i, k=k: (k * nt + i, 0))
    return pl.pallas_call(
        _combine_dense_kernel,
        out_shape=jax.ShapeDtypeStruct((t, d), F32),
        grid=(nt,),
        in_specs=[slab(0), slab(1), slab(2), slab(3),
                  pl.BlockSpec((tm, LANES), lambda i: (i, 0)),
                  pl.BlockSpec((tm, d), lambda i: (i, 0)),
                  pl.BlockSpec((1, d), lambda i: (0, 0)),
                  pl.BlockSpec((1, d), lambda i: (0, 0))],
        out_specs=pl.BlockSpec((tm, d), lambda i: (i, 0)),
        compiler_params=_params("parallel"),
        name="moe_combine",
    )(yg, yg, yg, yg, gate, h1, ln_g.reshape(1, d), ln_b.reshape(1, d))


def kernel(x, emb_ln_g, emb_ln_b, w_in, conv_qkv, a_log, dt_bias, dn_norm_g, w_a_o, b_glu, conv_dw, b_dw, conv_ln_g, conv_ln_b, w_b_o, b_b_o, b_gate, w_out, ln1_g, ln1_b, w_router, b_router, w_gu, b_gu, w_down, b_down, ln2_g, ln2_b):
    bsz, seq, d = x.shape
    t = bsz * seq
    x2 = x.reshape(t, d)
    u_qkvz, g, gt, glu, gates = _inproj(x2, emb_ln_g, emb_ln_b, w_in[0], b_glu[0], b_gate[0], a_log[0], dt_bias[0])

    qkv = _qkv_conv(u_qkvz, conv_qkv[0], bsz, seq)
    gtp = gt.reshape(2, N_HEADS // 2, 2, t // CHUNK, CHUNK).transpose(0, 1, 3, 2, 4).reshape(N_HEADS, 2 * t)
    o_f, o_b = _delta_rule(qkv, g, gtp, bsz, seq)
    yc = _dw_conv(glu, conv_dw[0], b_dw[0], bsz, seq)

    h1, h1p, logits = _mix(o_f, o_b, u_qkvz, yc, gates, x2, emb_ln_g, emb_ln_b, dn_norm_g[0],
                           w_a_o[0].astype(BF16), conv_ln_g[0], conv_ln_b[0], w_b_o[0].astype(BF16), b_b_o[0],
                           w_out[0].astype(BF16), ln1_g[0], ln1_b[0], w_router[0], b_router[0])

    gate, pos_kmajor, block_exp, block_rows, n_blocks = _route(logits)
    xs = _sc_scatter_rows(h1p, pos_kmajor, n_blocks * EXPERT_ROWS)
    ys = _experts(xs, block_exp, block_rows, w_gu[0].astype(BF16), b_gu[0], w_down[0].astype(BF16), b_down[0],
                  n_blocks)
    yg = _sc_gather_rows(ys, pos_kmajor)
    out = _combine_dense(yg, gate, h1, ln2_g[0], ln2_b[0])
    return out.reshape(bsz, seq, d)
```

```python
import functools

import jax
import jax.numpy as jnp
from jax import lax
from jax.experimental import pallas as pl
from jax.experimental.pallas import tpu as pltpu
from jax.experimental.pallas import tpu_sc as plsc

F32 = jnp.float32
BF16 = jnp.bfloat16

D_MODEL = 1024
N_HEADS = 8
HEAD_DIM = 128
WIDTH_A = N_HEADS * HEAD_DIM
SHORT_CONV = 5
CHUNK = 64
WIDTH_B = D_MODEL
DW_CONV = 31
N_EXPERTS = 32
TOP_K = 4
D_FF = D_MODEL
SWIGLU_ALPHA = 1.702
SWIGLU_LIMIT = 7.0
DN_ALPHA = 2.0 ** 0.25
LN_EPS = 1e-5
RMS_EPS = 1e-6
L2_EPS = 1e-6
LANES = 128
NEG_BIG = -1e30

ROW_TILE = 512
CONV_COLS = 512
DELTA_ROWS = 512
EXPERT_ROWS = 512
EXPERT_SLAB = 512
SC_CORES = 2
SC_SUBCORES = 16
SC_GATHER_ROWS = 64
VMEM_LIMIT = 56 * 1024 * 1024


def _params(*sem):
    return pltpu.CompilerParams(dimension_semantics=sem, vmem_limit_bytes=VMEM_LIMIT)


def _layer_norm(x, g, b):
    mu = jnp.mean(x, axis=-1, keepdims=True)
    xc = x - mu
    var = jnp.mean(xc * xc, axis=-1, keepdims=True)
    return xc * lax.rsqrt(var + LN_EPS) * g + b


def _sigmoid(x):
    return 0.5 * jnp.tanh(0.5 * x) + 0.5


def _silu(x):
    h = 0.5 * x
    return h + h * jnp.tanh(h)


def _dot(a, b):
    return jnp.dot(a, b, preferred_element_type=F32)


def _pack_bf16_pair(a, b):
    ua = lax.bitcast_convert_type(a.astype(BF16).astype(F32), jnp.uint32)
    ub = lax.bitcast_convert_type(b.astype(BF16).astype(F32), jnp.uint32)
    return (ua >> 16) | ub


def _unpack_bf16_pair(p):
    a = lax.bitcast_convert_type(p << 16, F32)
    b = lax.bitcast_convert_type(p & jnp.uint32(0xFFFF0000), F32)
    return a, b


def _split_bf16(a):
    hi = a.astype(BF16)
    return hi, (a - hi.astype(F32)).astype(BF16)


def _chunk_cumsum(x, reverse):
    rows = x.shape[0]
    pos = lax.broadcasted_iota(jnp.int32, x.shape, 0) % CHUNK
    s = 1
    while s < CHUNK:
        if reverse:
            shifted = pltpu.roll(x, rows - s, axis=0)
            x = x + jnp.where(pos < CHUNK - s, shifted, 0.0)
        else:
            shifted = pltpu.roll(x, s, axis=0)
            x = x + jnp.where(pos >= s, shifted, 0.0)
        s *= 2
    return x


def _inproj_kernel(x_ref, eg_ref, eb_ref, wq_ref, ws_ref, wga_ref, wgb_ref, bga_ref, bgb_ref, wgt_ref, bgt_ref,
                   alog_ref, dtb_ref, u_ref, g_ref, gt_ref, glu_ref, gate_ref, *, tn):
    h = _layer_norm(x_ref[...], eg_ref[...], eb_ref[...]).astype(BF16)

    for n0 in range(0, u_ref.shape[1], tn):
        u_ref[:, n0:n0 + tn] = _dot(h, wq_ref[:, n0:n0 + tn]).astype(u_ref.dtype)

    us = _dot(h, ws_ref[...])
    lane = lax.broadcasted_iota(jnp.int32, us.shape, 1)
    beta = _sigmoid(us)
    xs = us + dtb_ref[...]
    softplus = jnp.maximum(xs, 0.0) + jnp.log(1.0 + jnp.exp(-jnp.abs(xs)))
    log_a = -jnp.exp(alog_ref[...]) * softplus
    g_fwd = _chunk_cumsum(log_a, reverse=False)
    g_bwd = _chunk_cumsum(log_a, reverse=True)
    gates = jnp.where(lane < 2 * N_HEADS, beta, jnp.where(lane < 3 * N_HEADS, g_fwd, g_bwd))
    g_ref[...] = gates
    gt_ref[...] = gates.T[2 * N_HEADS:4 * N_HEADS, :]

    for n0 in range(0, glu_ref.shape[1], tn):
        lin = _dot(h, wga_ref[:, n0:n0 + tn]) + bga_ref[:, n0:n0 + tn]
        gt = _dot(h, wgb_ref[:, n0:n0 + tn]) + bgb_ref[:, n0:n0 + tn]
        glu_ref[:, n0:n0 + tn] = (lin * _sigmoid(gt)).astype(glu_ref.dtype)

    for n0 in range(0, gate_ref.shape[1], tn):
        gate_ref[:, n0:n0 + tn] = _sigmoid(_dot(h, wgt_ref[:, n0:n0 + tn])
                                           + bgt_ref[:, n0:n0 + tn]).astype(gate_ref.dtype)


def _inproj(x2, emb_g, emb_b, w_in, b_glu, b_gate, a_log, dt_bias):
    t, d = x2.shape
    tm = min(ROW_TILE, t)
    c0 = 4 * WIDTH_A
    c1 = c0 + 4 * N_HEADS
    c2 = c1 + 2 * WIDTH_B
    wb = w_in.astype(BF16)
    pad = LANES - 4 * N_HEADS
    w_small = jnp.pad(wb[:, c0:c1], ((0, 0), (0, pad)))
    alog = jnp.pad(a_log.reshape(1, 2 * N_HEADS), ((0, 0), (2 * N_HEADS, pad)))
    dtb = jnp.pad(dt_bias.reshape(1, 2 * N_HEADS), ((0, 0), (2 * N_HEADS, pad)))
    row = lambda i: (i, 0)

    def const(shape):
        return pl.BlockSpec(shape, lambda i: (0, 0), pipeline_mode=pl.Buffered(1))

    vec = lambda a: a.reshape(1, -1)
    bf = lambda n: jax.ShapeDtypeStruct((t, n), BF16)
    return pl.pallas_call(
        functools.partial(_inproj_kernel, tn=512),
        out_shape=(bf(c0), jax.ShapeDtypeStruct((t, LANES), F32), jax.ShapeDtypeStruct((2 * N_HEADS, t), F32),
                   bf(WIDTH_B), bf(2 * d)),
        grid=(t // tm,),
        in_specs=[pl.BlockSpec((tm, d), row), const((1, d)), const((1, d)),
                  const((d, c0)), const((d, LANES)),
                  const((d, WIDTH_B)), const((d, WIDTH_B)), const((1, WIDTH_B)), const((1, WIDTH_B)),
                  const((d, 2 * d)), const((1, 2 * d)), const((1, LANES)), const((1, LANES))],
        out_specs=(pl.BlockSpec((tm, c0), row), pl.BlockSpec((tm, LANES), row),
                   pl.BlockSpec((2 * N_HEADS, tm), lambda i: (0, i)),
                   pl.BlockSpec((tm, WIDTH_B), row), pl.BlockSpec((tm, 2 * d), row)),
        compiler_params=_params("parallel"),
        name="inproj",
    )(x2, vec(emb_g), vec(emb_b), wb[:, :c0], w_small, wb[:, c1:c1 + WIDTH_B], wb[:, c1 + WIDTH_B:c2],
      vec(b_glu[:WIDTH_B]), vec(b_glu[WIDTH_B:]), wb[:, c2:], vec(b_gate), alog, dtb)


def _conv_rows(xp_ref, plane, w, taps, base, r0, rows):
    acc = xp_ref[plane, base + r0:base + r0 + rows, :] * w[0:1, :]
    for k in range(1, taps):
        acc = acc + xp_ref[plane, base + k + r0:base + k + r0 + rows, :] * w[k:k + 1, :]
    return acc


def _fill_padded(xp_ref, plane, x, pad, seq):
    zeros = jnp.zeros((pad, LANES), F32)
    xp_ref[plane, 0:pad, :] = zeros
    xp_ref[plane, pad + seq:pad + seq + pad, :] = zeros
    xp_ref[plane, pad:pad + seq, :] = x.astype(F32)


def _qkv_conv_kernel(u_ref, w_ref, o_ref, xp_ref, *, seq, rows):
    pad = 8
    n_planes = u_ref.shape[1] // HEAD_DIM
    for p in range(n_planes):
        cols = slice(p * HEAD_DIM, (p + 1) * HEAD_DIM)
        head = pl.program_id(1) * n_planes + p
        _fill_padded(xp_ref, p, u_ref[:, cols], pad, seq)
        w = w_ref[:, cols]
        is_qk = head < 2 * N_HEADS
        head_scale = jnp.where(head < N_HEADS, HEAD_DIM ** -0.5, 1.0)
        for r0 in range(0, seq, rows):
            y = _silu(_conv_rows(xp_ref, p, w, SHORT_CONV, pad - SHORT_CONV // 2, r0, rows))
            inv = lax.rsqrt(jnp.sum(y * y, axis=-1, keepdims=True) + L2_EPS)
            scale = jnp.where(is_qk, inv * head_scale, 1.0)
            o_ref[r0:r0 + rows, cols] = (y * scale).astype(o_ref.dtype)


def _qkv_conv(u_qkvz, conv_w, bsz, seq):
    t = bsz * seq
    rows = min(256, seq)
    return pl.pallas_call(
        functools.partial(_qkv_conv_kernel, seq=seq, rows=rows),
        out_shape=jax.ShapeDtypeStruct((t, 3 * WIDTH_A), BF16),
        grid=(bsz, 3 * WIDTH_A // CONV_COLS),
        in_specs=[pl.BlockSpec((seq, CONV_COLS), lambda b, j: (b, j)),
                  pl.BlockSpec((SHORT_CONV, CONV_COLS), lambda b, j: (0, j))],
        out_specs=pl.BlockSpec((seq, CONV_COLS), lambda b, j: (b, j)),
        scratch_shapes=[pltpu.VMEM((CONV_COLS // LANES, seq + 16, LANES), F32)],
        compiler_params=_params("parallel", "parallel"),
        name="qkv_conv",
    )(u_qkvz, conv_w)


def _dw_conv_kernel(x_ref, w_ref, b_ref, o_ref, xp_ref, *, seq, rows):
    pad = 16
    for p in range(x_ref.shape[1] // LANES):
        cols = slice(p * LANES, (p + 1) * LANES)
        _fill_padded(xp_ref, p, x_ref[:, cols], pad, seq)
        w = w_ref[:, cols]
        for r0 in range(0, seq, rows):
            y = _conv_rows(xp_ref, p, w, DW_CONV, pad - DW_CONV // 2, r0, rows) + b_ref[:, cols]
            o_ref[r0:r0 + rows, cols] = y.astype(o_ref.dtype)


def _dw_conv(glu, conv_w, b_dw, bsz, seq):
    t = bsz * seq
    rows = min(256, seq)
    return pl.pallas_call(
        functools.partial(_dw_conv_kernel, seq=seq, rows=rows),
        out_shape=jax.ShapeDtypeStruct((t, WIDTH_B), BF16),
        grid=(bsz, WIDTH_B // CONV_COLS),
        in_specs=[pl.BlockSpec((seq, CONV_COLS), lambda b, j: (b, j)),
                  pl.BlockSpec((DW_CONV, CONV_COLS), lambda b, j: (0, j)),
                  pl.BlockSpec((1, CONV_COLS), lambda b, j: (0, j))],
        out_specs=pl.BlockSpec((seq, CONV_COLS), lambda b, j: (b, j)),
        scratch_shapes=[pltpu.VMEM((CONV_COLS // LANES, seq + 32, LANES), F32)],
        compiler_params=_params("parallel", "parallel"),
        name="dw_conv",
    )(glu, conv_w, b_dw.reshape(1, WIDTH_B))


def _bmm(a, b):
    return lax.dot_general(a, b, (((2,), (1,)), ((0,), (0,))), preferred_element_type=F32)


def _bmm_nt(a, b):
    return lax.dot_general(a, b, (((2,), (2,)), ((0,), (0,))), preferred_element_type=F32)


def _bmm_tn(a, b):
    return lax.dot_general(a, b, (((1,), (1,)), ((0,), (0,))), preferred_element_type=F32)


def _block_diag_rows(x, half):
    lane = lax.broadcasted_iota(jnp.int32, x.shape, 2)
    return jnp.concatenate([jnp.where(lane < half, x, 0.0), jnp.where(lane >= half, x, 0.0)], axis=1)


def _unit_tri_inverse(lmat, eye):
    def rhs(p):
        return _block_diag_rows(p, CHUNK).astype(BF16)

    x = eye - lmat
    p = _bmm(lmat.astype(BF16), rhs(lmat))
    s = 2
    while 2 * s < CHUNK:
        xp = _bmm(jnp.concatenate([x, p], axis=1).astype(BF16), rhs(p))
        x = x + xp[:, :CHUNK]
        p = xp[:, CHUNK:]
        s *= 2
    return x + _bmm(x.astype(BF16), rhs(p))


def _delta_kernel(qf_ref, kf_ref, vf_ref, gf_ref, gtpf_ref, qb_ref, kb_ref, vb_ref, gb_ref, gtpb_ref,
                  of_ref, ob_ref, s_ref, *, nc):
    @pl.when(pl.program_id(1) == 0)
    def _():
        s_ref[...] = jnp.zeros_like(s_ref)

    n_pairs = N_HEADS // 2
    n_inst = 2 * n_pairs
    pw = 2 * HEAD_DIM
    dirs = ((qf_ref, kf_ref, vf_ref, gf_ref, gtpf_ref, of_ref, False),
            (qb_ref, kb_ref, vb_ref, gb_ref, gtpb_ref, ob_ref, True))
    steps = [[d + ((nc - 1 - i) if d[6] else i,) for d in dirs] for i in range(nc)]

    ri = lax.broadcasted_iota(jnp.int32, (CHUNK, 2 * CHUNK), 0)
    ci = jnp.bitwise_and(lax.broadcasted_iota(jnp.int32, (CHUNK, 2 * CHUNK), 1), CHUNK - 1)
    inst = lax.broadcasted_iota(jnp.int32, (nc * n_inst, 1, 1), 0)
    sign = 1 - 2 * jnp.bitwise_and(jnp.right_shift(inst, n_pairs.bit_length() - 1), 1)
    rel = (ri - ci)[None] * sign
    incl = rel >= 0
    strict = rel > 0
    eye = (ri == ci).astype(F32)

    def pairs(which):
        return jnp.stack([d[which][d[7] * CHUNK:(d[7] + 1) * CHUNK, p * pw:(p + 1) * pw]
                          for st in steps for d in st for p in range(n_pairs)]).astype(F32)

    def pair_bcast(cols, width):
        return jnp.stack([jnp.concatenate([jnp.broadcast_to(cc[2 * p], (CHUNK, width)),
                                           jnp.broadcast_to(cc[2 * p + 1], (CHUNK, width))], axis=1)
                          for cc in cols for p in range(n_pairs)])

    qf = pairs(0)
    kf = pairs(1)
    vf = pairs(2)
    beta_c, g_c, glast_c = [], [], []
    for st in steps:
        for d in st:
            gblk = d[3][d[7] * CHUNK:(d[7] + 1) * CHUNK, :]
            off = N_HEADS if d[6] else 0
            last = 0 if d[6] else CHUNK - 1
            beta_c.append([gblk[:, off + hh:off + hh + 1] for hh in range(N_HEADS)])
            g_c.append([gblk[:, 2 * N_HEADS + off + hh:2 * N_HEADS + off + hh + 1] for hh in range(N_HEADS)])
            glast_c.append([gc[last:last + 1, :] for gc in g_c[-1]])
    beta = pair_bcast(beta_c, HEAD_DIM)
    eg = pair_bcast([[jnp.exp(gc) for gc in gcs] for gcs in g_c], HEAD_DIM)
    tail = pair_bcast([[jnp.exp(gl - gc) for gl, gc in zip(gls, gcs)] for gls, gcs in zip(glast_c, g_c)],
                      HEAD_DIM)
    gcol = pair_bcast(g_c, CHUNK)
    grow = jnp.stack([d[4][(n_pairs if d[6] else 0) + p:(n_pairs if d[6] else 0) + p + 1,
                           2 * d[7] * CHUNK:2 * (d[7] + 1) * CHUNK]
                      for st in steps for d in st for p in range(n_pairs)])

    decay = jnp.exp(jnp.where(incl, gcol - grow, NEG_BIG))
    kb = kf * beta
    kkqk = _bmm_nt(jnp.concatenate([kb, qf], axis=1).astype(BF16),
                   _block_diag_rows(kf, HEAD_DIM).astype(BF16))
    lmat = jnp.where(strict, kkqk[:, :CHUNK, :] * decay, 0.0)
    qk = (kkqk[:, CHUNK:, :] * decay).astype(BF16)
    tinv = _unit_tri_inverse(lmat, eye)
    rhs = jnp.concatenate([_block_diag_rows(vf * beta, HEAD_DIM),
                           _block_diag_rows(kb * eg, HEAD_DIM)], axis=2).astype(BF16)
    uw = _bmm(tinv.astype(BF16), rhs)
    u = uw[:, :, :pw]
    wq = jnp.concatenate([uw[:, :, pw:], qf * eg], axis=1).astype(BF16)
    kt = (kf * tail).astype(BF16)

    for i, st in enumerate(steps):
        sl = slice(i * n_inst, (i + 1) * n_inst)
        s_a = s_ref[:, 0]
        s_b = s_ref[:, 1]
        zero = jnp.zeros_like(s_a)
        s_bd = jnp.concatenate([jnp.concatenate([s_a, zero], axis=2),
                                jnp.concatenate([zero, s_b], axis=2)], axis=1).astype(BF16)
        ws = _bmm(wq[sl], s_bd)
        v_new = u[sl] - ws[:, :CHUNK, :]
        o = ws[:, CHUNK:, :] + _bmm(qk[sl], _block_diag_rows(v_new, HEAD_DIM).astype(BF16))
        v16 = v_new.astype(BF16)
        gl = [glast_c[2 * i + dd] for dd in range(2)]
        cd_a = jnp.stack([jnp.exp(gl[dd][2 * p]) for dd in range(2) for p in range(n_pairs)])
        cd_b = jnp.stack([jnp.exp(gl[dd][2 * p + 1]) for dd in range(2) for p in range(n_pairs)])
        s_ref[:, 0] = s_a * cd_a + _bmm_tn(kt[sl, :, :HEAD_DIM], v16[:, :, :HEAD_DIM])
        s_ref[:, 1] = s_b * cd_b + _bmm_tn(kt[sl, :, HEAD_DIM:], v16[:, :, HEAD_DIM:])
        for dd, d in enumerate(st):
            for p in range(n_pairs):
                d[5][d[7] * CHUNK:(d[7] + 1) * CHUNK, p * pw:(p + 1) * pw] = (
                    o[dd * n_pairs + p].astype(d[5].dtype))


def _delta_rule(qkv, g, gtp, bsz, seq):
    t = bsz * seq
    rows = min(DELTA_ROWS, seq)
    nblk = seq // rows

    def fwd(col):
        return lambda b, i: (b * nblk + i, col)

    def bwd(col):
        return lambda b, i: (b * nblk + nblk - 1 - i, col)

    def specs(m):
        return [pl.BlockSpec((rows, WIDTH_A), m(0)), pl.BlockSpec((rows, WIDTH_A), m(1)),
                pl.BlockSpec((rows, WIDTH_A), m(2)), pl.BlockSpec((rows, LANES), m(0)),
                pl.BlockSpec((N_HEADS, 2 * rows), lambda b, i, m=m: (0, m(0)(b, i)[0]))]

    out = jax.ShapeDtypeStruct((t, WIDTH_A), BF16)
    return pl.pallas_call(
        functools.partial(_delta_kernel, nc=rows // CHUNK),
        out_shape=(out, out),
        grid=(bsz, nblk),
        in_specs=specs(fwd) + specs(bwd),
        out_specs=(pl.BlockSpec((rows, WIDTH_A), fwd(0)), pl.BlockSpec((rows, WIDTH_A), bwd(0))),
        scratch_shapes=[pltpu.VMEM((N_HEADS, 2, HEAD_DIM, HEAD_DIM), F32)],
        compiler_params=_params("parallel", "arbitrary"),
        name="delta_rule",
    )(qkv, qkv, qkv, g, gtp, qkv, qkv, qkv, g, gtp)


def _mix_kernel(of_ref, ob_ref, z_ref, yc_ref, gate_ref, x_ref,
                eg_ref, eb_ref, ng_ref, wao_ref, cg_ref, cb_ref, wbo_ref, bbo_ref,
                wout_ref, l1g_ref, l1b_ref, wr_ref, br_ref,
                h1_ref, h1p_ref, logit_ref):
    o = of_ref[...].astype(F32) + ob_ref[...].astype(F32)
    z = z_ref[...].astype(F32)
    ng = ng_ref[...]
    parts = []
    for hh in range(N_HEADS):
        sl = slice(hh * HEAD_DIM, (hh + 1) * HEAD_DIM)
        oh = o[:, sl]
        zh = z[:, sl]
        inv = lax.rsqrt(jnp.mean(oh * oh, axis=-1, keepdims=True) + RMS_EPS)
        parts.append((oh * inv * ng * _silu(zh)).astype(BF16))
    y_a = _dot(jnp.concatenate(parts, axis=1), wao_ref[...])

    yc = _layer_norm(yc_ref[...].astype(F32), cg_ref[...], cb_ref[...])
    y_b = _dot(_silu(yc).astype(BF16), wbo_ref[...]) + bbo_ref[...]

    gates = gate_ref[...].astype(F32)
    mixed = gates[:, :D_MODEL] * y_a + gates[:, D_MODEL:] * y_b
    mix = _dot(mixed.astype(BF16), wout_ref[...])

    h0 = _layer_norm(x_ref[...], eg_ref[...], eb_ref[...])
    h1 = _layer_norm(DN_ALPHA * h0 + mix, l1g_ref[...], l1b_ref[...])
    h1_ref[...] = h1
    h1p_ref[...] = _pack_bf16_pair(h1[:, :D_MODEL // 2], h1[:, D_MODEL // 2:])
    h_hi, h_lo = _split_bf16(h1)
    p = _dot(h_hi, wr_ref[...])
    logit_ref[...] = p[:, :LANES] + p[:, LANES:] + _dot(h_lo, wr_ref[:, :LANES]) + br_ref[...]


def _mix(o_f, o_b, u_qkvz, yc, gates, x2, emb_g, emb_b, norm_g, w_a_o, cg, cb, w_b_o, b_b_o,
         w_out, l1g, l1b, w_router, b_router):
    t, d = x2.shape
    tm = min(ROW_TILE, t)
    row = lambda i: (i, 0)
    const = lambda i: (0, 0)
    wr = jnp.concatenate(_split_bf16(jnp.pad(w_router, ((0, 0), (0, LANES - N_EXPERTS)))), axis=1)
    br = jnp.pad(b_router.reshape(1, N_EXPERTS), ((0, 0), (0, LANES - N_EXPERTS)), constant_values=NEG_BIG)
    vec = lambda a: a.reshape(1, -1)
    return pl.pallas_call(
        _mix_kernel,
        out_shape=(jax.ShapeDtypeStruct((t, d), F32), jax.ShapeDtypeStruct((t, d // 2), jnp.uint32),
                   jax.ShapeDtypeStruct((t, LANES), F32)),
        grid=(t // tm,),
        in_specs=[pl.BlockSpec((tm, d), row), pl.BlockSpec((tm, d), row),
                  pl.BlockSpec((tm, d), lambda i: (i, 3)),
                  pl.BlockSpec((tm, d), row), pl.BlockSpec((tm, 2 * d), row), pl.BlockSpec((tm, d), row),
                  pl.BlockSpec((1, d), const), pl.BlockSpec((1, d), const),
                  pl.BlockSpec((1, HEAD_DIM), const), pl.BlockSpec((d, d), const),
                  pl.BlockSpec((1, d), const), pl.BlockSpec((1, d), const),
                  pl.BlockSpec((d, d), const), pl.BlockSpec((1, d), const),
                  pl.BlockSpec((d, d), const), pl.BlockSpec((1, d), const), pl.BlockSpec((1, d), const),
                  pl.BlockSpec((d, 2 * LANES), const), pl.BlockSpec((1, LANES), const)],
        out_specs=(pl.BlockSpec((tm, d), row), pl.BlockSpec((tm, d // 2), row), pl.BlockSpec((tm, LANES), row)),
        compiler_params=_params("parallel"),
        name="mix",
    )(o_f, o_b, u_qkvz, yc, gates, x2, vec(emb_g), vec(emb_b), vec(norm_g), w_a_o, vec(cg), vec(cb),
      w_b_o, vec(b_b_o), w_out, vec(l1g), vec(l1b), wr, br)


def _route_kernel(logit_ref, gate_ref, eidx_ref, rank_ref, cnt_ref, base_ref):
    @pl.when(pl.program_id(0) == 0)
    def _():
        base_ref[...] = jnp.zeros_like(base_ref)

    x = logit_ref[...]
    tm = x.shape[0]
    lane = lax.broadcasted_iota(jnp.int32, x.shape, 1)
    lane_f = lane.astype(F32)
    row = lax.broadcasted_iota(jnp.int32, x.shape, 0)
    sel = jnp.zeros(x.shape, F32)
    vals, idxs = [], []
    for _ in range(TOP_K):
        m = jnp.max(x, axis=1, keepdims=True)
        idx = jnp.min(jnp.where(x == m, lane_f, float(LANES)), axis=1, keepdims=True).astype(jnp.int32)
        hit = lane == idx
        sel = sel + hit.astype(F32)
        x = jnp.where(hit, -3e38, x)
        vals.append(m)
        idxs.append(idx)

    exps = [jnp.exp(v - vals[0]) for v in vals]
    denom = exps[0]
    for e in exps[1:]:
        denom = denom + e

    csum = sel
    s = 1
    while s < tm:
        csum = csum + jnp.where(row >= s, pltpu.roll(csum, s, axis=0), 0.0)
        s *= 2
    before = base_ref[...] + csum - sel

    gate = jnp.zeros(x.shape, F32)
    eidx = jnp.zeros(x.shape, F32)
    rank = jnp.zeros(x.shape, F32)
    for k in range(TOP_K):
        rk = jnp.sum(jnp.where(lane == idxs[k], before, 0.0), axis=1, keepdims=True)
        gate = jnp.where(lane == k, exps[k] / denom, gate)
        eidx = jnp.where(lane == k, idxs[k].astype(F32), eidx)
        rank = jnp.where(lane == k, rk, rank)
    gate_ref[...] = gate
    eidx_ref[...] = eidx.T[:8, :].astype(jnp.int32)
    rank_ref[...] = rank.T[:8, :].astype(jnp.int32)
    total = base_ref[...] + csum[tm - 1:tm, :]
    base_ref[...] = total
    cnt_ref[...] = total


def _route(logits):
    t = logits.shape[0]
    tm = min(ROW_TILE, t)
    bm = EXPERT_ROWS
    row = lambda i: (i, 0)
    gate, eidx, rank, cnt = pl.pallas_call(
        _route_kernel,
        out_shape=(jax.ShapeDtypeStruct((t, LANES), F32), jax.ShapeDtypeStruct((8, t), jnp.int32),
                   jax.ShapeDtypeStruct((8, t), jnp.int32), jax.ShapeDtypeStruct((1, LANES), F32)),
        grid=(t // tm,),
        in_specs=[pl.BlockSpec((tm, LANES), row)],
        out_specs=(pl.BlockSpec((tm, LANES), row), pl.BlockSpec((8, tm), lambda i: (0, i)),
                   pl.BlockSpec((8, tm), lambda i: (0, i)), pl.BlockSpec((1, LANES), lambda i: (0, 0))),
        scratch_shapes=[pltpu.VMEM((1, LANES), F32)],
        compiler_params=_params("arbitrary"),
        name="moe_route",
    )(logits)
    counts = cnt[0, :N_EXPERTS].astype(jnp.int32)
    padded = (counts + bm - 1) // bm * bm
    pad_end = jnp.cumsum(padded)
    pad_start = pad_end - padded
    onehot = eidx[:TOP_K, :, None] == jnp.arange(N_EXPERTS, dtype=jnp.int32)
    pos_kmajor = (jnp.sum(jnp.where(onehot, pad_start, 0), axis=-1) + rank[:TOP_K]).reshape(-1)
    n_blocks = -(-(t * TOP_K + N_EXPERTS * (bm - 1)) // bm)
    block_start = jnp.arange(n_blocks, dtype=jnp.int32) * bm
    block_exp = jnp.minimum(jnp.sum((block_start[:, None] >= pad_end[None, :]).astype(jnp.int32), axis=1),
                            N_EXPERTS - 1)
    block_rows = jnp.clip((pad_start + counts)[block_exp] - block_start, 0, bm).astype(jnp.int32)
    return gate, pos_kmajor.astype(jnp.int32), block_exp, block_rows, n_blocks


def _sc_worker_range(n_rows):
    per_worker = n_rows // (SC_CORES * SC_SUBCORES)
    wid = lax.axis_index("s") * SC_CORES + lax.axis_index("c")
    return wid * per_worker, per_worker


def _sc_scatter_rows(src, idx, n_out):
    t, d = src.shape
    n_copies = idx.shape[0] // t
    mesh = plsc.VectorSubcoreMesh(core_axis_name="c", subcore_axis_name="s")

    @functools.partial(
        pl.kernel, mesh=mesh,
        out_type=jax.ShapeDtypeStruct((n_out, d), src.dtype),
        scratch_types=([pltpu.VMEM((SC_GATHER_ROWS,), jnp.int32)] * n_copies
                       + [pltpu.VMEM((SC_GATHER_ROWS, d), src.dtype)]
                       + [pltpu.SemaphoreType.DMA] * n_copies),
        name="sc_scatter_rows",
    )
    def scatter(src_hbm, idx_hbm, out_hbm, *scratch):
        idx_v = scratch[:n_copies]
        rows_v = scratch[n_copies]
        sems = scratch[n_copies + 1:]
        base, per_worker = _sc_worker_range(t)

        @pl.loop(0, per_worker // SC_GATHER_ROWS)
        def _(j):
            off = pl.multiple_of(base + j * SC_GATHER_ROWS, SC_GATHER_ROWS)
            for k in range(n_copies):
                pltpu.sync_copy(idx_hbm.at[pl.ds(k * t + off, SC_GATHER_ROWS)], idx_v[k])
            pltpu.sync_copy(src_hbm.at[pl.ds(off, SC_GATHER_ROWS)], rows_v)
            copies = [pltpu.async_copy(rows_v, out_hbm.at[idx_v[k]], sems[k]) for k in range(n_copies)]
            for c in copies:
                c.wait()

    return scatter(src, idx)


def _expert_kernel(be_ref, nr_ref, x_ref, wgu_ref, bgu_ref, wd_ref, bd_ref, o_ref, wgu16_ref, wd16_ref, *, tn):
    i = pl.program_id(0)
    active = nr_ref[i] > 0
    new_expert = jnp.logical_or(i == 0, be_ref[i] != be_ref[jnp.maximum(i - 1, 0)])

    @pl.when(jnp.logical_and(active, new_expert))
    def _():
        wgu16_ref[...] = wgu_ref[0].astype(BF16)
        wd16_ref[...] = wd_ref[0].astype(BF16)

    @pl.when(active)
    def _():
        half = D_MODEL // 2
        row = lax.broadcasted_iota(jnp.int32, x_ref.shape, 0)
        x = jnp.where(row < nr_ref[i], x_ref[...], jnp.uint32(0))
        x_lo, x_hi = _unpack_bf16_pair(x)
        x_lo = x_lo.astype(BF16)
        x_hi = x_hi.astype(BF16)

        def up(c0):
            return (_dot(x_lo, wgu16_ref[:half, c0:c0 + tn]) + _dot(x_hi, wgu16_ref[half:, c0:c0 + tn])
                    + bgu_ref[0, :, c0:c0 + tn])

        y = bd_ref[0]
        for n0 in range(0, D_FF, tn):
            glu = jnp.minimum(up(n0), SWIGLU_LIMIT)
            lin = jnp.clip(up(D_FF + n0), -SWIGLU_LIMIT, SWIGLU_LIMIT)
            act = glu * _sigmoid(SWIGLU_ALPHA * glu) * (lin + 1.0)
            y = y + _dot(act.astype(BF16), wd16_ref[n0:n0 + tn, :])
        o_ref[...] = _pack_bf16_pair(y[:, :half], y[:, half:])

    @pl.when(jnp.logical_not(active))
    def _():
        o_ref[...] = jnp.zeros_like(o_ref)


def _experts(xs, block_exp, block_rows, w_gu, b_gu, w_down, b_down, n_blocks):
    d = D_MODEL
    dp = xs.shape[1]
    bm = EXPERT_ROWS
    grid_spec = pltpu.PrefetchScalarGridSpec(
        num_scalar_prefetch=2,
        grid=(n_blocks,),
        in_specs=[pl.BlockSpec((bm, dp), lambda i, be, nb: (i, 0)),
                  pl.BlockSpec((1, d, 2 * D_FF), lambda i, be, nb: (be[i], 0, 0)),
                  pl.BlockSpec((1, 1, 2 * D_FF), lambda i, be, nb: (be[i], 0, 0)),
                  pl.BlockSpec((1, D_FF, d), lambda i, be, nb: (be[i], 0, 0)),
                  pl.BlockSpec((1, 1, d), lambda i, be, nb: (be[i], 0, 0))],
        out_specs=pl.BlockSpec((bm, dp), lambda i, be, nb: (i, 0)),
        scratch_shapes=[pltpu.VMEM((d, 2 * D_FF), BF16), pltpu.VMEM((D_FF, d), BF16)],
    )
    return pl.pallas_call(
        functools.partial(_expert_kernel, tn=EXPERT_SLAB),
        out_shape=jax.ShapeDtypeStruct((n_blocks * bm, dp), jnp.uint32),
        grid_spec=grid_spec,
        compiler_params=_params("arbitrary"),
        name="moe_experts",
    )(block_exp, block_rows, xs, w_gu, b_gu.reshape(N_EXPERTS, 1, 2 * D_FF), w_down,
      b_down.reshape(N_EXPERTS, 1, d))


def _sc_gather_rows(table, idx):
    m = idx.shape[0]
    d = table.shape[1]
    mesh = plsc.VectorSubcoreMesh(core_axis_name="c", subcore_axis_name="s")

    @functools.partial(
        pl.kernel, mesh=mesh,
        out_type=jax.ShapeDtypeStruct((m, d), table.dtype),
        scratch_types=([pltpu.VMEM((SC_GATHER_ROWS,), jnp.int32)] * 2
                       + [pltpu.VMEM((SC_GATHER_ROWS, d), table.dtype)] * 2
                       + [pltpu.SemaphoreType.DMA] * 4),
        name="sc_gather_rows",
    )
    def gather(table_hbm, idx_hbm, out_hbm, idx0, idx1, rows0, rows1, g0, g1, w0, w1):
        base, per_worker = _sc_worker_range(m)

        @pl.loop(0, per_worker // (2 * SC_GATHER_ROWS))
        def _(j):
            off0 = pl.multiple_of(base + 2 * j * SC_GATHER_ROWS, SC_GATHER_ROWS)
            off1 = off0 + SC_GATHER_ROWS
            pltpu.sync_copy(idx_hbm.at[pl.ds(off0, SC_GATHER_ROWS)], idx0)
            gather0 = pltpu.async_copy(table_hbm.at[idx0], rows0, g0)
            pltpu.sync_copy(idx_hbm.at[pl.ds(off1, SC_GATHER_ROWS)], idx1)
            gather1 = pltpu.async_copy(table_hbm.at[idx1], rows1, g1)
            gather0.wait()
            write0 = pltpu.async_copy(rows0, out_hbm.at[pl.ds(off0, SC_GATHER_ROWS)], w0)
            gather1.wait()
            write1 = pltpu.async_copy(rows1, out_hbm.at[pl.ds(off1, SC_GATHER_ROWS)], w1)
            write0.wait()
            write1.wait()

    return gather(table, idx)


def _combine_dense_kernel(y0_ref, y1_ref, y2_ref, y3_ref, gate_ref, h_ref, g_ref, b_ref, o_ref):
    gate = gate_ref[...]
    f_lo = f_hi = None
    for k, y_ref in enumerate((y0_ref, y1_ref, y2_ref, y3_ref)):
        y_lo, y_hi = _unpack_bf16_pair(y_ref[...])
        gk = gate[:, k:k + 1]
        f_lo = gk * y_lo if f_lo is None else f_lo + gk * y_lo
        f_hi = gk * y_hi if f_hi is None else f_hi + gk * y_hi
    f = jnp.concatenate([f_lo, f_hi], axis=1)
    o_ref[...] = _layer_norm(DN_ALPHA * h_ref[...] + f, g_ref[...], b_ref[...])


def _combine_dense(yg, gate, h1, ln_g, ln_b):
    t, d = h1.shape
    tm = min(ROW_TILE, t)
    nt = t // tm
    dp = yg.shape[1]
    slab = lambda k: pl.BlockSpec((tm, dp), lambda i, k=k: (k * nt + i, 0))
    return pl.pallas_call(
        _combine_dense_kernel,
        out_shape=jax.ShapeDtypeStruct((t, d), F32),
        grid=(nt,),
        in_specs=[slab(0), slab(1), slab(2), slab(3),
                  pl.BlockSpec((tm, LANES), lambda i: (i, 0)),
                  pl.BlockSpec((tm, d), lambda i: (i, 0)),
                  pl.BlockSpec((1, d), lambda i: (0, 0)),
                  pl.BlockSpec((1, d), lambda i: (0, 0))],
        out_specs=pl.BlockSpec((tm, d), lambda i: (i, 0)),
        compiler_params=_params("parallel"),
        name="moe_combine",
    )(yg, yg, yg, yg, gate, h1, ln_g.reshape(1, d), ln_b.reshape(1, d))


def kernel(x, emb_ln_g, emb_ln_b, w_in, conv_qkv, a_log, dt_bias, dn_norm_g, w_a_o, b_glu, conv_dw, b_dw, conv_ln_g, conv_ln_b, w_b_o, b_b_o, b_gate, w_out, ln1_g, ln1_b, w_router, b_router, w_gu, b_gu, w_down, b_down, ln2_g, ln2_b):
    bsz, seq, d = x.shape
    t = bsz * seq
    x2 = x.reshape(t, d)
    u_qkvz, g, gt, glu, gates = _inproj(x2, emb_ln_g, emb_ln_b, w_in[0], b_glu[0], b_gate[0], a_log[0], dt_bias[0])

    qkv = _qkv_conv(u_qkvz, conv_qkv[0], bsz, seq)
    gtp = gt.reshape(2, N_HEADS // 2, 2, t // CHUNK, CHUNK).transpose(0, 1, 3, 2, 4).reshape(N_HEADS, 2 * t)
    o_f, o_b = _delta_rule(qkv, g, gtp, bsz, seq)
    yc = _dw_conv(glu, conv_dw[0], b_dw[0], bsz, seq)

    h1, h1p, logits = _mix(o_f, o_b, u_qkvz, yc, gates, x2, emb_ln_g, emb_ln_b, dn_norm_g[0],
                           w_a_o[0].astype(BF16), conv_ln_g[0], conv_ln_b[0], w_b_o[0].astype(BF16), b_b_o[0],
                           w_out[0].astype(BF16), ln1_g[0], ln1_b[0], w_router[0], b_router[0])

    gate, pos_kmajor, block_exp, block_rows, n_blocks = _route(logits)
    xs = _sc_scatter_rows(h1p, pos_kmajor, n_blocks * EXPERT_ROWS)
    ys = _experts(xs, block_exp, block_rows, w_gu[0], b_gu[0], w_down[0], b_down[0], n_blocks)
    yg = _sc_gather_rows(ys, pos_kmajor)
    out = _combine_dense(yg, gate, h1, ln2_g[0], ln2_b[0])
    return out.reshape(bsz, seq, d)
```

```python
import functools

import jax
import jax.numpy as jnp
from jax import lax
from jax.experimental import pallas as pl
from jax.experimental.pallas import tpu as pltpu
from jax.experimental.pallas import tpu_sc as plsc

F32 = jnp.float32
BF16 = jnp.bfloat16

D_MODEL = 1024
N_HEADS = 8
HEAD_DIM = 128
WIDTH_A = N_HEADS * HEAD_DIM
SHORT_CONV = 5
CHUNK = 64
WIDTH_B = D_MODEL
DW_CONV = 31
N_EXPERTS = 32
TOP_K = 4
D_FF = D_MODEL
SWIGLU_ALPHA = 1.702
SWIGLU_LIMIT = 7.0
DN_ALPHA = 2.0 ** 0.25
LN_EPS = 1e-5
RMS_EPS = 1e-6
L2_EPS = 1e-6
LANES = 128
NEG_BIG = -1e30

ROW_TILE = 512
DELTA_ROWS = 512
EXPERT_ROWS = 512
EXPERT_SLAB = 512
SC_CORES = 2
SC_SUBCORES = 16
SC_GATHER_ROWS = 64
VMEM_LIMIT = 56 * 1024 * 1024


def _params(*sem):
    return pltpu.CompilerParams(dimension_semantics=sem, vmem_limit_bytes=VMEM_LIMIT)


def _layer_norm(x, g, b):
    mu = jnp.mean(x, axis=-1, keepdims=True)
    xc = x - mu
    var = jnp.mean(xc * xc, axis=-1, keepdims=True)
    return xc * lax.rsqrt(var + LN_EPS) * g + b


def _sigmoid(x):
    return 0.5 * jnp.tanh(0.5 * x) + 0.5


def _silu(x):
    h = 0.5 * x
    return h + h * jnp.tanh(h)


def _dot(a, b):
    return jnp.dot(a, b, preferred_element_type=F32)


def _pack_bf16_pair(a, b):
    ua = lax.bitcast_convert_type(a.astype(BF16).astype(F32), jnp.uint32)
    ub = lax.bitcast_convert_type(b.astype(BF16).astype(F32), jnp.uint32)
    return (ua >> 16) | ub


def _unpack_bf16_pair(p):
    a = lax.bitcast_convert_type(p << 16, F32)
    b = lax.bitcast_convert_type(p & jnp.uint32(0xFFFF0000), F32)
    return a, b


def _split_bf16(a):
    hi = a.astype(BF16)
    return hi, (a - hi.astype(F32)).astype(BF16)


def _chunk_cumsum(x, reverse):
    rows = x.shape[0]
    pos = lax.broadcasted_iota(jnp.int32, x.shape, 0) % CHUNK
    s = 1
    while s < CHUNK:
        if reverse:
            shifted = pltpu.roll(x, rows - s, axis=0)
            x = x + jnp.where(pos < CHUNK - s, shifted, 0.0)
        else:
            shifted = pltpu.roll(x, s, axis=0)
            x = x + jnp.where(pos >= s, shifted, 0.0)
        s *= 2
    return x


def _inproj_kernel(x_ref, eg_ref, eb_ref, wq_ref, ws_ref, wga_ref, wgb_ref, bga_ref, bgb_ref, wgt_ref, bgt_ref,
                   alog_ref, dtb_ref, u_ref, g_ref, gt_ref, glu_ref, gate_ref, *, tn):
    h = _layer_norm(x_ref[...], eg_ref[...], eb_ref[...]).astype(BF16)

    for n0 in range(0, u_ref.shape[1], tn):
        u_ref[:, n0:n0 + tn] = _dot(h, wq_ref[:, n0:n0 + tn]).astype(u_ref.dtype)

    us = _dot(h, ws_ref[...])
    lane = lax.broadcasted_iota(jnp.int32, us.shape, 1)
    beta = _sigmoid(us)
    xs = us + dtb_ref[...]
    softplus = jnp.maximum(xs, 0.0) + jnp.log(1.0 + jnp.exp(-jnp.abs(xs)))
    log_a = -jnp.exp(alog_ref[...]) * softplus
    g_fwd = _chunk_cumsum(log_a, reverse=False)
    g_bwd = _chunk_cumsum(log_a, reverse=True)
    gates = jnp.where(lane < 2 * N_HEADS, beta, jnp.where(lane < 3 * N_HEADS, g_fwd, g_bwd))
    g_ref[...] = gates
    gt_ref[...] = gates.T[2 * N_HEADS:4 * N_HEADS, :]

    for n0 in range(0, glu_ref.shape[1], tn):
        lin = _dot(h, wga_ref[:, n0:n0 + tn]) + bga_ref[:, n0:n0 + tn]
        gt = _dot(h, wgb_ref[:, n0:n0 + tn]) + bgb_ref[:, n0:n0 + tn]
        glu_ref[:, n0:n0 + tn] = (lin * _sigmoid(gt)).astype(glu_ref.dtype)

    for n0 in range(0, gate_ref.shape[1], tn):
        gate_ref[:, n0:n0 + tn] = _sigmoid(_dot(h, wgt_ref[:, n0:n0 + tn])
                                           + bgt_ref[:, n0:n0 + tn]).astype(gate_ref.dtype)


def _inproj(x2, emb_g, emb_b, w_in, b_glu, b_gate, a_log, dt_bias):
    t, d = x2.shape
    tm = min(ROW_TILE, t)
    c0 = 4 * WIDTH_A
    c1 = c0 + 4 * N_HEADS
    c2 = c1 + 2 * WIDTH_B
    wb = w_in.astype(BF16)
    pad = LANES - 4 * N_HEADS
    w_small = jnp.pad(wb[:, c0:c1], ((0, 0), (0, pad)))
    alog = jnp.pad(a_log.reshape(1, 2 * N_HEADS), ((0, 0), (2 * N_HEADS, pad)))
    dtb = jnp.pad(dt_bias.reshape(1, 2 * N_HEADS), ((0, 0), (2 * N_HEADS, pad)))
    row = lambda i: (i, 0)

    def const(shape):
        return pl.BlockSpec(shape, lambda i: (0, 0), pipeline_mode=pl.Buffered(1))

    vec = lambda a: a.reshape(1, -1)
    bf = lambda n: jax.ShapeDtypeStruct((t, n), BF16)
    return pl.pallas_call(
        functools.partial(_inproj_kernel, tn=512),
        out_shape=(bf(c0), jax.ShapeDtypeStruct((t, LANES), F32), jax.ShapeDtypeStruct((2 * N_HEADS, t), F32),
                   bf(WIDTH_B), bf(2 * d)),
        grid=(t // tm,),
        in_specs=[pl.BlockSpec((tm, d), row), const((1, d)), const((1, d)),
                  const((d, c0)), const((d, LANES)),
                  const((d, WIDTH_B)), const((d, WIDTH_B)), const((1, WIDTH_B)), const((1, WIDTH_B)),
                  const((d, 2 * d)), const((1, 2 * d)), const((1, LANES)), const((1, LANES))],
        out_specs=(pl.BlockSpec((tm, c0), row), pl.BlockSpec((tm, LANES), row),
                   pl.BlockSpec((2 * N_HEADS, tm), lambda i: (0, i)),
                   pl.BlockSpec((tm, WIDTH_B), row), pl.BlockSpec((tm, 2 * d), row)),
        compiler_params=_params("parallel"),
        name="inproj",
    )(x2, vec(emb_g), vec(emb_b), wb[:, :c0], w_small, wb[:, c1:c1 + WIDTH_B], wb[:, c1 + WIDTH_B:c2],
      vec(b_glu[:WIDTH_B]), vec(b_glu[WIDTH_B:]), wb[:, c2:], vec(b_gate), alog, dtb)


def _conv_rows(xp_ref, w, taps, base, r0, rows):
    acc = xp_ref[base + r0:base + r0 + rows, :] * w[0:1, :]
    for k in range(1, taps):
        acc = acc + xp_ref[base + k + r0:base + k + r0 + rows, :] * w[k:k + 1, :]
    return acc


def _fill_padded(xp_ref, x_ref, pad, seq):
    zeros = jnp.zeros((pad, xp_ref.shape[1]), F32)
    xp_ref[0:pad, :] = zeros
    xp_ref[pad + seq:pad + seq + pad, :] = zeros
    xp_ref[pad:pad + seq, :] = x_ref[...].astype(F32)


def _qkv_conv_kernel(u_ref, w_ref, o_ref, xp_ref, *, seq, rows):
    pad = 8
    j = pl.program_id(1)
    _fill_padded(xp_ref, u_ref, pad, seq)
    w = w_ref[...]
    is_qk = j < 2 * N_HEADS
    head_scale = jnp.where(j < N_HEADS, HEAD_DIM ** -0.5, 1.0)
    for r0 in range(0, seq, rows):
        y = _silu(_conv_rows(xp_ref, w, SHORT_CONV, pad - SHORT_CONV // 2, r0, rows))
        inv = lax.rsqrt(jnp.sum(y * y, axis=-1, keepdims=True) + L2_EPS)
        scale = jnp.where(is_qk, inv * head_scale, 1.0)
        o_ref[r0:r0 + rows, :] = (y * scale).astype(o_ref.dtype)


def _qkv_conv(u_qkvz, conv_w, bsz, seq):
    t = bsz * seq
    ncol = 3 * N_HEADS
    rows = min(256, seq)
    return pl.pallas_call(
        functools.partial(_qkv_conv_kernel, seq=seq, rows=rows),
        out_shape=jax.ShapeDtypeStruct((t, 3 * WIDTH_A), BF16),
        grid=(bsz, ncol),
        in_specs=[pl.BlockSpec((seq, HEAD_DIM), lambda b, j: (b, j)),
                  pl.BlockSpec((SHORT_CONV, HEAD_DIM), lambda b, j: (0, j))],
        out_specs=pl.BlockSpec((seq, HEAD_DIM), lambda b, j: (b, j)),
        scratch_shapes=[pltpu.VMEM((seq + 16, HEAD_DIM), F32)],
        compiler_params=_params("parallel", "parallel"),
        name="qkv_conv",
    )(u_qkvz, conv_w)


def _dw_conv_kernel(x_ref, w_ref, b_ref, o_ref, xp_ref, *, seq, rows):
    pad = 16
    _fill_padded(xp_ref, x_ref, pad, seq)
    w = w_ref[...]
    for r0 in range(0, seq, rows):
        y = _conv_rows(xp_ref, w, DW_CONV, pad - DW_CONV // 2, r0, rows) + b_ref[...]
        o_ref[r0:r0 + rows, :] = y.astype(o_ref.dtype)


def _dw_conv(glu, conv_w, b_dw, bsz, seq):
    t = bsz * seq
    rows = min(256, seq)
    return pl.pallas_call(
        functools.partial(_dw_conv_kernel, seq=seq, rows=rows),
        out_shape=jax.ShapeDtypeStruct((t, WIDTH_B), BF16),
        grid=(bsz, WIDTH_B // LANES),
        in_specs=[pl.BlockSpec((seq, LANES), lambda b, j: (b, j)),
                  pl.BlockSpec((DW_CONV, LANES), lambda b, j: (0, j)),
                  pl.BlockSpec((1, LANES), lambda b, j: (0, j))],
        out_specs=pl.BlockSpec((seq, LANES), lambda b, j: (b, j)),
        scratch_shapes=[pltpu.VMEM((seq + 32, LANES), F32)],
        compiler_params=_params("parallel", "parallel"),
        name="dw_conv",
    )(glu, conv_w, b_dw.reshape(1, WIDTH_B))


def _bmm(a, b):
    return lax.dot_general(a, b, (((2,), (1,)), ((0,), (0,))), preferred_element_type=F32)


def _bmm_nt(a, b):
    return lax.dot_general(a, b, (((2,), (2,)), ((0,), (0,))), preferred_element_type=F32)


def _bmm_tn(a, b):
    return lax.dot_general(a, b, (((1,), (1,)), ((0,), (0,))), preferred_element_type=F32)


def _block_diag_rows(x, half):
    lane = lax.broadcasted_iota(jnp.int32, x.shape, 2)
    return jnp.concatenate([jnp.where(lane < half, x, 0.0), jnp.where(lane >= half, x, 0.0)], axis=1)


def _unit_tri_inverse(lmat, eye):
    def rhs(p):
        return _block_diag_rows(p, CHUNK).astype(BF16)

    x = eye - lmat
    p = _bmm(lmat.astype(BF16), rhs(lmat))
    s = 2
    while 2 * s < CHUNK:
        xp = _bmm(jnp.concatenate([x, p], axis=1).astype(BF16), rhs(p))
        x = x + xp[:, :CHUNK]
        p = xp[:, CHUNK:]
        s *= 2
    return x + _bmm(x.astype(BF16), rhs(p))


def _delta_kernel(qf_ref, kf_ref, vf_ref, gf_ref, gtpf_ref, qb_ref, kb_ref, vb_ref, gb_ref, gtpb_ref,
                  of_ref, ob_ref, s_ref, *, nc):
    @pl.when(pl.program_id(1) == 0)
    def _():
        s_ref[...] = jnp.zeros_like(s_ref)

    n_pairs = N_HEADS // 2
    n_inst = 2 * n_pairs
    pw = 2 * HEAD_DIM
    dirs = ((qf_ref, kf_ref, vf_ref, gf_ref, gtpf_ref, of_ref, False),
            (qb_ref, kb_ref, vb_ref, gb_ref, gtpb_ref, ob_ref, True))
    steps = [[d + ((nc - 1 - i) if d[6] else i,) for d in dirs] for i in range(nc)]

    ri = lax.broadcasted_iota(jnp.int32, (CHUNK, 2 * CHUNK), 0)
    ci = jnp.bitwise_and(lax.broadcasted_iota(jnp.int32, (CHUNK, 2 * CHUNK), 1), CHUNK - 1)
    inst = lax.broadcasted_iota(jnp.int32, (nc * n_inst, 1, 1), 0)
    sign = 1 - 2 * jnp.bitwise_and(jnp.right_shift(inst, n_pairs.bit_length() - 1), 1)
    rel = (ri - ci)[None] * sign
    incl = rel >= 0
    strict = rel > 0
    eye = (ri == ci).astype(F32)

    def pairs(which):
        return jnp.stack([d[which][d[7] * CHUNK:(d[7] + 1) * CHUNK, p * pw:(p + 1) * pw]
                          for st in steps for d in st for p in range(n_pairs)]).astype(F32)

    def pair_bcast(cols, width):
        return jnp.stack([jnp.concatenate([jnp.broadcast_to(cc[2 * p], (CHUNK, width)),
                                           jnp.broadcast_to(cc[2 * p + 1], (CHUNK, width))], axis=1)
                          for cc in cols for p in range(n_pairs)])

    qf = pairs(0)
    kf = pairs(1)
    vf = pairs(2)
    beta_c, g_c, glast_c = [], [], []
    for st in steps:
        for d in st:
            gblk = d[3][d[7] * CHUNK:(d[7] + 1) * CHUNK, :]
            off = N_HEADS if d[6] else 0
            last = 0 if d[6] else CHUNK - 1
            beta_c.append([gblk[:, off + hh:off + hh + 1] for hh in range(N_HEADS)])
            g_c.append([gblk[:, 2 * N_HEADS + off + hh:2 * N_HEADS + off + hh + 1] for hh in range(N_HEADS)])
            glast_c.append([gc[last:last + 1, :] for gc in g_c[-1]])
    beta = pair_bcast(beta_c, HEAD_DIM)
    eg = pair_bcast([[jnp.exp(gc) for gc in gcs] for gcs in g_c], HEAD_DIM)
    tail = pair_bcast([[jnp.exp(gl - gc) for gl, gc in zip(gls, gcs)] for gls, gcs in zip(glast_c, g_c)],
                      HEAD_DIM)
    gcol = pair_bcast(g_c, CHUNK)
    grow = jnp.stack([d[4][(n_pairs if d[6] else 0) + p:(n_pairs if d[6] else 0) + p + 1,
                           2 * d[7] * CHUNK:2 * (d[7] + 1) * CHUNK]
                      for st in steps for d in st for p in range(n_pairs)])

    decay = jnp.exp(jnp.where(incl, gcol - grow, NEG_BIG))
    kb = kf * beta
    kkqk = _bmm_nt(jnp.concatenate([kb, qf], axis=1).astype(BF16),
                   _block_diag_rows(kf, HEAD_DIM).astype(BF16))
    lmat = jnp.where(strict, kkqk[:, :CHUNK, :] * decay, 0.0)
    qk = (kkqk[:, CHUNK:, :] * decay).astype(BF16)
    tinv = _unit_tri_inverse(lmat, eye)
    rhs = jnp.concatenate([_block_diag_rows(vf * beta, HEAD_DIM),
                           _block_diag_rows(kb * eg, HEAD_DIM)], axis=2).astype(BF16)
    uw = _bmm(tinv.astype(BF16), rhs)
    u = uw[:, :, :pw]
    wq = jnp.concatenate([uw[:, :, pw:], qf * eg], axis=1).astype(BF16)
    kt = (kf * tail).astype(BF16)

    for i, st in enumerate(steps):
        sl = slice(i * n_inst, (i + 1) * n_inst)
        s_a = s_ref[:, 0]
        s_b = s_ref[:, 1]
        zero = jnp.zeros_like(s_a)
        s_bd = jnp.concatenate([jnp.concatenate([s_a, zero], axis=2),
                                jnp.concatenate([zero, s_b], axis=2)], axis=1).astype(BF16)
        ws = _bmm(wq[sl], s_bd)
        v_new = u[sl] - ws[:, :CHUNK, :]
        o = ws[:, CHUNK:, :] + _bmm(qk[sl], _block_diag_rows(v_new, HEAD_DIM).astype(BF16))
        v16 = v_new.astype(BF16)
        gl = [glast_c[2 * i + dd] for dd in range(2)]
        cd_a = jnp.stack([jnp.exp(gl[dd][2 * p]) for dd in range(2) for p in range(n_pairs)])
        cd_b = jnp.stack([jnp.exp(gl[dd][2 * p + 1]) for dd in range(2) for p in range(n_pairs)])
        s_ref[:, 0] = s_a * cd_a + _bmm_tn(kt[sl, :, :HEAD_DIM], v16[:, :, :HEAD_DIM])
        s_ref[:, 1] = s_b * cd_b + _bmm_tn(kt[sl, :, HEAD_DIM:], v16[:, :, HEAD_DIM:])
        for dd, d in enumerate(st):
            for p in range(n_pairs):
                d[5][d[7] * CHUNK:(d[7] + 1) * CHUNK, p * pw:(p + 1) * pw] = (
                    o[dd * n_pairs + p].astype(d[5].dtype))


def _delta_rule(qkv, g, gtp, bsz, seq):
    t = bsz * seq
    rows = min(DELTA_ROWS, seq)
    nblk = seq // rows

    def fwd(col):
        return lambda b, i: (b * nblk + i, col)

    def bwd(col):
        return lambda b, i: (b * nblk + nblk - 1 - i, col)

    def specs(m):
        return [pl.BlockSpec((rows, WIDTH_A), m(0)), pl.BlockSpec((rows, WIDTH_A), m(1)),
                pl.BlockSpec((rows, WIDTH_A), m(2)), pl.BlockSpec((rows, LANES), m(0)),
                pl.BlockSpec((N_HEADS, 2 * rows), lambda b, i, m=m: (0, m(0)(b, i)[0]))]

    out = jax.ShapeDtypeStruct((t, WIDTH_A), BF16)
    return pl.pallas_call(
        functools.partial(_delta_kernel, nc=rows // CHUNK),
        out_shape=(out, out),
        grid=(bsz, nblk),
        in_specs=specs(fwd) + specs(bwd),
        out_specs=(pl.BlockSpec((rows, WIDTH_A), fwd(0)), pl.BlockSpec((rows, WIDTH_A), bwd(0))),
        scratch_shapes=[pltpu.VMEM((N_HEADS, 2, HEAD_DIM, HEAD_DIM), F32)],
        compiler_params=_params("parallel", "arbitrary"),
        name="delta_rule",
    )(qkv, qkv, qkv, g, gtp, qkv, qkv, qkv, g, gtp)


def _mix_kernel(of_ref, ob_ref, z_ref, yc_ref, gate_ref, x_ref,
                eg_ref, eb_ref, ng_ref, wao_ref, cg_ref, cb_ref, wbo_ref, bbo_ref,
                wout_ref, l1g_ref, l1b_ref, wr_ref, br_ref,
                h1_ref, h1p_ref, logit_ref):
    o = of_ref[...].astype(F32) + ob_ref[...].astype(F32)
    z = z_ref[...].astype(F32)
    ng = ng_ref[...]
    parts = []
    for hh in range(N_HEADS):
        sl = slice(hh * HEAD_DIM, (hh + 1) * HEAD_DIM)
        oh = o[:, sl]
        zh = z[:, sl]
        inv = lax.rsqrt(jnp.mean(oh * oh, axis=-1, keepdims=True) + RMS_EPS)
        parts.append((oh * inv * ng * _silu(zh)).astype(BF16))
    y_a = _dot(jnp.concatenate(parts, axis=1), wao_ref[...])

    yc = _layer_norm(yc_ref[...].astype(F32), cg_ref[...], cb_ref[...])
    y_b = _dot(_silu(yc).astype(BF16), wbo_ref[...]) + bbo_ref[...]

    gates = gate_ref[...].astype(F32)
    mixed = gates[:, :D_MODEL] * y_a + gates[:, D_MODEL:] * y_b
    mix = _dot(mixed.astype(BF16), wout_ref[...])

    h0 = _layer_norm(x_ref[...], eg_ref[...], eb_ref[...])
    h1 = _layer_norm(DN_ALPHA * h0 + mix, l1g_ref[...], l1b_ref[...])
    h1_ref[...] = h1
    h1p_ref[...] = _pack_bf16_pair(h1[:, :D_MODEL // 2], h1[:, D_MODEL // 2:])
    h_hi, h_lo = _split_bf16(h1)
    p = _dot(h_hi, wr_ref[...])
    logit_ref[...] = p[:, :LANES] + p[:, LANES:] + _dot(h_lo, wr_ref[:, :LANES]) + br_ref[...]


def _mix(o_f, o_b, u_qkvz, yc, gates, x2, emb_g, emb_b, norm_g, w_a_o, cg, cb, w_b_o, b_b_o,
         w_out, l1g, l1b, w_router, b_router):
    t, d = x2.shape
    tm = min(ROW_TILE, t)
    row = lambda i: (i, 0)
    const = lambda i: (0, 0)
    wr = jnp.concatenate(_split_bf16(jnp.pad(w_router, ((0, 0), (0, LANES - N_EXPERTS)))), axis=1)
    br = jnp.pad(b_router.reshape(1, N_EXPERTS), ((0, 0), (0, LANES - N_EXPERTS)), constant_values=NEG_BIG)
    vec = lambda a: a.reshape(1, -1)
    return pl.pallas_call(
        _mix_kernel,
        out_shape=(jax.ShapeDtypeStruct((t, d), F32), jax.ShapeDtypeStruct((t, d // 2), jnp.uint32),
                   jax.ShapeDtypeStruct((t, LANES), F32)),
        grid=(t // tm,),
        in_specs=[pl.BlockSpec((tm, d), row), pl.BlockSpec((tm, d), row),
                  pl.BlockSpec((tm, d), lambda i: (i, 3)),
                  pl.BlockSpec((tm, d), row), pl.BlockSpec((tm, 2 * d), row), pl.BlockSpec((tm, d), row),
                  pl.BlockSpec((1, d), const), pl.BlockSpec((1, d), const),
                  pl.BlockSpec((1, HEAD_DIM), const), pl.BlockSpec((d, d), const),
                  pl.BlockSpec((1, d), const), pl.BlockSpec((1, d), const),
                  pl.BlockSpec((d, d), const), pl.BlockSpec((1, d), const),
                  pl.BlockSpec((d, d), const), pl.BlockSpec((1, d), const), pl.BlockSpec((1, d), const),
                  pl.BlockSpec((d, 2 * LANES), const), pl.BlockSpec((1, LANES), const)],
        out_specs=(pl.BlockSpec((tm, d), row), pl.BlockSpec((tm, d // 2), row), pl.BlockSpec((tm, LANES), row)),
        compiler_params=_params("parallel"),
        name="mix",
    )(o_f, o_b, u_qkvz, yc, gates, x2, vec(emb_g), vec(emb_b), vec(norm_g), w_a_o, vec(cg), vec(cb),
      w_b_o, vec(b_b_o), w_out, vec(l1g), vec(l1b), wr, br)


def _route_kernel(logit_ref, gate_ref, eidx_ref, rank_ref, cnt_ref, base_ref):
    @pl.when(pl.program_id(0) == 0)
    def _():
        base_ref[...] = jnp.zeros_like(base_ref)

    x = logit_ref[...].T[:N_EXPERTS, :]
    tm = x.shape[1]
    expert = lax.broadcasted_iota(jnp.int32, x.shape, 0).astype(F32)
    tok = lax.broadcasted_iota(jnp.int32, x.shape, 1)
    sel = jnp.zeros(x.shape, F32)
    vals, idxs = [], []
    for _ in range(TOP_K):
        m = jnp.max(x, axis=0, keepdims=True)
        idx = jnp.min(jnp.where(x == m, expert, float(N_EXPERTS)), axis=0, keepdims=True)
        hit = expert == idx
        sel = sel + hit.astype(F32)
        x = jnp.where(hit, -3e38, x)
        vals.append(m)
        idxs.append(idx)

    exps = [jnp.exp(v - vals[0]) for v in vals]
    denom = exps[0]
    for e in exps[1:]:
        denom = denom + e

    csum = sel
    s = 1
    while s < tm:
        csum = csum + jnp.where(tok >= s, pltpu.roll(csum, s, axis=1), 0.0)
        s *= 2
    before = base_ref[...] + csum - sel

    krow = lax.broadcasted_iota(jnp.int32, (8, tm), 0)
    gate = jnp.zeros((8, tm), F32)
    eidx = jnp.zeros((8, tm), F32)
    rank = jnp.zeros((8, tm), F32)
    for k in range(TOP_K):
        rk = jnp.sum(jnp.where(expert == idxs[k], before, 0.0), axis=0, keepdims=True)
        gate = jnp.where(krow == k, exps[k] / denom, gate)
        eidx = jnp.where(krow == k, idxs[k], eidx)
        rank = jnp.where(krow == k, rk, rank)
    eidx_ref[...] = eidx.astype(jnp.int32)
    rank_ref[...] = rank.astype(jnp.int32)
    gate_ref[...] = jnp.concatenate([gate, jnp.zeros((LANES - 8, tm), F32)], axis=0).T
    total = base_ref[...] + csum[:, tm - 1:tm]
    base_ref[...] = total
    cnt_ref[...] = jnp.broadcast_to(total, cnt_ref.shape)


def _route(logits):
    t = logits.shape[0]
    tm = min(ROW_TILE, t)
    bm = EXPERT_ROWS
    row = lambda i: (i, 0)
    gate, eidx, rank, cnt = pl.pallas_call(
        _route_kernel,
        out_shape=(jax.ShapeDtypeStruct((t, LANES), F32), jax.ShapeDtypeStruct((8, t), jnp.int32),
                   jax.ShapeDtypeStruct((8, t), jnp.int32), jax.ShapeDtypeStruct((N_EXPERTS, LANES), F32)),
        grid=(t // tm,),
        in_specs=[pl.BlockSpec((tm, LANES), row)],
        out_specs=(pl.BlockSpec((tm, LANES), row), pl.BlockSpec((8, tm), lambda i: (0, i)),
                   pl.BlockSpec((8, tm), lambda i: (0, i)), pl.BlockSpec((N_EXPERTS, LANES), lambda i: (0, 0))),
        scratch_shapes=[pltpu.VMEM((N_EXPERTS, 1), F32)],
        compiler_params=_params("arbitrary"),
        name="moe_route",
    )(logits)
    counts = cnt[:, 0].astype(jnp.int32)
    padded = (counts + bm - 1) // bm * bm
    pad_end = jnp.cumsum(padded)
    pad_start = pad_end - padded
    onehot = eidx[:TOP_K, :, None] == jnp.arange(N_EXPERTS, dtype=jnp.int32)
    pos_kmajor = (jnp.sum(jnp.where(onehot, pad_start, 0), axis=-1) + rank[:TOP_K]).reshape(-1)
    n_blocks = -(-(t * TOP_K + N_EXPERTS * (bm - 1)) // bm)
    block_start = jnp.arange(n_blocks, dtype=jnp.int32) * bm
    block_exp = jnp.minimum(jnp.sum((block_start[:, None] >= pad_end[None, :]).astype(jnp.int32), axis=1),
                            N_EXPERTS - 1)
    block_rows = jnp.clip((pad_start + counts)[block_exp] - block_start, 0, bm).astype(jnp.int32)
    return gate, pos_kmajor.astype(jnp.int32), block_exp, block_rows, n_blocks


def _sc_worker_range(n_rows):
    per_worker = n_rows // (SC_CORES * SC_SUBCORES)
    wid = lax.axis_index("s") * SC_CORES + lax.axis_index("c")
    return wid * per_worker, per_worker


def _sc_scatter_rows(src, idx, n_out):
    t, d = src.shape
    n_copies = idx.shape[0] // t
    assert t % (SC_CORES * SC_SUBCORES * SC_GATHER_ROWS) == 0, "rows must split evenly over the subcores"
    mesh = plsc.VectorSubcoreMesh(core_axis_name="c", subcore_axis_name="s")

    @functools.partial(
        pl.kernel, mesh=mesh,
        out_type=jax.ShapeDtypeStruct((n_out, d), src.dtype),
        scratch_types=([pltpu.VMEM((SC_GATHER_ROWS,), jnp.int32)] * n_copies
                       + [pltpu.VMEM((SC_GATHER_ROWS, d), src.dtype)]
                       + [pltpu.SemaphoreType.DMA] * n_copies),
        name="sc_scatter_rows",
    )
    def scatter(src_hbm, idx_hbm, out_hbm, *scratch):
        idx_v = scratch[:n_copies]
        rows_v = scratch[n_copies]
        sems = scratch[n_copies + 1:]
        base, per_worker = _sc_worker_range(t)

        @pl.loop(0, per_worker // SC_GATHER_ROWS)
        def _(j):
            off = pl.multiple_of(base + j * SC_GATHER_ROWS, SC_GATHER_ROWS)
            for k in range(n_copies):
                pltpu.sync_copy(idx_hbm.at[pl.ds(k * t + off, SC_GATHER_ROWS)], idx_v[k])
            pltpu.sync_copy(src_hbm.at[pl.ds(off, SC_GATHER_ROWS)], rows_v)
            copies = [pltpu.async_copy(rows_v, out_hbm.at[idx_v[k]], sems[k]) for k in range(n_copies)]
            for c in copies:
                c.wait()

    return scatter(src, idx)


def _expert_kernel(be_ref, nr_ref, x_ref, wgu_ref, bgu_ref, wd_ref, bd_ref, o_ref, wgu16_ref, wd16_ref, *, tn):
    i = pl.program_id(0)
    active = nr_ref[i] > 0
    new_expert = jnp.logical_or(i == 0, be_ref[i] != be_ref[jnp.maximum(i - 1, 0)])

    @pl.when(jnp.logical_and(active, new_expert))
    def _():
        wgu16_ref[...] = wgu_ref[0].astype(BF16)
        wd16_ref[...] = wd_ref[0].astype(BF16)

    @pl.when(active)
    def _():
        half = D_MODEL // 2
        row = lax.broadcasted_iota(jnp.int32, x_ref.shape, 0)
        x = jnp.where(row < nr_ref[i], x_ref[...], jnp.uint32(0))
        x_lo, x_hi = _unpack_bf16_pair(x)
        x_lo = x_lo.astype(BF16)
        x_hi = x_hi.astype(BF16)

        def up(c0):
            return (_dot(x_lo, wgu16_ref[:half, c0:c0 + tn]) + _dot(x_hi, wgu16_ref[half:, c0:c0 + tn])
                    + bgu_ref[0, :, c0:c0 + tn])

        y = bd_ref[0]
        for n0 in range(0, D_FF, tn):
            glu = jnp.minimum(up(n0), SWIGLU_LIMIT)
            lin = jnp.clip(up(D_FF + n0), -SWIGLU_LIMIT, SWIGLU_LIMIT)
            act = glu * _sigmoid(SWIGLU_ALPHA * glu) * (lin + 1.0)
            y = y + _dot(act.astype(BF16), wd16_ref[n0:n0 + tn, :])
        o_ref[...] = _pack_bf16_pair(y[:, :half], y[:, half:])

    @pl.when(jnp.logical_not(active))
    def _():
        o_ref[...] = jnp.zeros_like(o_ref)


def _experts(xs, block_exp, block_rows, w_gu, b_gu, w_down, b_down, n_blocks):
    d = D_MODEL
    dp = xs.shape[1]
    bm = EXPERT_ROWS
    grid_spec = pltpu.PrefetchScalarGridSpec(
        num_scalar_prefetch=2,
        grid=(n_blocks,),
        in_specs=[pl.BlockSpec((bm, dp), lambda i, be, nb: (i, 0)),
                  pl.BlockSpec((1, d, 2 * D_FF), lambda i, be, nb: (be[i], 0, 0)),
                  pl.BlockSpec((1, 1, 2 * D_FF), lambda i, be, nb: (be[i], 0, 0)),
                  pl.BlockSpec((1, D_FF, d), lambda i, be, nb: (be[i], 0, 0)),
                  pl.BlockSpec((1, 1, d), lambda i, be, nb: (be[i], 0, 0))],
        out_specs=pl.BlockSpec((bm, dp), lambda i, be, nb: (i, 0)),
        scratch_shapes=[pltpu.VMEM((d, 2 * D_FF), BF16), pltpu.VMEM((D_FF, d), BF16)],
    )
    return pl.pallas_call(
        functools.partial(_expert_kernel, tn=EXPERT_SLAB),
        out_shape=jax.ShapeDtypeStruct((n_blocks * bm, dp), jnp.uint32),
        grid_spec=grid_spec,
        compiler_params=_params("arbitrary"),
        name="moe_experts",
    )(block_exp, block_rows, xs, w_gu, b_gu.reshape(N_EXPERTS, 1, 2 * D_FF), w_down,
      b_down.reshape(N_EXPERTS, 1, d))


def _sc_gather_rows(table, idx):
    m = idx.shape[0]
    d = table.shape[1]
    assert m % (SC_CORES * SC_SUBCORES * 2 * SC_GATHER_ROWS) == 0, "rows must split evenly over the subcores"
    mesh = plsc.VectorSubcoreMesh(core_axis_name="c", subcore_axis_name="s")

    @functools.partial(
        pl.kernel, mesh=mesh,
        out_type=jax.ShapeDtypeStruct((m, d), table.dtype),
        scratch_types=([pltpu.VMEM((SC_GATHER_ROWS,), jnp.int32)] * 2
                       + [pltpu.VMEM((SC_GATHER_ROWS, d), table.dtype)] * 2
                       + [pltpu.SemaphoreType.DMA] * 4),
        name="sc_gather_rows",
    )
    def gather(table_hbm, idx_hbm, out_hbm, idx0, idx1, rows0, rows1, g0, g1, w0, w1):
        base, per_worker = _sc_worker_range(m)

        @pl.loop(0, per_worker // (2 * SC_GATHER_ROWS))
        def _(j):
            off0 = pl.multiple_of(base + 2 * j * SC_GATHER_ROWS, SC_GATHER_ROWS)
            off1 = off0 + SC_GATHER_ROWS
            pltpu.sync_copy(idx_hbm.at[pl.ds(off0, SC_GATHER_ROWS)], idx0)
            gather0 = pltpu.async_copy(table_hbm.at[idx0], rows0, g0)
            pltpu.sync_copy(idx_hbm.at[pl.ds(off1, SC_GATHER_ROWS)], idx1)
            gather1 = pltpu.async_copy(table_hbm.at[idx1], rows1, g1)
            gather0.wait()
            write0 = pltpu.async_copy(rows0, out_hbm.at[pl.ds(off0, SC_GATHER_ROWS)], w0)
            gather1.wait()
            write1 = pltpu.async_copy(rows1, out_hbm.at[pl.ds(off1, SC_GATHER_ROWS)], w1)
            write0.wait()
            write1.wait()

    return gather(table, idx)


def _combine_dense_kernel(y0_ref, y1_ref, y2_ref, y3_ref, gate_ref, h_ref, g_ref, b_ref, o_ref):
    gate = gate_ref[...]
    f_lo = f_hi = None
    for k, y_ref in enumerate((y0_ref, y1_ref, y2_ref, y3_ref)):
        y_lo, y_hi = _unpack_bf16_pair(y_ref[...])
        gk = gate[:, k:k + 1]
        f_lo = gk * y_lo if f_lo is None else f_lo + gk * y_lo
        f_hi = gk * y_hi if f_hi is None else f_hi + gk * y_hi
    f = jnp.concatenate([f_lo, f_hi], axis=1)
    o_ref[...] = _layer_norm(DN_ALPHA * h_ref[...] + f, g_ref[...], b_ref[...])


def _combine_dense(yg, gate, h1, ln_g, ln_b):
    t, d = h1.shape
    tm = min(ROW_TILE, t)
    nt = t // tm
    dp = yg.shape[1]
    slab = lambda k: pl.BlockSpec((tm, dp), lambda i, k=k: (k * nt + i, 0))
    return pl.pallas_call(
        _combine_dense_kernel,
        out_shape=jax.ShapeDtypeStruct((t, d), F32),
        grid=(nt,),
        in_specs=[slab(0), slab(1), slab(2), slab(3),
                  pl.BlockSpec((tm, LANES), lambda i: (i, 0)),
                  pl.BlockSpec((tm, d), lambda i: (i, 0)),
                  pl.BlockSpec((1, d), lambda i: (0, 0)),
                  pl.BlockSpec((1, d), lambda i: (0, 0))],
        out_specs=pl.BlockSpec((tm, d), lambda i: (i, 0)),
        compiler_params=_params("parallel"),
        name="moe_combine",
    )(yg, yg, yg, yg, gate, h1, ln_g.reshape(1, d), ln_b.reshape(1, d))


def kernel(x, emb_ln_g, emb_ln_b, w_in, conv_qkv, a_log, dt_bias, dn_norm_g, w_a_o, b_glu, conv_dw, b_dw, conv_ln_g, conv_ln_b, w_b_o, b_b_o, b_gate, w_out, ln1_g, ln1_b, w_router, b_router, w_gu, b_gu, w_down, b_down, ln2_g, ln2_b):
    bsz, seq, d = x.shape
    t = bsz * seq
    assert d == D_MODEL and seq % min(DELTA_ROWS, seq) == 0 and seq % 256 == 0 and t % min(ROW_TILE, t) == 0
    x2 = x.reshape(t, d)
    u_qkvz, g, gt, glu, gates = _inproj(x2, emb_ln_g, emb_ln_b, w_in[0], b_glu[0], b_gate[0], a_log[0], dt_bias[0])

    qkv = _qkv_conv(u_qkvz, conv_qkv[0], bsz, seq)
    gtp = gt.reshape(2, N_HEADS // 2, 2, t // CHUNK, CHUNK).transpose(0, 1, 3, 2, 4).reshape(N_HEADS, 2 * t)
    o_f, o_b = _delta_rule(qkv, g, gtp, bsz, seq)
    yc = _dw_conv(glu, conv_dw[0], b_dw[0], bsz, seq)

    h1, h1p, logits = _mix(o_f, o_b, u_qkvz, yc, gates, x2, emb_ln_g, emb_ln_b, dn_norm_g[0],
                           w_a_o[0].astype(BF16), conv_ln_g[0], conv_ln_b[0], w_b_o[0].astype(BF16), b_b_o[0],
                           w_out[0].astype(BF16), ln1_g[0], ln1_b[0], w_router[0], b_router[0])

    gate, pos_kmajor, block_exp, block_rows, n_blocks = _route(logits)
    xs = _sc_scatter_rows(h1p, pos_kmajor, n_blocks * EXPERT_ROWS)
    ys = _experts(xs, block_exp, block_rows, w_gu[0], b_gu[0], w_down[0], b_down[0], n_blocks)
    yg = _sc_gather_rows(ys, pos_kmajor)
    out = _combine_dense(yg, gate, h1, ln2_g[0], ln2_b[0])
    return out.reshape(bsz, seq, d)
```

```python
import functools

import jax
import jax.numpy as jnp
from jax import lax
from jax.experimental import pallas as pl
from jax.experimental.pallas import tpu as pltpu
from jax.experimental.pallas import tpu_sc as plsc

F32 = jnp.float32
BF16 = jnp.bfloat16

D_MODEL = 1024
N_HEADS = 8
HEAD_DIM = 128
WIDTH_A = N_HEADS * HEAD_DIM
SHORT_CONV = 5
CHUNK = 64
WIDTH_B = D_MODEL
DW_CONV = 31
N_EXPERTS = 32
TOP_K = 4
D_FF = D_MODEL
SWIGLU_ALPHA = 1.702
SWIGLU_LIMIT = 7.0
DN_ALPHA = 2.0 ** 0.25
LN_EPS = 1e-5
RMS_EPS = 1e-6
L2_EPS = 1e-6
LANES = 128
NEG_BIG = -1e30

ROW_TILE = 512
CONV_HALO = 16
DELTA_ROWS = 512
EXPERT_ROWS = 512
EXPERT_SLAB = 512
SC_CORES = 2
SC_SUBCORES = 16
SC_GATHER_ROWS = 64
VMEM_LIMIT = 56 * 1024 * 1024


def _params(*sem):
    return pltpu.CompilerParams(dimension_semantics=sem, vmem_limit_bytes=VMEM_LIMIT)


def _layer_norm(x, g, b):
    mu = jnp.mean(x, axis=-1, keepdims=True)
    xc = x - mu
    var = jnp.mean(xc * xc, axis=-1, keepdims=True)
    return xc * lax.rsqrt(var + LN_EPS) * g + b


def _sigmoid(x):
    return 0.5 * jnp.tanh(0.5 * x) + 0.5


def _silu(x):
    h = 0.5 * x
    return h + h * jnp.tanh(h)


def _dot(a, b):
    return jnp.dot(a, b, preferred_element_type=F32)


def _pack_bf16_pair(a, b):
    ua = lax.bitcast_convert_type(a.astype(BF16).astype(F32), jnp.uint32)
    ub = lax.bitcast_convert_type(b.astype(BF16).astype(F32), jnp.uint32)
    return (ua >> 16) | ub


def _unpack_bf16_pair(p):
    a = lax.bitcast_convert_type(p << 16, F32)
    b = lax.bitcast_convert_type(p & jnp.uint32(0xFFFF0000), F32)
    return a, b


def _split_bf16(a):
    hi = a.astype(BF16)
    return hi, (a - hi.astype(F32)).astype(BF16)


def _chunk_cumsum(x, reverse):
    rows = x.shape[0]
    pos = lax.broadcasted_iota(jnp.int32, x.shape, 0) % CHUNK
    s = 1
    while s < CHUNK:
        if reverse:
            shifted = pltpu.roll(x, rows - s, axis=0)
            x = x + jnp.where(pos < CHUNK - s, shifted, 0.0)
        else:
            shifted = pltpu.roll(x, s, axis=0)
            x = x + jnp.where(pos >= s, shifted, 0.0)
        s *= 2
    return x


def _inproj_kernel(x_ref, xp_ref, xn_ref, eg_ref, eb_ref, wq_ref, cw_ref, ws_ref, wga_ref, wgb_ref, bga_ref, bgb_ref,
                   wgt_ref, bgt_ref, alog_ref, dtb_ref, qkv_ref, z_ref, g_ref, gt_ref, glu_ref, gate_ref, up_ref,
                   *, tn, tiles_per_seq):
    tm = x_ref.shape[0]
    halo = xp_ref.shape[0]
    i = pl.program_id(0)
    pos = i % tiles_per_seq
    h32 = _layer_norm(x_ref[...], eg_ref[...], eb_ref[...])
    h = h32.astype(BF16)
    h_prev = jnp.where(pos > 0, _layer_norm(xp_ref[...], eg_ref[...], eb_ref[...]), 0.0)
    h_next = jnp.where(pos < tiles_per_seq - 1, _layer_norm(xn_ref[...], eg_ref[...], eb_ref[...]), 0.0)
    h_ext = jnp.concatenate([h_prev, h32, h_next], axis=0).astype(BF16)

    base = halo - SHORT_CONV // 2
    for n0 in range(0, qkv_ref.shape[1], tn):
        u = _dot(h_ext, wq_ref[:, n0:n0 + tn])
        for p in range(tn // HEAD_DIM):
            plane = n0 // HEAD_DIM + p
            cols = slice(n0 + p * HEAD_DIM, n0 + (p + 1) * HEAD_DIM)
            up_ref[plane] = u[:, p * HEAD_DIM:(p + 1) * HEAD_DIM]
            w = cw_ref[:, cols]
            acc = up_ref[plane, base:base + tm, :] * w[0:1, :]
            for k in range(1, SHORT_CONV):
                acc = acc + up_ref[plane, base + k:base + k + tm, :] * w[k:k + 1, :]
            y = _silu(acc)
            if plane < 2 * N_HEADS:
                inv = lax.rsqrt(jnp.sum(y * y, axis=-1, keepdims=True) + L2_EPS)
                y = y * (inv * (HEAD_DIM ** -0.5) if plane < N_HEADS else inv)
            qkv_ref[:, cols] = y.astype(qkv_ref.dtype)

    c_z = qkv_ref.shape[1]
    for n0 in range(0, z_ref.shape[1], tn):
        z_ref[:, n0:n0 + tn] = _dot(h, wq_ref[:, c_z + n0:c_z + n0 + tn]).astype(z_ref.dtype)

    us = _dot(h, ws_ref[...])
    lane = lax.broadcasted_iota(jnp.int32, us.shape, 1)
    beta = _sigmoid(us)
    xs = us + dtb_ref[...]
    softplus = jnp.maximum(xs, 0.0) + jnp.log(1.0 + jnp.exp(-jnp.abs(xs)))
    log_a = -jnp.exp(alog_ref[...]) * softplus
    g_fwd = _chunk_cumsum(log_a, reverse=False)
    g_bwd = _chunk_cumsum(log_a, reverse=True)
    gates = jnp.where(lane < 2 * N_HEADS, beta, jnp.where(lane < 3 * N_HEADS, g_fwd, g_bwd))
    g_ref[...] = gates
    gt_ref[...] = gates.T[2 * N_HEADS:4 * N_HEADS, :]

    for n0 in range(0, glu_ref.shape[1], tn):
        lin = _dot(h, wga_ref[:, n0:n0 + tn]) + bga_ref[:, n0:n0 + tn]
        gt = _dot(h, wgb_ref[:, n0:n0 + tn]) + bgb_ref[:, n0:n0 + tn]
        glu_ref[:, n0:n0 + tn] = (lin * _sigmoid(gt)).astype(glu_ref.dtype)

    for n0 in range(0, gate_ref.shape[1], tn):
        gate_ref[:, n0:n0 + tn] = _sigmoid(_dot(h, wgt_ref[:, n0:n0 + tn])
                                           + bgt_ref[:, n0:n0 + tn]).astype(gate_ref.dtype)


def _inproj(x2, emb_g, emb_b, w_in, conv_w, b_glu, b_gate, a_log, dt_bias, seq):
    t, d = x2.shape
    tm = min(ROW_TILE, seq)
    assert seq % tm == 0 and tm % CONV_HALO == 0
    tiles_per_seq = seq // tm
    hb = tm // CONV_HALO
    last_hb = t // CONV_HALO - 1
    c0 = 4 * WIDTH_A
    c1 = c0 + 4 * N_HEADS
    c2 = c1 + 2 * WIDTH_B
    wb = w_in.astype(BF16)
    pad = LANES - 4 * N_HEADS
    w_small = jnp.pad(wb[:, c0:c1], ((0, 0), (0, pad)))
    alog = jnp.pad(a_log.reshape(1, 2 * N_HEADS), ((0, 0), (2 * N_HEADS, pad)))
    dtb = jnp.pad(dt_bias.reshape(1, 2 * N_HEADS), ((0, 0), (2 * N_HEADS, pad)))
    row = lambda i: (i, 0)

    def const(shape):
        return pl.BlockSpec(shape, lambda i: (0, 0), pipeline_mode=pl.Buffered(1))

    vec = lambda a: a.reshape(1, -1)
    bf = lambda n: jax.ShapeDtypeStruct((t, n), BF16)
    return pl.pallas_call(
        functools.partial(_inproj_kernel, tn=512, tiles_per_seq=tiles_per_seq),
        out_shape=(bf(3 * WIDTH_A), bf(WIDTH_A), jax.ShapeDtypeStruct((t, LANES), F32),
                   jax.ShapeDtypeStruct((2 * N_HEADS, t), F32), bf(WIDTH_B), bf(2 * d)),
        grid=(t // tm,),
        in_specs=[pl.BlockSpec((tm, d), row),
                  pl.BlockSpec((CONV_HALO, d), lambda i: (jnp.maximum(i * hb - 1, 0), 0)),
                  pl.BlockSpec((CONV_HALO, d), lambda i: (jnp.minimum((i + 1) * hb, last_hb), 0)),
                  const((1, d)), const((1, d)),
                  const((d, c0)), const((SHORT_CONV, 3 * WIDTH_A)), const((d, LANES)),
                  const((d, WIDTH_B)), const((d, WIDTH_B)), const((1, WIDTH_B)), const((1, WIDTH_B)),
                  const((d, 2 * d)), const((1, 2 * d)), const((1, LANES)), const((1, LANES))],
        out_specs=(pl.BlockSpec((tm, 3 * WIDTH_A), row), pl.BlockSpec((tm, WIDTH_A), row),
                   pl.BlockSpec((tm, LANES), row), pl.BlockSpec((2 * N_HEADS, tm), lambda i: (0, i)),
                   pl.BlockSpec((tm, WIDTH_B), row), pl.BlockSpec((tm, 2 * d), row)),
        scratch_shapes=[pltpu.VMEM((3 * N_HEADS, tm + 2 * CONV_HALO, HEAD_DIM), F32)],
        compiler_params=_params("parallel"),
        name="inproj",
    )(x2, x2, x2, vec(emb_g), vec(emb_b), wb[:, :c0], conv_w, w_small, wb[:, c1:c1 + WIDTH_B],
      wb[:, c1 + WIDTH_B:c2], vec(b_glu[:WIDTH_B]), vec(b_glu[WIDTH_B:]), wb[:, c2:], vec(b_gate), alog, dtb)


def _conv_rows(xp_ref, w, taps, base, r0, rows):
    acc = xp_ref[base + r0:base + r0 + rows, :] * w[0:1, :]
    for k in range(1, taps):
        acc = acc + xp_ref[base + k + r0:base + k + r0 + rows, :] * w[k:k + 1, :]
    return acc


def _fill_padded(xp_ref, x_ref, pad, seq):
    zeros = jnp.zeros((pad, xp_ref.shape[1]), F32)
    xp_ref[0:pad, :] = zeros
    xp_ref[pad + seq:pad + seq + pad, :] = zeros
    xp_ref[pad:pad + seq, :] = x_ref[...].astype(F32)


def _dw_conv_kernel(x_ref, w_ref, b_ref, o_ref, xp_ref, *, seq, rows):
    pad = 16
    _fill_padded(xp_ref, x_ref, pad, seq)
    w = w_ref[...]
    for r0 in range(0, seq, rows):
        y = _conv_rows(xp_ref, w, DW_CONV, pad - DW_CONV // 2, r0, rows) + b_ref[...]
        o_ref[r0:r0 + rows, :] = y.astype(o_ref.dtype)


def _dw_conv(glu, conv_w, b_dw, bsz, seq):
    t = bsz * seq
    rows = min(256, seq)
    return pl.pallas_call(
        functools.partial(_dw_conv_kernel, seq=seq, rows=rows),
        out_shape=jax.ShapeDtypeStruct((t, WIDTH_B), BF16),
        grid=(bsz, WIDTH_B // LANES),
        in_specs=[pl.BlockSpec((seq, LANES), lambda b, j: (b, j)),
                  pl.BlockSpec((DW_CONV, LANES), lambda b, j: (0, j)),
                  pl.BlockSpec((1, LANES), lambda b, j: (0, j))],
        out_specs=pl.BlockSpec((seq, LANES), lambda b, j: (b, j)),
        scratch_shapes=[pltpu.VMEM((seq + 32, LANES), F32)],
        compiler_params=_params("parallel", "parallel"),
        name="dw_conv",
    )(glu, conv_w, b_dw.reshape(1, WIDTH_B))


def _bmm(a, b):
    return lax.dot_general(a, b, (((2,), (1,)), ((0,), (0,))), preferred_element_type=F32)


def _bmm_nt(a, b):
    return lax.dot_general(a, b, (((2,), (2,)), ((0,), (0,))), preferred_element_type=F32)


def _bmm_tn(a, b):
    return lax.dot_general(a, b, (((1,), (1,)), ((0,), (0,))), preferred_element_type=F32)


def _block_diag_rows(x, half):
    lane = lax.broadcasted_iota(jnp.int32, x.shape, 2)
    return jnp.concatenate([jnp.where(lane < half, x, 0.0), jnp.where(lane >= half, x, 0.0)], axis=1)


def _unit_tri_inverse(lmat, eye):
    def rhs(p):
        return _block_diag_rows(p, CHUNK).astype(BF16)

    x = eye - lmat
    p = _bmm(lmat.astype(BF16), rhs(lmat))
    s = 2
    while 2 * s < CHUNK:
        xp = _bmm(jnp.concatenate([x, p], axis=1).astype(BF16), rhs(p))
        x = x + xp[:, :CHUNK]
        p = xp[:, CHUNK:]
        s *= 2
    return x + _bmm(x.astype(BF16), rhs(p))


def _delta_kernel(qf_ref, kf_ref, vf_ref, gf_ref, gtpf_ref, qb_ref, kb_ref, vb_ref, gb_ref, gtpb_ref,
                  of_ref, ob_ref, s_ref, *, nc):
    @pl.when(pl.program_id(1) == 0)
    def _():
        s_ref[...] = jnp.zeros_like(s_ref)

    n_pairs = N_HEADS // 2
    n_inst = 2 * n_pairs
    pw = 2 * HEAD_DIM
    dirs = ((qf_ref, kf_ref, vf_ref, gf_ref, gtpf_ref, of_ref, False),
            (qb_ref, kb_ref, vb_ref, gb_ref, gtpb_ref, ob_ref, True))
    steps = [[d + ((nc - 1 - i) if d[6] else i,) for d in dirs] for i in range(nc)]

    ri = lax.broadcasted_iota(jnp.int32, (CHUNK, 2 * CHUNK), 0)
    ci = jnp.bitwise_and(lax.broadcasted_iota(jnp.int32, (CHUNK, 2 * CHUNK), 1), CHUNK - 1)
    inst = lax.broadcasted_iota(jnp.int32, (nc * n_inst, 1, 1), 0)
    sign = 1 - 2 * jnp.bitwise_and(jnp.right_shift(inst, n_pairs.bit_length() - 1), 1)
    rel = (ri - ci)[None] * sign
    incl = rel >= 0
    strict = rel > 0
    eye = (ri == ci).astype(F32)

    def pairs(which):
        return jnp.stack([d[which][d[7] * CHUNK:(d[7] + 1) * CHUNK, p * pw:(p + 1) * pw]
                          for st in steps for d in st for p in range(n_pairs)]).astype(F32)

    def pair_bcast(cols, width):
        return jnp.stack([jnp.concatenate([jnp.broadcast_to(cc[2 * p], (CHUNK, width)),
                                           jnp.broadcast_to(cc[2 * p + 1], (CHUNK, width))], axis=1)
                          for cc in cols for p in range(n_pairs)])

    qf = pairs(0)
    kf = pairs(1)
    vf = pairs(2)
    beta_c, g_c, glast_c = [], [], []
    for st in steps:
        for d in st:
            gblk = d[3][d[7] * CHUNK:(d[7] + 1) * CHUNK, :]
            off = N_HEADS if d[6] else 0
            last = 0 if d[6] else CHUNK - 1
            beta_c.append([gblk[:, off + hh:off + hh + 1] for hh in range(N_HEADS)])
            g_c.append([gblk[:, 2 * N_HEADS + off + hh:2 * N_HEADS + off + hh + 1] for hh in range(N_HEADS)])
            glast_c.append([gc[last:last + 1, :] for gc in g_c[-1]])
    beta = pair_bcast(beta_c, HEAD_DIM)
    eg = pair_bcast([[jnp.exp(gc) for gc in gcs] for gcs in g_c], HEAD_DIM)
    tail = pair_bcast([[jnp.exp(gl - gc) for gl, gc in zip(gls, gcs)] for gls, gcs in zip(glast_c, g_c)],
                      HEAD_DIM)
    gcol = pair_bcast(g_c, CHUNK)
    grow = jnp.stack([d[4][(n_pairs if d[6] else 0) + p:(n_pairs if d[6] else 0) + p + 1,
                           2 * d[7] * CHUNK:2 * (d[7] + 1) * CHUNK]
                      for st in steps for d in st for p in range(n_pairs)])

    decay = jnp.exp(jnp.where(incl, gcol - grow, NEG_BIG))
    kb = kf * beta
    kkqk = _bmm_nt(jnp.concatenate([kb, qf], axis=1).astype(BF16),
                   _block_diag_rows(kf, HEAD_DIM).astype(BF16))
    lmat = jnp.where(strict, kkqk[:, :CHUNK, :] * decay, 0.0)
    qk = (kkqk[:, CHUNK:, :] * decay).astype(BF16)
    tinv = _unit_tri_inverse(lmat, eye)
    rhs = jnp.concatenate([_block_diag_rows(vf * beta, HEAD_DIM),
                           _block_diag_rows(kb * eg, HEAD_DIM)], axis=2).astype(BF16)
    uw = _bmm(tinv.astype(BF16), rhs)
    u = uw[:, :, :pw]
    wq = jnp.concatenate([uw[:, :, pw:], qf * eg], axis=1).astype(BF16)
    kt = (kf * tail).astype(BF16)

    for i, st in enumerate(steps):
        sl = slice(i * n_inst, (i + 1) * n_inst)
        s_a = s_ref[:, 0]
        s_b = s_ref[:, 1]
        zero = jnp.zeros_like(s_a)
        s_bd = jnp.concatenate([jnp.concatenate([s_a, zero], axis=2),
                                jnp.concatenate([zero, s_b], axis=2)], axis=1).astype(BF16)
        ws = _bmm(wq[sl], s_bd)
        v_new = u[sl] - ws[:, :CHUNK, :]
        o = ws[:, CHUNK:, :] + _bmm(qk[sl], _block_diag_rows(v_new, HEAD_DIM).astype(BF16))
        v16 = v_new.astype(BF16)
        gl = [glast_c[2 * i + dd] for dd in range(2)]
        cd_a = jnp.stack([jnp.exp(gl[dd][2 * p]) for dd in range(2) for p in range(n_pairs)])
        cd_b = jnp.stack([jnp.exp(gl[dd][2 * p + 1]) for dd in range(2) for p in range(n_pairs)])
        s_ref[:, 0] = s_a * cd_a + _bmm_tn(kt[sl, :, :HEAD_DIM], v16[:, :, :HEAD_DIM])
        s_ref[:, 1] = s_b * cd_b + _bmm_tn(kt[sl, :, HEAD_DIM:], v16[:, :, HEAD_DIM:])
        for dd, d in enumerate(st):
            for p in range(n_pairs):
                d[5][d[7] * CHUNK:(d[7] + 1) * CHUNK, p * pw:(p + 1) * pw] = (
                    o[dd * n_pairs + p].astype(d[5].dtype))


def _delta_rule(qkv, g, gtp, bsz, seq):
    t = bsz * seq
    rows = min(DELTA_ROWS, seq)
    nblk = seq // rows

    def fwd(col):
        return lambda b, i: (b * nblk + i, col)

    def bwd(col):
        return lambda b, i: (b * nblk + nblk - 1 - i, col)

    def specs(m):
        return [pl.BlockSpec((rows, WIDTH_A), m(0)), pl.BlockSpec((rows, WIDTH_A), m(1)),
                pl.BlockSpec((rows, WIDTH_A), m(2)), pl.BlockSpec((rows, LANES), m(0)),
                pl.BlockSpec((N_HEADS, 2 * rows), lambda b, i, m=m: (0, m(0)(b, i)[0]))]

    out = jax.ShapeDtypeStruct((t, WIDTH_A), BF16)
    return pl.pallas_call(
        functools.partial(_delta_kernel, nc=rows // CHUNK),
        out_shape=(out, out),
        grid=(bsz, nblk),
        in_specs=specs(fwd) + specs(bwd),
        out_specs=(pl.BlockSpec((rows, WIDTH_A), fwd(0)), pl.BlockSpec((rows, WIDTH_A), bwd(0))),
        scratch_shapes=[pltpu.VMEM((N_HEADS, 2, HEAD_DIM, HEAD_DIM), F32)],
        compiler_params=_params("parallel", "arbitrary"),
        name="delta_rule",
    )(qkv, qkv, qkv, g, gtp, qkv, qkv, qkv, g, gtp)


def _mix_kernel(of_ref, ob_ref, z_ref, yc_ref, gate_ref, x_ref,
                eg_ref, eb_ref, ng_ref, wao_ref, cg_ref, cb_ref, wbo_ref, bbo_ref,
                wout_ref, l1g_ref, l1b_ref, wr_ref, br_ref,
                h1_ref, h1p_ref, logit_ref):
    o = of_ref[...].astype(F32) + ob_ref[...].astype(F32)
    z = z_ref[...].astype(F32)
    ng = ng_ref[...]
    parts = []
    for hh in range(N_HEADS):
        sl = slice(hh * HEAD_DIM, (hh + 1) * HEAD_DIM)
        oh = o[:, sl]
        zh = z[:, sl]
        inv = lax.rsqrt(jnp.mean(oh * oh, axis=-1, keepdims=True) + RMS_EPS)
        parts.append((oh * inv * ng * _silu(zh)).astype(BF16))
    y_a = _dot(jnp.concatenate(parts, axis=1), wao_ref[...])

    yc = _layer_norm(yc_ref[...].astype(F32), cg_ref[...], cb_ref[...])
    y_b = _dot(_silu(yc).astype(BF16), wbo_ref[...]) + bbo_ref[...]

    gates = gate_ref[...].astype(F32)
    mixed = gates[:, :D_MODEL] * y_a + gates[:, D_MODEL:] * y_b
    mix = _dot(mixed.astype(BF16), wout_ref[...])

    h0 = _layer_norm(x_ref[...], eg_ref[...], eb_ref[...])
    h1 = _layer_norm(DN_ALPHA * h0 + mix, l1g_ref[...], l1b_ref[...])
    h1_ref[...] = h1
    h1p_ref[...] = _pack_bf16_pair(h1[:, :D_MODEL // 2], h1[:, D_MODEL // 2:])
    h_hi, h_lo = _split_bf16(h1)
    p = _dot(h_hi, wr_ref[...])
    logit_ref[...] = p[:, :LANES] + p[:, LANES:] + _dot(h_lo, wr_ref[:, :LANES]) + br_ref[...]


def _mix(o_f, o_b, z, yc, gates, x2, emb_g, emb_b, norm_g, w_a_o, cg, cb, w_b_o, b_b_o,
         w_out, l1g, l1b, w_router, b_router):
    t, d = x2.shape
    tm = min(ROW_TILE, t)
    row = lambda i: (i, 0)
    const = lambda i: (0, 0)
    wr = jnp.concatenate(_split_bf16(jnp.pad(w_router, ((0, 0), (0, LANES - N_EXPERTS)))), axis=1)
    br = jnp.pad(b_router.reshape(1, N_EXPERTS), ((0, 0), (0, LANES - N_EXPERTS)), constant_values=NEG_BIG)
    vec = lambda a: a.reshape(1, -1)
    return pl.pallas_call(
        _mix_kernel,
        out_shape=(jax.ShapeDtypeStruct((t, d), F32), jax.ShapeDtypeStruct((t, d // 2), jnp.uint32),
                   jax.ShapeDtypeStruct((t, LANES), F32)),
        grid=(t // tm,),
        in_specs=[pl.BlockSpec((tm, d), row), pl.BlockSpec((tm, d), row),
                  pl.BlockSpec((tm, d), row),
                  pl.BlockSpec((tm, d), row), pl.BlockSpec((tm, 2 * d), row), pl.BlockSpec((tm, d), row),
                  pl.BlockSpec((1, d), const), pl.BlockSpec((1, d), const),
                  pl.BlockSpec((1, HEAD_DIM), const), pl.BlockSpec((d, d), const),
                  pl.BlockSpec((1, d), const), pl.BlockSpec((1, d), const),
                  pl.BlockSpec((d, d), const), pl.BlockSpec((1, d), const),
                  pl.BlockSpec((d, d), const), pl.BlockSpec((1, d), const), pl.BlockSpec((1, d), const),
                  pl.BlockSpec((d, 2 * LANES), const), pl.BlockSpec((1, LANES), const)],
        out_specs=(pl.BlockSpec((tm, d), row), pl.BlockSpec((tm, d // 2), row), pl.BlockSpec((tm, LANES), row)),
        compiler_params=_params("parallel"),
        name="mix",
    )(o_f, o_b, z, yc, gates, x2, vec(emb_g), vec(emb_b), vec(norm_g), w_a_o, vec(cg), vec(cb),
      w_b_o, vec(b_b_o), w_out, vec(l1g), vec(l1b), wr, br)


def _route_kernel(logit_ref, gate_ref, eidx_ref, rank_ref, cnt_ref, base_ref):
    @pl.when(pl.program_id(0) == 0)
    def _():
        base_ref[...] = jnp.zeros_like(base_ref)

    x = logit_ref[...].T[:N_EXPERTS, :]
    tm = x.shape[1]
    expert = lax.broadcasted_iota(jnp.int32, x.shape, 0).astype(F32)
    tok = lax.broadcasted_iota(jnp.int32, x.shape, 1)
    sel = jnp.zeros(x.shape, F32)
    vals, idxs = [], []
    for _ in range(TOP_K):
        m = jnp.max(x, axis=0, keepdims=True)
        idx = jnp.min(jnp.where(x == m, expert, float(N_EXPERTS)), axis=0, keepdims=True)
        hit = expert == idx
        sel = sel + hit.astype(F32)
        x = jnp.where(hit, -3e38, x)
        vals.append(m)
        idxs.append(idx)

    exps = [jnp.exp(v - vals[0]) for v in vals]
    denom = exps[0]
    for e in exps[1:]:
        denom = denom + e

    csum = sel
    s = 1
    while s < tm:
        csum = csum + jnp.where(tok >= s, pltpu.roll(csum, s, axis=1), 0.0)
        s *= 2
    before = base_ref[...] + csum - sel

    krow = lax.broadcasted_iota(jnp.int32, (8, tm), 0)
    gate = jnp.zeros((8, tm), F32)
    eidx = jnp.zeros((8, tm), F32)
    rank = jnp.zeros((8, tm), F32)
    for k in range(TOP_K):
        rk = jnp.sum(jnp.where(expert == idxs[k], before, 0.0), axis=0, keepdims=True)
        gate = jnp.where(krow == k, exps[k] / denom, gate)
        eidx = jnp.where(krow == k, idxs[k], eidx)
        rank = jnp.where(krow == k, rk, rank)
    eidx_ref[...] = eidx.astype(jnp.int32)
    rank_ref[...] = rank.astype(jnp.int32)
    gate_ref[...] = jnp.concatenate([gate, jnp.zeros((LANES - 8, tm), F32)], axis=0).T
    total = base_ref[...] + csum[:, tm - 1:tm]
    base_ref[...] = total
    cnt_ref[...] = jnp.broadcast_to(total, cnt_ref.shape)


def _route(logits):
    t = logits.shape[0]
    tm = min(ROW_TILE, t)
    bm = EXPERT_ROWS
    row = lambda i: (i, 0)
    gate, eidx, rank, cnt = pl.pallas_call(
        _route_kernel,
        out_shape=(jax.ShapeDtypeStruct((t, LANES), F32), jax.ShapeDtypeStruct((8, t), jnp.int32),
                   jax.ShapeDtypeStruct((8, t), jnp.int32), jax.ShapeDtypeStruct((N_EXPERTS, LANES), F32)),
        grid=(t // tm,),
        in_specs=[pl.BlockSpec((tm, LANES), row)],
        out_specs=(pl.BlockSpec((tm, LANES), row), pl.BlockSpec((8, tm), lambda i: (0, i)),
                   pl.BlockSpec((8, tm), lambda i: (0, i)), pl.BlockSpec((N_EXPERTS, LANES), lambda i: (0, 0))),
        scratch_shapes=[pltpu.VMEM((N_EXPERTS, 1), F32)],
        compiler_params=_params("arbitrary"),
        name="moe_route",
    )(logits)
    counts = cnt[:, 0].astype(jnp.int32)
    padded = (counts + bm - 1) // bm * bm
    pad_end = jnp.cumsum(padded)
    pad_start = pad_end - padded
    onehot = eidx[:TOP_K, :, None] == jnp.arange(N_EXPERTS, dtype=jnp.int32)
    pos_kmajor = (jnp.sum(jnp.where(onehot, pad_start, 0), axis=-1) + rank[:TOP_K]).reshape(-1)
    n_blocks = -(-(t * TOP_K + N_EXPERTS * (bm - 1)) // bm)
    block_start = jnp.arange(n_blocks, dtype=jnp.int32) * bm
    block_exp = jnp.minimum(jnp.sum((block_start[:, None] >= pad_end[None, :]).astype(jnp.int32), axis=1),
                            N_EXPERTS - 1)
    block_rows = jnp.clip((pad_start + counts)[block_exp] - block_start, 0, bm).astype(jnp.int32)
    return gate, pos_kmajor.astype(jnp.int32), block_exp, block_rows, n_blocks


def _sc_worker_range(n_rows):
    per_worker = n_rows // (SC_CORES * SC_SUBCORES)
    wid = lax.axis_index("s") * SC_CORES + lax.axis_index("c")
    return wid * per_worker, per_worker


def _sc_scatter_rows(src, idx, n_out):
    t, d = src.shape
    n_copies = idx.shape[0] // t
    assert t % (SC_CORES * SC_SUBCORES * SC_GATHER_ROWS) == 0, "rows must split evenly over the subcores"
    mesh = plsc.VectorSubcoreMesh(core_axis_name="c", subcore_axis_name="s")

    @functools.partial(
        pl.kernel, mesh=mesh,
        out_type=jax.ShapeDtypeStruct((n_out, d), src.dtype),
        scratch_types=([pltpu.VMEM((SC_GATHER_ROWS,), jnp.int32)] * n_copies
                       + [pltpu.VMEM((SC_GATHER_ROWS, d), src.dtype)]
                       + [pltpu.SemaphoreType.DMA] * n_copies),
        name="sc_scatter_rows",
    )
    def scatter(src_hbm, idx_hbm, out_hbm, *scratch):
        idx_v = scratch[:n_copies]
        rows_v = scratch[n_copies]
        sems = scratch[n_copies + 1:]
        base, per_worker = _sc_worker_range(t)

        @pl.loop(0, per_worker // SC_GATHER_ROWS)
        def _(j):
            off = pl.multiple_of(base + j * SC_GATHER_ROWS, SC_GATHER_ROWS)
            for k in range(n_copies):
                pltpu.sync_copy(idx_hbm.at[pl.ds(k * t + off, SC_GATHER_ROWS)], idx_v[k])
            pltpu.sync_copy(src_hbm.at[pl.ds(off, SC_GATHER_ROWS)], rows_v)
            copies = [pltpu.async_copy(rows_v, out_hbm.at[idx_v[k]], sems[k]) for k in range(n_copies)]
            for c in copies:
                c.wait()

    return scatter(src, idx)


def _expert_kernel(be_ref, nr_ref, x_ref, wgu_ref, bgu_ref, wd_ref, bd_ref, o_ref, wgu16_ref, wd16_ref, *, tn):
    i = pl.program_id(0)
    active = nr_ref[i] > 0
    new_expert = jnp.logical_or(i == 0, be_ref[i] != be_ref[jnp.maximum(i - 1, 0)])

    @pl.when(jnp.logical_and(active, new_expert))
    def _():
        wgu16_ref[...] = wgu_ref[0].astype(BF16)
        wd16_ref[...] = wd_ref[0].astype(BF16)

    @pl.when(active)
    def _():
        half = D_MODEL // 2
        row = lax.broadcasted_iota(jnp.int32, x_ref.shape, 0)
        x = jnp.where(row < nr_ref[i], x_ref[...], jnp.uint32(0))
        x_lo, x_hi = _unpack_bf16_pair(x)
        x_lo = x_lo.astype(BF16)
        x_hi = x_hi.astype(BF16)

        def up(c0):
            return (_dot(x_lo, wgu16_ref[:half, c0:c0 + tn]) + _dot(x_hi, wgu16_ref[half:, c0:c0 + tn])
                    + bgu_ref[0, :, c0:c0 + tn])

        y = bd_ref[0]
        for n0 in range(0, D_FF, tn):
            glu = jnp.minimum(up(n0), SWIGLU_LIMIT)
            lin = jnp.clip(up(D_FF + n0), -SWIGLU_LIMIT, SWIGLU_LIMIT)
            act = glu * _sigmoid(SWIGLU_ALPHA * glu) * (lin + 1.0)
            y = y + _dot(act.astype(BF16), wd16_ref[n0:n0 + tn, :])
        o_ref[...] = _pack_bf16_pair(y[:, :half], y[:, half:])

    @pl.when(jnp.logical_not(active))
    def _():
        o_ref[...] = jnp.zeros_like(o_ref)


def _experts(xs, block_exp, block_rows, w_gu, b_gu, w_down, b_down, n_blocks):
    d = D_MODEL
    dp = xs.shape[1]
    bm = EXPERT_ROWS
    grid_spec = pltpu.PrefetchScalarGridSpec(
        num_scalar_prefetch=2,
        grid=(n_blocks,),
        in_specs=[pl.BlockSpec((bm, dp), lambda i, be, nb: (i, 0)),
                  pl.BlockSpec((1, d, 2 * D_FF), lambda i, be, nb: (be[i], 0, 0)),
                  pl.BlockSpec((1, 1, 2 * D_FF), lambda i, be, nb: (be[i], 0, 0)),
                  pl.BlockSpec((1, D_FF, d), lambda i, be, nb: (be[i], 0, 0)),
                  pl.BlockSpec((1, 1, d), lambda i, be, nb: (be[i], 0, 0))],
        out_specs=pl.BlockSpec((bm, dp), lambda i, be, nb: (i, 0)),
        scratch_shapes=[pltpu.VMEM((d, 2 * D_FF), BF16), pltpu.VMEM((D_FF, d), BF16)],
    )
    return pl.pallas_call(
        functools.partial(_expert_kernel, tn=EXPERT_SLAB),
        out_shape=jax.ShapeDtypeStruct((n_blocks * bm, dp), jnp.uint32),
        grid_spec=grid_spec,
        compiler_params=_params("arbitrary"),
        name="moe_experts",
    )(block_exp, block_rows, xs, w_gu, b_gu.reshape(N_EXPERTS, 1, 2 * D_FF), w_down,
      b_down.reshape(N_EXPERTS, 1, d))


def _sc_gather_rows(table, idx):
    m = idx.shape[0]
    d = table.shape[1]
    assert m % (SC_CORES * SC_SUBCORES * 2 * SC_GATHER_ROWS) == 0, "rows must split evenly over the subcores"
    mesh = plsc.VectorSubcoreMesh(core_axis_name="c", subcore_axis_name="s")

    @functools.partial(
        pl.kernel, mesh=mesh,
        out_type=jax.ShapeDtypeStruct((m, d), table.dtype),
        scratch_types=([pltpu.VMEM((SC_GATHER_ROWS,), jnp.int32)] * 2
                       + [pltpu.VMEM((SC_GATHER_ROWS, d), table.dtype)] * 2
                       + [pltpu.SemaphoreType.DMA] * 4),
        name="sc_gather_rows",
    )
    def gather(table_hbm, idx_hbm, out_hbm, idx0, idx1, rows0, rows1, g0, g1, w0, w1):
        base, per_worker = _sc_worker_range(m)

        @pl.loop(0, per_worker // (2 * SC_GATHER_ROWS))
        def _(j):
            off0 = pl.multiple_of(base + 2 * j * SC_GATHER_ROWS, SC_GATHER_ROWS)
            off1 = off0 + SC_GATHER_ROWS
            pltpu.sync_copy(idx_hbm.at[pl.ds(off0, SC_GATHER_ROWS)], idx0)
            gather0 = pltpu.async_copy(table_hbm.at[idx0], rows0, g0)
            pltpu.sync_copy(idx_hbm.at[pl.ds(off1, SC_GATHER_ROWS)], idx1)
            gather1 = pltpu.async_copy(table_hbm.at[idx1], rows1, g1)
            gather0.wait()
            write0 = pltpu.async_copy(rows0, out_hbm.at[pl.ds(off0, SC_GATHER_ROWS)], w0)
            gather1.wait()
            write1 = pltpu.async_copy(rows1, out_hbm.at[pl.ds(off1, SC_GATHER_ROWS)], w1)
            write0.wait()
            write1.wait()

    return gather(table, idx)


def _combine_dense_kernel(y0_ref, y1_ref, y2_ref, y3_ref, gate_ref, h_ref, g_ref, b_ref, o_ref):
    gate = gate_ref[...]
    f_lo = f_hi = None
    for k, y_ref in enumerate((y0_ref, y1_ref, y2_ref, y3_ref)):
        y_lo, y_hi = _unpack_bf16_pair(y_ref[...])
        gk = gate[:, k:k + 1]
        f_lo = gk * y_lo if f_lo is None else f_lo + gk * y_lo
        f_hi = gk * y_hi if f_hi is None else f_hi + gk * y_hi
    f = jnp.concatenate([f_lo, f_hi], axis=1)
    o_ref[...] = _layer_norm(DN_ALPHA * h_ref[...] + f, g_ref[...], b_ref[...])


def _combine_dense(yg, gate, h1, ln_g, ln_b):
    t, d = h1.shape
    tm = min(ROW_TILE, t)
    nt = t // tm
    dp = yg.shape[1]
    slab = lambda k: pl.BlockSpec((tm, dp), lambda i, k=k: (k * nt + i, 0))
    return pl.pallas_call(
        _combine_dense_kernel,
        out_shape=jax.ShapeDtypeStruct((t, d), F32),
        grid=(nt,),
        in_specs=[slab(0), slab(1), slab(2), slab(3),
                  pl.BlockSpec((tm, LANES), lambda i: (i, 0)),
                  pl.BlockSpec((tm, d), lambda i: (i, 0)),
                  pl.BlockSpec((1, d), lambda i: (0, 0)),
                  pl.BlockSpec((1, d), lambda i: (0, 0))],
        out_specs=pl.BlockSpec((tm, d), lambda i: (i, 0)),
        compiler_params=_params("parallel"),
        name="moe_combine",
    )(yg, yg, yg, yg, gate, h1, ln_g.reshape(1, d), ln_b.reshape(1, d))


def kernel(x, emb_ln_g, emb_ln_b, w_in, conv_qkv, a_log, dt_bias, dn_norm_g, w_a_o, b_glu, conv_dw, b_dw, conv_ln_g, conv_ln_b, w_b_o, b_b_o, b_gate, w_out, ln1_g, ln1_b, w_router, b_router, w_gu, b_gu, w_down, b_down, ln2_g, ln2_b):
    bsz, seq, d = x.shape
    t = bsz * seq
    assert d == D_MODEL and seq % min(DELTA_ROWS, seq) == 0 and seq % 256 == 0 and t % min(ROW_TILE, t) == 0
    x2 = x.reshape(t, d)
    qkv, z, g, gt, glu, gates = _inproj(x2, emb_ln_g, emb_ln_b, w_in[0], conv_qkv[0], b_glu[0], b_gate[0], a_log[0],
                                        dt_bias[0], seq)
    gtp = gt.reshape(2, N_HEADS // 2, 2, t // CHUNK, CHUNK).transpose(0, 1, 3, 2, 4).reshape(N_HEADS, 2 * t)
    o_f, o_b = _delta_rule(qkv, g, gtp, bsz, seq)
    yc = _dw_conv(glu, conv_dw[0], b_dw[0], bsz, seq)

    h1, h1p, logits = _mix(o_f, o_b, z, yc, gates, x2, emb_ln_g, emb_ln_b, dn_norm_g[0],
                           w_a_o[0].astype(BF16), conv_ln_g[0], conv_ln_b[0], w_b_o[0].astype(BF16), b_b_o[0],
                           w_out[0].astype(BF16), ln1_g[0], ln1_b[0], w_router[0], b_router[0])

    gate, pos_kmajor, block_exp, block_rows, n_blocks = _route(logits)
    xs = _sc_scatter_rows(h1p, pos_kmajor, n_blocks * EXPERT_ROWS)
    ys = _experts(xs, block_exp, block_rows, w_gu[0], b_gu[0], w_down[0], b_down[0], n_blocks)
    yg = _sc_gather_rows(ys, pos_kmajor)
    out = _combine_dense(yg, gate, h1, ln2_g[0], ln2_b[0])
    return out.reshape(bsz, seq, d)
```

```python
import functools

import jax
import jax.numpy as jnp
from jax import lax
from jax.experimental import pallas as pl
from jax.experimental.pallas import tpu as pltpu
from jax.experimental.pallas import tpu_sc as plsc

F32 = jnp.float32
BF16 = jnp.bfloat16

D_MODEL = 1024
N_HEADS = 8
HEAD_DIM = 128
WIDTH_A = N_HEADS * HEAD_DIM
SHORT_CONV = 5
CHUNK = 64
WIDTH_B = D_MODEL
DW_CONV = 31
N_EXPERTS = 32
TOP_K = 4
D_FF = D_MODEL
SWIGLU_ALPHA = 1.702
SWIGLU_LIMIT = 7.0
DN_ALPHA = 2.0 ** 0.25
LN_EPS = 1e-5
RMS_EPS = 1e-6
L2_EPS = 1e-6
LANES = 128
NEG_BIG = -1e30

ROW_TILE = 512
CONV_HALO = 16
DELTA_ROWS = 512
EXPERT_ROWS = 512
EXPERT_SLAB = 512
SC_CORES = 2
SC_SUBCORES = 16
SC_GATHER_ROWS = 64
VMEM_LIMIT = 56 * 1024 * 1024


def _params(*sem):
    return pltpu.CompilerParams(dimension_semantics=sem, vmem_limit_bytes=VMEM_LIMIT)


def _layer_norm(x, g, b):
    mu = jnp.mean(x, axis=-1, keepdims=True)
    xc = x - mu
    var = jnp.mean(xc * xc, axis=-1, keepdims=True)
    return xc * lax.rsqrt(var + LN_EPS) * g + b


def _sigmoid(x):
    return 0.5 * jnp.tanh(0.5 * x) + 0.5


def _silu(x):
    h = 0.5 * x
    return h + h * jnp.tanh(h)


def _dot(a, b):
    return jnp.dot(a, b, preferred_element_type=F32)


def _pack_bf16_pair(a, b):
    ua = lax.bitcast_convert_type(a.astype(BF16).astype(F32), jnp.uint32)
    ub = lax.bitcast_convert_type(b.astype(BF16).astype(F32), jnp.uint32)
    return (ua >> 16) | ub


def _unpack_bf16_pair(p):
    a = lax.bitcast_convert_type(p << 16, F32)
    b = lax.bitcast_convert_type(p & jnp.uint32(0xFFFF0000), F32)
    return a, b


def _split_bf16(a):
    hi = a.astype(BF16)
    return hi, (a - hi.astype(F32)).astype(BF16)


def _chunk_cumsum(x, reverse):
    rows = x.shape[0]
    pos = lax.broadcasted_iota(jnp.int32, x.shape, 0) % CHUNK
    s = 1
    while s < CHUNK:
        if reverse:
            shifted = pltpu.roll(x, rows - s, axis=0)
            x = x + jnp.where(pos < CHUNK - s, shifted, 0.0)
        else:
            shifted = pltpu.roll(x, s, axis=0)
            x = x + jnp.where(pos >= s, shifted, 0.0)
        s *= 2
    return x


def _inproj_kernel(x_ref, xp_ref, xn_ref, eg_ref, eb_ref, wq_ref, cw_ref, ws_ref, wga_ref, wgb_ref, bga_ref, bgb_ref,
                   wgt_ref, bgt_ref, alog_ref, dtb_ref, qkv_ref, z_ref, g_ref, gt_ref, glu_ref, gate_ref, up_ref,
                   *, tn, tiles_per_seq):
    tm = x_ref.shape[0]
    halo = xp_ref.shape[0]
    i = pl.program_id(0)
    pos = i % tiles_per_seq
    h32 = _layer_norm(x_ref[...], eg_ref[...], eb_ref[...])
    h = h32.astype(BF16)
    h_prev = jnp.where(pos > 0, _layer_norm(xp_ref[...], eg_ref[...], eb_ref[...]), 0.0)
    h_next = jnp.where(pos < tiles_per_seq - 1, _layer_norm(xn_ref[...], eg_ref[...], eb_ref[...]), 0.0)
    h_ext = jnp.concatenate([h_prev, h32, h_next], axis=0).astype(BF16)

    base = halo - SHORT_CONV // 2
    for n0 in range(0, qkv_ref.shape[1], tn):
        u = _dot(h_ext, wq_ref[:, n0:n0 + tn])
        for p in range(tn // HEAD_DIM):
            plane = n0 // HEAD_DIM + p
            cols = slice(n0 + p * HEAD_DIM, n0 + (p + 1) * HEAD_DIM)
            up_ref[plane] = u[:, p * HEAD_DIM:(p + 1) * HEAD_DIM]
            w = cw_ref[:, cols]
            acc = up_ref[plane, base:base + tm, :] * w[0:1, :]
            for k in range(1, SHORT_CONV):
                acc = acc + up_ref[plane, base + k:base + k + tm, :] * w[k:k + 1, :]
            y = _silu(acc)
            if plane < 2 * N_HEADS:
                inv = lax.rsqrt(jnp.sum(y * y, axis=-1, keepdims=True) + L2_EPS)
                y = y * (inv * (HEAD_DIM ** -0.5) if plane < N_HEADS else inv)
            qkv_ref[:, cols] = y.astype(qkv_ref.dtype)

    c_z = qkv_ref.shape[1]
    for n0 in range(0, z_ref.shape[1], tn):
        z_ref[:, n0:n0 + tn] = _dot(h, wq_ref[:, c_z + n0:c_z + n0 + tn]).astype(z_ref.dtype)

    us = _dot(h, ws_ref[...])
    lane = lax.broadcasted_iota(jnp.int32, us.shape, 1)
    beta = _sigmoid(us)
    xs = us + dtb_ref[...]
    softplus = jnp.maximum(xs, 0.0) + jnp.log(1.0 + jnp.exp(-jnp.abs(xs)))
    log_a = -jnp.exp(alog_ref[...]) * softplus
    g_fwd = _chunk_cumsum(log_a, reverse=False)
    g_bwd = _chunk_cumsum(log_a, reverse=True)
    odd_group = jnp.bitwise_and(jnp.right_shift(lane, (N_HEADS // 2).bit_length() - 1), 1) == 1
    bwd = ((lane >= 3 * N_HEADS) & (lane < 4 * N_HEADS)) | ((lane >= 4 * N_HEADS) & odd_group)
    gates = jnp.where(lane < 2 * N_HEADS, beta, jnp.where(bwd, g_bwd, g_fwd))
    g_ref[...] = gates
    g_t = gates.T
    first = g_t[4 * N_HEADS:5 * N_HEADS, :]
    second = g_t[5 * N_HEADS:6 * N_HEADS, :]
    gt_ref[...] = jnp.concatenate([part[:, c * CHUNK:(c + 1) * CHUNK]
                                   for c in range(tm // CHUNK) for part in (first, second)], axis=1)

    for n0 in range(0, glu_ref.shape[1], tn):
        lin = _dot(h, wga_ref[:, n0:n0 + tn]) + bga_ref[:, n0:n0 + tn]
        gt = _dot(h, wgb_ref[:, n0:n0 + tn]) + bgb_ref[:, n0:n0 + tn]
        glu_ref[:, n0:n0 + tn] = (lin * _sigmoid(gt)).astype(glu_ref.dtype)

    for n0 in range(0, gate_ref.shape[1], tn):
        gate_ref[:, n0:n0 + tn] = _sigmoid(_dot(h, wgt_ref[:, n0:n0 + tn])
                                           + bgt_ref[:, n0:n0 + tn]).astype(gate_ref.dtype)


def _inproj(x2, emb_g, emb_b, w_in, conv_w, b_glu, b_gate, a_log, dt_bias, seq):
    t, d = x2.shape
    tm = min(ROW_TILE, seq)
    assert seq % tm == 0 and tm % CONV_HALO == 0
    tiles_per_seq = seq // tm
    hb = tm // CONV_HALO
    last_hb = t // CONV_HALO - 1
    c0 = 4 * WIDTH_A
    c1 = c0 + 4 * N_HEADS
    c2 = c1 + 2 * WIDTH_B
    wb = w_in.astype(BF16)
    pad = LANES - 6 * N_HEADS
    perm = jnp.array([dd * N_HEADS + 2 * p + hh for hh in range(2) for dd in range(2) for p in range(N_HEADS // 2)],
                     jnp.int32)
    twice = lambda a: jnp.concatenate([a, a[:, perm]], axis=1)
    w_small = jnp.pad(jnp.concatenate([wb[:, c0:c0 + 2 * N_HEADS], twice(wb[:, c0 + 2 * N_HEADS:c1])], axis=1),
                      ((0, 0), (0, pad)))
    alog = jnp.pad(twice(a_log.reshape(1, 2 * N_HEADS)), ((0, 0), (2 * N_HEADS, pad)))
    dtb = jnp.pad(twice(dt_bias.reshape(1, 2 * N_HEADS)), ((0, 0), (2 * N_HEADS, pad)))
    row = lambda i: (i, 0)

    def const(shape):
        return pl.BlockSpec(shape, lambda i: (0, 0), pipeline_mode=pl.Buffered(1))

    vec = lambda a: a.reshape(1, -1)
    bf = lambda n: jax.ShapeDtypeStruct((t, n), BF16)
    return pl.pallas_call(
        functools.partial(_inproj_kernel, tn=512, tiles_per_seq=tiles_per_seq),
        out_shape=(bf(3 * WIDTH_A), bf(WIDTH_A), jax.ShapeDtypeStruct((t, LANES), F32),
                   jax.ShapeDtypeStruct((N_HEADS, 2 * t), F32), bf(WIDTH_B), bf(2 * d)),
        grid=(t // tm,),
        in_specs=[pl.BlockSpec((tm, d), row),
                  pl.BlockSpec((CONV_HALO, d), lambda i: (jnp.maximum(i * hb - 1, 0), 0)),
                  pl.BlockSpec((CONV_HALO, d), lambda i: (jnp.minimum((i + 1) * hb, last_hb), 0)),
                  const((1, d)), const((1, d)),
                  const((d, c0)), const((SHORT_CONV, 3 * WIDTH_A)), const((d, LANES)),
                  const((d, WIDTH_B)), const((d, WIDTH_B)), const((1, WIDTH_B)), const((1, WIDTH_B)),
                  const((d, 2 * d)), const((1, 2 * d)), const((1, LANES)), const((1, LANES))],
        out_specs=(pl.BlockSpec((tm, 3 * WIDTH_A), row), pl.BlockSpec((tm, WIDTH_A), row),
                   pl.BlockSpec((tm, LANES), row), pl.BlockSpec((N_HEADS, 2 * tm), lambda i: (0, i)),
                   pl.BlockSpec((tm, WIDTH_B), row), pl.BlockSpec((tm, 2 * d), row)),
        scratch_shapes=[pltpu.VMEM((3 * N_HEADS, tm + 2 * CONV_HALO, HEAD_DIM), F32)],
        compiler_params=_params("parallel"),
        name="inproj",
    )(x2, x2, x2, vec(emb_g), vec(emb_b), wb[:, :c0], conv_w, w_small, wb[:, c1:c1 + WIDTH_B],
      wb[:, c1 + WIDTH_B:c2], vec(b_glu[:WIDTH_B]), vec(b_glu[WIDTH_B:]), wb[:, c2:], vec(b_gate), alog, dtb)


def _conv_rows(xp_ref, w, taps, base, r0, rows):
    acc = xp_ref[base + r0:base + r0 + rows, :] * w[0:1, :]
    for k in range(1, taps):
        acc = acc + xp_ref[base + k + r0:base + k + r0 + rows, :] * w[k:k + 1, :]
    return acc


def _fill_padded(xp_ref, x_ref, pad, seq):
    zeros = jnp.zeros((pad, xp_ref.shape[1]), F32)
    xp_ref[0:pad, :] = zeros
    xp_ref[pad + seq:pad + seq + pad, :] = zeros
    xp_ref[pad:pad + seq, :] = x_ref[...].astype(F32)


def _dw_conv_kernel(x_ref, w_ref, b_ref, o_ref, xp_ref, *, seq, rows):
    pad = 16
    _fill_padded(xp_ref, x_ref, pad, seq)
    w = w_ref[...]
    for r0 in range(0, seq, rows):
        y = _conv_rows(xp_ref, w, DW_CONV, pad - DW_CONV // 2, r0, rows) + b_ref[...]
        o_ref[r0:r0 + rows, :] = y.astype(o_ref.dtype)


def _dw_conv(glu, conv_w, b_dw, bsz, seq):
    t = bsz * seq
    rows = min(256, seq)
    return pl.pallas_call(
        functools.partial(_dw_conv_kernel, seq=seq, rows=rows),
        out_shape=jax.ShapeDtypeStruct((t, WIDTH_B), BF16),
        grid=(bsz, WIDTH_B // LANES),
        in_specs=[pl.BlockSpec((seq, LANES), lambda b, j: (b, j)),
                  pl.BlockSpec((DW_CONV, LANES), lambda b, j: (0, j)),
                  pl.BlockSpec((1, LANES), lambda b, j: (0, j))],
        out_specs=pl.BlockSpec((seq, LANES), lambda b, j: (b, j)),
        scratch_shapes=[pltpu.VMEM((seq + 32, LANES), F32)],
        compiler_params=_params("parallel", "parallel"),
        name="dw_conv",
    )(glu, conv_w, b_dw.reshape(1, WIDTH_B))


def _bmm(a, b):
    return lax.dot_general(a, b, (((2,), (1,)), ((0,), (0,))), preferred_element_type=F32)


def _bmm_nt(a, b):
    return lax.dot_general(a, b, (((2,), (2,)), ((0,), (0,))), preferred_element_type=F32)


def _bmm_tn(a, b):
    return lax.dot_general(a, b, (((1,), (1,)), ((0,), (0,))), preferred_element_type=F32)


def _block_diag_rows(x, half):
    lane = lax.broadcasted_iota(jnp.int32, x.shape, 2)
    return jnp.concatenate([jnp.where(lane < half, x, 0.0), jnp.where(lane >= half, x, 0.0)], axis=1)


def _unit_tri_inverse(lmat, eye):
    def rhs(p):
        return _block_diag_rows(p, CHUNK).astype(BF16)

    x = eye - lmat
    p = _bmm(lmat.astype(BF16), rhs(lmat))
    s = 2
    while 2 * s < CHUNK:
        xp = _bmm(jnp.concatenate([x, p], axis=1).astype(BF16), rhs(p))
        x = x + xp[:, :CHUNK]
        p = xp[:, CHUNK:]
        s *= 2
    return x + _bmm(x.astype(BF16), rhs(p))


def _delta_kernel(qf_ref, kf_ref, vf_ref, gf_ref, gtpf_ref, qb_ref, kb_ref, vb_ref, gb_ref, gtpb_ref,
                  of_ref, ob_ref, s_ref, *, nc):
    @pl.when(pl.program_id(1) == 0)
    def _():
        s_ref[...] = jnp.zeros_like(s_ref)

    n_pairs = N_HEADS // 2
    n_inst = 2 * n_pairs
    pw = 2 * HEAD_DIM
    dirs = ((qf_ref, kf_ref, vf_ref, gf_ref, gtpf_ref, of_ref, False),
            (qb_ref, kb_ref, vb_ref, gb_ref, gtpb_ref, ob_ref, True))
    steps = [[d + ((nc - 1 - i) if d[6] else i,) for d in dirs] for i in range(nc)]

    ri = lax.broadcasted_iota(jnp.int32, (CHUNK, 2 * CHUNK), 0)
    ci = jnp.bitwise_and(lax.broadcasted_iota(jnp.int32, (CHUNK, 2 * CHUNK), 1), CHUNK - 1)
    inst = lax.broadcasted_iota(jnp.int32, (nc * n_inst, 1, 1), 0)
    sign = 1 - 2 * jnp.bitwise_and(jnp.right_shift(inst, n_pairs.bit_length() - 1), 1)
    rel = (ri - ci)[None] * sign
    incl = rel >= 0
    strict = rel > 0
    eye = (ri == ci).astype(F32)

    def pairs(which):
        return jnp.stack([d[which][d[7] * CHUNK:(d[7] + 1) * CHUNK, p * pw:(p + 1) * pw]
                          for st in steps for d in st for p in range(n_pairs)]).astype(F32)

    def pair_bcast(cols, width):
        return jnp.stack([jnp.concatenate([jnp.broadcast_to(cc[2 * p], (CHUNK, width)),
                                           jnp.broadcast_to(cc[2 * p + 1], (CHUNK, width))], axis=1)
                          for cc in cols for p in range(n_pairs)])

    qf = pairs(0)
    kf = pairs(1)
    vf = pairs(2)
    beta_c, g_c, glast_c = [], [], []
    for st in steps:
        for d in st:
            gblk = d[3][d[7] * CHUNK:(d[7] + 1) * CHUNK, :]
            off = N_HEADS if d[6] else 0
            last = 0 if d[6] else CHUNK - 1
            beta_c.append([gblk[:, off + hh:off + hh + 1] for hh in range(N_HEADS)])
            g_c.append([gblk[:, 2 * N_HEADS + off + hh:2 * N_HEADS + off + hh + 1] for hh in range(N_HEADS)])
            glast_c.append([gc[last:last + 1, :] for gc in g_c[-1]])
    beta = pair_bcast(beta_c, HEAD_DIM)
    eg = pair_bcast([[jnp.exp(gc) for gc in gcs] for gcs in g_c], HEAD_DIM)
    tail = pair_bcast([[jnp.exp(gl - gc) for gl, gc in zip(gls, gcs)] for gls, gcs in zip(glast_c, g_c)],
                      HEAD_DIM)
    gcol = pair_bcast(g_c, CHUNK)
    grow = jnp.stack([d[4][(n_pairs if d[6] else 0) + p:(n_pairs if d[6] else 0) + p + 1,
                           2 * d[7] * CHUNK:2 * (d[7] + 1) * CHUNK]
                      for st in steps for d in st for p in range(n_pairs)])

    decay = jnp.exp(jnp.where(incl, gcol - grow, NEG_BIG))
    kb = kf * beta
    kkqk = _bmm_nt(jnp.concatenate([kb, qf], axis=1).astype(BF16),
                   _block_diag_rows(kf, HEAD_DIM).astype(BF16))
    lmat = jnp.where(strict, kkqk[:, :CHUNK, :] * decay, 0.0)
    qk = (kkqk[:, CHUNK:, :] * decay).astype(BF16)
    tinv = _unit_tri_inverse(lmat, eye)
    rhs = jnp.concatenate([_block_diag_rows(vf * beta, HEAD_DIM),
                           _block_diag_rows(kb * eg, HEAD_DIM)], axis=2).astype(BF16)
    uw = _bmm(tinv.astype(BF16), rhs)
    u = uw[:, :, :pw]
    wq = jnp.concatenate([uw[:, :, pw:], qf * eg], axis=1).astype(BF16)
    kt = (kf * tail).astype(BF16)

    for i, st in enumerate(steps):
        sl = slice(i * n_inst, (i + 1) * n_inst)
        s_a = s_ref[:, 0]
        s_b = s_ref[:, 1]
        zero = jnp.zeros_like(s_a)
        s_bd = jnp.concatenate([jnp.concatenate([s_a, zero], axis=2),
                                jnp.concatenate([zero, s_b], axis=2)], axis=1).astype(BF16)
        ws = _bmm(wq[sl], s_bd)
        v_new = u[sl] - ws[:, :CHUNK, :]
        o = ws[:, CHUNK:, :] + _bmm(qk[sl], _block_diag_rows(v_new, HEAD_DIM).astype(BF16))
        v16 = v_new.astype(BF16)
        gl = [glast_c[2 * i + dd] for dd in range(2)]
        cd_a = jnp.stack([jnp.exp(gl[dd][2 * p]) for dd in range(2) for p in range(n_pairs)])
        cd_b = jnp.stack([jnp.exp(gl[dd][2 * p + 1]) for dd in range(2) for p in range(n_pairs)])
        s_ref[:, 0] = s_a * cd_a + _bmm_tn(kt[sl, :, :HEAD_DIM], v16[:, :, :HEAD_DIM])
        s_ref[:, 1] = s_b * cd_b + _bmm_tn(kt[sl, :, HEAD_DIM:], v16[:, :, HEAD_DIM:])
        for dd, d in enumerate(st):
            for p in range(n_pairs):
                d[5][d[7] * CHUNK:(d[7] + 1) * CHUNK, p * pw:(p + 1) * pw] = (
                    o[dd * n_pairs + p].astype(d[5].dtype))


def _delta_rule(qkv, g, gtp, bsz, seq):
    t = bsz * seq
    rows = min(DELTA_ROWS, seq)
    nblk = seq // rows

    def fwd(col):
        return lambda b, i: (b * nblk + i, col)

    def bwd(col):
        return lambda b, i: (b * nblk + nblk - 1 - i, col)

    def specs(m):
        return [pl.BlockSpec((rows, WIDTH_A), m(0)), pl.BlockSpec((rows, WIDTH_A), m(1)),
                pl.BlockSpec((rows, WIDTH_A), m(2)), pl.BlockSpec((rows, LANES), m(0)),
                pl.BlockSpec((N_HEADS, 2 * rows), lambda b, i, m=m: (0, m(0)(b, i)[0]))]

    out = jax.ShapeDtypeStruct((t, WIDTH_A), BF16)
    return pl.pallas_call(
        functools.partial(_delta_kernel, nc=rows // CHUNK),
        out_shape=(out, out),
        grid=(bsz, nblk),
        in_specs=specs(fwd) + specs(bwd),
        out_specs=(pl.BlockSpec((rows, WIDTH_A), fwd(0)), pl.BlockSpec((rows, WIDTH_A), bwd(0))),
        scratch_shapes=[pltpu.VMEM((N_HEADS, 2, HEAD_DIM, HEAD_DIM), F32)],
        compiler_params=_params("parallel", "arbitrary"),
        name="delta_rule",
    )(qkv, qkv, qkv, g, gtp, qkv, qkv, qkv, g, gtp)


def _mix_kernel(of_ref, ob_ref, z_ref, yc_ref, gate_ref, x_ref,
                eg_ref, eb_ref, ng_ref, wao_ref, cg_ref, cb_ref, wbo_ref, bbo_ref,
                wout_ref, l1g_ref, l1b_ref, wr_ref, br_ref,
                h1_ref, h1p_ref, logit_ref):
    o = of_ref[...].astype(F32) + ob_ref[...].astype(F32)
    z = z_ref[...].astype(F32)
    ng = ng_ref[...]
    parts = []
    for hh in range(N_HEADS):
        sl = slice(hh * HEAD_DIM, (hh + 1) * HEAD_DIM)
        oh = o[:, sl]
        zh = z[:, sl]
        inv = lax.rsqrt(jnp.mean(oh * oh, axis=-1, keepdims=True) + RMS_EPS)
        parts.append((oh * inv * ng * _silu(zh)).astype(BF16))
    y_a = _dot(jnp.concatenate(parts, axis=1), wao_ref[...])

    yc = _layer_norm(yc_ref[...].astype(F32), cg_ref[...], cb_ref[...])
    y_b = _dot(_silu(yc).astype(BF16), wbo_ref[...]) + bbo_ref[...]

    gates = gate_ref[...].astype(F32)
    mixed = gates[:, :D_MODEL] * y_a + gates[:, D_MODEL:] * y_b
    mix = _dot(mixed.astype(BF16), wout_ref[...])

    h0 = _layer_norm(x_ref[...], eg_ref[...], eb_ref[...])
    h1 = _layer_norm(DN_ALPHA * h0 + mix, l1g_ref[...], l1b_ref[...])
    h1_ref[...] = h1
    h1p_ref[...] = _pack_bf16_pair(h1[:, :D_MODEL // 2], h1[:, D_MODEL // 2:])
    h_hi, h_lo = _split_bf16(h1)
    p = _dot(h_hi, wr_ref[...])
    logit_ref[...] = p[:, :LANES] + p[:, LANES:] + _dot(h_lo, wr_ref[:, :LANES]) + br_ref[...]


def _mix(o_f, o_b, z, yc, gates, x2, emb_g, emb_b, norm_g, w_a_o, cg, cb, w_b_o, b_b_o,
         w_out, l1g, l1b, w_router, b_router):
    t, d = x2.shape
    tm = min(ROW_TILE, t)
    row = lambda i: (i, 0)
    const = lambda i: (0, 0)
    wr = jnp.concatenate(_split_bf16(jnp.pad(w_router, ((0, 0), (0, LANES - N_EXPERTS)))), axis=1)
    br = jnp.pad(b_router.reshape(1, N_EXPERTS), ((0, 0), (0, LANES - N_EXPERTS)), constant_values=NEG_BIG)
    vec = lambda a: a.reshape(1, -1)
    return pl.pallas_call(
        _mix_kernel,
        out_shape=(jax.ShapeDtypeStruct((t, d), F32), jax.ShapeDtypeStruct((t, d // 2), jnp.uint32),
                   jax.ShapeDtypeStruct((t, LANES), F32)),
        grid=(t // tm,),
        in_specs=[pl.BlockSpec((tm, d), row), pl.BlockSpec((tm, d), row),
                  pl.BlockSpec((tm, d), row),
                  pl.BlockSpec((tm, d), row), pl.BlockSpec((tm, 2 * d), row), pl.BlockSpec((tm, d), row),
                  pl.BlockSpec((1, d), const), pl.BlockSpec((1, d), const),
                  pl.BlockSpec((1, HEAD_DIM), const), pl.BlockSpec((d, d), const),
                  pl.BlockSpec((1, d), const), pl.BlockSpec((1, d), const),
                  pl.BlockSpec((d, d), const), pl.BlockSpec((1, d), const),
                  pl.BlockSpec((d, d), const), pl.BlockSpec((1, d), const), pl.BlockSpec((1, d), const),
                  pl.BlockSpec((d, 2 * LANES), const), pl.BlockSpec((1, LANES), const)],
        out_specs=(pl.BlockSpec((tm, d), row), pl.BlockSpec((tm, d // 2), row), pl.BlockSpec((tm, LANES), row)),
        compiler_params=_params("parallel"),
        name="mix",
    )(o_f, o_b, z, yc, gates, x2, vec(emb_g), vec(emb_b), vec(norm_g), w_a_o, vec(cg), vec(cb),
      w_b_o, vec(b_b_o), w_out, vec(l1g), vec(l1b), wr, br)


def _route_kernel(logit_ref, gate_ref, eidx_ref, rank_ref, start_ref, blk_ref, base_ref, *, bm):
    @pl.when(pl.program_id(0) == 0)
    def _():
        base_ref[...] = jnp.zeros_like(base_ref)

    x = logit_ref[...].T[:N_EXPERTS, :]
    tm = x.shape[1]
    expert = lax.broadcasted_iota(jnp.int32, x.shape, 0).astype(F32)
    tok = lax.broadcasted_iota(jnp.int32, x.shape, 1)
    sel = jnp.zeros(x.shape, F32)
    vals, idxs = [], []
    for _ in range(TOP_K):
        m = jnp.max(x, axis=0, keepdims=True)
        idx = jnp.min(jnp.where(x == m, expert, float(N_EXPERTS)), axis=0, keepdims=True)
        hit = expert == idx
        sel = sel + hit.astype(F32)
        x = jnp.where(hit, -3e38, x)
        vals.append(m)
        idxs.append(idx)

    exps = [jnp.exp(v - vals[0]) for v in vals]
    denom = exps[0]
    for e in exps[1:]:
        denom = denom + e

    csum = sel
    s = 1
    while s < tm:
        csum = csum + jnp.where(tok >= s, pltpu.roll(csum, s, axis=1), 0.0)
        s *= 2
    before = base_ref[...] + csum - sel

    krow = lax.broadcasted_iota(jnp.int32, (8, tm), 0)
    gate = jnp.zeros((8, tm), F32)
    eidx = jnp.zeros((8, tm), F32)
    rank = jnp.zeros((8, tm), F32)
    for k in range(TOP_K):
        rk = jnp.sum(jnp.where(expert == idxs[k], before, 0.0), axis=0, keepdims=True)
        gate = jnp.where(krow == k, exps[k] / denom, gate)
        eidx = jnp.where(krow == k, idxs[k], eidx)
        rank = jnp.where(krow == k, rk, rank)
    eidx_ref[...] = eidx.astype(jnp.int32)
    rank_ref[...] = rank.astype(jnp.int32)
    gate_ref[...] = jnp.concatenate([gate, jnp.zeros((LANES - 8, tm), F32)], axis=0).T
    total = base_ref[...] + csum[:, tm - 1:tm]
    base_ref[...] = total

    @pl.when(pl.program_id(0) == pl.num_programs(0) - 1)
    def _():
        shape = (N_EXPERTS, LANES)
        sub = lax.broadcasted_iota(jnp.int32, shape, 0)
        lane = lax.broadcasted_iota(jnp.int32, shape, 1)
        counts = jnp.broadcast_to(total, shape)
        padded = jnp.floor((counts + (bm - 1)) * (1.0 / bm)) * bm
        pad_end = padded
        s = 1
        while s < N_EXPERTS:
            pad_end = pad_end + jnp.where(sub >= s, pltpu.roll(pad_end, s, axis=0), 0.0)
            s *= 2
        pad_start = pad_end - padded
        start_ref[...] = pad_start.astype(jnp.int32)
        for j in range(blk_ref.shape[1] // LANES):
            bstart = ((lane + j * LANES) * bm).astype(F32)
            bexp = jnp.minimum(jnp.sum((bstart >= pad_end).astype(F32), axis=0, keepdims=True), N_EXPERTS - 1.0)
            used_end = jnp.sum(jnp.where(sub.astype(F32) == bexp, pad_start + counts, 0.0), axis=0, keepdims=True)
            brows = jnp.clip(used_end - bstart[0:1, :], 0.0, float(bm))
            blk_ref[:, j * LANES:(j + 1) * LANES] = jnp.where(sub == 0, bexp, jnp.where(sub == 1, brows, 0.0)
                                                              ).astype(jnp.int32)


def _route(logits):
    t = logits.shape[0]
    tm = min(ROW_TILE, t)
    bm = EXPERT_ROWS
    row = lambda i: (i, 0)
    n_blocks = -(-(t * TOP_K + N_EXPERTS * (bm - 1)) // bm)
    blk_lanes = -(-n_blocks // LANES) * LANES
    gate, eidx, rank, start, blk = pl.pallas_call(
        functools.partial(_route_kernel, bm=bm),
        out_shape=(jax.ShapeDtypeStruct((t, LANES), F32), jax.ShapeDtypeStruct((8, t), jnp.int32),
                   jax.ShapeDtypeStruct((8, t), jnp.int32), jax.ShapeDtypeStruct((N_EXPERTS, LANES), jnp.int32),
                   jax.ShapeDtypeStruct((N_EXPERTS, blk_lanes), jnp.int32)),
        grid=(t // tm,),
        in_specs=[pl.BlockSpec((tm, LANES), row)],
        out_specs=(pl.BlockSpec((tm, LANES), row), pl.BlockSpec((8, tm), lambda i: (0, i)),
                   pl.BlockSpec((8, tm), lambda i: (0, i)), pl.BlockSpec((N_EXPERTS, LANES), lambda i: (0, 0)),
                   pl.BlockSpec((N_EXPERTS, blk_lanes), lambda i: (0, 0))),
        scratch_shapes=[pltpu.VMEM((N_EXPERTS, 1), F32)],
        compiler_params=_params("arbitrary"),
        name="moe_route",
    )(logits)
    pad_start = start[:, 0]
    onehot = eidx[:TOP_K, :, None] == jnp.arange(N_EXPERTS, dtype=jnp.int32)
    pos_kmajor = (jnp.sum(jnp.where(onehot, pad_start, 0), axis=-1) + rank[:TOP_K]).reshape(-1)
    return gate, pos_kmajor.astype(jnp.int32), blk[0, :n_blocks], blk[1, :n_blocks], n_blocks


def _sc_worker_range(n_rows):
    per_worker = n_rows // (SC_CORES * SC_SUBCORES)
    wid = lax.axis_index("s") * SC_CORES + lax.axis_index("c")
    return wid * per_worker, per_worker


def _sc_scatter_rows(src, idx, n_out):
    t, d = src.shape
    n_copies = idx.shape[0] // t
    assert t % (SC_CORES * SC_SUBCORES * SC_GATHER_ROWS) == 0, "rows must split evenly over the subcores"
    mesh = plsc.VectorSubcoreMesh(core_axis_name="c", subcore_axis_name="s")

    @functools.partial(
        pl.kernel, mesh=mesh,
        out_type=jax.ShapeDtypeStruct((n_out, d), src.dtype),
        scratch_types=([pltpu.VMEM((SC_GATHER_ROWS,), jnp.int32)] * n_copies
                       + [pltpu.VMEM((SC_GATHER_ROWS, d), src.dtype)]
                       + [pltpu.SemaphoreType.DMA] * n_copies),
        name="sc_scatter_rows",
    )
    def scatter(src_hbm, idx_hbm, out_hbm, *scratch):
        idx_v = scratch[:n_copies]
        rows_v = scratch[n_copies]
        sems = scratch[n_copies + 1:]
        base, per_worker = _sc_worker_range(t)

        @pl.loop(0, per_worker // SC_GATHER_ROWS)
        def _(j):
            off = pl.multiple_of(base + j * SC_GATHER_ROWS, SC_GATHER_ROWS)
            for k in range(n_copies):
                pltpu.sync_copy(idx_hbm.at[pl.ds(k * t + off, SC_GATHER_ROWS)], idx_v[k])
            pltpu.sync_copy(src_hbm.at[pl.ds(off, SC_GATHER_ROWS)], rows_v)
            copies = [pltpu.async_copy(rows_v, out_hbm.at[idx_v[k]], sems[k]) for k in range(n_copies)]
            for c in copies:
                c.wait()

    return scatter(src, idx)


def _expert_kernel(be_ref, nr_ref, x_ref, wgu_ref, bgu_ref, wd_ref, bd_ref, o_ref, wgu16_ref, wd16_ref, *, tn):
    i = pl.program_id(0)
    active = nr_ref[i] > 0
    new_expert = jnp.logical_or(i == 0, be_ref[i] != be_ref[jnp.maximum(i - 1, 0)])

    @pl.when(jnp.logical_and(active, new_expert))
    def _():
        wgu16_ref[...] = wgu_ref[0].astype(BF16)
        wd16_ref[...] = wd_ref[0].astype(BF16)

    @pl.when(active)
    def _():
        half = D_MODEL // 2
        row = lax.broadcasted_iota(jnp.int32, x_ref.shape, 0)
        x = jnp.where(row < nr_ref[i], x_ref[...], jnp.uint32(0))
        x_lo, x_hi = _unpack_bf16_pair(x)
        x_lo = x_lo.astype(BF16)
        x_hi = x_hi.astype(BF16)

        def up(c0):
            return (_dot(x_lo, wgu16_ref[:half, c0:c0 + tn]) + _dot(x_hi, wgu16_ref[half:, c0:c0 + tn])
                    + bgu_ref[0, :, c0:c0 + tn])

        y = bd_ref[0]
        for n0 in range(0, D_FF, tn):
            glu = jnp.minimum(up(n0), SWIGLU_LIMIT)
            lin = jnp.clip(up(D_FF + n0), -SWIGLU_LIMIT, SWIGLU_LIMIT)
            act = glu * _sigmoid(SWIGLU_ALPHA * glu) * (lin + 1.0)
            y = y + _dot(act.astype(BF16), wd16_ref[n0:n0 + tn, :])
        o_ref[...] = _pack_bf16_pair(y[:, :half], y[:, half:])

    @pl.when(jnp.logical_not(active))
    def _():
        o_ref[...] = jnp.zeros_like(o_ref)


def _experts(xs, block_exp, block_rows, w_gu, b_gu, w_down, b_down, n_blocks):
    d = D_MODEL
    dp = xs.shape[1]
    bm = EXPERT_ROWS
    grid_spec = pltpu.PrefetchScalarGridSpec(
        num_scalar_prefetch=2,
        grid=(n_blocks,),
        in_specs=[pl.BlockSpec((bm, dp), lambda i, be, nb: (i, 0)),
                  pl.BlockSpec((1, d, 2 * D_FF), lambda i, be, nb: (be[i], 0, 0)),
                  pl.BlockSpec((1, 1, 2 * D_FF), lambda i, be, nb: (be[i], 0, 0)),
                  pl.BlockSpec((1, D_FF, d), lambda i, be, nb: (be[i], 0, 0)),
                  pl.BlockSpec((1, 1, d), lambda i, be, nb: (be[i], 0, 0))],
        out_specs=pl.BlockSpec((bm, dp), lambda i, be, nb: (i, 0)),
        scratch_shapes=[pltpu.VMEM((d, 2 * D_FF), BF16), pltpu.VMEM((D_FF, d), BF16)],
    )
    return pl.pallas_call(
        functools.partial(_expert_kernel, tn=EXPERT_SLAB),
        out_shape=jax.ShapeDtypeStruct((n_blocks * bm, dp), jnp.uint32),
        grid_spec=grid_spec,
        compiler_params=_params("arbitrary"),
        name="moe_experts",
    )(block_exp, block_rows, xs, w_gu, b_gu.reshape(N_EXPERTS, 1, 2 * D_FF), w_down,
      b_down.reshape(N_EXPERTS, 1, d))


def _sc_gather_rows(table, idx):
    m = idx.shape[0]
    d = table.shape[1]
    assert m % (SC_CORES * SC_SUBCORES * 2 * SC_GATHER_ROWS) == 0, "rows must split evenly over the subcores"
    mesh = plsc.VectorSubcoreMesh(core_axis_name="c", subcore_axis_name="s")

    @functools.partial(
        pl.kernel, mesh=mesh,
        out_type=jax.ShapeDtypeStruct((m, d), table.dtype),
        scratch_types=([pltpu.VMEM((SC_GATHER_ROWS,), jnp.int32)] * 2
                       + [pltpu.VMEM((SC_GATHER_ROWS, d), table.dtype)] * 2
                       + [pltpu.SemaphoreType.DMA] * 4),
        name="sc_gather_rows",
    )
    def gather(table_hbm, idx_hbm, out_hbm, idx0, idx1, rows0, rows1, g0, g1, w0, w1):
        base, per_worker = _sc_worker_range(m)

        @pl.loop(0, per_worker // (2 * SC_GATHER_ROWS))
        def _(j):
            off0 = pl.multiple_of(base + 2 * j * SC_GATHER_ROWS, SC_GATHER_ROWS)
            off1 = off0 + SC_GATHER_ROWS
            pltpu.sync_copy(idx_hbm.at[pl.ds(off0, SC_GATHER_ROWS)], idx0)
            gather0 = pltpu.async_copy(table_hbm.at[idx0], rows0, g0)
            pltpu.sync_copy(idx_hbm.at[pl.ds(off1, SC_GATHER_ROWS)], idx1)
            gather1 = pltpu.async_copy(table_hbm.at[idx1], rows1, g1)
            gather0.wait()
            write0 = pltpu.async_copy(rows0, out_hbm.at[pl.ds(off0, SC_GATHER_ROWS)], w0)
            gather1.wait()
            write1 = pltpu.async_copy(rows1, out_hbm.at[pl.ds(off1, SC_GATHER_ROWS)], w1)
            write0.wait()
            write1.wait()

    return gather(table, idx)


def _combine_dense_kernel(y0_ref, y1_ref, y2_ref, y3_ref, gate_ref, h_ref, g_ref, b_ref, o_ref):
    gate = gate_ref[...]
    f_lo = f_hi = None
    for k, y_ref in enumerate((y0_ref, y1_ref, y2_ref, y3_ref)):
        y_lo, y_hi = _unpack_bf16_pair(y_ref[...])
        gk = gate[:, k:k + 1]
        f_lo = gk * y_lo if f_lo is None else f_lo + gk * y_lo
        f_hi = gk * y_hi if f_hi is None else f_hi + gk * y_hi
    f = jnp.concatenate([f_lo, f_hi], axis=1)
    o_ref[...] = _layer_norm(DN_ALPHA * h_ref[...] + f, g_ref[...], b_ref[...])


def _combine_dense(yg, gate, h1, ln_g, ln_b):
    t, d = h1.shape
    tm = min(ROW_TILE, t)
    nt = t // tm
    dp = yg.shape[1]
    slab = lambda k: pl.BlockSpec((tm, dp), lambda i, k=k: (k * nt + i, 0))
    return pl.pallas_call(
        _combine_dense_kernel,
        out_shape=jax.ShapeDtypeStruct((t, d), F32),
        grid=(nt,),
        in_specs=[slab(0), slab(1), slab(2), slab(3),
                  pl.BlockSpec((tm, LANES), lambda i: (i, 0)),
                  pl.BlockSpec((tm, d), lambda i: (i, 0)),
                  pl.BlockSpec((1, d), lambda i: (0, 0)),
                  pl.BlockSpec((1, d), lambda i: (0, 0))],
        out_specs=pl.BlockSpec((tm, d), lambda i: (i, 0)),
        compiler_params=_params("parallel"),
        name="moe_combine",
    )(yg, yg, yg, yg, gate, h1, ln_g.reshape(1, d), ln_b.reshape(1, d))


def kernel(x, emb_ln_g, emb_ln_b, w_in, conv_qkv, a_log, dt_bias, dn_norm_g, w_a_o, b_glu, conv_dw, b_dw, conv_ln_g, conv_ln_b, w_b_o, b_b_o, b_gate, w_out, ln1_g, ln1_b, w_router, b_router, w_gu, b_gu, w_down, b_down, ln2_g, ln2_b):
    bsz, seq, d = x.shape
    t = bsz * seq
    assert d == D_MODEL and seq % min(DELTA_ROWS, seq) == 0 and seq % 256 == 0 and t % min(ROW_TILE, t) == 0
    x2 = x.reshape(t, d)
    qkv, z, g, gtp, glu, gates = _inproj(x2, emb_ln_g, emb_ln_b, w_in[0], conv_qkv[0], b_glu[0], b_gate[0], a_log[0],
                                         dt_bias[0], seq)
    o_f, o_b = _delta_rule(qkv, g, gtp, bsz, seq)
    yc = _dw_conv(glu, conv_dw[0], b_dw[0], bsz, seq)

    h1, h1p, logits = _mix(o_f, o_b, z, yc, gates, x2, emb_ln_g, emb_ln_b, dn_norm_g[0],
                           w_a_o[0].astype(BF16), conv_ln_g[0], conv_ln_b[0], w_b_o[0].astype(BF16), b_b_o[0],
                           w_out[0].astype(BF16), ln1_g[0], ln1_b[0], w_router[0], b_router[0])

    gate, pos_kmajor, block_exp, block_rows, n_blocks = _route(logits)
    xs = _sc_scatter_rows(h1p, pos_kmajor, n_blocks * EXPERT_ROWS)
    ys = _experts(xs, block_exp, block_rows, w_gu[0], b_gu[0], w_down[0], b_down[0], n_blocks)
    yg = _sc_gather_rows(ys, pos_kmajor)
    out = _combine_dense(yg, gate, h1, ln2_g[0], ln2_b[0])
    return out.reshape(bsz, seq, d)
```

```python
import functools

import jax
import jax.numpy as jnp
from jax import lax
from jax.experimental import pallas as pl
from jax.experimental.pallas import tpu as pltpu
from jax.experimental.pallas import tpu_sc as plsc

F32 = jnp.float32
BF16 = jnp.bfloat16

D_MODEL = 1024
N_HEADS = 8
HEAD_DIM = 128
WIDTH_A = N_HEADS * HEAD_DIM
SHORT_CONV = 5
CHUNK = 64
WIDTH_B = D_MODEL
DW_CONV = 31
N_EXPERTS = 32
TOP_K = 4
D_FF = D_MODEL
SWIGLU_ALPHA = 1.702
SWIGLU_LIMIT = 7.0
DN_ALPHA = 2.0 ** 0.25
LN_EPS = 1e-5
RMS_EPS = 1e-6
L2_EPS = 1e-6
LANES = 128
NEG_BIG = -1e30

ROW_TILE = 512
CONV_HALO = 16
DELTA_ROWS = 512
EXPERT_ROWS = 512
EXPERT_SLAB = 512
SC_CORES = 2
SC_SUBCORES = 16
SC_GATHER_ROWS = 64
VMEM_LIMIT = 56 * 1024 * 1024


def _params(*sem):
    return pltpu.CompilerParams(dimension_semantics=sem, vmem_limit_bytes=VMEM_LIMIT)


def _layer_norm(x, g, b):
    mu = jnp.mean(x, axis=-1, keepdims=True)
    xc = x - mu
    var = jnp.mean(xc * xc, axis=-1, keepdims=True)
    return xc * lax.rsqrt(var + LN_EPS) * g + b


def _sigmoid(x):
    return 0.5 * jnp.tanh(0.5 * x) + 0.5


def _silu(x):
    h = 0.5 * x
    return h + h * jnp.tanh(h)


def _dot(a, b):
    return jnp.dot(a, b, preferred_element_type=F32)


def _pack_bf16_pair(a, b):
    ua = lax.bitcast_convert_type(a.astype(BF16).astype(F32), jnp.uint32)
    ub = lax.bitcast_convert_type(b.astype(BF16).astype(F32), jnp.uint32)
    return (ua >> 16) | ub


def _unpack_bf16_pair(p):
    a = lax.bitcast_convert_type(p << 16, F32)
    b = lax.bitcast_convert_type(p & jnp.uint32(0xFFFF0000), F32)
    return a, b


def _split_bf16(a):
    hi = a.astype(BF16)
    return hi, (a - hi.astype(F32)).astype(BF16)


def _chunk_cumsum(x, reverse):
    rows = x.shape[0]
    pos = lax.broadcasted_iota(jnp.int32, x.shape, 0) % CHUNK
    s = 1
    while s < CHUNK:
        if reverse:
            shifted = pltpu.roll(x, rows - s, axis=0)
            x = x + jnp.where(pos < CHUNK - s, shifted, 0.0)
        else:
            shifted = pltpu.roll(x, s, axis=0)
            x = x + jnp.where(pos >= s, shifted, 0.0)
        s *= 2
    return x


def _inproj_kernel(x_ref, xp_ref, xn_ref, eg_ref, eb_ref, wq_ref, cw_ref, ws_ref, wga_ref, wgb_ref, bga_ref, bgb_ref,
                   wgt_ref, bgt_ref, alog_ref, dtb_ref, qkv_ref, z_ref, g_ref, gt_ref, glu_ref, gate_ref, up_ref,
                   *, tn, tiles_per_seq):
    tm = x_ref.shape[0]
    halo = xp_ref.shape[0]
    i = pl.program_id(0)
    pos = i % tiles_per_seq
    h32 = _layer_norm(x_ref[...], eg_ref[...], eb_ref[...])
    h = h32.astype(BF16)
    h_prev = jnp.where(pos > 0, _layer_norm(xp_ref[...], eg_ref[...], eb_ref[...]), 0.0)
    h_next = jnp.where(pos < tiles_per_seq - 1, _layer_norm(xn_ref[...], eg_ref[...], eb_ref[...]), 0.0)
    h_ext = jnp.concatenate([h_prev, h32, h_next], axis=0).astype(BF16)

    base = halo - SHORT_CONV // 2
    for n0 in range(0, qkv_ref.shape[1], tn):
        u = _dot(h_ext, wq_ref[:, n0:n0 + tn])
        for p in range(tn // HEAD_DIM):
            plane = n0 // HEAD_DIM + p
            cols = slice(n0 + p * HEAD_DIM, n0 + (p + 1) * HEAD_DIM)
            up_ref[plane] = u[:, p * HEAD_DIM:(p + 1) * HEAD_DIM]
            w = cw_ref[:, cols]
            acc = up_ref[plane, base:base + tm, :] * w[0:1, :]
            for k in range(1, SHORT_CONV):
                acc = acc + up_ref[plane, base + k:base + k + tm, :] * w[k:k + 1, :]
            y = _silu(acc)
            if plane < 2 * N_HEADS:
                inv = lax.rsqrt(jnp.sum(y * y, axis=-1, keepdims=True) + L2_EPS)
                y = y * (inv * (HEAD_DIM ** -0.5) if plane < N_HEADS else inv)
            qkv_ref[:, cols] = y.astype(qkv_ref.dtype)

    c_z = qkv_ref.shape[1]
    for n0 in range(0, z_ref.shape[1], tn):
        z_ref[:, n0:n0 + tn] = _dot(h, wq_ref[:, c_z + n0:c_z + n0 + tn]).astype(z_ref.dtype)

    us = _dot(h, ws_ref[...])
    lane = lax.broadcasted_iota(jnp.int32, us.shape, 1)
    beta = _sigmoid(us)
    xs = us + dtb_ref[...]
    softplus = jnp.maximum(xs, 0.0) + jnp.log(1.0 + jnp.exp(-jnp.abs(xs)))
    log_a = -jnp.exp(alog_ref[...]) * softplus
    g_fwd = _chunk_cumsum(log_a, reverse=False)
    g_bwd = _chunk_cumsum(log_a, reverse=True)
    odd_group = jnp.bitwise_and(jnp.right_shift(lane, (N_HEADS // 2).bit_length() - 1), 1) == 1
    bwd = ((lane >= 3 * N_HEADS) & (lane < 4 * N_HEADS)) | ((lane >= 4 * N_HEADS) & odd_group)
    gates = jnp.where(lane < 2 * N_HEADS, beta, jnp.where(bwd, g_bwd, g_fwd))
    g_ref[...] = gates
    g_t = gates.T
    first = g_t[4 * N_HEADS:5 * N_HEADS, :]
    second = g_t[5 * N_HEADS:6 * N_HEADS, :]
    gt_ref[...] = jnp.concatenate([part[:, c * CHUNK:(c + 1) * CHUNK]
                                   for c in range(tm // CHUNK) for part in (first, second)], axis=1)

    for n0 in range(0, glu_ref.shape[1], tn):
        lin = _dot(h, wga_ref[:, n0:n0 + tn]) + bga_ref[:, n0:n0 + tn]
        gt = _dot(h, wgb_ref[:, n0:n0 + tn]) + bgb_ref[:, n0:n0 + tn]
        glu_ref[:, n0:n0 + tn] = (lin * _sigmoid(gt)).astype(glu_ref.dtype)

    for n0 in range(0, gate_ref.shape[1], tn):
        gate_ref[:, n0:n0 + tn] = _sigmoid(_dot(h, wgt_ref[:, n0:n0 + tn])
                                           + bgt_ref[:, n0:n0 + tn]).astype(gate_ref.dtype)


def _inproj(x2, emb_g, emb_b, w_in, conv_w, b_glu, b_gate, a_log, dt_bias, seq):
    t, d = x2.shape
    tm = min(ROW_TILE, seq)
    assert seq % tm == 0 and tm % CONV_HALO == 0
    tiles_per_seq = seq // tm
    hb = tm // CONV_HALO
    last_hb = t // CONV_HALO - 1
    c0 = 4 * WIDTH_A
    c1 = c0 + 4 * N_HEADS
    c2 = c1 + 2 * WIDTH_B
    wcols = lambda a, b: w_in[:, a:b].astype(BF16)
    pad = LANES - 6 * N_HEADS
    perm = jnp.array([dd * N_HEADS + 2 * p + hh for hh in range(2) for dd in range(2) for p in range(N_HEADS // 2)],
                     jnp.int32)
    twice = lambda a: jnp.concatenate([a, a[:, perm]], axis=1)
    w_small = jnp.pad(jnp.concatenate([wcols(c0, c0 + 2 * N_HEADS), twice(wcols(c0 + 2 * N_HEADS, c1))], axis=1),
                      ((0, 0), (0, pad)))
    alog = jnp.pad(twice(a_log.reshape(1, 2 * N_HEADS)), ((0, 0), (2 * N_HEADS, pad)))
    dtb = jnp.pad(twice(dt_bias.reshape(1, 2 * N_HEADS)), ((0, 0), (2 * N_HEADS, pad)))
    row = lambda i: (i, 0)

    def const(shape):
        return pl.BlockSpec(shape, lambda i: (0, 0), pipeline_mode=pl.Buffered(1))

    vec = lambda a: a.reshape(1, -1)
    bf = lambda n: jax.ShapeDtypeStruct((t, n), BF16)
    return pl.pallas_call(
        functools.partial(_inproj_kernel, tn=512, tiles_per_seq=tiles_per_seq),
        out_shape=(bf(3 * WIDTH_A), bf(WIDTH_A), jax.ShapeDtypeStruct((t, LANES), F32),
                   jax.ShapeDtypeStruct((N_HEADS, 2 * t), F32), bf(WIDTH_B), bf(2 * d)),
        grid=(t // tm,),
        in_specs=[pl.BlockSpec((tm, d), row),
                  pl.BlockSpec((CONV_HALO, d), lambda i: (jnp.maximum(i * hb - 1, 0), 0)),
                  pl.BlockSpec((CONV_HALO, d), lambda i: (jnp.minimum((i + 1) * hb, last_hb), 0)),
                  const((1, d)), const((1, d)),
                  const((d, c0)), const((SHORT_CONV, 3 * WIDTH_A)), const((d, LANES)),
                  const((d, WIDTH_B)), const((d, WIDTH_B)), const((1, WIDTH_B)), const((1, WIDTH_B)),
                  const((d, 2 * d)), const((1, 2 * d)), const((1, LANES)), const((1, LANES))],
        out_specs=(pl.BlockSpec((tm, 3 * WIDTH_A), row), pl.BlockSpec((tm, WIDTH_A), row),
                   pl.BlockSpec((tm, LANES), row), pl.BlockSpec((N_HEADS, 2 * tm), lambda i: (0, i)),
                   pl.BlockSpec((tm, WIDTH_B), row), pl.BlockSpec((tm, 2 * d), row)),
        scratch_shapes=[pltpu.VMEM((3 * N_HEADS, tm + 2 * CONV_HALO, HEAD_DIM), F32)],
        compiler_params=_params("parallel"),
        name="inproj",
    )(x2, x2, x2, vec(emb_g), vec(emb_b), wcols(0, c0), conv_w, w_small, wcols(c1, c1 + WIDTH_B),
      wcols(c1 + WIDTH_B, c2), vec(b_glu[:WIDTH_B]), vec(b_glu[WIDTH_B:]), wcols(c2, w_in.shape[1]), vec(b_gate),
      alog, dtb)


def _conv_rows(xp_ref, w, taps, base, r0, rows):
    acc = xp_ref[base + r0:base + r0 + rows, :] * w[0:1, :]
    for k in range(1, taps):
        acc = acc + xp_ref[base + k + r0:base + k + r0 + rows, :] * w[k:k + 1, :]
    return acc


def _fill_padded(xp_ref, x_ref, pad, seq):
    zeros = jnp.zeros((pad, xp_ref.shape[1]), F32)
    xp_ref[0:pad, :] = zeros
    xp_ref[pad + seq:pad + seq + pad, :] = zeros
    xp_ref[pad:pad + seq, :] = x_ref[...].astype(F32)


def _dw_conv_kernel(x_ref, w_ref, b_ref, o_ref, xp_ref, *, seq, rows):
    pad = 16
    _fill_padded(xp_ref, x_ref, pad, seq)
    w = w_ref[...]
    for r0 in range(0, seq, rows):
        y = _conv_rows(xp_ref, w, DW_CONV, pad - DW_CONV // 2, r0, rows) + b_ref[...]
        o_ref[r0:r0 + rows, :] = y.astype(o_ref.dtype)


def _dw_conv(glu, conv_w, b_dw, bsz, seq):
    t = bsz * seq
    rows = min(256, seq)
    return pl.pallas_call(
        functools.partial(_dw_conv_kernel, seq=seq, rows=rows),
        out_shape=jax.ShapeDtypeStruct((t, WIDTH_B), BF16),
        grid=(bsz, WIDTH_B // LANES),
        in_specs=[pl.BlockSpec((seq, LANES), lambda b, j: (b, j)),
                  pl.BlockSpec((DW_CONV, LANES), lambda b, j: (0, j)),
                  pl.BlockSpec((1, LANES), lambda b, j: (0, j))],
        out_specs=pl.BlockSpec((seq, LANES), lambda b, j: (b, j)),
        scratch_shapes=[pltpu.VMEM((seq + 32, LANES), F32)],
        compiler_params=_params("parallel", "parallel"),
        name="dw_conv",
    )(glu, conv_w, b_dw.reshape(1, WIDTH_B))


def _bmm(a, b):
    return lax.dot_general(a, b, (((2,), (1,)), ((0,), (0,))), preferred_element_type=F32)


def _bmm_nt(a, b):
    return lax.dot_general(a, b, (((2,), (2,)), ((0,), (0,))), preferred_element_type=F32)


def _bmm_tn(a, b):
    return lax.dot_general(a, b, (((1,), (1,)), ((0,), (0,))), preferred_element_type=F32)


def _block_diag_rows(x, half):
    lane = lax.broadcasted_iota(jnp.int32, x.shape, 2)
    return jnp.concatenate([jnp.where(lane < half, x, 0.0), jnp.where(lane >= half, x, 0.0)], axis=1)


def _unit_tri_inverse(lmat, eye):
    def rhs(p):
        return _block_diag_rows(p, CHUNK).astype(BF16)

    x = eye - lmat
    p = _bmm(lmat.astype(BF16), rhs(lmat))
    s = 2
    while 2 * s < CHUNK:
        xp = _bmm(jnp.concatenate([x, p], axis=1).astype(BF16), rhs(p))
        x = x + xp[:, :CHUNK]
        p = xp[:, CHUNK:]
        s *= 2
    return x + _bmm(x.astype(BF16), rhs(p))


def _delta_kernel(qf_ref, kf_ref, vf_ref, gf_ref, gtpf_ref, qb_ref, kb_ref, vb_ref, gb_ref, gtpb_ref,
                  of_ref, ob_ref, s_ref, *, nc):
    @pl.when(pl.program_id(1) == 0)
    def _():
        s_ref[...] = jnp.zeros_like(s_ref)

    n_pairs = N_HEADS // 2
    n_inst = 2 * n_pairs
    pw = 2 * HEAD_DIM
    dirs = ((qf_ref, kf_ref, vf_ref, gf_ref, gtpf_ref, of_ref, False),
            (qb_ref, kb_ref, vb_ref, gb_ref, gtpb_ref, ob_ref, True))
    steps = [[d + ((nc - 1 - i) if d[6] else i,) for d in dirs] for i in range(nc)]

    ri = lax.broadcasted_iota(jnp.int32, (CHUNK, 2 * CHUNK), 0)
    ci = jnp.bitwise_and(lax.broadcasted_iota(jnp.int32, (CHUNK, 2 * CHUNK), 1), CHUNK - 1)
    inst = lax.broadcasted_iota(jnp.int32, (nc * n_inst, 1, 1), 0)
    sign = 1 - 2 * jnp.bitwise_and(jnp.right_shift(inst, n_pairs.bit_length() - 1), 1)
    rel = (ri - ci)[None] * sign
    incl = rel >= 0
    strict = rel > 0
    eye = (ri == ci).astype(F32)

    def pairs(which):
        return jnp.stack([d[which][d[7] * CHUNK:(d[7] + 1) * CHUNK, p * pw:(p + 1) * pw]
                          for st in steps for d in st for p in range(n_pairs)]).astype(F32)

    def pair_bcast(cols, width):
        return jnp.stack([jnp.concatenate([jnp.broadcast_to(cc[2 * p], (CHUNK, width)),
                                           jnp.broadcast_to(cc[2 * p + 1], (CHUNK, width))], axis=1)
                          for cc in cols for p in range(n_pairs)])

    qf = pairs(0)
    kf = pairs(1)
    vf = pairs(2)
    beta_c, g_c, glast_c = [], [], []
    for st in steps:
        for d in st:
            gblk = d[3][d[7] * CHUNK:(d[7] + 1) * CHUNK, :]
            off = N_HEADS if d[6] else 0
            last = 0 if d[6] else CHUNK - 1
            beta_c.append([gblk[:, off + hh:off + hh + 1] for hh in range(N_HEADS)])
            g_c.append([gblk[:, 2 * N_HEADS + off + hh:2 * N_HEADS + off + hh + 1] for hh in range(N_HEADS)])
            glast_c.append([gc[last:last + 1, :] for gc in g_c[-1]])
    beta = pair_bcast(beta_c, HEAD_DIM)
    eg = pair_bcast([[jnp.exp(gc) for gc in gcs] for gcs in g_c], HEAD_DIM)
    tail = pair_bcast([[jnp.exp(gl - gc) for gl, gc in zip(gls, gcs)] for gls, gcs in zip(glast_c, g_c)],
                      HEAD_DIM)
    gcol = pair_bcast(g_c, CHUNK)
    grow = jnp.stack([d[4][(n_pairs if d[6] else 0) + p:(n_pairs if d[6] else 0) + p + 1,
                           2 * d[7] * CHUNK:2 * (d[7] + 1) * CHUNK]
                      for st in steps for d in st for p in range(n_pairs)])

    decay = jnp.exp(jnp.where(incl, gcol - grow, NEG_BIG))
    kb = kf * beta
    kkqk = _bmm_nt(jnp.concatenate([kb, qf], axis=1).astype(BF16),
                   _block_diag_rows(kf, HEAD_DIM).astype(BF16))
    lmat = jnp.where(strict, kkqk[:, :CHUNK, :] * decay, 0.0)
    qk = (kkqk[:, CHUNK:, :] * decay).astype(BF16)
    tinv = _unit_tri_inverse(lmat, eye)
    rhs = jnp.concatenate([_block_diag_rows(vf * beta, HEAD_DIM),
                           _block_diag_rows(kb * eg, HEAD_DIM)], axis=2).astype(BF16)
    uw = _bmm(tinv.astype(BF16), rhs)
    u = uw[:, :, :pw]
    wq = jnp.concatenate([uw[:, :, pw:], qf * eg], axis=1).astype(BF16)
    kt = (kf * tail).astype(BF16)

    for i, st in enumerate(steps):
        sl = slice(i * n_inst, (i + 1) * n_inst)
        s_a = s_ref[:, 0]
        s_b = s_ref[:, 1]
        zero = jnp.zeros_like(s_a)
        s_bd = jnp.concatenate([jnp.concatenate([s_a, zero], axis=2),
                                jnp.concatenate([zero, s_b], axis=2)], axis=1).astype(BF16)
        ws = _bmm(wq[sl], s_bd)
        v_new = u[sl] - ws[:, :CHUNK, :]
        o = ws[:, CHUNK:, :] + _bmm(qk[sl], _block_diag_rows(v_new, HEAD_DIM).astype(BF16))
        v16 = v_new.astype(BF16)
        gl = [glast_c[2 * i + dd] for dd in range(2)]
        cd_a = jnp.stack([jnp.exp(gl[dd][2 * p]) for dd in range(2) for p in range(n_pairs)])
        cd_b = jnp.stack([jnp.exp(gl[dd][2 * p + 1]) for dd in range(2) for p in range(n_pairs)])
        s_ref[:, 0] = s_a * cd_a + _bmm_tn(kt[sl, :, :HEAD_DIM], v16[:, :, :HEAD_DIM])
        s_ref[:, 1] = s_b * cd_b + _bmm_tn(kt[sl, :, HEAD_DIM:], v16[:, :, HEAD_DIM:])
        for dd, d in enumerate(st):
            for p in range(n_pairs):
                d[5][d[7] * CHUNK:(d[7] + 1) * CHUNK, p * pw:(p + 1) * pw] = (
                    o[dd * n_pairs + p].astype(d[5].dtype))


def _delta_rule(qkv, g, gtp, bsz, seq):
    t = bsz * seq
    rows = min(DELTA_ROWS, seq)
    nblk = seq // rows

    def fwd(col):
        return lambda b, i: (b * nblk + i, col)

    def bwd(col):
        return lambda b, i: (b * nblk + nblk - 1 - i, col)

    def specs(m):
        return [pl.BlockSpec((rows, WIDTH_A), m(0)), pl.BlockSpec((rows, WIDTH_A), m(1)),
                pl.BlockSpec((rows, WIDTH_A), m(2)), pl.BlockSpec((rows, LANES), m(0)),
                pl.BlockSpec((N_HEADS, 2 * rows), lambda b, i, m=m: (0, m(0)(b, i)[0]))]

    out = jax.ShapeDtypeStruct((t, WIDTH_A), BF16)
    return pl.pallas_call(
        functools.partial(_delta_kernel, nc=rows // CHUNK),
        out_shape=(out, out),
        grid=(bsz, nblk),
        in_specs=specs(fwd) + specs(bwd),
        out_specs=(pl.BlockSpec((rows, WIDTH_A), fwd(0)), pl.BlockSpec((rows, WIDTH_A), bwd(0))),
        scratch_shapes=[pltpu.VMEM((N_HEADS, 2, HEAD_DIM, HEAD_DIM), F32)],
        compiler_params=_params("parallel", "arbitrary"),
        name="delta_rule",
    )(qkv, qkv, qkv, g, gtp, qkv, qkv, qkv, g, gtp)


def _mix_kernel(of_ref, ob_ref, z_ref, yc_ref, gate_ref, x_ref,
                eg_ref, eb_ref, ng_ref, wao_ref, cg_ref, cb_ref, wbo_ref, bbo_ref,
                wout_ref, l1g_ref, l1b_ref, wr_ref, br_ref,
                h1_ref, h1p_ref, logit_ref):
    o = of_ref[...].astype(F32) + ob_ref[...].astype(F32)
    z = z_ref[...].astype(F32)
    ng = ng_ref[...]
    parts = []
    for hh in range(N_HEADS):
        sl = slice(hh * HEAD_DIM, (hh + 1) * HEAD_DIM)
        oh = o[:, sl]
        zh = z[:, sl]
        inv = lax.rsqrt(jnp.mean(oh * oh, axis=-1, keepdims=True) + RMS_EPS)
        parts.append((oh * inv * ng * _silu(zh)).astype(BF16))
    y_a = _dot(jnp.concatenate(parts, axis=1), wao_ref[...])

    yc = _layer_norm(yc_ref[...].astype(F32), cg_ref[...], cb_ref[...])
    y_b = _dot(_silu(yc).astype(BF16), wbo_ref[...]) + bbo_ref[...]

    gates = gate_ref[...].astype(F32)
    mixed = gates[:, :D_MODEL] * y_a + gates[:, D_MODEL:] * y_b
    mix = _dot(mixed.astype(BF16), wout_ref[...])

    h0 = _layer_norm(x_ref[...], eg_ref[...], eb_ref[...])
    h1 = _layer_norm(DN_ALPHA * h0 + mix, l1g_ref[...], l1b_ref[...])
    h1_ref[...] = h1
    h1p_ref[...] = _pack_bf16_pair(h1[:, :D_MODEL // 2], h1[:, D_MODEL // 2:])
    h_hi, h_lo = _split_bf16(h1)
    p = _dot(h_hi, wr_ref[...])
    logit_ref[...] = p[:, :LANES] + p[:, LANES:] + _dot(h_lo, wr_ref[:, :LANES]) + br_ref[...]


def _mix(o_f, o_b, z, yc, gates, x2, emb_g, emb_b, norm_g, w_a_o, cg, cb, w_b_o, b_b_o,
         w_out, l1g, l1b, w_router, b_router):
    t, d = x2.shape
    tm = min(ROW_TILE, t)
    row = lambda i: (i, 0)
    const = lambda i: (0, 0)
    wr = jnp.concatenate(_split_bf16(jnp.pad(w_router, ((0, 0), (0, LANES - N_EXPERTS)))), axis=1)
    br = jnp.pad(b_router.reshape(1, N_EXPERTS), ((0, 0), (0, LANES - N_EXPERTS)), constant_values=NEG_BIG)
    vec = lambda a: a.reshape(1, -1)
    return pl.pallas_call(
        _mix_kernel,
        out_shape=(jax.ShapeDtypeStruct((t, d), F32), jax.ShapeDtypeStruct((t, d // 2), jnp.uint32),
                   jax.ShapeDtypeStruct((t, LANES), F32)),
        grid=(t // tm,),
        in_specs=[pl.BlockSpec((tm, d), row), pl.BlockSpec((tm, d), row),
                  pl.BlockSpec((tm, d), row),
                  pl.BlockSpec((tm, d), row), pl.BlockSpec((tm, 2 * d), row), pl.BlockSpec((tm, d), row),
                  pl.BlockSpec((1, d), const), pl.BlockSpec((1, d), const),
                  pl.BlockSpec((1, HEAD_DIM), const), pl.BlockSpec((d, d), const),
                  pl.BlockSpec((1, d), const), pl.BlockSpec((1, d), const),
                  pl.BlockSpec((d, d), const), pl.BlockSpec((1, d), const),
                  pl.BlockSpec((d, d), const), pl.BlockSpec((1, d), const), pl.BlockSpec((1, d), const),
                  pl.BlockSpec((d, 2 * LANES), const), pl.BlockSpec((1, LANES), const)],
        out_specs=(pl.BlockSpec((tm, d), row), pl.BlockSpec((tm, d // 2), row), pl.BlockSpec((tm, LANES), row)),
        compiler_params=_params("parallel"),
        name="mix",
    )(o_f, o_b, z, yc, gates, x2, vec(emb_g), vec(emb_b), vec(norm_g), w_a_o, vec(cg), vec(cb),
      w_b_o, vec(b_b_o), w_out, vec(l1g), vec(l1b), wr, br)


def _route_kernel(logit_ref, gate_ref, eidx_ref, rank_ref, start_ref, blk_ref, base_ref, *, bm):
    @pl.when(pl.program_id(0) == 0)
    def _():
        base_ref[...] = jnp.zeros_like(base_ref)

    x = logit_ref[...].T[:N_EXPERTS, :]
    tm = x.shape[1]
    expert = lax.broadcasted_iota(jnp.int32, x.shape, 0).astype(F32)
    tok = lax.broadcasted_iota(jnp.int32, x.shape, 1)
    sel = jnp.zeros(x.shape, F32)
    vals, idxs = [], []
    for _ in range(TOP_K):
        m = jnp.max(x, axis=0, keepdims=True)
        idx = jnp.min(jnp.where(x == m, expert, float(N_EXPERTS)), axis=0, keepdims=True)
        hit = expert == idx
        sel = sel + hit.astype(F32)
        x = jnp.where(hit, -3e38, x)
        vals.append(m)
        idxs.append(idx)

    exps = [jnp.exp(v - vals[0]) for v in vals]
    denom = exps[0]
    for e in exps[1:]:
        denom = denom + e

    csum = sel
    s = 1
    while s < tm:
        csum = csum + jnp.where(tok >= s, pltpu.roll(csum, s, axis=1), 0.0)
        s *= 2
    before = base_ref[...] + csum - sel

    krow = lax.broadcasted_iota(jnp.int32, (8, tm), 0)
    gate = jnp.zeros((8, tm), F32)
    eidx = jnp.zeros((8, tm), F32)
    rank = jnp.zeros((8, tm), F32)
    for k in range(TOP_K):
        rk = jnp.sum(jnp.where(expert == idxs[k], before, 0.0), axis=0, keepdims=True)
        gate = jnp.where(krow == k, exps[k] / denom, gate)
        eidx = jnp.where(krow == k, idxs[k], eidx)
        rank = jnp.where(krow == k, rk, rank)
    eidx_ref[...] = eidx.astype(jnp.int32)
    rank_ref[...] = rank.astype(jnp.int32)
    gate_ref[...] = jnp.concatenate([gate, jnp.zeros((LANES - 8, tm), F32)], axis=0).T
    total = base_ref[...] + csum[:, tm - 1:tm]
    base_ref[...] = total

    @pl.when(pl.program_id(0) == pl.num_programs(0) - 1)
    def _():
        shape = (N_EXPERTS, LANES)
        sub = lax.broadcasted_iota(jnp.int32, shape, 0)
        lane = lax.broadcasted_iota(jnp.int32, shape, 1)
        counts = jnp.broadcast_to(total, shape)
        padded = jnp.floor((counts + (bm - 1)) * (1.0 / bm)) * bm
        pad_end = padded
        s = 1
        while s < N_EXPERTS:
            pad_end = pad_end + jnp.where(sub >= s, pltpu.roll(pad_end, s, axis=0), 0.0)
            s *= 2
        pad_start = pad_end - padded
        start_ref[...] = pad_start.astype(jnp.int32)
        for j in range(blk_ref.shape[1] // LANES):
            bstart = ((lane + j * LANES) * bm).astype(F32)
            bexp = jnp.minimum(jnp.sum((bstart >= pad_end).astype(F32), axis=0, keepdims=True), N_EXPERTS - 1.0)
            used_end = jnp.sum(jnp.where(sub.astype(F32) == bexp, pad_start + counts, 0.0), axis=0, keepdims=True)
            brows = jnp.clip(used_end - bstart[0:1, :], 0.0, float(bm))
            blk_ref[:, j * LANES:(j + 1) * LANES] = jnp.where(sub == 0, bexp, jnp.where(sub == 1, brows, 0.0)
                                                              ).astype(jnp.int32)


def _route(logits):
    t = logits.shape[0]
    tm = min(ROW_TILE, t)
    bm = EXPERT_ROWS
    row = lambda i: (i, 0)
    n_blocks = -(-(t * TOP_K + N_EXPERTS * (bm - 1)) // bm)
    blk_lanes = -(-n_blocks // LANES) * LANES
    gate, eidx, rank, start, blk = pl.pallas_call(
        functools.partial(_route_kernel, bm=bm),
        out_shape=(jax.ShapeDtypeStruct((t, LANES), F32), jax.ShapeDtypeStruct((8, t), jnp.int32),
                   jax.ShapeDtypeStruct((8, t), jnp.int32), jax.ShapeDtypeStruct((N_EXPERTS, LANES), jnp.int32),
                   jax.ShapeDtypeStruct((N_EXPERTS, blk_lanes), jnp.int32)),
        grid=(t // tm,),
        in_specs=[pl.BlockSpec((tm, LANES), row)],
        out_specs=(pl.BlockSpec((tm, LANES), row), pl.BlockSpec((8, tm), lambda i: (0, i)),
                   pl.BlockSpec((8, tm), lambda i: (0, i)), pl.BlockSpec((N_EXPERTS, LANES), lambda i: (0, 0)),
                   pl.BlockSpec((N_EXPERTS, blk_lanes), lambda i: (0, 0))),
        scratch_shapes=[pltpu.VMEM((N_EXPERTS, 1), F32)],
        compiler_params=_params("arbitrary"),
        name="moe_route",
    )(logits)
    pad_start = start[:, 0]
    onehot = eidx[:TOP_K, :, None] == jnp.arange(N_EXPERTS, dtype=jnp.int32)
    pos_kmajor = (jnp.sum(jnp.where(onehot, pad_start, 0), axis=-1) + rank[:TOP_K]).reshape(-1)
    return gate, pos_kmajor.astype(jnp.int32), blk[0, :n_blocks], blk[1, :n_blocks], n_blocks


def _sc_worker_range(n_rows):
    per_worker = n_rows // (SC_CORES * SC_SUBCORES)
    wid = lax.axis_index("s") * SC_CORES + lax.axis_index("c")
    return wid * per_worker, per_worker


def _sc_scatter_rows(src, idx, n_out):
    t, d = src.shape
    n_copies = idx.shape[0] // t
    assert t % (SC_CORES * SC_SUBCORES * SC_GATHER_ROWS) == 0, "rows must split evenly over the subcores"
    mesh = plsc.VectorSubcoreMesh(core_axis_name="c", subcore_axis_name="s")

    @functools.partial(
        pl.kernel, mesh=mesh,
        out_type=jax.ShapeDtypeStruct((n_out, d), src.dtype),
        scratch_types=([pltpu.VMEM((SC_GATHER_ROWS,), jnp.int32)] * n_copies
                       + [pltpu.VMEM((SC_GATHER_ROWS, d), src.dtype)]
                       + [pltpu.SemaphoreType.DMA] * n_copies),
        name="sc_scatter_rows",
    )
    def scatter(src_hbm, idx_hbm, out_hbm, *scratch):
        idx_v = scratch[:n_copies]
        rows_v = scratch[n_copies]
        sems = scratch[n_copies + 1:]
        base, per_worker = _sc_worker_range(t)

        @pl.loop(0, per_worker // SC_GATHER_ROWS)
        def _(j):
            off = pl.multiple_of(base + j * SC_GATHER_ROWS, SC_GATHER_ROWS)
            for k in range(n_copies):
                pltpu.sync_copy(idx_hbm.at[pl.ds(k * t + off, SC_GATHER_ROWS)], idx_v[k])
            pltpu.sync_copy(src_hbm.at[pl.ds(off, SC_GATHER_ROWS)], rows_v)
            copies = [pltpu.async_copy(rows_v, out_hbm.at[idx_v[k]], sems[k]) for k in range(n_copies)]
            for c in copies:
                c.wait()

    return scatter(src, idx)


def _expert_kernel(be_ref, nr_ref, x_ref, wgu_ref, bgu_ref, wd_ref, bd_ref, o_ref, wgu16_ref, wd16_ref, *, tn):
    i = pl.program_id(0)
    active = nr_ref[i] > 0
    new_expert = jnp.logical_or(i == 0, be_ref[i] != be_ref[jnp.maximum(i - 1, 0)])

    @pl.when(jnp.logical_and(active, new_expert))
    def _():
        wgu16_ref[...] = wgu_ref[0].astype(BF16)
        wd16_ref[...] = wd_ref[0].astype(BF16)

    @pl.when(active)
    def _():
        half = D_MODEL // 2
        row = lax.broadcasted_iota(jnp.int32, x_ref.shape, 0)
        x = jnp.where(row < nr_ref[i], x_ref[...], jnp.uint32(0))
        x_lo, x_hi = _unpack_bf16_pair(x)
        x_lo = x_lo.astype(BF16)
        x_hi = x_hi.astype(BF16)

        def up(c0):
            return (_dot(x_lo, wgu16_ref[:half, c0:c0 + tn]) + _dot(x_hi, wgu16_ref[half:, c0:c0 + tn])
                    + bgu_ref[0, :, c0:c0 + tn])

        y = bd_ref[0]
        for n0 in range(0, D_FF, tn):
            glu = jnp.minimum(up(n0), SWIGLU_LIMIT)
            lin = jnp.clip(up(D_FF + n0), -SWIGLU_LIMIT, SWIGLU_LIMIT)
            act = glu * _sigmoid(SWIGLU_ALPHA * glu) * (lin + 1.0)
            y = y + _dot(act.astype(BF16), wd16_ref[n0:n0 + tn, :])
        o_ref[...] = _pack_bf16_pair(y[:, :half], y[:, half:])

    @pl.when(jnp.logical_not(active))
    def _():
        o_ref[...] = jnp.zeros_like(o_ref)


def _experts(xs, block_exp, block_rows, w_gu, b_gu, w_down, b_down, n_blocks):
    d = D_MODEL
    dp = xs.shape[1]
    bm = EXPERT_ROWS
    grid_spec = pltpu.PrefetchScalarGridSpec(
        num_scalar_prefetch=2,
        grid=(n_blocks,),
        in_specs=[pl.BlockSpec((bm, dp), lambda i, be, nb: (i, 0)),
                  pl.BlockSpec((1, d, 2 * D_FF), lambda i, be, nb: (be[i], 0, 0)),
                  pl.BlockSpec((1, 1, 2 * D_FF), lambda i, be, nb: (be[i], 0, 0)),
                  pl.BlockSpec((1, D_FF, d), lambda i, be, nb: (be[i], 0, 0)),
                  pl.BlockSpec((1, 1, d), lambda i, be, nb: (be[i], 0, 0))],
        out_specs=pl.BlockSpec((bm, dp), lambda i, be, nb: (i, 0)),
        scratch_shapes=[pltpu.VMEM((d, 2 * D_FF), BF16), pltpu.VMEM((D_FF, d), BF16)],
    )
    return pl.pallas_call(
        functools.partial(_expert_kernel, tn=EXPERT_SLAB),
        out_shape=jax.ShapeDtypeStruct((n_blocks * bm, dp), jnp.uint32),
        grid_spec=grid_spec,
        compiler_params=_params("arbitrary"),
        name="moe_experts",
    )(block_exp, block_rows, xs, w_gu, b_gu.reshape(N_EXPERTS, 1, 2 * D_FF), w_down,
      b_down.reshape(N_EXPERTS, 1, d))


def _sc_gather_rows(table, idx):
    m = idx.shape[0]
    d = table.shape[1]
    assert m % (SC_CORES * SC_SUBCORES * 2 * SC_GATHER_ROWS) == 0, "rows must split evenly over the subcores"
    mesh = plsc.VectorSubcoreMesh(core_axis_name="c", subcore_axis_name="s")

    @functools.partial(
        pl.kernel, mesh=mesh,
        out_type=jax.ShapeDtypeStruct((m, d), table.dtype),
        scratch_types=([pltpu.VMEM((SC_GATHER_ROWS,), jnp.int32)] * 2
                       + [pltpu.VMEM((SC_GATHER_ROWS, d), table.dtype)] * 2
                       + [pltpu.SemaphoreType.DMA] * 4),
        name="sc_gather_rows",
    )
    def gather(table_hbm, idx_hbm, out_hbm, idx0, idx1, rows0, rows1, g0, g1, w0, w1):
        base, per_worker = _sc_worker_range(m)

        @pl.loop(0, per_worker // (2 * SC_GATHER_ROWS))
        def _(j):
            off0 = pl.multiple_of(base + 2 * j * SC_GATHER_ROWS, SC_GATHER_ROWS)
            off1 = off0 + SC_GATHER_ROWS
            pltpu.sync_copy(idx_hbm.at[pl.ds(off0, SC_GATHER_ROWS)], idx0)
            gather0 = pltpu.async_copy(table_hbm.at[idx0], rows0, g0)
            pltpu.sync_copy(idx_hbm.at[pl.ds(off1, SC_GATHER_ROWS)], idx1)
            gather1 = pltpu.async_copy(table_hbm.at[idx1], rows1, g1)
            gather0.wait()
            write0 = pltpu.async_copy(rows0, out_hbm.at[pl.ds(off0, SC_GATHER_ROWS)], w0)
            gather1.wait()
            write1 = pltpu.async_copy(rows1, out_hbm.at[pl.ds(off1, SC_GATHER_ROWS)], w1)
            write0.wait()
            write1.wait()

    return gather(table, idx)


def _combine_dense_kernel(y0_ref, y1_ref, y2_ref, y3_ref, gate_ref, h_ref, g_ref, b_ref, o_ref):
    gate = gate_ref[...]
    f_lo = f_hi = None
    for k, y_ref in enumerate((y0_ref, y1_ref, y2_ref, y3_ref)):
        y_lo, y_hi = _unpack_bf16_pair(y_ref[...])
        gk = gate[:, k:k + 1]
        f_lo = gk * y_lo if f_lo is None else f_lo + gk * y_lo
        f_hi = gk * y_hi if f_hi is None else f_hi + gk * y_hi
    f = jnp.concatenate([f_lo, f_hi], axis=1)
    o_ref[...] = _layer_norm(DN_ALPHA * h_ref[...] + f, g_ref[...], b_ref[...])


def _combine_dense(yg, gate, h1, ln_g, ln_b):
    t, d = h1.shape
    tm = min(ROW_TILE, t)
    nt = t // tm
    dp = yg.shape[1]
    slab = lambda k: pl.BlockSpec((tm, dp), lambda i, k=k: (k * nt + i, 0))
    return pl.pallas_call(
        _combine_dense_kernel,
        out_shape=jax.ShapeDtypeStruct((t, d), F32),
        grid=(nt,),
        in_specs=[slab(0), slab(1), slab(2), slab(3),
                  pl.BlockSpec((tm, LANES), lambda i: (i, 0)),
                  pl.BlockSpec((tm, d), lambda i: (i, 0)),
                  pl.BlockSpec((1, d), lambda i: (0, 0)),
                  pl.BlockSpec((1, d), lambda i: (0, 0))],
        out_specs=pl.BlockSpec((tm, d), lambda i: (i, 0)),
        compiler_params=_params("parallel"),
        name="moe_combine",
    )(yg, yg, yg, yg, gate, h1, ln_g.reshape(1, d), ln_b.reshape(1, d))


def kernel(x, emb_ln_g, emb_ln_b, w_in, conv_qkv, a_log, dt_bias, dn_norm_g, w_a_o, b_glu, conv_dw, b_dw, conv_ln_g, conv_ln_b, w_b_o, b_b_o, b_gate, w_out, ln1_g, ln1_b, w_router, b_router, w_gu, b_gu, w_down, b_down, ln2_g, ln2_b):
    bsz, seq, d = x.shape
    t = bsz * seq
    assert d == D_MODEL and seq % min(DELTA_ROWS, seq) == 0 and seq % 256 == 0 and t % min(ROW_TILE, t) == 0
    x2 = x.reshape(t, d)
    qkv, z, g, gtp, glu, gates = _inproj(x2, emb_ln_g, emb_ln_b, w_in[0], conv_qkv[0], b_glu[0], b_gate[0], a_log[0],
                                         dt_bias[0], seq)
    o_f, o_b = _delta_rule(qkv, g, gtp, bsz, seq)
    yc = _dw_conv(glu, conv_dw[0], b_dw[0], bsz, seq)

    h1, h1p, logits = _mix(o_f, o_b, z, yc, gates, x2, emb_ln_g, emb_ln_b, dn_norm_g[0],
                           w_a_o[0].astype(BF16), conv_ln_g[0], conv_ln_b[0], w_b_o[0].astype(BF16), b_b_o[0],
                           w_out[0].astype(BF16), ln1_g[0], ln1_b[0], w_router[0], b_router[0])

    gate, pos_kmajor, block_exp, block_rows, n_blocks = _route(logits)
    xs = _sc_scatter_rows(h1p, pos_kmajor, n_blocks * EXPERT_ROWS)
    ys = _experts(xs, block_exp, block_rows, w_gu[0], b_gu[0], w_down[0], b_down[0], n_blocks)
    yg = _sc_gather_rows(ys, pos_kmajor)
    out = _combine_dense(yg, gate, h1, ln2_g[0], ln2_b[0])
    return out.reshape(bsz, seq, d)
```

```python
import functools

import jax
import jax.numpy as jnp
from jax import lax
from jax.experimental import pallas as pl
from jax.experimental.pallas import tpu as pltpu
from jax.experimental.pallas import tpu_sc as plsc

F32 = jnp.float32
BF16 = jnp.bfloat16

D_MODEL = 1024
N_HEADS = 8
HEAD_DIM = 128
WIDTH_A = N_HEADS * HEAD_DIM
SHORT_CONV = 5
CHUNK = 64
WIDTH_B = D_MODEL
DW_CONV = 31
N_EXPERTS = 32
TOP_K = 4
D_FF = D_MODEL
SWIGLU_ALPHA = 1.702
SWIGLU_LIMIT = 7.0
DN_ALPHA = 2.0 ** 0.25
LN_EPS = 1e-5
RMS_EPS = 1e-6
L2_EPS = 1e-6
LANES = 128
NEG_BIG = -1e30

ROW_TILE = 512
CONV_HALO = 16
DELTA_ROWS = 512
EXPERT_ROWS = 512
EXPERT_SLAB = 512
SC_CORES = 2
SC_SUBCORES = 16
SC_GATHER_ROWS = 64
VMEM_LIMIT = 56 * 1024 * 1024


def _params(*sem):
    return pltpu.CompilerParams(dimension_semantics=sem, vmem_limit_bytes=VMEM_LIMIT)


def _layer_norm(x, g, b):
    mu = jnp.mean(x, axis=-1, keepdims=True)
    xc = x - mu
    var = jnp.mean(xc * xc, axis=-1, keepdims=True)
    return xc * lax.rsqrt(var + LN_EPS) * g + b


def _sigmoid(x):
    return 0.5 * jnp.tanh(0.5 * x) + 0.5


def _silu(x):
    h = 0.5 * x
    return h + h * jnp.tanh(h)


def _dot(a, b):
    return jnp.dot(a, b, preferred_element_type=F32)


def _pack_bf16_pair(a, b):
    ua = lax.bitcast_convert_type(a.astype(BF16).astype(F32), jnp.uint32)
    ub = lax.bitcast_convert_type(b.astype(BF16).astype(F32), jnp.uint32)
    return (ua >> 16) | ub


def _unpack_bf16_pair(p):
    a = lax.bitcast_convert_type(p << 16, F32)
    b = lax.bitcast_convert_type(p & jnp.uint32(0xFFFF0000), F32)
    return a, b


def _split_bf16(a):
    hi = a.astype(BF16)
    return hi, (a - hi.astype(F32)).astype(BF16)


def _chunk_cumsum(x, reverse):
    rows = x.shape[0]
    pos = lax.broadcasted_iota(jnp.int32, x.shape, 0) % CHUNK
    s = 1
    while s < CHUNK:
        if reverse:
            shifted = pltpu.roll(x, rows - s, axis=0)
            x = x + jnp.where(pos < CHUNK - s, shifted, 0.0)
        else:
            shifted = pltpu.roll(x, s, axis=0)
            x = x + jnp.where(pos >= s, shifted, 0.0)
        s *= 2
    return x


def _inproj_kernel(x_ref, xp_ref, xn_ref, eg_ref, eb_ref, wq_ref, cw_ref, ws_ref, wga_ref, wgb_ref, bga_ref, bgb_ref,
                   wgt_ref, bgt_ref, alog_ref, dtb_ref, qkv_ref, z_ref, g_ref, gt_ref, glu_ref, gate_ref, up_ref,
                   *, tn, tiles_per_seq):
    tm = x_ref.shape[0]
    halo = xp_ref.shape[0]
    i = pl.program_id(0)
    pos = i % tiles_per_seq
    h32 = _layer_norm(x_ref[...], eg_ref[...], eb_ref[...])
    h = h32.astype(BF16)
    h_prev = jnp.where(pos > 0, _layer_norm(xp_ref[...], eg_ref[...], eb_ref[...]), 0.0)
    h_next = jnp.where(pos < tiles_per_seq - 1, _layer_norm(xn_ref[...], eg_ref[...], eb_ref[...]), 0.0)
    h_ext = jnp.concatenate([h_prev, h32, h_next], axis=0).astype(BF16)

    base = halo - SHORT_CONV // 2
    for n0 in range(0, qkv_ref.shape[1], tn):
        u = _dot(h_ext, wq_ref[:, n0:n0 + tn])
        for p in range(tn // HEAD_DIM):
            plane = n0 // HEAD_DIM + p
            cols = slice(n0 + p * HEAD_DIM, n0 + (p + 1) * HEAD_DIM)
            up_ref[plane] = u[:, p * HEAD_DIM:(p + 1) * HEAD_DIM]
            w = cw_ref[:, cols]
            acc = up_ref[plane, base:base + tm, :] * w[0:1, :]
            for k in range(1, SHORT_CONV):
                acc = acc + up_ref[plane, base + k:base + k + tm, :] * w[k:k + 1, :]
            y = _silu(acc)
            if plane < 2 * N_HEADS:
                inv = lax.rsqrt(jnp.sum(y * y, axis=-1, keepdims=True) + L2_EPS)
                y = y * (inv * (HEAD_DIM ** -0.5) if plane < N_HEADS else inv)
            qkv_ref[:, cols] = y.astype(qkv_ref.dtype)

    c_z = qkv_ref.shape[1]
    for n0 in range(0, z_ref.shape[1], tn):
        z_ref[:, n0:n0 + tn] = _dot(h, wq_ref[:, c_z + n0:c_z + n0 + tn]).astype(z_ref.dtype)

    us = _dot(h, ws_ref[...])
    lane = lax.broadcasted_iota(jnp.int32, us.shape, 1)
    beta = _sigmoid(us)
    xs = us + dtb_ref[...]
    softplus = jnp.maximum(xs, 0.0) + jnp.log(1.0 + jnp.exp(-jnp.abs(xs)))
    log_a = -jnp.exp(alog_ref[...]) * softplus
    g_fwd = _chunk_cumsum(log_a, reverse=False)
    g_bwd = _chunk_cumsum(log_a, reverse=True)
    odd_group = jnp.bitwise_and(jnp.right_shift(lane, (N_HEADS // 2).bit_length() - 1), 1) == 1
    bwd = ((lane >= 3 * N_HEADS) & (lane < 4 * N_HEADS)) | ((lane >= 4 * N_HEADS) & odd_group)
    gates = jnp.where(lane < 2 * N_HEADS, beta, jnp.where(bwd, g_bwd, g_fwd))
    g_ref[...] = gates
    g_t = gates.T
    first = g_t[4 * N_HEADS:5 * N_HEADS, :]
    second = g_t[5 * N_HEADS:6 * N_HEADS, :]
    gt_ref[...] = jnp.concatenate([part[:, c * CHUNK:(c + 1) * CHUNK]
                                   for c in range(tm // CHUNK) for part in (first, second)], axis=1)

    for n0 in range(0, glu_ref.shape[1], tn):
        lin = _dot(h, wga_ref[:, n0:n0 + tn]) + bga_ref[:, n0:n0 + tn]
        gt = _dot(h, wgb_ref[:, n0:n0 + tn]) + bgb_ref[:, n0:n0 + tn]
        glu_ref[:, n0:n0 + tn] = (lin * _sigmoid(gt)).astype(glu_ref.dtype)

    for n0 in range(0, gate_ref.shape[1], tn):
        gate_ref[:, n0:n0 + tn] = _sigmoid(_dot(h, wgt_ref[:, n0:n0 + tn])
                                           + bgt_ref[:, n0:n0 + tn]).astype(gate_ref.dtype)


def _split_w_in_kernel(w_ref, wq_ref, ws_ref, wga_ref, wgb_ref, wgt_ref, *, c0, c1):
    c2 = c1 + WIDTH_B
    c3 = c2 + WIDTH_B
    wq_ref[...] = w_ref[:, :c0].astype(BF16)
    ws_ref[...] = w_ref[:, c0:c0 + LANES].astype(BF16)
    wga_ref[...] = w_ref[:, c1:c2].astype(BF16)
    wgb_ref[...] = w_ref[:, c2:c3].astype(BF16)
    wgt_ref[...] = w_ref[:, c3:].astype(BF16)


def _split_w_in(w_in, c0, c1):
    d, n = w_in.shape
    rb = 128
    n_gt = n - c1 - 2 * WIDTH_B
    row = lambda i: (i, 0)
    widths = (c0, LANES, WIDTH_B, WIDTH_B, n_gt)
    return pl.pallas_call(
        functools.partial(_split_w_in_kernel, c0=c0, c1=c1),
        out_shape=tuple(jax.ShapeDtypeStruct((d, w), BF16) for w in widths),
        grid=(d // rb,),
        in_specs=[pl.BlockSpec((rb, n), row)],
        out_specs=tuple(pl.BlockSpec((rb, w), row) for w in widths),
        compiler_params=_params("parallel"),
        name="split_w_in",
    )(w_in)


def _inproj(x2, emb_g, emb_b, w_in, conv_w, b_glu, b_gate, a_log, dt_bias, seq):
    t, d = x2.shape
    tm = min(ROW_TILE, seq)
    assert seq % tm == 0 and tm % CONV_HALO == 0
    tiles_per_seq = seq // tm
    hb = tm // CONV_HALO
    last_hb = t // CONV_HALO - 1
    c0 = 4 * WIDTH_A
    c1 = c0 + 4 * N_HEADS
    c2 = c1 + 2 * WIDTH_B
    w_qkvz, w_s, w_ga, w_gb, w_gt = _split_w_in(w_in, c0, c1)
    pad = LANES - 6 * N_HEADS
    perm = jnp.array([dd * N_HEADS + 2 * p + hh for hh in range(2) for dd in range(2) for p in range(N_HEADS // 2)],
                     jnp.int32)
    twice = lambda a: jnp.concatenate([a, a[:, perm]], axis=1)
    w_small = jnp.pad(jnp.concatenate([w_s[:, :2 * N_HEADS], twice(w_s[:, 2 * N_HEADS:4 * N_HEADS])], axis=1),
                      ((0, 0), (0, pad)))
    alog = jnp.pad(twice(a_log.reshape(1, 2 * N_HEADS)), ((0, 0), (2 * N_HEADS, pad)))
    dtb = jnp.pad(twice(dt_bias.reshape(1, 2 * N_HEADS)), ((0, 0), (2 * N_HEADS, pad)))
    row = lambda i: (i, 0)

    def const(shape):
        return pl.BlockSpec(shape, lambda i: (0, 0), pipeline_mode=pl.Buffered(1))

    vec = lambda a: a.reshape(1, -1)
    bf = lambda n: jax.ShapeDtypeStruct((t, n), BF16)
    return pl.pallas_call(
        functools.partial(_inproj_kernel, tn=512, tiles_per_seq=tiles_per_seq),
        out_shape=(bf(3 * WIDTH_A), bf(WIDTH_A), jax.ShapeDtypeStruct((t, LANES), F32),
                   jax.ShapeDtypeStruct((N_HEADS, 2 * t), F32), bf(WIDTH_B), bf(2 * d)),
        grid=(t // tm,),
        in_specs=[pl.BlockSpec((tm, d), row),
                  pl.BlockSpec((CONV_HALO, d), lambda i: (jnp.maximum(i * hb - 1, 0), 0)),
                  pl.BlockSpec((CONV_HALO, d), lambda i: (jnp.minimum((i + 1) * hb, last_hb), 0)),
                  const((1, d)), const((1, d)),
                  const((d, c0)), const((SHORT_CONV, 3 * WIDTH_A)), const((d, LANES)),
                  const((d, WIDTH_B)), const((d, WIDTH_B)), const((1, WIDTH_B)), const((1, WIDTH_B)),
                  const((d, 2 * d)), const((1, 2 * d)), const((1, LANES)), const((1, LANES))],
        out_specs=(pl.BlockSpec((tm, 3 * WIDTH_A), row), pl.BlockSpec((tm, WIDTH_A), row),
                   pl.BlockSpec((tm, LANES), row), pl.BlockSpec((N_HEADS, 2 * tm), lambda i: (0, i)),
                   pl.BlockSpec((tm, WIDTH_B), row), pl.BlockSpec((tm, 2 * d), row)),
        scratch_shapes=[pltpu.VMEM((3 * N_HEADS, tm + 2 * CONV_HALO, HEAD_DIM), F32)],
        compiler_params=_params("parallel"),
        name="inproj",
    )(x2, x2, x2, vec(emb_g), vec(emb_b), w_qkvz, conv_w, w_small, w_ga, w_gb,
      vec(b_glu[:WIDTH_B]), vec(b_glu[WIDTH_B:]), w_gt, vec(b_gate), alog, dtb)


def _conv_rows(xp_ref, w, taps, base, r0, rows):
    acc = xp_ref[base + r0:base + r0 + rows, :] * w[0:1, :]
    for k in range(1, taps):
        acc = acc + xp_ref[base + k + r0:base + k + r0 + rows, :] * w[k:k + 1, :]
    return acc


def _fill_padded(xp_ref, x_ref, pad, seq):
    zeros = jnp.zeros((pad, xp_ref.shape[1]), F32)
    xp_ref[0:pad, :] = zeros
    xp_ref[pad + seq:pad + seq + pad, :] = zeros
    xp_ref[pad:pad + seq, :] = x_ref[...].astype(F32)


def _dw_conv_kernel(x_ref, w_ref, b_ref, o_ref, xp_ref, *, seq, rows):
    pad = 16
    _fill_padded(xp_ref, x_ref, pad, seq)
    w = w_ref[...]
    for r0 in range(0, seq, rows):
        y = _conv_rows(xp_ref, w, DW_CONV, pad - DW_CONV // 2, r0, rows) + b_ref[...]
        o_ref[r0:r0 + rows, :] = y.astype(o_ref.dtype)


def _dw_conv(glu, conv_w, b_dw, bsz, seq):
    t = bsz * seq
    rows = min(256, seq)
    return pl.pallas_call(
        functools.partial(_dw_conv_kernel, seq=seq, rows=rows),
        out_shape=jax.ShapeDtypeStruct((t, WIDTH_B), BF16),
        grid=(bsz, WIDTH_B // LANES),
        in_specs=[pl.BlockSpec((seq, LANES), lambda b, j: (b, j)),
                  pl.BlockSpec((DW_CONV, LANES), lambda b, j: (0, j)),
                  pl.BlockSpec((1, LANES), lambda b, j: (0, j))],
        out_specs=pl.BlockSpec((seq, LANES), lambda b, j: (b, j)),
        scratch_shapes=[pltpu.VMEM((seq + 32, LANES), F32)],
        compiler_params=_params("parallel", "parallel"),
        name="dw_conv",
    )(glu, conv_w, b_dw.reshape(1, WIDTH_B))


def _bmm(a, b):
    return lax.dot_general(a, b, (((2,), (1,)), ((0,), (0,))), preferred_element_type=F32)


def _bmm_nt(a, b):
    return lax.dot_general(a, b, (((2,), (2,)), ((0,), (0,))), preferred_element_type=F32)


def _bmm_tn(a, b):
    return lax.dot_general(a, b, (((1,), (1,)), ((0,), (0,))), preferred_element_type=F32)


def _block_diag_rows(x, half):
    lane = lax.broadcasted_iota(jnp.int32, x.shape, 2)
    return jnp.concatenate([jnp.where(lane < half, x, 0.0), jnp.where(lane >= half, x, 0.0)], axis=1)


def _unit_tri_inverse(lmat, eye):
    def rhs(p):
        return _block_diag_rows(p, CHUNK).astype(BF16)

    x = eye - lmat
    p = _bmm(lmat.astype(BF16), rhs(lmat))
    s = 2
    while 2 * s < CHUNK:
        xp = _bmm(jnp.concatenate([x, p], axis=1).astype(BF16), rhs(p))
        x = x + xp[:, :CHUNK]
        p = xp[:, CHUNK:]
        s *= 2
    return x + _bmm(x.astype(BF16), rhs(p))


def _delta_kernel(qf_ref, kf_ref, vf_ref, gf_ref, gtpf_ref, qb_ref, kb_ref, vb_ref, gb_ref, gtpb_ref,
                  of_ref, ob_ref, s_ref, *, nc):
    @pl.when(pl.program_id(1) == 0)
    def _():
        s_ref[...] = jnp.zeros_like(s_ref)

    n_pairs = N_HEADS // 2
    n_inst = 2 * n_pairs
    pw = 2 * HEAD_DIM
    dirs = ((qf_ref, kf_ref, vf_ref, gf_ref, gtpf_ref, of_ref, False),
            (qb_ref, kb_ref, vb_ref, gb_ref, gtpb_ref, ob_ref, True))
    steps = [[d + ((nc - 1 - i) if d[6] else i,) for d in dirs] for i in range(nc)]

    ri = lax.broadcasted_iota(jnp.int32, (CHUNK, 2 * CHUNK), 0)
    ci = jnp.bitwise_and(lax.broadcasted_iota(jnp.int32, (CHUNK, 2 * CHUNK), 1), CHUNK - 1)
    inst = lax.broadcasted_iota(jnp.int32, (nc * n_inst, 1, 1), 0)
    sign = 1 - 2 * jnp.bitwise_and(jnp.right_shift(inst, n_pairs.bit_length() - 1), 1)
    rel = (ri - ci)[None] * sign
    incl = rel >= 0
    strict = rel > 0
    eye = (ri == ci).astype(F32)

    def pairs(which):
        return jnp.stack([d[which][d[7] * CHUNK:(d[7] + 1) * CHUNK, p * pw:(p + 1) * pw]
                          for st in steps for d in st for p in range(n_pairs)]).astype(F32)

    def pair_bcast(cols, width):
        return jnp.stack([jnp.concatenate([jnp.broadcast_to(cc[2 * p], (CHUNK, width)),
                                           jnp.broadcast_to(cc[2 * p + 1], (CHUNK, width))], axis=1)
                          for cc in cols for p in range(n_pairs)])

    qf = pairs(0)
    kf = pairs(1)
    vf = pairs(2)
    beta_c, g_c, glast_c = [], [], []
    for st in steps:
        for d in st:
            gblk = d[3][d[7] * CHUNK:(d[7] + 1) * CHUNK, :]
            off = N_HEADS if d[6] else 0
            last = 0 if d[6] else CHUNK - 1
            beta_c.append([gblk[:, off + hh:off + hh + 1] for hh in range(N_HEADS)])
            g_c.append([gblk[:, 2 * N_HEADS + off + hh:2 * N_HEADS + off + hh + 1] for hh in range(N_HEADS)])
            glast_c.append([gc[last:last + 1, :] for gc in g_c[-1]])
    beta = pair_bcast(beta_c, HEAD_DIM)
    eg = pair_bcast([[jnp.exp(gc) for gc in gcs] for gcs in g_c], HEAD_DIM)
    tail = pair_bcast([[jnp.exp(gl - gc) for gl, gc in zip(gls, gcs)] for gls, gcs in zip(glast_c, g_c)],
                      HEAD_DIM)
    gcol = pair_bcast(g_c, CHUNK)
    grow = jnp.stack([d[4][(n_pairs if d[6] else 0) + p:(n_pairs if d[6] else 0) + p + 1,
                           2 * d[7] * CHUNK:2 * (d[7] + 1) * CHUNK]
                      for st in steps for d in st for p in range(n_pairs)])

    decay = jnp.exp(jnp.where(incl, gcol - grow, NEG_BIG))
    kb = kf * beta
    kkqk = _bmm_nt(jnp.concatenate([kb, qf], axis=1).astype(BF16),
                   _block_diag_rows(kf, HEAD_DIM).astype(BF16))
    lmat = jnp.where(strict, kkqk[:, :CHUNK, :] * decay, 0.0)
    qk = (kkqk[:, CHUNK:, :] * decay).astype(BF16)
    tinv = _unit_tri_inverse(lmat, eye)
    rhs = jnp.concatenate([_block_diag_rows(vf * beta, HEAD_DIM),
                           _block_diag_rows(kb * eg, HEAD_DIM)], axis=2).astype(BF16)
    uw = _bmm(tinv.astype(BF16), rhs)
    u = uw[:, :, :pw]
    wq = jnp.concatenate([uw[:, :, pw:], qf * eg], axis=1).astype(BF16)
    kt = (kf * tail).astype(BF16)

    for i, st in enumerate(steps):
        sl = slice(i * n_inst, (i + 1) * n_inst)
        s_a = s_ref[:, 0]
        s_b = s_ref[:, 1]
        zero = jnp.zeros_like(s_a)
        s_bd = jnp.concatenate([jnp.concatenate([s_a, zero], axis=2),
                                jnp.concatenate([zero, s_b], axis=2)], axis=1).astype(BF16)
        ws = _bmm(wq[sl], s_bd)
        v_new = u[sl] - ws[:, :CHUNK, :]
        o = ws[:, CHUNK:, :] + _bmm(qk[sl], _block_diag_rows(v_new, HEAD_DIM).astype(BF16))
        v16 = v_new.astype(BF16)
        gl = [glast_c[2 * i + dd] for dd in range(2)]
        cd_a = jnp.stack([jnp.exp(gl[dd][2 * p]) for dd in range(2) for p in range(n_pairs)])
        cd_b = jnp.stack([jnp.exp(gl[dd][2 * p + 1]) for dd in range(2) for p in range(n_pairs)])
        s_ref[:, 0] = s_a * cd_a + _bmm_tn(kt[sl, :, :HEAD_DIM], v16[:, :, :HEAD_DIM])
        s_ref[:, 1] = s_b * cd_b + _bmm_tn(kt[sl, :, HEAD_DIM:], v16[:, :, HEAD_DIM:])
        for dd, d in enumerate(st):
            for p in range(n_pairs):
                d[5][d[7] * CHUNK:(d[7] + 1) * CHUNK, p * pw:(p + 1) * pw] = (
                    o[dd * n_pairs + p].astype(d[5].dtype))


def _delta_rule(qkv, g, gtp, bsz, seq):
    t = bsz * seq
    rows = min(DELTA_ROWS, seq)
    nblk = seq // rows

    def fwd(col):
        return lambda b, i: (b * nblk + i, col)

    def bwd(col):
        return lambda b, i: (b * nblk + nblk - 1 - i, col)

    def specs(m):
        return [pl.BlockSpec((rows, WIDTH_A), m(0)), pl.BlockSpec((rows, WIDTH_A), m(1)),
                pl.BlockSpec((rows, WIDTH_A), m(2)), pl.BlockSpec((rows, LANES), m(0)),
                pl.BlockSpec((N_HEADS, 2 * rows), lambda b, i, m=m: (0, m(0)(b, i)[0]))]

    out = jax.ShapeDtypeStruct((t, WIDTH_A), BF16)
    return pl.pallas_call(
        functools.partial(_delta_kernel, nc=rows // CHUNK),
        out_shape=(out, out),
        grid=(bsz, nblk),
        in_specs=specs(fwd) + specs(bwd),
        out_specs=(pl.BlockSpec((rows, WIDTH_A), fwd(0)), pl.BlockSpec((rows, WIDTH_A), bwd(0))),
        scratch_shapes=[pltpu.VMEM((N_HEADS, 2, HEAD_DIM, HEAD_DIM), F32)],
        compiler_params=_params("parallel", "arbitrary"),
        name="delta_rule",
    )(qkv, qkv, qkv, g, gtp, qkv, qkv, qkv, g, gtp)


def _mix_kernel(of_ref, ob_ref, z_ref, yc_ref, gate_ref, x_ref,
                eg_ref, eb_ref, ng_ref, wao_ref, cg_ref, cb_ref, wbo_ref, bbo_ref,
                wout_ref, l1g_ref, l1b_ref, wr_ref, br_ref,
                h1_ref, h1p_ref, logit_ref):
    o = of_ref[...].astype(F32) + ob_ref[...].astype(F32)
    z = z_ref[...].astype(F32)
    ng = ng_ref[...]
    parts = []
    for hh in range(N_HEADS):
        sl = slice(hh * HEAD_DIM, (hh + 1) * HEAD_DIM)
        oh = o[:, sl]
        zh = z[:, sl]
        inv = lax.rsqrt(jnp.mean(oh * oh, axis=-1, keepdims=True) + RMS_EPS)
        parts.append((oh * inv * ng * _silu(zh)).astype(BF16))
    y_a = _dot(jnp.concatenate(parts, axis=1), wao_ref[...])

    yc = _layer_norm(yc_ref[...].astype(F32), cg_ref[...], cb_ref[...])
    y_b = _dot(_silu(yc).astype(BF16), wbo_ref[...]) + bbo_ref[...]

    gates = gate_ref[...].astype(F32)
    mixed = gates[:, :D_MODEL] * y_a + gates[:, D_MODEL:] * y_b
    mix = _dot(mixed.astype(BF16), wout_ref[...])

    h0 = _layer_norm(x_ref[...], eg_ref[...], eb_ref[...])
    h1 = _layer_norm(DN_ALPHA * h0 + mix, l1g_ref[...], l1b_ref[...])
    h1_ref[...] = h1
    h1p_ref[...] = _pack_bf16_pair(h1[:, :D_MODEL // 2], h1[:, D_MODEL // 2:])
    h_hi, h_lo = _split_bf16(h1)
    p = _dot(h_hi, wr_ref[...])
    logit_ref[...] = p[:, :LANES] + p[:, LANES:] + _dot(h_lo, wr_ref[:, :LANES]) + br_ref[...]


def _mix(o_f, o_b, z, yc, gates, x2, emb_g, emb_b, norm_g, w_a_o, cg, cb, w_b_o, b_b_o,
         w_out, l1g, l1b, w_router, b_router):
    t, d = x2.shape
    tm = min(ROW_TILE, t)
    row = lambda i: (i, 0)
    const = lambda i: (0, 0)
    wr = jnp.concatenate(_split_bf16(jnp.pad(w_router, ((0, 0), (0, LANES - N_EXPERTS)))), axis=1)
    br = jnp.pad(b_router.reshape(1, N_EXPERTS), ((0, 0), (0, LANES - N_EXPERTS)), constant_values=NEG_BIG)
    vec = lambda a: a.reshape(1, -1)
    return pl.pallas_call(
        _mix_kernel,
        out_shape=(jax.ShapeDtypeStruct((t, d), F32), jax.ShapeDtypeStruct((t, d // 2), jnp.uint32),
                   jax.ShapeDtypeStruct((t, LANES), F32)),
        grid=(t // tm,),
        in_specs=[pl.BlockSpec((tm, d), row), pl.BlockSpec((tm, d), row),
                  pl.BlockSpec((tm, d), row),
                  pl.BlockSpec((tm, d), row), pl.BlockSpec((tm, 2 * d), row), pl.BlockSpec((tm, d), row),
                  pl.BlockSpec((1, d), const), pl.BlockSpec((1, d), const),
                  pl.BlockSpec((1, HEAD_DIM), const), pl.BlockSpec((d, d), const),
                  pl.BlockSpec((1, d), const), pl.BlockSpec((1, d), const),
                  pl.BlockSpec((d, d), const), pl.BlockSpec((1, d), const),
                  pl.BlockSpec((d, d), const), pl.BlockSpec((1, d), const), pl.BlockSpec((1, d), const),
                  pl.BlockSpec((d, 2 * LANES), const), pl.BlockSpec((1, LANES), const)],
        out_specs=(pl.BlockSpec((tm, d), row), pl.BlockSpec((tm, d // 2), row), pl.BlockSpec((tm, LANES), row)),
        compiler_params=_params("parallel"),
        name="mix",
    )(o_f, o_b, z, yc, gates, x2, vec(emb_g), vec(emb_b), vec(norm_g), w_a_o, vec(cg), vec(cb),
      w_b_o, vec(b_b_o), w_out, vec(l1g), vec(l1b), wr, br)


def _route_kernel(logit_ref, gate_ref, eidx_ref, rank_ref, start_ref, blk_ref, base_ref, *, bm):
    @pl.when(pl.program_id(0) == 0)
    def _():
        base_ref[...] = jnp.zeros_like(base_ref)

    x = logit_ref[...].T[:N_EXPERTS, :]
    tm = x.shape[1]
    expert = lax.broadcasted_iota(jnp.int32, x.shape, 0).astype(F32)
    tok = lax.broadcasted_iota(jnp.int32, x.shape, 1)
    sel = jnp.zeros(x.shape, F32)
    vals, idxs = [], []
    for _ in range(TOP_K):
        m = jnp.max(x, axis=0, keepdims=True)
        idx = jnp.min(jnp.where(x == m, expert, float(N_EXPERTS)), axis=0, keepdims=True)
        hit = expert == idx
        sel = sel + hit.astype(F32)
        x = jnp.where(hit, -3e38, x)
        vals.append(m)
        idxs.append(idx)

    exps = [jnp.exp(v - vals[0]) for v in vals]
    denom = exps[0]
    for e in exps[1:]:
        denom = denom + e

    csum = sel
    s = 1
    while s < tm:
        csum = csum + jnp.where(tok >= s, pltpu.roll(csum, s, axis=1), 0.0)
        s *= 2
    before = base_ref[...] + csum - sel

    krow = lax.broadcasted_iota(jnp.int32, (8, tm), 0)
    gate = jnp.zeros((8, tm), F32)
    eidx = jnp.zeros((8, tm), F32)
    rank = jnp.zeros((8, tm), F32)
    for k in range(TOP_K):
        rk = jnp.sum(jnp.where(expert == idxs[k], before, 0.0), axis=0, keepdims=True)
        gate = jnp.where(krow == k, exps[k] / denom, gate)
        eidx = jnp.where(krow == k, idxs[k], eidx)
        rank = jnp.where(krow == k, rk, rank)
    eidx_ref[...] = eidx.astype(jnp.int32)
    rank_ref[...] = rank.astype(jnp.int32)
    gate_ref[...] = jnp.concatenate([gate, jnp.zeros((LANES - 8, tm), F32)], axis=0).T
    total = base_ref[...] + csum[:, tm - 1:tm]
    base_ref[...] = total

    @pl.when(pl.program_id(0) == pl.num_programs(0) - 1)
    def _():
        shape = (N_EXPERTS, LANES)
        sub = lax.broadcasted_iota(jnp.int32, shape, 0)
        lane = lax.broadcasted_iota(jnp.int32, shape, 1)
        counts = jnp.broadcast_to(total, shape)
        padded = jnp.floor((counts + (bm - 1)) * (1.0 / bm)) * bm
        pad_end = padded
        s = 1
        while s < N_EXPERTS:
            pad_end = pad_end + jnp.where(sub >= s, pltpu.roll(pad_end, s, axis=0), 0.0)
            s *= 2
        pad_start = pad_end - padded
        start_ref[...] = pad_start.astype(jnp.int32)
        for j in range(blk_ref.shape[1] // LANES):
            bstart = ((lane + j * LANES) * bm).astype(F32)
            bexp = jnp.minimum(jnp.sum((bstart >= pad_end).astype(F32), axis=0, keepdims=True), N_EXPERTS - 1.0)
            used_end = jnp.sum(jnp.where(sub.astype(F32) == bexp, pad_start + counts, 0.0), axis=0, keepdims=True)
            brows = jnp.clip(used_end - bstart[0:1, :], 0.0, float(bm))
            blk_ref[:, j * LANES:(j + 1) * LANES] = jnp.where(sub == 0, bexp, jnp.where(sub == 1, brows, 0.0)
                                                              ).astype(jnp.int32)


def _route(logits):
    t = logits.shape[0]
    tm = min(ROW_TILE, t)
    bm = EXPERT_ROWS
    row = lambda i: (i, 0)
    n_blocks = -(-(t * TOP_K + N_EXPERTS * (bm - 1)) // bm)
    blk_lanes = -(-n_blocks // LANES) * LANES
    gate, eidx, rank, start, blk = pl.pallas_call(
        functools.partial(_route_kernel, bm=bm),
        out_shape=(jax.ShapeDtypeStruct((t, LANES), F32), jax.ShapeDtypeStruct((8, t), jnp.int32),
                   jax.ShapeDtypeStruct((8, t), jnp.int32), jax.ShapeDtypeStruct((N_EXPERTS, LANES), jnp.int32),
                   jax.ShapeDtypeStruct((N_EXPERTS, blk_lanes), jnp.int32)),
        grid=(t // tm,),
        in_specs=[pl.BlockSpec((tm, LANES), row)],
        out_specs=(pl.BlockSpec((tm, LANES), row), pl.BlockSpec((8, tm), lambda i: (0, i)),
                   pl.BlockSpec((8, tm), lambda i: (0, i)), pl.BlockSpec((N_EXPERTS, LANES), lambda i: (0, 0)),
                   pl.BlockSpec((N_EXPERTS, blk_lanes), lambda i: (0, 0))),
        scratch_shapes=[pltpu.VMEM((N_EXPERTS, 1), F32)],
        compiler_params=_params("arbitrary"),
        name="moe_route",
    )(logits)
    pad_start = start[:, 0]
    onehot = eidx[:TOP_K, :, None] == jnp.arange(N_EXPERTS, dtype=jnp.int32)
    pos_kmajor = (jnp.sum(jnp.where(onehot, pad_start, 0), axis=-1) + rank[:TOP_K]).reshape(-1)
    return gate, pos_kmajor.astype(jnp.int32), blk[0, :n_blocks], blk[1, :n_blocks], n_blocks


def _sc_worker_range(n_rows):
    per_worker = n_rows // (SC_CORES * SC_SUBCORES)
    wid = lax.axis_index("s") * SC_CORES + lax.axis_index("c")
    return wid * per_worker, per_worker


def _sc_scatter_rows(src, idx, n_out):
    t, d = src.shape
    n_copies = idx.shape[0] // t
    assert t % (SC_CORES * SC_SUBCORES * SC_GATHER_ROWS) == 0, "rows must split evenly over the subcores"
    mesh = plsc.VectorSubcoreMesh(core_axis_name="c", subcore_axis_name="s")

    @functools.partial(
        pl.kernel, mesh=mesh,
        out_type=jax.ShapeDtypeStruct((n_out, d), src.dtype),
        scratch_types=([pltpu.VMEM((SC_GATHER_ROWS,), jnp.int32)] * n_copies
                       + [pltpu.VMEM((SC_GATHER_ROWS, d), src.dtype)]
                       + [pltpu.SemaphoreType.DMA] * n_copies),
        name="sc_scatter_rows",
    )
    def scatter(src_hbm, idx_hbm, out_hbm, *scratch):
        idx_v = scratch[:n_copies]
        rows_v = scratch[n_copies]
        sems = scratch[n_copies + 1:]
        base, per_worker = _sc_worker_range(t)

        @pl.loop(0, per_worker // SC_GATHER_ROWS)
        def _(j):
            off = pl.multiple_of(base + j * SC_GATHER_ROWS, SC_GATHER_ROWS)
            for k in range(n_copies):
                pltpu.sync_copy(idx_hbm.at[pl.ds(k * t + off, SC_GATHER_ROWS)], idx_v[k])
            pltpu.sync_copy(src_hbm.at[pl.ds(off, SC_GATHER_ROWS)], rows_v)
            copies = [pltpu.async_copy(rows_v, out_hbm.at[idx_v[k]], sems[k]) for k in range(n_copies)]
            for c in copies:
                c.wait()

    return scatter(src, idx)


def _expert_kernel(be_ref, nr_ref, x_ref, wgu_ref, bgu_ref, wd_ref, bd_ref, o_ref, wgu16_ref, wd16_ref, *, tn):
    i = pl.program_id(0)
    active = nr_ref[i] > 0
    new_expert = jnp.logical_or(i == 0, be_ref[i] != be_ref[jnp.maximum(i - 1, 0)])

    @pl.when(jnp.logical_and(active, new_expert))
    def _():
        wgu16_ref[...] = wgu_ref[0].astype(BF16)
        wd16_ref[...] = wd_ref[0].astype(BF16)

    @pl.when(active)
    def _():
        half = D_MODEL // 2
        row = lax.broadcasted_iota(jnp.int32, x_ref.shape, 0)
        x = jnp.where(row < nr_ref[i], x_ref[...], jnp.uint32(0))
        x_lo, x_hi = _unpack_bf16_pair(x)
        x_lo = x_lo.astype(BF16)
        x_hi = x_hi.astype(BF16)

        def up(c0):
            return (_dot(x_lo, wgu16_ref[:half, c0:c0 + tn]) + _dot(x_hi, wgu16_ref[half:, c0:c0 + tn])
                    + bgu_ref[0, :, c0:c0 + tn])

        y = bd_ref[0]
        for n0 in range(0, D_FF, tn):
            glu = jnp.minimum(up(n0), SWIGLU_LIMIT)
            lin = jnp.clip(up(D_FF + n0), -SWIGLU_LIMIT, SWIGLU_LIMIT)
            act = glu * _sigmoid(SWIGLU_ALPHA * glu) * (lin + 1.0)
            y = y + _dot(act.astype(BF16), wd16_ref[n0:n0 + tn, :])
        o_ref[...] = _pack_bf16_pair(y[:, :half], y[:, half:])

    @pl.when(jnp.logical_not(active))
    def _():
        o_ref[...] = jnp.zeros_like(o_ref)


def _experts(xs, block_exp, block_rows, w_gu, b_gu, w_down, b_down, n_blocks):
    d = D_MODEL
    dp = xs.shape[1]
    bm = EXPERT_ROWS
    grid_spec = pltpu.PrefetchScalarGridSpec(
        num_scalar_prefetch=2,
        grid=(n_blocks,),
        in_specs=[pl.BlockSpec((bm, dp), lambda i, be, nb: (i, 0)),
                  pl.BlockSpec((1, d, 2 * D_FF), lambda i, be, nb: (be[i], 0, 0)),
                  pl.BlockSpec((1, 1, 2 * D_FF), lambda i, be, nb: (be[i], 0, 0)),
                  pl.BlockSpec((1, D_FF, d), lambda i, be, nb: (be[i], 0, 0)),
                  pl.BlockSpec((1, 1, d), lambda i, be, nb: (be[i], 0, 0))],
        out_specs=pl.BlockSpec((bm, dp), lambda i, be, nb: (i, 0)),
        scratch_shapes=[pltpu.VMEM((d, 2 * D_FF), BF16), pltpu.VMEM((D_FF, d), BF16)],
    )
    return pl.pallas_call(
        functools.partial(_expert_kernel, tn=EXPERT_SLAB),
        out_shape=jax.ShapeDtypeStruct((n_blocks * bm, dp), jnp.uint32),
        grid_spec=grid_spec,
        compiler_params=_params("arbitrary"),
        name="moe_experts",
    )(block_exp, block_rows, xs, w_gu, b_gu.reshape(N_EXPERTS, 1, 2 * D_FF), w_down,
      b_down.reshape(N_EXPERTS, 1, d))


def _sc_gather_rows(table, idx):
    m = idx.shape[0]
    d = table.shape[1]
    assert m % (SC_CORES * SC_SUBCORES * 2 * SC_GATHER_ROWS) == 0, "rows must split evenly over the subcores"
    mesh = plsc.VectorSubcoreMesh(core_axis_name="c", subcore_axis_name="s")

    @functools.partial(
        pl.kernel, mesh=mesh,
        out_type=jax.ShapeDtypeStruct((m, d), table.dtype),
        scratch_types=([pltpu.VMEM((SC_GATHER_ROWS,), jnp.int32)] * 2
                       + [pltpu.VMEM((SC_GATHER_ROWS, d), table.dtype)] * 2
                       + [pltpu.SemaphoreType.DMA] * 4),
        name="sc_gather_rows",
    )
    def gather(table_hbm, idx_hbm, out_hbm, idx0, idx1, rows0, rows1, g0, g1, w0, w1):
        base, per_worker = _sc_worker_range(m)

        @pl.loop(0, per_worker // (2 * SC_GATHER_ROWS))
        def _(j):
            off0 = pl.multiple_of(base + 2 * j * SC_GATHER_ROWS, SC_GATHER_ROWS)
            off1 = off0 + SC_GATHER_ROWS
            pltpu.sync_copy(idx_hbm.at[pl.ds(off0, SC_GATHER_ROWS)], idx0)
            gather0 = pltpu.async_copy(table_hbm.at[idx0], rows0, g0)
            pltpu.sync_copy(idx_hbm.at[pl.ds(off1, SC_GATHER_ROWS)], idx1)
            gather1 = pltpu.async_copy(table_hbm.at[idx1], rows1, g1)
            gather0.wait()
            write0 = pltpu.async_copy(rows0, out_hbm.at[pl.ds(off0, SC_GATHER_ROWS)], w0)
            gather1.wait()
            write1 = pltpu.async_copy(rows1, out_hbm.at[pl.ds(off1, SC_GATHER_ROWS)], w1)
            write0.wait()
            write1.wait()

    return gather(table, idx)


def _combine_dense_kernel(y0_ref, y1_ref, y2_ref, y3_ref, gate_ref, h_ref, g_ref, b_ref, o_ref):
    gate = gate_ref[...]
    f_lo = f_hi = None
    for k, y_ref in enumerate((y0_ref, y1_ref, y2_ref, y3_ref)):
        y_lo, y_hi = _unpack_bf16_pair(y_ref[...])
        gk = gate[:, k:k + 1]
        f_lo = gk * y_lo if f_lo is None else f_lo + gk * y_lo
        f_hi = gk * y_hi if f_hi is None else f_hi + gk * y_hi
    f = jnp.concatenate([f_lo, f_hi], axis=1)
    o_ref[...] = _layer_norm(DN_ALPHA * h_ref[...] + f, g_ref[...], b_ref[...])


def _combine_dense(yg, gate, h1, ln_g, ln_b):
    t, d = h1.shape
    tm = min(ROW_TILE, t)
    nt = t // tm
    dp = yg.shape[1]
    slab = lambda k: pl.BlockSpec((tm, dp), lambda i, k=k: (k * nt + i, 0))
    return pl.pallas_call(
        _combine_dense_kernel,
        out_shape=jax.ShapeDtypeStruct((t, d), F32),
        grid=(nt,),
        in_specs=[slab(0), slab(1), slab(2), slab(3),
                  pl.BlockSpec((tm, LANES), lambda i: (i, 0)),
                  pl.BlockSpec((tm, d), lambda i: (i, 0)),
                  pl.BlockSpec((1, d), lambda i: (0, 0)),
                  pl.BlockSpec((1, d), lambda i: (0, 0))],
        out_specs=pl.BlockSpec((tm, d), lambda i: (i, 0)),
        compiler_params=_params("parallel"),
        name="moe_combine",
    )(yg, yg, yg, yg, gate, h1, ln_g.reshape(1, d), ln_b.reshape(1, d))


def kernel(x, emb_ln_g, emb_ln_b, w_in, conv_qkv, a_log, dt_bias, dn_norm_g, w_a_o, b_glu, conv_dw, b_dw, conv_ln_g, conv_ln_b, w_b_o, b_b_o, b_gate, w_out, ln1_g, ln1_b, w_router, b_router, w_gu, b_gu, w_down, b_down, ln2_g, ln2_b):
    bsz, seq, d = x.shape
    t = bsz * seq
    assert d == D_MODEL and seq % min(DELTA_ROWS, seq) == 0 and seq % 256 == 0 and t % min(ROW_TILE, t) == 0
    x2 = x.reshape(t, d)
    qkv, z, g, gtp, glu, gates = _inproj(x2, emb_ln_g, emb_ln_b, w_in[0], conv_qkv[0], b_glu[0], b_gate[0], a_log[0],
                                         dt_bias[0], seq)
    o_f, o_b = _delta_rule(qkv, g, gtp, bsz, seq)
    yc = _dw_conv(glu, conv_dw[0], b_dw[0], bsz, seq)

    h1, h1p, logits = _mix(o_f, o_b, z, yc, gates, x2, emb_ln_g, emb_ln_b, dn_norm_g[0],
                           w_a_o[0].astype(BF16), conv_ln_g[0], conv_ln_b[0], w_b_o[0].astype(BF16), b_b_o[0],
                           w_out[0].astype(BF16), ln1_g[0], ln1_b[0], w_router[0], b_router[0])

    gate, pos_kmajor, block_exp, block_rows, n_blocks = _route(logits)
    xs = _sc_scatter_rows(h1p, pos_kmajor, n_blocks * EXPERT_ROWS)
    ys = _experts(xs, block_exp, block_rows, w_gu[0], b_gu[0], w_down[0], b_down[0], n_blocks)
    yg = _sc_gather_rows(ys, pos_kmajor)
    out = _combine_dense(yg, gate, h1, ln2_g[0], ln2_b[0])
    return out.reshape(bsz, seq, d)
```

```python
import functools

import jax
import jax.numpy as jnp
from jax import lax
from jax.experimental import pallas as pl
from jax.experimental.pallas import tpu as pltpu
from jax.experimental.pallas import tpu_sc as plsc

F32 = jnp.float32
BF16 = jnp.bfloat16

D_MODEL = 1024
N_HEADS = 8
HEAD_DIM = 128
WIDTH_A = N_HEADS * HEAD_DIM
SHORT_CONV = 5
CHUNK = 64
WIDTH_B = D_MODEL
DW_CONV = 31
N_EXPERTS = 32
TOP_K = 4
D_FF = D_MODEL
SWIGLU_ALPHA = 1.702
SWIGLU_LIMIT = 7.0
DN_ALPHA = 2.0 ** 0.25
LN_EPS = 1e-5
RMS_EPS = 1e-6
L2_EPS = 1e-6
LANES = 128
NEG_BIG = -1e30

ROW_TILE = 512
CONV_HALO = 16
DELTA_ROWS = 512
EXPERT_ROWS = 512
EXPERT_SLAB = 512
SC_CORES = 2
SC_SUBCORES = 16
SC_GATHER_ROWS = 64
VMEM_LIMIT = 56 * 1024 * 1024


def _params(*sem):
    return pltpu.CompilerParams(dimension_semantics=sem, vmem_limit_bytes=VMEM_LIMIT)


def _layer_norm(x, g, b):
    mu = jnp.mean(x, axis=-1, keepdims=True)
    xc = x - mu
    var = jnp.mean(xc * xc, axis=-1, keepdims=True)
    return xc * lax.rsqrt(var + LN_EPS) * g + b


def _sigmoid(x):
    return 0.5 * jnp.tanh(0.5 * x) + 0.5


def _silu(x):
    h = 0.5 * x
    return h + h * jnp.tanh(h)


def _dot(a, b):
    return jnp.dot(a, b, preferred_element_type=F32)


def _dot_nt(a, b):
    return lax.dot_general(a, b, (((1,), (1,)), ((), ())), preferred_element_type=F32)


def _pack_bf16_pair(a, b):
    ua = lax.bitcast_convert_type(a.astype(BF16).astype(F32), jnp.uint32)
    ub = lax.bitcast_convert_type(b.astype(BF16).astype(F32), jnp.uint32)
    return (ua >> 16) | ub


def _unpack_bf16_pair(p):
    a = lax.bitcast_convert_type(p << 16, F32)
    b = lax.bitcast_convert_type(p & jnp.uint32(0xFFFF0000), F32)
    return a, b


def _split_bf16(a):
    hi = a.astype(BF16)
    return hi, (a - hi.astype(F32)).astype(BF16)


def _chunk_cumsum(x, reverse):
    rows = x.shape[0]
    pos = lax.broadcasted_iota(jnp.int32, x.shape, 0) % CHUNK
    s = 1
    while s < CHUNK:
        if reverse:
            shifted = pltpu.roll(x, rows - s, axis=0)
            x = x + jnp.where(pos < CHUNK - s, shifted, 0.0)
        else:
            shifted = pltpu.roll(x, s, axis=0)
            x = x + jnp.where(pos >= s, shifted, 0.0)
        s *= 2
    return x


def _inproj_kernel(x_ref, xp_ref, xn_ref, eg_ref, eb_ref, wq_ref, cw_ref, ws_ref, wga_ref, wgb_ref, bga_ref, bgb_ref,
                   wgt_ref, bgt_ref, alog_ref, dtb_ref, qkv_ref, z_ref, g_ref, gt_ref, glu_ref, gate_ref, up_ref,
                   *, tn, tiles_per_seq):
    tm = x_ref.shape[0]
    halo = xp_ref.shape[0]
    i = pl.program_id(0)
    pos = i % tiles_per_seq
    h32 = _layer_norm(x_ref[...], eg_ref[...], eb_ref[...])
    h = h32.astype(BF16)
    h_prev = jnp.where(pos > 0, _layer_norm(xp_ref[...], eg_ref[...], eb_ref[...]), 0.0)
    h_next = jnp.where(pos < tiles_per_seq - 1, _layer_norm(xn_ref[...], eg_ref[...], eb_ref[...]), 0.0)
    h_ext = jnp.concatenate([h_prev, h32, h_next], axis=0).astype(BF16)

    base = halo - SHORT_CONV // 2
    for n0 in range(0, qkv_ref.shape[1], tn):
        u = _dot_nt(h_ext, wq_ref[n0:n0 + tn, :])
        for p in range(tn // HEAD_DIM):
            plane = n0 // HEAD_DIM + p
            cols = slice(n0 + p * HEAD_DIM, n0 + (p + 1) * HEAD_DIM)
            up_ref[plane] = u[:, p * HEAD_DIM:(p + 1) * HEAD_DIM]
            w = cw_ref[:, cols]
            acc = up_ref[plane, base:base + tm, :] * w[0:1, :]
            for k in range(1, SHORT_CONV):
                acc = acc + up_ref[plane, base + k:base + k + tm, :] * w[k:k + 1, :]
            y = _silu(acc)
            if plane < 2 * N_HEADS:
                inv = lax.rsqrt(jnp.sum(y * y, axis=-1, keepdims=True) + L2_EPS)
                y = y * (inv * (HEAD_DIM ** -0.5) if plane < N_HEADS else inv)
            qkv_ref[:, cols] = y.astype(qkv_ref.dtype)

    c_z = qkv_ref.shape[1]
    for n0 in range(0, z_ref.shape[1], tn):
        z_ref[:, n0:n0 + tn] = _dot_nt(h, wq_ref[c_z + n0:c_z + n0 + tn, :]).astype(z_ref.dtype)

    us = _dot(h, ws_ref[...])
    lane = lax.broadcasted_iota(jnp.int32, us.shape, 1)
    beta = _sigmoid(us)
    xs = us + dtb_ref[...]
    softplus = jnp.maximum(xs, 0.0) + jnp.log(1.0 + jnp.exp(-jnp.abs(xs)))
    log_a = -jnp.exp(alog_ref[...]) * softplus
    g_fwd = _chunk_cumsum(log_a, reverse=False)
    g_bwd = _chunk_cumsum(log_a, reverse=True)
    odd_group = jnp.bitwise_and(jnp.right_shift(lane, (N_HEADS // 2).bit_length() - 1), 1) == 1
    bwd = ((lane >= 3 * N_HEADS) & (lane < 4 * N_HEADS)) | ((lane >= 4 * N_HEADS) & odd_group)
    gates = jnp.where(lane < 2 * N_HEADS, beta, jnp.where(bwd, g_bwd, g_fwd))
    g_ref[...] = gates
    g_t = gates.T
    first = g_t[4 * N_HEADS:5 * N_HEADS, :]
    second = g_t[5 * N_HEADS:6 * N_HEADS, :]
    gt_ref[...] = jnp.concatenate([part[:, c * CHUNK:(c + 1) * CHUNK]
                                   for c in range(tm // CHUNK) for part in (first, second)], axis=1)

    for n0 in range(0, glu_ref.shape[1], tn):
        lin = _dot_nt(h, wga_ref[n0:n0 + tn, :]) + bga_ref[:, n0:n0 + tn]
        gt = _dot_nt(h, wgb_ref[n0:n0 + tn, :]) + bgb_ref[:, n0:n0 + tn]
        glu_ref[:, n0:n0 + tn] = (lin * _sigmoid(gt)).astype(glu_ref.dtype)

    for n0 in range(0, gate_ref.shape[1], tn):
        gate_ref[:, n0:n0 + tn] = _sigmoid(_dot_nt(h, wgt_ref[n0:n0 + tn, :])
                                           + bgt_ref[:, n0:n0 + tn]).astype(gate_ref.dtype)


def _inproj(x2, emb_g, emb_b, w_in, conv_w, b_glu, b_gate, a_log, dt_bias, seq):
    t, d = x2.shape
    tm = min(ROW_TILE, seq)
    assert seq % tm == 0 and tm % CONV_HALO == 0
    tiles_per_seq = seq // tm
    hb = tm // CONV_HALO
    last_hb = t // CONV_HALO - 1
    c0 = 4 * WIDTH_A
    c1 = c0 + 4 * N_HEADS
    c2 = c1 + 2 * WIDTH_B
    wt = jnp.swapaxes(w_in, 0, 1).astype(BF16)
    ws32 = wt[c0:c1].T
    pad = LANES - 6 * N_HEADS
    perm = jnp.array([dd * N_HEADS + 2 * p + hh for hh in range(2) for dd in range(2) for p in range(N_HEADS // 2)],
                     jnp.int32)
    twice = lambda a: jnp.concatenate([a, a[:, perm]], axis=1)
    w_small = jnp.pad(jnp.concatenate([ws32[:, :2 * N_HEADS], twice(ws32[:, 2 * N_HEADS:])], axis=1),
                      ((0, 0), (0, pad)))
    alog = jnp.pad(twice(a_log.reshape(1, 2 * N_HEADS)), ((0, 0), (2 * N_HEADS, pad)))
    dtb = jnp.pad(twice(dt_bias.reshape(1, 2 * N_HEADS)), ((0, 0), (2 * N_HEADS, pad)))
    row = lambda i: (i, 0)

    def const(shape):
        return pl.BlockSpec(shape, lambda i: (0, 0), pipeline_mode=pl.Buffered(1))

    vec = lambda a: a.reshape(1, -1)
    bf = lambda n: jax.ShapeDtypeStruct((t, n), BF16)
    return pl.pallas_call(
        functools.partial(_inproj_kernel, tn=512, tiles_per_seq=tiles_per_seq),
        out_shape=(bf(3 * WIDTH_A), bf(WIDTH_A), jax.ShapeDtypeStruct((t, LANES), F32),
                   jax.ShapeDtypeStruct((N_HEADS, 2 * t), F32), bf(WIDTH_B), bf(2 * d)),
        grid=(t // tm,),
        in_specs=[pl.BlockSpec((tm, d), row),
                  pl.BlockSpec((CONV_HALO, d), lambda i: (jnp.maximum(i * hb - 1, 0), 0)),
                  pl.BlockSpec((CONV_HALO, d), lambda i: (jnp.minimum((i + 1) * hb, last_hb), 0)),
                  const((1, d)), const((1, d)),
                  const((c0, d)), const((SHORT_CONV, 3 * WIDTH_A)), const((d, LANES)),
                  const((WIDTH_B, d)), const((WIDTH_B, d)), const((1, WIDTH_B)), const((1, WIDTH_B)),
                  const((2 * d, d)), const((1, 2 * d)), const((1, LANES)), const((1, LANES))],
        out_specs=(pl.BlockSpec((tm, 3 * WIDTH_A), row), pl.BlockSpec((tm, WIDTH_A), row),
                   pl.BlockSpec((tm, LANES), row), pl.BlockSpec((N_HEADS, 2 * tm), lambda i: (0, i)),
                   pl.BlockSpec((tm, WIDTH_B), row), pl.BlockSpec((tm, 2 * d), row)),
        scratch_shapes=[pltpu.VMEM((3 * N_HEADS, tm + 2 * CONV_HALO, HEAD_DIM), F32)],
        compiler_params=_params("parallel"),
        name="inproj",
    )(x2, x2, x2, vec(emb_g), vec(emb_b), wt[:c0], conv_w, w_small, wt[c1:c1 + WIDTH_B],
      wt[c1 + WIDTH_B:c2], vec(b_glu[:WIDTH_B]), vec(b_glu[WIDTH_B:]), wt[c2:], vec(b_gate), alog, dtb)


def _conv_rows(xp_ref, w, taps, base, r0, rows):
    acc = xp_ref[base + r0:base + r0 + rows, :] * w[0:1, :]
    for k in range(1, taps):
        acc = acc + xp_ref[base + k + r0:base + k + r0 + rows, :] * w[k:k + 1, :]
    return acc


def _fill_padded(xp_ref, x_ref, pad, seq):
    zeros = jnp.zeros((pad, xp_ref.shape[1]), F32)
    xp_ref[0:pad, :] = zeros
    xp_ref[pad + seq:pad + seq + pad, :] = zeros
    xp_ref[pad:pad + seq, :] = x_ref[...].astype(F32)


def _dw_conv_kernel(x_ref, w_ref, b_ref, o_ref, xp_ref, *, seq, rows):
    pad = 16
    _fill_padded(xp_ref, x_ref, pad, seq)
    w = w_ref[...]
    for r0 in range(0, seq, rows):
        y = _conv_rows(xp_ref, w, DW_CONV, pad - DW_CONV // 2, r0, rows) + b_ref[...]
        o_ref[r0:r0 + rows, :] = y.astype(o_ref.dtype)


def _dw_conv(glu, conv_w, b_dw, bsz, seq):
    t = bsz * seq
    rows = min(256, seq)
    return pl.pallas_call(
        functools.partial(_dw_conv_kernel, seq=seq, rows=rows),
        out_shape=jax.ShapeDtypeStruct((t, WIDTH_B), BF16),
        grid=(bsz, WIDTH_B // LANES),
        in_specs=[pl.BlockSpec((seq, LANES), lambda b, j: (b, j)),
                  pl.BlockSpec((DW_CONV, LANES), lambda b, j: (0, j)),
                  pl.BlockSpec((1, LANES), lambda b, j: (0, j))],
        out_specs=pl.BlockSpec((seq, LANES), lambda b, j: (b, j)),
        scratch_shapes=[pltpu.VMEM((seq + 32, LANES), F32)],
        compiler_params=_params("parallel", "parallel"),
        name="dw_conv",
    )(glu, conv_w, b_dw.reshape(1, WIDTH_B))


def _bmm(a, b):
    return lax.dot_general(a, b, (((2,), (1,)), ((0,), (0,))), preferred_element_type=F32)


def _bmm_nt(a, b):
    return lax.dot_general(a, b, (((2,), (2,)), ((0,), (0,))), preferred_element_type=F32)


def _bmm_tn(a, b):
    return lax.dot_general(a, b, (((1,), (1,)), ((0,), (0,))), preferred_element_type=F32)


def _block_diag_rows(x, half):
    lane = lax.broadcasted_iota(jnp.int32, x.shape, 2)
    return jnp.concatenate([jnp.where(lane < half, x, 0.0), jnp.where(lane >= half, x, 0.0)], axis=1)


def _unit_tri_inverse(lmat, eye):
    def rhs(p):
        return _block_diag_rows(p, CHUNK).astype(BF16)

    x = eye - lmat
    p = _bmm(lmat.astype(BF16), rhs(lmat))
    s = 2
    while 2 * s < CHUNK:
        xp = _bmm(jnp.concatenate([x, p], axis=1).astype(BF16), rhs(p))
        x = x + xp[:, :CHUNK]
        p = xp[:, CHUNK:]
        s *= 2
    return x + _bmm(x.astype(BF16), rhs(p))


def _delta_kernel(qf_ref, kf_ref, vf_ref, gf_ref, gtpf_ref, qb_ref, kb_ref, vb_ref, gb_ref, gtpb_ref,
                  of_ref, ob_ref, s_ref, *, nc):
    @pl.when(pl.program_id(1) == 0)
    def _():
        s_ref[...] = jnp.zeros_like(s_ref)

    n_pairs = N_HEADS // 2
    n_inst = 2 * n_pairs
    pw = 2 * HEAD_DIM
    dirs = ((qf_ref, kf_ref, vf_ref, gf_ref, gtpf_ref, of_ref, False),
            (qb_ref, kb_ref, vb_ref, gb_ref, gtpb_ref, ob_ref, True))
    steps = [[d + ((nc - 1 - i) if d[6] else i,) for d in dirs] for i in range(nc)]

    ri = lax.broadcasted_iota(jnp.int32, (CHUNK, 2 * CHUNK), 0)
    ci = jnp.bitwise_and(lax.broadcasted_iota(jnp.int32, (CHUNK, 2 * CHUNK), 1), CHUNK - 1)
    inst = lax.broadcasted_iota(jnp.int32, (nc * n_inst, 1, 1), 0)
    sign = 1 - 2 * jnp.bitwise_and(jnp.right_shift(inst, n_pairs.bit_length() - 1), 1)
    rel = (ri - ci)[None] * sign
    incl = rel >= 0
    strict = rel > 0
    eye = (ri == ci).astype(F32)

    def pairs(which):
        return jnp.stack([d[which][d[7] * CHUNK:(d[7] + 1) * CHUNK, p * pw:(p + 1) * pw]
                          for st in steps for d in st for p in range(n_pairs)]).astype(F32)

    def pair_bcast(cols, width):
        return jnp.stack([jnp.concatenate([jnp.broadcast_to(cc[2 * p], (CHUNK, width)),
                                           jnp.broadcast_to(cc[2 * p + 1], (CHUNK, width))], axis=1)
                          for cc in cols for p in range(n_pairs)])

    qf = pairs(0)
    kf = pairs(1)
    vf = pairs(2)
    beta_c, g_c, glast_c = [], [], []
    for st in steps:
        for d in st:
            gblk = d[3][d[7] * CHUNK:(d[7] + 1) * CHUNK, :]
            off = N_HEADS if d[6] else 0
            last = 0 if d[6] else CHUNK - 1
            beta_c.append([gblk[:, off + hh:off + hh + 1] for hh in range(N_HEADS)])
            g_c.append([gblk[:, 2 * N_HEADS + off + hh:2 * N_HEADS + off + hh + 1] for hh in range(N_HEADS)])
            glast_c.append([gc[last:last + 1, :] for gc in g_c[-1]])
    beta = pair_bcast(beta_c, HEAD_DIM)
    eg = pair_bcast([[jnp.exp(gc) for gc in gcs] for gcs in g_c], HEAD_DIM)
    tail = pair_bcast([[jnp.exp(gl - gc) for gl, gc in zip(gls, gcs)] for gls, gcs in zip(glast_c, g_c)],
                      HEAD_DIM)
    gcol = pair_bcast(g_c, CHUNK)
    grow = jnp.stack([d[4][(n_pairs if d[6] else 0) + p:(n_pairs if d[6] else 0) + p + 1,
                           2 * d[7] * CHUNK:2 * (d[7] + 1) * CHUNK]
                      for st in steps for d in st for p in range(n_pairs)])

    decay = jnp.exp(jnp.where(incl, gcol - grow, NEG_BIG))
    kb = kf * beta
    kkqk = _bmm_nt(jnp.concatenate([kb, qf], axis=1).astype(BF16),
                   _block_diag_rows(kf, HEAD_DIM).astype(BF16))
    lmat = jnp.where(strict, kkqk[:, :CHUNK, :] * decay, 0.0)
    qk = (kkqk[:, CHUNK:, :] * decay).astype(BF16)
    tinv = _unit_tri_inverse(lmat, eye)
    rhs = jnp.concatenate([_block_diag_rows(vf * beta, HEAD_DIM),
                           _block_diag_rows(kb * eg, HEAD_DIM)], axis=2).astype(BF16)
    uw = _bmm(tinv.astype(BF16), rhs)
    u = uw[:, :, :pw]
    wq = jnp.concatenate([uw[:, :, pw:], qf * eg], axis=1).astype(BF16)
    kt = (kf * tail).astype(BF16)

    for i, st in enumerate(steps):
        sl = slice(i * n_inst, (i + 1) * n_inst)
        s_a = s_ref[:, 0]
        s_b = s_ref[:, 1]
        zero = jnp.zeros_like(s_a)
        s_bd = jnp.concatenate([jnp.concatenate([s_a, zero], axis=2),
                                jnp.concatenate([zero, s_b], axis=2)], axis=1).astype(BF16)
        ws = _bmm(wq[sl], s_bd)
        v_new = u[sl] - ws[:, :CHUNK, :]
        o = ws[:, CHUNK:, :] + _bmm(qk[sl], _block_diag_rows(v_new, HEAD_DIM).astype(BF16))
        v16 = v_new.astype(BF16)
        gl = [glast_c[2 * i + dd] for dd in range(2)]
        cd_a = jnp.stack([jnp.exp(gl[dd][2 * p]) for dd in range(2) for p in range(n_pairs)])
        cd_b = jnp.stack([jnp.exp(gl[dd][2 * p + 1]) for dd in range(2) for p in range(n_pairs)])
        s_ref[:, 0] = s_a * cd_a + _bmm_tn(kt[sl, :, :HEAD_DIM], v16[:, :, :HEAD_DIM])
        s_ref[:, 1] = s_b * cd_b + _bmm_tn(kt[sl, :, HEAD_DIM:], v16[:, :, HEAD_DIM:])
        for dd, d in enumerate(st):
            for p in range(n_pairs):
                d[5][d[7] * CHUNK:(d[7] + 1) * CHUNK, p * pw:(p + 1) * pw] = (
                    o[dd * n_pairs + p].astype(d[5].dtype))


def _delta_rule(qkv, g, gtp, bsz, seq):
    t = bsz * seq
    rows = min(DELTA_ROWS, seq)
    nblk = seq // rows

    def fwd(col):
        return lambda b, i: (b * nblk + i, col)

    def bwd(col):
        return lambda b, i: (b * nblk + nblk - 1 - i, col)

    def specs(m):
        return [pl.BlockSpec((rows, WIDTH_A), m(0)), pl.BlockSpec((rows, WIDTH_A), m(1)),
                pl.BlockSpec((rows, WIDTH_A), m(2)), pl.BlockSpec((rows, LANES), m(0)),
                pl.BlockSpec((N_HEADS, 2 * rows), lambda b, i, m=m: (0, m(0)(b, i)[0]))]

    out = jax.ShapeDtypeStruct((t, WIDTH_A), BF16)
    return pl.pallas_call(
        functools.partial(_delta_kernel, nc=rows // CHUNK),
        out_shape=(out, out),
        grid=(bsz, nblk),
        in_specs=specs(fwd) + specs(bwd),
        out_specs=(pl.BlockSpec((rows, WIDTH_A), fwd(0)), pl.BlockSpec((rows, WIDTH_A), bwd(0))),
        scratch_shapes=[pltpu.VMEM((N_HEADS, 2, HEAD_DIM, HEAD_DIM), F32)],
        compiler_params=_params("parallel", "arbitrary"),
        name="delta_rule",
    )(qkv, qkv, qkv, g, gtp, qkv, qkv, qkv, g, gtp)


def _mix_kernel(of_ref, ob_ref, z_ref, yc_ref, gate_ref, x_ref,
                eg_ref, eb_ref, ng_ref, wao_ref, cg_ref, cb_ref, wbo_ref, bbo_ref,
                wout_ref, l1g_ref, l1b_ref, wr_ref, br_ref,
                h1_ref, h1p_ref, logit_ref):
    o = of_ref[...].astype(F32) + ob_ref[...].astype(F32)
    z = z_ref[...].astype(F32)
    ng = ng_ref[...]
    parts = []
    for hh in range(N_HEADS):
        sl = slice(hh * HEAD_DIM, (hh + 1) * HEAD_DIM)
        oh = o[:, sl]
        zh = z[:, sl]
        inv = lax.rsqrt(jnp.mean(oh * oh, axis=-1, keepdims=True) + RMS_EPS)
        parts.append((oh * inv * ng * _silu(zh)).astype(BF16))
    y_a = _dot(jnp.concatenate(parts, axis=1), wao_ref[...])

    yc = _layer_norm(yc_ref[...].astype(F32), cg_ref[...], cb_ref[...])
    y_b = _dot(_silu(yc).astype(BF16), wbo_ref[...]) + bbo_ref[...]

    gates = gate_ref[...].astype(F32)
    mixed = gates[:, :D_MODEL] * y_a + gates[:, D_MODEL:] * y_b
    mix = _dot(mixed.astype(BF16), wout_ref[...])

    h0 = _layer_norm(x_ref[...], eg_ref[...], eb_ref[...])
    h1 = _layer_norm(DN_ALPHA * h0 + mix, l1g_ref[...], l1b_ref[...])
    h1_ref[...] = h1
    h1p_ref[...] = _pack_bf16_pair(h1[:, :D_MODEL // 2], h1[:, D_MODEL // 2:])
    h_hi, h_lo = _split_bf16(h1)
    p = _dot(h_hi, wr_ref[...])
    logit_ref[...] = p[:, :LANES] + p[:, LANES:] + _dot(h_lo, wr_ref[:, :LANES]) + br_ref[...]


def _mix(o_f, o_b, z, yc, gates, x2, emb_g, emb_b, norm_g, w_a_o, cg, cb, w_b_o, b_b_o,
         w_out, l1g, l1b, w_router, b_router):
    t, d = x2.shape
    tm = min(ROW_TILE, t)
    row = lambda i: (i, 0)
    const = lambda i: (0, 0)
    wr = jnp.concatenate(_split_bf16(jnp.pad(w_router, ((0, 0), (0, LANES - N_EXPERTS)))), axis=1)
    br = jnp.pad(b_router.reshape(1, N_EXPERTS), ((0, 0), (0, LANES - N_EXPERTS)), constant_values=NEG_BIG)
    vec = lambda a: a.reshape(1, -1)
    return pl.pallas_call(
        _mix_kernel,
        out_shape=(jax.ShapeDtypeStruct((t, d), F32), jax.ShapeDtypeStruct((t, d // 2), jnp.uint32),
                   jax.ShapeDtypeStruct((t, LANES), F32)),
        grid=(t // tm,),
        in_specs=[pl.BlockSpec((tm, d), row), pl.BlockSpec((tm, d), row),
                  pl.BlockSpec((tm, d), row),
                  pl.BlockSpec((tm, d), row), pl.BlockSpec((tm, 2 * d), row), pl.BlockSpec((tm, d), row),
                  pl.BlockSpec((1, d), const), pl.BlockSpec((1, d), const),
                  pl.BlockSpec((1, HEAD_DIM), const), pl.BlockSpec((d, d), const),
                  pl.BlockSpec((1, d), const), pl.BlockSpec((1, d), const),
                  pl.BlockSpec((d, d), const), pl.BlockSpec((1, d), const),
                  pl.BlockSpec((d, d), const), pl.BlockSpec((1, d), const), pl.BlockSpec((1, d), const),
                  pl.BlockSpec((d, 2 * LANES), const), pl.BlockSpec((1, LANES), const)],
        out_specs=(pl.BlockSpec((tm, d), row), pl.BlockSpec((tm, d // 2), row), pl.BlockSpec((tm, LANES), row)),
        compiler_params=_params("parallel"),
        name="mix",
    )(o_f, o_b, z, yc, gates, x2, vec(emb_g), vec(emb_b), vec(norm_g), w_a_o, vec(cg), vec(cb),
      w_b_o, vec(b_b_o), w_out, vec(l1g), vec(l1b), wr, br)


def _route_kernel(logit_ref, gate_ref, eidx_ref, rank_ref, start_ref, blk_ref, base_ref, *, bm):
    @pl.when(pl.program_id(0) == 0)
    def _():
        base_ref[...] = jnp.zeros_like(base_ref)

    x = logit_ref[...].T[:N_EXPERTS, :]
    tm = x.shape[1]
    expert = lax.broadcasted_iota(jnp.int32, x.shape, 0).astype(F32)
    tok = lax.broadcasted_iota(jnp.int32, x.shape, 1)
    sel = jnp.zeros(x.shape, F32)
    vals, idxs = [], []
    for _ in range(TOP_K):
        m = jnp.max(x, axis=0, keepdims=True)
        idx = jnp.min(jnp.where(x == m, expert, float(N_EXPERTS)), axis=0, keepdims=True)
        hit = expert == idx
        sel = sel + hit.astype(F32)
        x = jnp.where(hit, -3e38, x)
        vals.append(m)
        idxs.append(idx)

    exps = [jnp.exp(v - vals[0]) for v in vals]
    denom = exps[0]
    for e in exps[1:]:
        denom = denom + e

    csum = sel
    s = 1
    while s < tm:
        csum = csum + jnp.where(tok >= s, pltpu.roll(csum, s, axis=1), 0.0)
        s *= 2
    before = base_ref[...] + csum - sel

    krow = lax.broadcasted_iota(jnp.int32, (8, tm), 0)
    gate = jnp.zeros((8, tm), F32)
    eidx = jnp.zeros((8, tm), F32)
    rank = jnp.zeros((8, tm), F32)
    for k in range(TOP_K):
        rk = jnp.sum(jnp.where(expert == idxs[k], before, 0.0), axis=0, keepdims=True)
        gate = jnp.where(krow == k, exps[k] / denom, gate)
        eidx = jnp.where(krow == k, idxs[k], eidx)
        rank = jnp.where(krow == k, rk, rank)
    eidx_ref[...] = eidx.astype(jnp.int32)
    rank_ref[...] = rank.astype(jnp.int32)
    gate_ref[...] = jnp.concatenate([gate, jnp.zeros((LANES - 8, tm), F32)], axis=0).T
    total = base_ref[...] + csum[:, tm - 1:tm]
    base_ref[...] = total

    @pl.when(pl.program_id(0) == pl.num_programs(0) - 1)
    def _():
        shape = (N_EXPERTS, LANES)
        sub = lax.broadcasted_iota(jnp.int32, shape, 0)
        lane = lax.broadcasted_iota(jnp.int32, shape, 1)
        counts = jnp.broadcast_to(total, shape)
        padded = jnp.floor((counts + (bm - 1)) * (1.0 / bm)) * bm
        pad_end = padded
        s = 1
        while s < N_EXPERTS:
            pad_end = pad_end + jnp.where(sub >= s, pltpu.roll(pad_end, s, axis=0), 0.0)
            s *= 2
        pad_start = pad_end - padded
        start_ref[...] = pad_start.astype(jnp.int32)
        for j in range(blk_ref.shape[1] // LANES):
            bstart = ((lane + j * LANES) * bm).astype(F32)
            bexp = jnp.minimum(jnp.sum((bstart >= pad_end).astype(F32), axis=0, keepdims=True), N_EXPERTS - 1.0)
            used_end = jnp.sum(jnp.where(sub.astype(F32) == bexp, pad_start + counts, 0.0), axis=0, keepdims=True)
            brows = jnp.clip(used_end - bstart[0:1, :], 0.0, float(bm))
            blk_ref[:, j * LANES:(j + 1) * LANES] = jnp.where(sub == 0, bexp, jnp.where(sub == 1, brows, 0.0)
                                                              ).astype(jnp.int32)


def _route(logits):
    t = logits.shape[0]
    tm = min(ROW_TILE, t)
    bm = EXPERT_ROWS
    row = lambda i: (i, 0)
    n_blocks = -(-(t * TOP_K + N_EXPERTS * (bm - 1)) // bm)
    blk_lanes = -(-n_blocks // LANES) * LANES
    gate, eidx, rank, start, blk = pl.pallas_call(
        functools.partial(_route_kernel, bm=bm),
        out_shape=(jax.ShapeDtypeStruct((t, LANES), F32), jax.ShapeDtypeStruct((8, t), jnp.int32),
                   jax.ShapeDtypeStruct((8, t), jnp.int32), jax.ShapeDtypeStruct((N_EXPERTS, LANES), jnp.int32),
                   jax.ShapeDtypeStruct((N_EXPERTS, blk_lanes), jnp.int32)),
        grid=(t // tm,),
        in_specs=[pl.BlockSpec((tm, LANES), row)],
        out_specs=(pl.BlockSpec((tm, LANES), row), pl.BlockSpec((8, tm), lambda i: (0, i)),
                   pl.BlockSpec((8, tm), lambda i: (0, i)), pl.BlockSpec((N_EXPERTS, LANES), lambda i: (0, 0)),
                   pl.BlockSpec((N_EXPERTS, blk_lanes), lambda i: (0, 0))),
        scratch_shapes=[pltpu.VMEM((N_EXPERTS, 1), F32)],
        compiler_params=_params("arbitrary"),
        name="moe_route",
    )(logits)
    pad_start = start[:, 0]
    onehot = eidx[:TOP_K, :, None] == jnp.arange(N_EXPERTS, dtype=jnp.int32)
    pos_kmajor = (jnp.sum(jnp.where(onehot, pad_start, 0), axis=-1) + rank[:TOP_K]).reshape(-1)
    return gate, pos_kmajor.astype(jnp.int32), blk[0, :n_blocks], blk[1, :n_blocks], n_blocks


def _sc_worker_range(n_rows):
    per_worker = n_rows // (SC_CORES * SC_SUBCORES)
    wid = lax.axis_index("s") * SC_CORES + lax.axis_index("c")
    return wid * per_worker, per_worker


def _sc_scatter_rows(src, idx, n_out):
    t, d = src.shape
    n_copies = idx.shape[0] // t
    assert t % (SC_CORES * SC_SUBCORES * SC_GATHER_ROWS) == 0, "rows must split evenly over the subcores"
    mesh = plsc.VectorSubcoreMesh(core_axis_name="c", subcore_axis_name="s")

    @functools.partial(
        pl.kernel, mesh=mesh,
        out_type=jax.ShapeDtypeStruct((n_out, d), src.dtype),
        scratch_types=([pltpu.VMEM((SC_GATHER_ROWS,), jnp.int32)] * n_copies
                       + [pltpu.VMEM((SC_GATHER_ROWS, d), src.dtype)]
                       + [pltpu.SemaphoreType.DMA] * n_copies),
        name="sc_scatter_rows",
    )
    def scatter(src_hbm, idx_hbm, out_hbm, *scratch):
        idx_v = scratch[:n_copies]
        rows_v = scratch[n_copies]
        sems = scratch[n_copies + 1:]
        base, per_worker = _sc_worker_range(t)

        @pl.loop(0, per_worker // SC_GATHER_ROWS)
        def _(j):
            off = pl.multiple_of(base + j * SC_GATHER_ROWS, SC_GATHER_ROWS)
            for k in range(n_copies):
                pltpu.sync_copy(idx_hbm.at[pl.ds(k * t + off, SC_GATHER_ROWS)], idx_v[k])
            pltpu.sync_copy(src_hbm.at[pl.ds(off, SC_GATHER_ROWS)], rows_v)
            copies = [pltpu.async_copy(rows_v, out_hbm.at[idx_v[k]], sems[k]) for k in range(n_copies)]
            for c in copies:
                c.wait()

    return scatter(src, idx)


def _expert_kernel(be_ref, nr_ref, x_ref, wgu_ref, bgu_ref, wd_ref, bd_ref, o_ref, wgu16_ref, wd16_ref, *, tn):
    i = pl.program_id(0)
    active = nr_ref[i] > 0
    new_expert = jnp.logical_or(i == 0, be_ref[i] != be_ref[jnp.maximum(i - 1, 0)])

    @pl.when(jnp.logical_and(active, new_expert))
    def _():
        wgu16_ref[...] = wgu_ref[0].astype(BF16)
        wd16_ref[...] = wd_ref[0].astype(BF16)

    @pl.when(active)
    def _():
        half = D_MODEL // 2
        row = lax.broadcasted_iota(jnp.int32, x_ref.shape, 0)
        x = jnp.where(row < nr_ref[i], x_ref[...], jnp.uint32(0))
        x_lo, x_hi = _unpack_bf16_pair(x)
        x_lo = x_lo.astype(BF16)
        x_hi = x_hi.astype(BF16)

        def up(c0):
            return (_dot(x_lo, wgu16_ref[:half, c0:c0 + tn]) + _dot(x_hi, wgu16_ref[half:, c0:c0 + tn])
                    + bgu_ref[0, :, c0:c0 + tn])

        y = bd_ref[0]
        for n0 in range(0, D_FF, tn):
            glu = jnp.minimum(up(n0), SWIGLU_LIMIT)
            lin = jnp.clip(up(D_FF + n0), -SWIGLU_LIMIT, SWIGLU_LIMIT)
            act = glu * _sigmoid(SWIGLU_ALPHA * glu) * (lin + 1.0)
            y = y + _dot(act.astype(BF16), wd16_ref[n0:n0 + tn, :])
        o_ref[...] = _pack_bf16_pair(y[:, :half], y[:, half:])

    @pl.when(jnp.logical_not(active))
    def _():
        o_ref[...] = jnp.zeros_like(o_ref)


def _experts(xs, block_exp, block_rows, w_gu, b_gu, w_down, b_down, n_blocks):
    d = D_MODEL
    dp = xs.shape[1]
    bm = EXPERT_ROWS
    grid_spec = pltpu.PrefetchScalarGridSpec(
        num_scalar_prefetch=2,
        grid=(n_blocks,),
        in_specs=[pl.BlockSpec((bm, dp), lambda i, be, nb: (i, 0)),
                  pl.BlockSpec((1, d, 2 * D_FF), lambda i, be, nb: (be[i], 0, 0)),
                  pl.BlockSpec((1, 1, 2 * D_FF), lambda i, be, nb: (be[i], 0, 0)),
                  pl.BlockSpec((1, D_FF, d), lambda i, be, nb: (be[i], 0, 0)),
                  pl.BlockSpec((1, 1, d), lambda i, be, nb: (be[i], 0, 0))],
        out_specs=pl.BlockSpec((bm, dp), lambda i, be, nb: (i, 0)),
        scratch_shapes=[pltpu.VMEM((d, 2 * D_FF), BF16), pltpu.VMEM((D_FF, d), BF16)],
    )
    return pl.pallas_call(
        functools.partial(_expert_kernel, tn=EXPERT_SLAB),
        out_shape=jax.ShapeDtypeStruct((n_blocks * bm, dp), jnp.uint32),
        grid_spec=grid_spec,
        compiler_params=_params("arbitrary"),
        name="moe_experts",
    )(block_exp, block_rows, xs, w_gu, b_gu.reshape(N_EXPERTS, 1, 2 * D_FF), w_down,
      b_down.reshape(N_EXPERTS, 1, d))


def _sc_gather_rows(table, idx):
    m = idx.shape[0]
    d = table.shape[1]
    assert m % (SC_CORES * SC_SUBCORES * 2 * SC_GATHER_ROWS) == 0, "rows must split evenly over the subcores"
    mesh = plsc.VectorSubcoreMesh(core_axis_name="c", subcore_axis_name="s")

    @functools.partial(
        pl.kernel, mesh=mesh,
        out_type=jax.ShapeDtypeStruct((m, d), table.dtype),
        scratch_types=([pltpu.VMEM((SC_GATHER_ROWS,), jnp.int32)] * 2
                       + [pltpu.VMEM((SC_GATHER_ROWS, d), table.dtype)] * 2
                       + [pltpu.SemaphoreType.DMA] * 4),
        name="sc_gather_rows",
    )
    def gather(table_hbm, idx_hbm, out_hbm, idx0, idx1, rows0, rows1, g0, g1, w0, w1):
        base, per_worker = _sc_worker_range(m)

        @pl.loop(0, per_worker // (2 * SC_GATHER_ROWS))
        def _(j):
            off0 = pl.multiple_of(base + 2 * j * SC_GATHER_ROWS, SC_GATHER_ROWS)
            off1 = off0 + SC_GATHER_ROWS
            pltpu.sync_copy(idx_hbm.at[pl.ds(off0, SC_GATHER_ROWS)], idx0)
            gather0 = pltpu.async_copy(table_hbm.at[idx0], rows0, g0)
            pltpu.sync_copy(idx_hbm.at[pl.ds(off1, SC_GATHER_ROWS)], idx1)
            gather1 = pltpu.async_copy(table_hbm.at[idx1], rows1, g1)
            gather0.wait()
            write0 = pltpu.async_copy(rows0, out_hbm.at[pl.ds(off0, SC_GATHER_ROWS)], w0)
            gather1.wait()
            write1 = pltpu.async_copy(rows1, out_hbm.at[pl.ds(off1, SC_GATHER_ROWS)], w1)
            write0.wait()
            write1.wait()

    return gather(table, idx)


def _combine_dense_kernel(y0_ref, y1_ref, y2_ref, y3_ref, gate_ref, h_ref, g_ref, b_ref, o_ref):
    gate = gate_ref[...]
    f_lo = f_hi = None
    for k, y_ref in enumerate((y0_ref, y1_ref, y2_ref, y3_ref)):
        y_lo, y_hi = _unpack_bf16_pair(y_ref[...])
        gk = gate[:, k:k + 1]
        f_lo = gk * y_lo if f_lo is None else f_lo + gk * y_lo
        f_hi = gk * y_hi if f_hi is None else f_hi + gk * y_hi
    f = jnp.concatenate([f_lo, f_hi], axis=1)
    o_ref[...] = _layer_norm(DN_ALPHA * h_ref[...] + f, g_ref[...], b_ref[...])


def _combine_dense(yg, gate, h1, ln_g, ln_b):
    t, d = h1.shape
    tm = min(ROW_TILE, t)
    nt = t // tm
    dp = yg.shape[1]
    slab = lambda k: pl.BlockSpec((tm, dp), lambda i, k=k: (k * nt + i, 0))
    return pl.pallas_call(
        _combine_dense_kernel,
        out_shape=jax.ShapeDtypeStruct((t, d), F32),
        grid=(nt,),
        in_specs=[slab(0), slab(1), slab(2), slab(3),
                  pl.BlockSpec((tm, LANES), lambda i: (i, 0)),
                  pl.BlockSpec((tm, d), lambda i: (i, 0)),
                  pl.BlockSpec((1, d), lambda i: (0, 0)),
                  pl.BlockSpec((1, d), lambda i: (0, 0))],
        out_specs=pl.BlockSpec((tm, d), lambda i: (i, 0)),
        compiler_params=_params("parallel"),
        name="moe_combine",
    )(yg, yg, yg, yg, gate, h1, ln_g.reshape(1, d), ln_b.reshape(1, d))


def kernel(x, emb_ln_g, emb_ln_b, w_in, conv_qkv, a_log, dt_bias, dn_norm_g, w_a_o, b_glu, conv_dw, b_dw, conv_ln_g, conv_ln_b, w_b_o, b_b_o, b_gate, w_out, ln1_g, ln1_b, w_router, b_router, w_gu, b_gu, w_down, b_down, ln2_g, ln2_b):
    bsz, seq, d = x.shape
    t = bsz * seq
    assert d == D_MODEL and seq % min(DELTA_ROWS, seq) == 0 and seq % 256 == 0 and t % min(ROW_TILE, t) == 0
    x2 = x.reshape(t, d)
    qkv, z, g, gtp, glu, gates = _inproj(x2, emb_ln_g, emb_ln_b, w_in[0], conv_qkv[0], b_glu[0], b_gate[0], a_log[0],
                                         dt_bias[0], seq)
    o_f, o_b = _delta_rule(qkv, g, gtp, bsz, seq)
    yc = _dw_conv(glu, conv_dw[0], b_dw[0], bsz, seq)

    h1, h1p, logits = _mix(o_f, o_b, z, yc, gates, x2, emb_ln_g, emb_ln_b, dn_norm_g[0],
                           w_a_o[0].astype(BF16), conv_ln_g[0], conv_ln_b[0], w_b_o[0].astype(BF16), b_b_o[0],
                           w_out[0].astype(BF16), ln1_g[0], ln1_b[0], w_router[0], b_router[0])

    gate, pos_kmajor, block_exp, block_rows, n_blocks = _route(logits)
    xs = _sc_scatter_rows(h1p, pos_kmajor, n_blocks * EXPERT_ROWS)
    ys = _experts(xs, block_exp, block_rows, w_gu[0], b_gu[0], w_down[0], b_down[0], n_blocks)
    yg = _sc_gather_rows(ys, pos_kmajor)
    out = _combine_dense(yg, gate, h1, ln2_g[0], ln2_b[0])
    return out.reshape(bsz, seq, d)
```
